```python
import math, functools
import jax, jax.numpy as jnp
from jax import lax
import numpy as np

D_MODEL = 1024
BATCH = 8
SEQ = 4096
DEPTH = 2
DEC_BATCH = 32
DEC_SEQ = 8
PAST_LEN = 16384
PAGE_SIZE = 128

EPS = 1e-6
NEG = -1e30
TINY = 1e-30
CONV_W = 4
D_FF = 4 * D_MODEL
N_BRANCH = 4
H_A = 4
DK_A = 64
DV_A = 64
CHUNK_A = 64
QK_A = H_A * DK_A
CONV_A = 2 * QK_A + H_A * DV_A
H_B = 4
DH_B = 64
H_IDX = 4
D_IDX = 32
TOPK_MAX = 256
Q_BLOCK = 128
N_BUCKETS = 32
MAX_DISTANCE = 128
H_C = 8
P_C = 64
N_C = 64
G_C = 2
D_INNER_C = H_C * P_C
CHUNK_C = 128
CONV_C = D_INNER_C + 2 * G_C * N_C
H_D = 4
DK_D = 64
DV_D = 64
CHUNK_D = 64

CONV_CH = CONV_A + CONV_C
W_A = H_A * DV_A
W_B = H_B * DH_B
W_C = D_INNER_C
W_D = H_D * DV_D
MIX_W = W_A + W_B + W_C + W_D
IN_WIDTHS = (CONV_CH, W_A, H_A, H_A, W_B, DH_B, DH_B, H_IDX * D_IDX, D_IDX, H_IDX, D_INNER_C, H_C, H_D * DK_D, H_D * DK_D, W_D, W_D, N_BRANCH * D_MODEL)
P_IN = sum(IN_WIDTHS)

kernel_name = 'hybrid_deltanet_dsa_ssd_hgrn2_step'


def split_in(p):
    parts, o = [], 0
    for w in IN_WIDTHS:
        parts.append(p[..., o:o + w])
        o += w
    return parts


def rmsnorm(x, g):
    xf = x.astype(jnp.float32)
    r = lax.rsqrt(jnp.mean(xf * xf, axis=-1, keepdims=True) + EPS)
    return (xf * r).astype(x.dtype) * g


def l2norm(x):
    xf = x.astype(jnp.float32)
    return xf * lax.rsqrt(jnp.sum(xf * xf, axis=-1, keepdims=True) + EPS)


def causal_conv(u, buf, w, b):
    T = u.shape[1]
    cat = jnp.concatenate([buf.astype(u.dtype), u], axis=1)
    y = sum(cat[:, j:j + T] * w[j] for j in range(CONV_W)) + b
    return y, cat[:, T:]


def _to_chunks(a, C):
    Bn, T, H = a.shape[:3]
    a = a.reshape((Bn, T // C, C, H) + a.shape[3:])
    return jnp.moveaxis(a, (1, 3), (0, 2))


def _from_chunks(a):
    n, Bn, H, C = a.shape[:4]
    return jnp.moveaxis(a, (0, 2), (1, 3)).reshape((Bn, n * C, H) + a.shape[4:])


def gated_delta_chunked(q, k, v, g, beta, s0):
    f32 = jnp.float32
    C = math.gcd(q.shape[1], CHUNK_A)
    qc, kc, vc, gch, bch = (_to_chunks(a.astype(f32), C) for a in (q * DK_A ** -0.5, k, v, g, beta))
    strict = jnp.tril(jnp.ones((C, C), bool), -1)
    incl = jnp.tril(jnp.ones((C, C), bool))
    eye = jnp.eye(C, dtype=f32)

    def step(S, inp):
        qq, kk, vv, gg, bb = inp
        gcum = jnp.cumsum(gg, axis=-1)
        diff = gcum[..., :, None] - gcum[..., None, :]
        dec_s = jnp.where(strict, jnp.exp(jnp.where(strict, diff, 0.0)), 0.0)
        dec_i = jnp.where(incl, jnp.exp(jnp.where(incl, diff, 0.0)), 0.0)
        kb = kk * bb[..., None]
        lower = jnp.einsum('bhik,bhjk->bhij', kb, kk) * dec_s
        rhs = jnp.concatenate([vv * bb[..., None], kb * jnp.exp(gcum)[..., None]], axis=-1)
        sol = lax.linalg.triangular_solve(lower + eye, rhs, left_side=True, lower=True, unit_diagonal=True)
        u, w = sol[..., :DV_A], sol[..., DV_A:]
        v_new = u - jnp.einsum('bhik,bhkv->bhiv', w, S)
        attn = jnp.einsum('bhik,bhjk->bhij', qq, kk) * dec_i
        o = jnp.einsum('bhik,bhkv->bhiv', qq * jnp.exp(gcum)[..., None], S) + jnp.einsum('bhij,bhjv->bhiv', attn, v_new)
        g_last = gcum[..., -1:]
        S = S * jnp.exp(g_last)[..., None] + jnp.einsum('bhjk,bhjv->bhkv', kk * jnp.exp(g_last - gcum)[..., None], v_new)
        return S, o

    S, o = lax.scan(step, s0.astype(f32), (qc, kc, vc, gch, bch))
    return _from_chunks(o), S


def gated_linear_scan(q, k, v, log_decay, s0, chunk):
    f32 = jnp.float32
    C = math.gcd(q.shape[1], chunk)
    per_channel = log_decay.ndim == 4
    if not per_channel:
        log_decay = log_decay[..., None]
    qc, kc, vc, gch = (_to_chunks(a.astype(f32), C) for a in (q, k, v, log_decay))
    incl = jnp.tril(jnp.ones((C, C), bool))[..., None]

    def step(S, inp):
        qq, kk, vv, gg = inp
        gcum = jnp.cumsum(gg, axis=2)
        diff = gcum[:, :, :, None, :] - gcum[:, :, None, :, :]
        dec = jnp.where(incl, jnp.exp(jnp.where(incl, diff, 0.0)), 0.0)
        if per_channel:
            attn = jnp.einsum('bhik,bhjk,bhijk->bhij', qq, kk, dec)
        else:
            attn = jnp.einsum('bhik,bhjk->bhij', qq, kk) * dec[..., 0]
        o = jnp.einsum('bhik,bhkv->bhiv', qq * jnp.exp(gcum), S) + jnp.einsum('bhij,bhjv->bhiv', attn, vv)
        g_last = gcum[:, :, -1:]
        S = S * jnp.exp(jnp.swapaxes(g_last, -1, -2)) + jnp.einsum('bhjk,bhjv->bhkv', kk * jnp.exp(g_last - gcum), vv)
        return S, o

    S, o = lax.scan(step, s0.astype(f32), (qc, kc, vc, gch))
    return _from_chunks(o), S


def rel_bucket(dist):
    max_exact = N_BUCKETS // 2
    d = jnp.maximum(dist, 0)
    log_ratio = jnp.log(jnp.maximum(d, 1).astype(jnp.float32) / max_exact) / math.log(MAX_DISTANCE / max_exact)
    large = jnp.minimum(max_exact + (jnp.maximum(log_ratio, 0.0) * (N_BUCKETS - max_exact)).astype(jnp.int32), N_BUCKETS - 1)
    return jnp.where(d < max_exact, d, large)


def gather_rows(a, idx):
    return jax.vmap(lambda rows, i: rows[i])(a, idx)


def index_select(qi, wi, ki_all, pos_q, n_sel):
    s = jax.nn.relu(jnp.einsum('bqhd,bsd->bqhs', qi, ki_all) * D_IDX ** -0.5)
    score = jnp.einsum('bqhs,bqh->bqs', s, wi).astype(jnp.float32)
    visible = jnp.arange(ki_all.shape[1])[None, :] <= pos_q[:, None]
    score = jnp.where(visible, score, NEG)
    return lax.top_k(score, n_sel)[1]


def sparse_attend(q, k_sel, v_sel, idx, pos_q, rel_bias):
    Bn, Q = q.shape[:2]
    dist = pos_q[None, :, None] - idx
    logits = jnp.einsum('bqhd,bqkd->bqhk', q, k_sel).astype(jnp.float32) * DH_B ** -0.5
    bias = jnp.moveaxis(rel_bias[rel_bucket(dist)], -1, 2).astype(jnp.float32)
    logits = jnp.where((dist >= 0)[:, :, None, :], logits + bias, NEG)
    p = jax.nn.softmax(logits, axis=-1).astype(v_sel.dtype)
    return jnp.einsum('bqhk,bqkd->bqhd', p, v_sel).reshape(Bn, Q, H_B * DH_B)


def dsa_prompt(q, k, v, qi, ki, wi, rel_bias):
    Bn, T = q.shape[:2]
    n_sel = min(TOPK_MAX, T // 4)
    qb = math.gcd(T, Q_BLOCK)
    nb = T // qb

    def blocks(a):
        return jnp.moveaxis(a.reshape((Bn, nb, qb) + a.shape[2:]), 1, 0)

    def one_block(args):
        qq, qqi, wwi, b0 = args
        pos_q = b0 * qb + jnp.arange(qb)
        idx = index_select(qqi, wwi, ki, pos_q, n_sel)
        return sparse_attend(qq, gather_rows(k, idx), gather_rows(v, idx), idx, pos_q, rel_bias)

    o = lax.map(one_block, (blocks(q), blocks(qi), blocks(wi), jnp.arange(nb)))
    return jnp.moveaxis(o, 0, 1).reshape(Bn, T, H_B * DH_B)


def dsa_sample(q, k, v, qi, ki, wi, rel_bias, ck, cv, cki, page_table):
    DB, DS = q.shape[:2]
    n_sel = min(TOPK_MAX, (PAST_LEN + DS) // 4)
    ki_past = cki[page_table].reshape(DB, PAST_LEN, D_IDX)
    ki_all = jnp.concatenate([ki_past.astype(ki.dtype), ki], axis=1)
    pos_q = PAST_LEN + jnp.arange(DS)
    idx = index_select(qi, wi, ki_all, pos_q, n_sel)
    is_past = (idx < PAST_LEN)[..., None]
    ip = jnp.minimum(idx, PAST_LEN - 1)
    phys = jax.vmap(lambda pt, i: pt[i])(page_table, ip // PAGE_SIZE)
    off = ip % PAGE_SIZE
    inew = jnp.clip(idx - PAST_LEN, 0, DS - 1)
    k_sel = jnp.where(is_past, ck[phys, off].astype(k.dtype), gather_rows(k, inew))
    v_sel = jnp.where(is_past, cv[phys, off].astype(v.dtype), gather_rows(v, inew))
    return sparse_attend(q, k_sel, v_sel, idx, pos_q, rel_bias)


def hybrid_layer(x, conv_buf, s_delta, s_ssm, s_hgrn, lw, sparse_attn):
    Bn, T, _ = x.shape
    f32 = jnp.float32
    h = rmsnorm(x, lw['norm_mix'])
    (pc, z_a, a_a, b_a, q_b, k_b, v_b, q_i, k_i, w_i, z_c, dt_c, q_d, f_d, i_d, g_d, gate) = split_in(h @ lw['w_in'])
    u, conv_new = causal_conv(pc, conv_buf, lw['conv_w'], lw['conv_b'])
    u = jax.nn.silu(u)

    qa = l2norm(u[..., :QK_A].reshape(Bn, T, H_A, DK_A))
    ka = l2norm(u[..., QK_A:2 * QK_A].reshape(Bn, T, H_A, DK_A))
    va = u[..., 2 * QK_A:CONV_A].reshape(Bn, T, H_A, DV_A)
    beta = jax.nn.sigmoid(b_a.astype(f32))
    g = -jnp.exp(lw['a_log_a'].astype(f32)) * jax.nn.softplus(a_a.astype(f32) + lw['dt_bias_a'])
    o_a, s_delta_new = gated_delta_chunked(qa, ka, va, g, beta, s_delta)
    o_a = (rmsnorm(o_a.astype(x.dtype), lw['gnorm_a']) * jax.nn.silu(z_a).reshape(Bn, T, H_A, DV_A)).reshape(Bn, T, W_A)

    o_b = sparse_attn(q_b.reshape(Bn, T, H_B, DH_B), k_b, v_b, q_i.reshape(Bn, T, H_IDX, D_IDX), k_i, w_i * H_IDX ** -0.5)

    xc = u[..., CONV_A:CONV_A + D_INNER_C].reshape(Bn, T, H_C, P_C)
    bc = u[..., CONV_A + D_INNER_C:CONV_A + D_INNER_C + G_C * N_C].reshape(Bn, T, G_C, N_C)
    cc = u[..., CONV_A + D_INNER_C + G_C * N_C:].reshape(Bn, T, G_C, N_C)
    bc = jnp.repeat(bc, H_C // G_C, axis=2)
    cc = jnp.repeat(cc, H_C // G_C, axis=2)
    dt = jax.nn.softplus(dt_c.astype(f32) + lw['dt_bias_c'])
    y, s_ssm_new = gated_linear_scan(cc, bc * dt[..., None], xc, -jnp.exp(lw['a_log_c'].astype(f32)) * dt, s_ssm, CHUNK_C)
    y = y.astype(x.dtype) + lw['d_skip_c'][:, None] * xc
    y = (y * jax.nn.silu(z_c).reshape(Bn, T, H_C, P_C)).reshape(Bn, T, G_C, D_INNER_C // G_C)
    o_c = rmsnorm(y, lw['gnorm_c'].reshape(G_C, D_INNER_C // G_C)).reshape(Bn, T, W_C)

    lb = lw['lb']
    fr = f_d.astype(f32).reshape(Bn, T, H_D, DK_D)
    f_gate = lb + (1.0 - lb) * jax.nn.sigmoid(fr)
    logf = jnp.log(jnp.maximum(f_gate, TINY))
    kd = (1.0 - lb) * jax.nn.sigmoid(-fr)
    o_d, s_hgrn_new = gated_linear_scan(jax.nn.silu(q_d).reshape(Bn, T, H_D, DK_D), kd, i_d.reshape(Bn, T, H_D, DV_D), logf, s_hgrn, CHUNK_D)
    o_d = (rmsnorm(o_d.astype(x.dtype), lw['gnorm_d']) * jax.nn.silu(g_d).reshape(Bn, T, H_D, DV_D)).reshape(Bn, T, W_D)

    gates = jax.nn.sigmoid(gate.reshape(Bn, T, N_BRANCH, D_MODEL))
    wb = lw['w_branch']
    merged = (gates[:, :, 0] * (o_a @ wb[:W_A])
              + gates[:, :, 1] * (o_b @ wb[W_A:W_A + W_B])
              + gates[:, :, 2] * (o_c @ wb[W_A + W_B:W_A + W_B + W_C])
              + gates[:, :, 3] * (o_d @ wb[W_A + W_B + W_C:]))
    x = x + merged @ lw['w_out']

    h2 = rmsnorm(x, lw['norm_mlp'])
    x = x + jnp.square(jax.nn.relu(h2 @ lw['w_up'])) @ lw['w_down']
    dtp = x.dtype
    return x, (k_b, v_b, k_i, conv_new, s_delta_new.astype(dtp), s_ssm_new.astype(dtp), s_hgrn_new.astype(dtp))


def _stack_layers(per_layer):
    return [jnp.stack(items) for items in zip(*per_layer)]


def setup_inputs(seed: int = 0) -> dict:
    key = jax.random.key(seed)
    ks = jax.random.split(key, 32)
    f32 = jnp.float32

    def nrm(i, shape, scale=1.0):
        return jax.random.normal(ks[i], shape, f32) * scale

    def unif(i, shape, lo, hi):
        return jax.random.uniform(ks[i], shape, f32, lo, hi)

    n_pages = PAST_LEN // PAGE_SIZE
    n_used = DEC_BATCH * n_pages
    n_pool = n_used + n_used // 4
    page_table = jax.random.permutation(ks[0], n_pool)[:n_used].reshape(DEC_BATCH, n_pages).astype(jnp.int32)
    return {
        'x_prompt': nrm(1, (BATCH, SEQ, D_MODEL)),
        'x_sample': nrm(2, (DEC_BATCH, DEC_SEQ, D_MODEL)),
        'cache_k': nrm(3, (DEPTH, n_pool, PAGE_SIZE, DH_B)),
        'cache_v': nrm(4, (DEPTH, n_pool, PAGE_SIZE, DH_B)),
        'cache_kidx': nrm(5, (DEPTH, n_pool, PAGE_SIZE, D_IDX)),
        'state_conv': nrm(6, (DEPTH, DEC_BATCH, CONV_W - 1, CONV_CH)),
        'state_delta': nrm(7, (DEPTH, DEC_BATCH, H_A, DK_A, DV_A), DK_A ** -0.5),
        'state_ssm': nrm(8, (DEPTH, DEC_BATCH, H_C, N_C, P_C), 0.5),
        'state_hgrn': nrm(9, (DEPTH, DEC_BATCH, H_D, DK_D, DV_D)),
        'page_table': page_table,
        'norm_mix': 1.0 + nrm(10, (DEPTH, D_MODEL), 0.01),
        'w_in': nrm(11, (DEPTH, D_MODEL, P_IN), D_MODEL ** -0.5),
        'conv_w': nrm(12, (DEPTH, CONV_W, CONV_CH), CONV_W ** -0.5),
        'conv_b': nrm(13, (DEPTH, CONV_CH), 0.01),
        'a_log_a': jnp.log(unif(14, (DEPTH, H_A), 1.0, 16.0)),
        'dt_bias_a': nrm(15, (DEPTH, H_A), 0.5) - 4.0,
        'gnorm_a': 1.0 + nrm(16, (DEPTH, DV_A), 0.01),
        'rel_bias': nrm(17, (N_BUCKETS, H_B), 0.1),
        'a_log_c': jnp.log(unif(18, (DEPTH, H_C), 1.0, 16.0)),
        'dt_bias_c': nrm(19, (DEPTH, H_C), 0.5) - 4.0,
        'd_skip_c': 1.0 + nrm(20, (DEPTH, H_C), 0.1),
        'gnorm_c': 1.0 + nrm(21, (DEPTH, D_INNER_C), 0.01),
        'hgrn_gamma': nrm(22, (DEPTH, H_D * DK_D)),
        'gnorm_d': 1.0 + nrm(23, (DEPTH, DV_D), 0.01),
        'w_branch': nrm(24, (DEPTH, MIX_W, D_MODEL), (MIX_W / N_BRANCH) ** -0.5),
        'w_out': nrm(25, (DEPTH, D_MODEL, D_MODEL), D_MODEL ** -0.5),
        'norm_mlp': 1.0 + nrm(26, (DEPTH, D_MODEL), 0.01),
        'w_up': nrm(27, (DEPTH, D_MODEL, D_FF), D_MODEL ** -0.5),
        'w_down': nrm(28, (DEPTH, D_FF, D_MODEL), D_FF ** -0.5),
        'norm_final': 1.0 + nrm(29, (D_MODEL,), 0.01),
    }


def reference(x_prompt, x_sample, cache_k, cache_v, cache_kidx, state_conv, state_delta, state_ssm, state_hgrn, page_table,
              norm_mix, w_in, conv_w, conv_b, a_log_a, dt_bias_a, gnorm_a, rel_bias, a_log_c, dt_bias_c, d_skip_c,
              gnorm_c, hgrn_gamma, gnorm_d, w_branch, w_out, norm_mlp, w_up, w_down, norm_final):
    pg = jax.nn.softmax(hgrn_gamma.astype(jnp.float32), axis=0)
    lower_bounds = jnp.cumsum(pg, axis=0) - pg[0]
    nb, dtp = x_prompt.shape[0], x_prompt.dtype
    zero_conv = jnp.zeros((nb, CONV_W - 1, CONV_CH), dtp)
    zero_delta = jnp.zeros((nb, H_A, DK_A, DV_A), dtp)
    zero_ssm = jnp.zeros((nb, H_C, N_C, P_C), dtp)
    zero_hgrn = jnp.zeros((nb, H_D, DK_D, DV_D), dtp)
    yp, ys = x_prompt, x_sample
    new_p, new_s = [], []
    for l in range(DEPTH):
        lw = dict(norm_mix=norm_mix[l], w_in=w_in[l], conv_w=conv_w[l], conv_b=conv_b[l], a_log_a=a_log_a[l],
                  dt_bias_a=dt_bias_a[l], gnorm_a=gnorm_a[l], a_log_c=a_log_c[l], dt_bias_c=dt_bias_c[l],
                  d_skip_c=d_skip_c[l], gnorm_c=gnorm_c[l], gnorm_d=gnorm_d[l], w_branch=w_branch[l], w_out=w_out[l],
                  norm_mlp=norm_mlp[l], w_up=w_up[l], w_down=w_down[l], lb=lower_bounds[l].reshape(H_D, DK_D))
        attn_p = functools.partial(dsa_prompt, rel_bias=rel_bias)
        attn_s = functools.partial(dsa_sample, rel_bias=rel_bias, ck=cache_k[l], cv=cache_v[l], cki=cache_kidx[l], page_table=page_table)
        yp, st_p = hybrid_layer(yp, zero_conv, zero_delta, zero_ssm, zero_hgrn, lw, attn_p)
        ys, st_s = hybrid_layer(ys, state_conv[l], state_delta[l], state_ssm[l], state_hgrn[l], lw, attn_s)
        new_p.append(st_p)
        new_s.append(st_s)
    y_prompt = rmsnorm(yp, norm_final)
    y_sample = rmsnorm(ys, norm_final)
    k_p, v_p, ki_p, conv_p, delta_p, ssm_p, hgrn_p = _stack_layers(new_p)
    k_s, v_s, ki_s, conv_s, delta_s, ssm_s, hgrn_s = _stack_layers(new_s)
    return (y_prompt, y_sample, k_p, v_p, ki_p, conv_p, delta_p, ssm_p, hgrn_p, k_s, v_s, ki_s, conv_s, delta_s, ssm_s, hgrn_s)
```

```python
import functools
import math

import numpy as np
import jax
import jax.numpy as jnp
from jax import lax
from jax.experimental import pallas as pl
from jax.experimental.pallas import tpu as pltpu

F32 = jnp.float32
BF16 = jnp.bfloat16
I32 = jnp.int32

D_MODEL = 1024
PAGE_SIZE = 128
EPS = 1e-6
NEG = -1e30
TINY = 1e-30
CONV_W = 4
D_FF = 4 * D_MODEL
N_BRANCH = 4
H_A, DK_A, DV_A = 4, 64, 64
H_B, DH_B = 4, 64
H_IDX, D_IDX = 4, 32
TOPK_MAX = 256
N_BUCKETS = 32
MAX_DISTANCE = 128
H_C, P_C, N_C, G_C = 8, 64, 64, 2
H_D, DK_D, DV_D = 4, 64, 64
QK_A = H_A * DK_A
CONV_A = 2 * QK_A + H_A * DV_A
D_INNER_C = H_C * P_C
CONV_C = D_INNER_C + 2 * G_C * N_C
CONV_CH = CONV_A + CONV_C
W_A, W_B, W_C, W_D = H_A * DV_A, H_B * DH_B, D_INNER_C, H_D * DV_D
MIX_W = W_A + W_B + W_C + W_D
IN_WIDTHS = (CONV_CH, W_A, H_A, H_A, W_B, DH_B, DH_B, H_IDX * D_IDX, D_IDX, H_IDX, D_INNER_C, H_C,
             H_D * DK_D, H_D * DK_D, W_D, W_D, N_BRANCH * D_MODEL)
P_IN = sum(IN_WIDTHS)

LANE = 128
SUB = 8
TC = 128
QB = 128
VMEM_LIMIT = 48 * 1024 * 1024

OFF_GATE, OFF_D, OFF_ZC, OFF_ZA, OFF_QB, OFF_PC, OFF_KV, OFF_QI, OFF_SM = 0, 4096, 5120, 5632, 5888, 6144, 7680, 7808, 7936
P_PAD = 8064
SM_KI, SM_WI, SM_AA, SM_BA, SM_DT = 0, 32, 36, 40, 44


def _src_offsets():
    offs, o = [], 0
    for w in IN_WIDTHS:
        offs.append(o)
        o += w
    return offs


def _permute_w_in(w):
    (o_pc, o_za, o_aa, o_ba, o_qb, o_kb, o_vb, o_qi, o_ki, o_wi, o_zc, o_dt, o_qd, o_fd, o_id, o_gd, o_gate) = _src_offsets()
    seg = lambda o, n: w[:, o:o + n]
    small = jnp.concatenate([seg(o_ki, D_IDX), seg(o_wi, H_IDX), seg(o_aa, H_A), seg(o_ba, H_A), seg(o_dt, H_C),
                             jnp.zeros((w.shape[0], LANE - (D_IDX + H_IDX + 2 * H_A + H_C)), w.dtype)], axis=1)
    out = jnp.concatenate([
        seg(o_gate, N_BRANCH * D_MODEL),
        seg(o_qd, 4 * W_D),
        seg(o_zc, D_INNER_C), seg(o_za, W_A), seg(o_qb, W_B), seg(o_pc, CONV_CH),
        seg(o_kb, 2 * DH_B),
        seg(o_qi, H_IDX * D_IDX), small], axis=1)
    assert out.shape[1] == P_PAD
    return out.astype(BF16)


def _bdot(a, b):
    return jnp.dot(a.astype(BF16), b.astype(BF16), preferred_element_type=F32)


def _bdot_nt(a, b):
    return lax.dot_general(a.astype(BF16), b.astype(BF16), (((1,), (1,)), ((), ())), preferred_element_type=F32)


def _split2(a):
    hi = a.astype(BF16)
    lo = (a - hi.astype(F32)).astype(BF16)
    return hi, lo


def _split3(a):
    hi = a.astype(BF16)
    r = a - hi.astype(F32)
    mid = r.astype(BF16)
    lo = (r - mid.astype(F32)).astype(BF16)
    return hi, mid, lo


def _dot01_left(m01, x):
    hi, mid, lo = _split3(x)
    d = lambda p: jnp.dot(m01, p, preferred_element_type=F32)
    return d(hi) + (d(mid) + d(lo))


def _dot01_right(x, m01):
    hi, mid, lo = _split3(x)
    d = lambda p: jnp.dot(p, m01, preferred_element_type=F32)
    return d(hi) + (d(mid) + d(lo))


def _xdot(a, b):
    ah, al = _split2(a)
    bh, bl = _split2(b)
    d = lambda p, q: jnp.dot(p, q, preferred_element_type=F32)
    return d(ah, bh) + (d(ah, bl) + d(al, bh))


def _sigmoid(x):
    return 1.0 / (1.0 + jnp.exp(-x))


def _silu(x):
    return x * _sigmoid(x)


def _softplus(x):
    return jnp.maximum(x, 0.0) + jnp.log(1.0 + jnp.exp(-jnp.abs(x)))


def _iota2(shape, dim):
    return lax.broadcasted_iota(I32, shape, dim)


def _pad_rows(x, rows):
    if x.shape[0] == rows:
        return x
    return jnp.concatenate([x, jnp.zeros((rows - x.shape[0],) + x.shape[1:], x.dtype)], axis=0)


def _tri_inv(L):
    n = L.shape[0]
    ii, jj = _iota2((n, n), 0), _iota2((n, n), 1)
    blk = (ii >> 3) == (jj >> 3)
    N = jnp.where(blk, -L, 0.0)
    X = jnp.where(ii == jj, 1.0, 0.0) + N
    N2 = _xdot(N, N)
    X = X + _xdot(X, N2)
    N4 = _xdot(N2, N2)
    X = X + _xdot(X, N4)
    sh = 4
    while (1 << sh) <= n:
        nblk = (ii >> sh) == (jj >> sh)
        Bm = jnp.where(nblk & jnp.logical_not(blk), L, 0.0)
        X = X - _xdot(_xdot(X, Bm), X)
        blk = nblk
        sh += 1
    return X


def _conv_silu(pc, ext_scr, cw_ref, cb_ref):
    ext_scr[SUB:SUB + TC, :] = pc
    y = cb_ref[...]
    for j in range(CONV_W):
        y = y + ext_scr[SUB - (CONV_W - 1) + j:SUB - (CONV_W - 1) + j + TC, :] * cw_ref[j:j + 1, :]
    ext_scr[0:SUB, :] = ext_scr[TC:TC + SUB, :]
    return _silu(y)


def _in_proj_kernel(x_ref, g_ref, w_ref, o_ref, h_scr):
    @pl.when(pl.program_id(1) == 0)
    def _():
        x = x_ref[...]
        r = lax.rsqrt(jnp.mean(x * x, axis=-1, keepdims=True) + EPS)
        h_scr[...] = ((x * r) * g_ref[...]).astype(BF16)

    o_ref[...] = jnp.dot(h_scr[...], w_ref[...], preferred_element_type=F32)


def _in_proj(x2d, g, w_perm):
    n = x2d.shape[0]
    tm = min(n, 512)
    tn = 1152
    return pl.pallas_call(
        _in_proj_kernel,
        grid=(n // tm, P_PAD // tn),
        in_specs=[pl.BlockSpec((tm, D_MODEL), lambda i, j: (i, 0)),
                  pl.BlockSpec((1, D_MODEL), lambda i, j: (0, 0)),
                  pl.BlockSpec((D_MODEL, tn), lambda i, j: (0, j))],
        out_specs=pl.BlockSpec((tm, tn), lambda i, j: (i, j)),
        out_shape=jax.ShapeDtypeStruct((n, P_PAD), F32),
        scratch_shapes=[pltpu.VMEM((tm, D_MODEL), BF16)],
        compiler_params=pltpu.CompilerParams(dimension_semantics=("arbitrary", "arbitrary"), vmem_limit_bytes=VMEM_LIMIT),
        name="in_proj",
    )(x2d, g.reshape(1, D_MODEL), w_perm)


def _mix_a_kernel(pc_ref, za_ref, sm_ref, conv0_ref, s0_ref, cw_ref, cb_ref, hp_ref, hpt_ref, gn_ref,
                  tril_ref, triu_ref, bo_ref, o_ref, sout_ref, ext_scr, s_scr, ob_scr, *, t_valid, rows):
    CA = 64

    @pl.when(pl.program_id(1) == 0)
    def _():
        ext_scr[0:SUB, :] = conv0_ref[...]
        s_scr[...] = s0_ref[...]

    u = _conv_silu(_pad_rows(pc_ref[...], TC), ext_scr, cw_ref, cb_ref)
    bo = bo_ref[...]
    q_raw, k_raw, va = u[:, 0:QK_A], u[:, QK_A:2 * QK_A], u[:, 2 * QK_A:CONV_A]
    qq = q_raw * lax.rsqrt(_dot01_right(q_raw * q_raw, bo) + EPS) * (DK_A ** -0.5)
    ka = k_raw * lax.rsqrt(_dot01_right(k_raw * k_raw, bo) + EPS)
    kat = ka.T

    sm = _pad_rows(sm_ref[...], TC)
    smt = sm.T
    hp, hpt = hp_ref[...], hpt_ref[...]
    g_col = -jnp.exp(hp[0:1, 0:H_A]) * _softplus(sm[:, SM_AA:SM_AA + H_A] + hp[1:2, 0:H_A])
    beta = _sigmoid(sm[:, SM_BA:SM_BA + H_A])
    g_row = -jnp.exp(hpt[0:H_A, 0:1]) * _softplus(smt[SM_AA:SM_AA + H_A, :] + hpt[0:H_A, 1:2])
    if t_valid < TC:
        g_col = jnp.where(_iota2(g_col.shape, 0) < t_valid, g_col, 0.0)
        beta = jnp.where(_iota2(beta.shape, 0) < t_valid, beta, 0.0)
        g_row = jnp.where(_iota2(g_row.shape, 1) < t_valid, g_row, 0.0)
    gcum_col = _dot01_left(tril_ref[...], g_col)
    gcum_row = _dot01_right(g_row, triu_ref[...])

    ii, jj = _iota2((CA, CA), 0), _iota2((CA, CA), 1)
    incl, strict = ii >= jj, ii > jj
    for sc in range(TC // CA):
        r0 = sc * CA
        for h in range(H_A):
            c0 = h * DK_A
            gc = gcum_col[r0:r0 + CA, h:h + 1]
            gr = gcum_row[h:h + 1, r0:r0 + CA]
            e = jnp.exp(jnp.where(incl, gc - gr, 0.0))
            dec_i = jnp.where(incl, e, 0.0)
            dec_s = jnp.where(strict, e, 0.0)
            kah = ka[r0:r0 + CA, c0:c0 + DK_A]
            qh = qq[r0:r0 + CA, c0:c0 + DK_A]
            vh = va[r0:r0 + CA, c0:c0 + DV_A]
            bh = beta[r0:r0 + CA, h:h + 1]
            kb = kah * bh
            tinv = _tri_inv(_bdot_nt(kb, kah) * dec_s)
            eg = jnp.exp(gc)
            sol = _xdot(tinv, jnp.concatenate([vh * bh, kb * eg], axis=1))
            S = s_scr[h]
            v_new = sol[:, 0:DV_A] - _bdot(sol[:, DV_A:], S)
            attn = _bdot_nt(qh, kah) * dec_i
            ob_scr[r0:r0 + CA, c0:c0 + DV_A] = _bdot(qh * eg, S) + _bdot(attn, v_new)
            gl = gc[CA - 1:CA, :]
            s_scr[h] = S * jnp.exp(gl) + _bdot(kat[c0:c0 + DK_A, r0:r0 + CA], v_new * jnp.exp(gl - gc))

    o = ob_scr[...]
    o = o * lax.rsqrt(_dot01_right(o * o, bo) * (1.0 / DV_A) + EPS) * gn_ref[...]
    res = o * _silu(_pad_rows(za_ref[...], TC))
    o_ref[...] = res[0:rows]
    sout_ref[...] = s_scr[...]


def _mix_c_kernel(pc_ref, zc_ref, sm_ref, conv0_ref, s0_ref, cw_ref, cb_ref, hp_ref, hpt_ref, gn_ref,
                  tril_ref, triu_ref, bo_ref, o_ref, sout_ref, ext_scr, s_scr, yb_scr, *, t_valid, rows):
    @pl.when(pl.program_id(1) == 0)
    def _():
        ext_scr[0:SUB, :] = conv0_ref[...]
        s_scr[...] = s0_ref[...]

    u = _conv_silu(_pad_rows(pc_ref[...], TC), ext_scr, cw_ref, cb_ref)
    xc = u[:, 0:D_INNER_C]
    bcm = u[:, D_INNER_C:D_INNER_C + G_C * N_C]
    ccm = u[:, D_INNER_C + G_C * N_C:CONV_C]
    bct = bcm.T
    zc = _pad_rows(zc_ref[...], TC)

    sm = _pad_rows(sm_ref[...], TC)
    smt = sm.T
    hp, hpt = hp_ref[...], hpt_ref[...]
    dt_col = _softplus(sm[:, SM_DT:SM_DT + H_C] + hp[3:4, 0:H_C])
    dt_row = _softplus(smt[SM_DT:SM_DT + H_C, :] + hpt[0:H_C, 3:4])
    if t_valid < TC:
        dt_col = jnp.where(_iota2(dt_col.shape, 0) < t_valid, dt_col, 0.0)
        dt_row = jnp.where(_iota2(dt_row.shape, 1) < t_valid, dt_row, 0.0)
    gcum_col = _dot01_left(tril_ref[...], -jnp.exp(hp[2:3, 0:H_C]) * dt_col)
    gcum_row = _dot01_right(-jnp.exp(hpt[0:H_C, 2:3]) * dt_row, triu_ref[...])

    incl = _iota2((TC, TC), 0) >= _iota2((TC, TC), 1)
    hg = H_C // G_C
    for g in range(G_C):
        cc_g = ccm[:, g * N_C:(g + 1) * N_C]
        gram = _bdot_nt(cc_g, bcm[:, g * N_C:(g + 1) * N_C])
        for hh in range(hg):
            h = g * hg + hh
            c0 = h * P_C
            gc = gcum_col[:, h:h + 1]
            gr = gcum_row[h:h + 1, :]
            dec = jnp.where(incl, jnp.exp(jnp.where(incl, gc - gr, 0.0)), 0.0)
            attn = gram * dt_row[h:h + 1, :] * dec
            xh = xc[:, c0:c0 + P_C]
            S = s_scr[h]
            o = jnp.exp(gc) * _bdot(cc_g, S) + _bdot(attn, xh)
            gl = gc[TC - 1:TC, :]
            s_scr[h] = S * jnp.exp(gl) + _bdot(bct[g * N_C:(g + 1) * N_C, :], xh * (dt_col[:, h:h + 1] * jnp.exp(gl - gc)))
            yb_scr[:, c0:c0 + P_C] = (o + hp[4:5, h:h + 1] * xh) * _silu(zc[:, c0:c0 + P_C])

    y = yb_scr[...]
    y = y * lax.rsqrt(_dot01_right(y * y, bo_ref[...]) * (1.0 / (D_INNER_C // G_C)) + EPS) * gn_ref[...]
    o_ref[...] = y[0:rows]
    sout_ref[...] = s_scr[...]


_D_LEVELS = 6


def _mix_d_kernel(dg_ref, s0_ref, lb_ref, gn_ref, tril_ref, sel_ref, bo_ref, o_ref, sout_ref, s_scr, ob_scr, *, t_valid, rows):
    CD = 64
    W = H_D * DK_D

    @pl.when(pl.program_id(1) == 0)
    def _():
        s_scr[...] = s0_ref[...]

    dg = _pad_rows(dg_ref[...], TC)
    q = _silu(dg[:, 0:W])
    fr = dg[:, W:2 * W]
    v = dg[:, 2 * W:3 * W]
    gd = dg[:, 3 * W:4 * W]
    lb = lb_ref[...]
    logf = jnp.log(jnp.maximum(lb + (1.0 - lb) * _sigmoid(fr), TINY))
    kd = (1.0 - lb) * _sigmoid(-fr)
    if t_valid < TC:
        ok = _iota2((TC, W), 0) < t_valid
        logf = jnp.where(ok, logf, 0.0)
        kd = jnp.where(ok, kd, 0.0)
    gcum = _dot01_left(tril_ref[...], logf)
    gm = _dot01_left(sel_ref[...], gcum)
    vt = v.T
    qdec = q * jnp.exp(gcum)

    ii, jj = _iota2((CD, CD), 0), _iota2((CD, CD), 1)
    lhs, rhs, msk = [q], [kd], [ii == jj]
    for lv in range(1, _D_LEVELS + 1):
        gml = gm[(lv - 1) * TC:lv * TC, :]
        lhs.append(q * jnp.exp(jnp.minimum(gcum - gml, 0.0)))
        rhs.append(kd * jnp.exp(jnp.minimum(gml - gcum, 0.0)))
        half = 1 << (lv - 1)
        msk.append(((ii >> lv) == (jj >> lv)) & ((ii & half) != 0) & ((jj & half) == 0))

    for sc in range(TC // CD):
        r0 = sc * CD
        for h in range(H_D):
            c0 = h * DK_D
            attn = jnp.zeros((CD, CD), F32)
            for a, b, m in zip(lhs, rhs, msk):
                attn = attn + jnp.where(m, _bdot_nt(a[r0:r0 + CD, c0:c0 + DK_D], b[r0:r0 + CD, c0:c0 + DK_D]), 0.0)
            st = s_scr[h]
            ob_scr[r0:r0 + CD, c0:c0 + DV_D] = _bdot_nt(qdec[r0:r0 + CD, c0:c0 + DK_D], st) + _bdot(attn, v[r0:r0 + CD, c0:c0 + DV_D])
            gch = gcum[r0:r0 + CD, c0:c0 + DK_D]
            gl = gch[CD - 1:CD, :]
            s_scr[h] = st * jnp.exp(gl) + _bdot(vt[c0:c0 + DV_D, r0:r0 + CD], kd[r0:r0 + CD, c0:c0 + DK_D] * jnp.exp(gl - gch))

    o = ob_scr[...]
    o = o * lax.rsqrt(_dot01_right(o * o, bo_ref[...]) * (1.0 / DV_D) + EPS) * gn_ref[...]
    res = o * _silu(gd)
    o_ref[...] = res[0:rows]
    sout_ref[...] = s_scr[...]


def _np_consts():
    i = np.arange(TC)
    same64 = (i[:, None] // 64) == (i[None, :] // 64)
    tril2 = (same64 & (i[:, None] >= i[None, :])).astype(np.float32)
    tril1 = (i[:, None] >= i[None, :]).astype(np.float32)
    sel = np.zeros((_D_LEVELS * TC, TC), np.float32)
    for lv in range(1, _D_LEVELS + 1):
        m = ((i >> lv) << lv) + (1 << (lv - 1)) - 1
        sel[(lv - 1) * TC + i, m] = 1.0
    bo = lambda n, w: ((np.arange(n)[:, None] // w) == (np.arange(n)[None, :] // w)).astype(np.float32)
    return dict(tril2=tril2, triu2=tril2.T.copy(), tril1=tril1, triu1=tril1.T.copy(), sel=sel,
                bo64=bo(256, 64), bo256=bo(D_INNER_C, D_INNER_C // G_C))


def _bucket_starts():
    max_exact = N_BUCKETS // 2
    d = np.arange(0, 4 * MAX_DISTANCE)
    lr = np.log(np.maximum(d, 1).astype(np.float32) / max_exact) / math.log(MAX_DISTANCE / max_exact)
    large = np.minimum(max_exact + (np.maximum(lr, 0.0) * (N_BUCKETS - max_exact)).astype(np.int32), N_BUCKETS - 1)
    bucket = np.where(d < max_exact, d, large)
    return [int(np.argmax(bucket == b)) for b in range(N_BUCKETS)]


_BUCKET_STARTS = _bucket_starts()
_SAT_DIST = _BUCKET_STARTS[N_BUCKETS - 1]


def _bias_from_dist(d, rb_ref, h):
    out = jnp.full(d.shape, rb_ref[N_BUCKETS - 1, h], F32)
    for b in range(N_BUCKETS - 2, -1, -1):
        out = jnp.where(d < _BUCKET_STARTS[b + 1], rb_ref[b, h], out)
    return out


def _sortable(x):
    b = lax.bitcast_convert_type(x + 0.0, I32)
    return jnp.where(b < 0, b ^ 0x7FFFFFFF, b)


_NEG_KEY = int(np.array([NEG], np.float32).view(np.int32)[0]) ^ 0x7FFFFFFF
_INT_MIN = -2 ** 31


def _selected_i32(k, idx, thr, xthr):
    s = jnp.where(k > thr, 1, jnp.where(k == thr, jnp.where(idx <= xthr, 1, 0), 0))
    return jnp.where(k == _NEG_KEY, 0, s)


def _selected(k, idx, thr, xthr):
    return _selected_i32(k, idx, thr, xthr) > 0


def _const_spec(shape):
    nd = len(shape)
    return pl.BlockSpec(shape, lambda *_: (0,) * nd)


def _mix_call(kernel_fn, name, P, B, T, seq_specs, batch_args, const_args, out_w, n_state, scratch):
    rows = min(T, TC)
    nc = max(T // TC, 1)
    assert rows * nc == T
    in_specs = [pl.BlockSpec((rows, w), functools.partial(lambda b, c, ci: (b * nc + c, ci), ci=off // w)) for off, w in seq_specs]
    args = [P] * len(seq_specs)
    for a in batch_args:
        nd = a.ndim
        in_specs.append(pl.BlockSpec((None,) + a.shape[1:], functools.partial(lambda b, c, nd: (b,) + (0,) * (nd - 1), nd=nd)))
        args.append(a)
    for a in const_args:
        in_specs.append(_const_spec(a.shape))
        args.append(a)
    st_shape = (B, n_state, 64, 64)
    return pl.pallas_call(
        functools.partial(kernel_fn, t_valid=rows, rows=rows),
        grid=(B, nc),
        in_specs=in_specs,
        out_specs=[pl.BlockSpec((rows, out_w), lambda b, c: (b * nc + c, 0)),
                   pl.BlockSpec((None, n_state, 64, 64), lambda b, c: (b, 0, 0, 0))],
        out_shape=[jax.ShapeDtypeStruct((B * T, out_w), F32), jax.ShapeDtypeStruct(st_shape, F32)],
        scratch_shapes=scratch,
        compiler_params=pltpu.CompilerParams(dimension_semantics=("arbitrary", "arbitrary"), vmem_limit_bytes=VMEM_LIMIT),
        name=name,
    )(*args)


def _dsa_prompt_kernel(rb_ref, qb_ref, qi_ref, smq_ref, kv_ref, sma_ref, o_ref,
                       kvt_scr, key_scr, lg_scr, bias_scr, acc_scr, *, n_sel, idx_bits):
    j = pl.program_id(1)
    nch = j + 1

    @pl.when((pl.program_id(0) == 0) & (j == 0))
    def _():
        srow, tcol = _iota2((QB, QB), 0), _iota2((QB, QB), 1)
        for h in range(H_B):
            bias_scr[h, 0] = jnp.full((QB, QB), rb_ref[N_BUCKETS - 1, h], F32)
            bias_scr[h, 1] = _bias_from_dist(tcol - srow + QB, rb_ref, h)
            bias_scr[h, 2] = _bias_from_dist(tcol - srow, rb_ref, h)

    @pl.when(j == 0)
    def _():
        def tr(c, _):
            r = pl.multiple_of(c * QB, QB)
            kvt_scr[c] = kv_ref[pl.ds(r, QB), :].T
            return 0
        lax.fori_loop(0, kv_ref.shape[0] // QB, tr, 0)

    qi = qi_ref[...]
    wt = smq_ref[...].T[SM_WI:SM_WI + H_IDX, :] * (H_IDX ** -0.5)
    srow, tcol = _iota2((QB, QB), 0), _iota2((QB, QB), 1)

    def score_body(c, _):
        r = pl.multiple_of(c * QB, QB)
        ki = sma_ref[pl.ds(r, QB), :][:, SM_KI:SM_KI + D_IDX]
        sc = jnp.zeros((QB, QB), F32)
        for h in range(H_IDX):
            s = _bdot_nt(ki, qi[:, h * D_IDX:(h + 1) * D_IDX])
            sc = sc + jnp.maximum(s * (D_IDX ** -0.5), 0.0) * wt[h:h + 1, :]
        sc = jnp.where(srow + r <= tcol + j * QB, sc, NEG)
        key_scr[pl.ds(r, QB), :] = _sortable(sc)
        return 0

    lax.fori_loop(0, nch, score_body, 0)

    def count(pred):
        def body(c, acc):
            r = pl.multiple_of(c * QB, QB)
            return acc + pred(key_scr[pl.ds(r, QB), :], r).reshape(QB // SUB, SUB, QB).sum(axis=0)
        return lax.fori_loop(0, nch, body, jnp.zeros((SUB, QB), I32)).sum(axis=0, keepdims=True)

    cnt_ge = lambda t: count(lambda k, r: jnp.where(k >= t, 1, 0))
    thr = jnp.where(cnt_ge(jnp.zeros((1, QB), I32)) >= n_sel, 0, _INT_MIN).astype(I32)

    def bit_body(i, t):
        trial = t | jnp.left_shift(jnp.int32(1), 30 - i)
        return jnp.where(cnt_ge(trial) >= n_sel, trial, t)

    thr = lax.fori_loop(0, 31, bit_body, thr)
    quota = n_sel - count(lambda k, r: jnp.where(k > thr, 1, 0))

    def idx_body(i, x):
        trial = x | jnp.left_shift(jnp.int32(1), idx_bits - 1 - i)
        below = count(lambda k, r: jnp.where(k == thr, jnp.where(srow + r < trial, 1, 0), 0))
        return jnp.where(below < quota, trial, x)

    xthr = lax.fori_loop(0, idx_bits, idx_body, jnp.zeros((1, QB), I32))

    qb = qb_ref[...]

    def logit_body(c, ms):
        r = pl.multiple_of(c * QB, QB)
        kc = kv_ref[pl.ds(r, QB), :][:, 0:DH_B]
        k = key_scr[pl.ds(r, QB), :]
        sel = _selected(k, srow + r, thr, xthr)
        bidx = jnp.clip(c - j + 2, 0, 2)
        out = []
        for h in range(H_B):
            l = _bdot_nt(kc, qb[:, h * DH_B:(h + 1) * DH_B]) * (DH_B ** -0.5) + bias_scr[h, bidx]
            l = jnp.where(sel, l, NEG)
            lg_scr[h, pl.ds(r, QB), :] = l
            out.append(jnp.maximum(ms[h], l.max(axis=0, keepdims=True)))
        return tuple(out)

    ms = lax.fori_loop(0, nch, logit_body, tuple(jnp.full((1, QB), NEG, F32) for _ in range(H_B)))
    acc_scr[...] = jnp.zeros(acc_scr.shape, F32)

    def pv_body(c, ss):
        r = pl.multiple_of(c * QB, QB)
        vt = kvt_scr[c][DH_B:2 * DH_B, :]
        out = []
        for h in range(H_B):
            e = jnp.exp(lg_scr[h, pl.ds(r, QB), :] - ms[h])
            acc_scr[h] = acc_scr[h] + _bdot(vt, e)
            out.append(ss[h] + e.sum(axis=0, keepdims=True))
        return tuple(out)

    ss = lax.fori_loop(0, nch, pv_body, tuple(jnp.zeros((1, QB), F32) for _ in range(H_B)))
    ot = jnp.concatenate([acc_scr[h] / ss[h] for h in range(H_B)], axis=0)
    o_ref[...] = ot.T


def _dsa_prompt(P, rel_bias, B, T):
    nb = T // QB
    n_sel = min(TOPK_MAX, T // 4)
    idx_bits = max(1, int(math.ceil(math.log2(T))))
    return pl.pallas_call(
        functools.partial(_dsa_prompt_kernel, n_sel=n_sel, idx_bits=idx_bits),
        grid=(B, nb),
        in_specs=[pl.BlockSpec(memory_space=pltpu.SMEM),
                  pl.BlockSpec((QB, W_B), lambda b, j: (b * nb + j, OFF_QB // W_B)),
                  pl.BlockSpec((QB, LANE), lambda b, j: (b * nb + j, OFF_QI // LANE)),
                  pl.BlockSpec((QB, LANE), lambda b, j: (b * nb + j, OFF_SM // LANE)),
                  pl.BlockSpec((T, LANE), lambda b, j: (b, OFF_KV // LANE)),
                  pl.BlockSpec((T, LANE), lambda b, j: (b, OFF_SM // LANE))],
        out_specs=pl.BlockSpec((QB, W_B), lambda b, j: (b * nb + j, 0)),
        out_shape=jax.ShapeDtypeStruct((B * T, W_B), F32),
        scratch_shapes=[pltpu.VMEM((nb, LANE, QB), F32), pltpu.VMEM((T, QB), I32), pltpu.VMEM((H_B, T, QB), F32),
                        pltpu.VMEM((H_B, 3, QB, QB), F32), pltpu.VMEM((H_B, DH_B, QB), F32)],
        compiler_params=pltpu.CompilerParams(dimension_semantics=("arbitrary", "arbitrary"), vmem_limit_bytes=VMEM_LIMIT),
        name="dsa_prompt",
    )(rel_bias, P, P, P, P, P)


PG = 8
GW = PG * PAGE_SIZE


def _stack_heads(x, w):
    return jnp.concatenate([x[:, h * w:(h + 1) * w] for h in range(x.shape[1] // w)], axis=0)


def _dsa_s_scores_kernel(pt_ref, qi_ref, sm_ref, *refs, ng, ds):
    page_refs, o_ref = refs[:PG], refs[PG]
    g = pl.program_id(1)
    sm = sm_ref[...]
    qst = _stack_heads(qi_ref[...], D_IDX)
    wcol = jnp.concatenate([sm[:, SM_WI + h:SM_WI + h + 1] for h in range(H_IDX)], axis=0) * (H_IDX ** -0.5)

    def tile_scores(kidx):
        s = jnp.maximum(_bdot_nt(qst, kidx) * (D_IDX ** -0.5), 0.0) * wcol
        out = s[0:ds]
        for h in range(1, H_IDX):
            out = out + s[h * ds:(h + 1) * ds]
        return out

    @pl.when(g < ng)
    def _():
        for i in range(PG):
            o_ref[:, i * PAGE_SIZE:(i + 1) * PAGE_SIZE] = tile_scores(page_refs[i][...])

    @pl.when(g == ng)
    def _():
        sc = tile_scores(_pad_rows(sm[:, SM_KI:SM_KI + D_IDX], PAGE_SIZE))
        lane, row = _iota2(sc.shape, 1), _iota2(sc.shape, 0)
        o_ref[:, 0:PAGE_SIZE] = jnp.where(lane <= row, sc, NEG)
        o_ref[:, PAGE_SIZE:GW] = jnp.full((ds, GW - PAGE_SIZE), NEG, F32)


def _dsa_s_attn_kernel(pt_ref, rb_ref, sc_ref, qb_ref, kvn_ref, *refs, ng, ds, n_sel, idx_bits):
    k_refs, v_refs = refs[:PG], refs[PG:2 * PG]
    o_ref, key_scr, thr_scr, bias_scr, m_scr, l_scr, acc_scr = refs[2 * PG:]
    g = pl.program_id(1)
    ntile = (ng + 1) * PG
    hr = H_B * ds

    @pl.when(g == 0)
    def _():
        for i in range(ntile):
            key_scr[:, i * LANE:(i + 1) * LANE] = _sortable(sc_ref[:, i * LANE:(i + 1) * LANE])
        lane = _iota2((ds, LANE), 1)

        def count(pred):
            acc = jnp.zeros((ds, LANE), I32)
            for i in range(ntile):
                acc = acc + pred(key_scr[:, i * LANE:(i + 1) * LANE], i * LANE)
            return acc.sum(axis=1, keepdims=True)

        cnt_ge = lambda t: count(lambda k, off: jnp.where(k >= t, 1, 0))
        thr = jnp.where(cnt_ge(jnp.zeros((ds, 1), I32)) >= n_sel, 0, _INT_MIN).astype(I32)

        def bit_body(i, t):
            trial = t | jnp.left_shift(jnp.int32(1), 30 - i)
            return jnp.where(cnt_ge(trial) >= n_sel, trial, t)

        thr = lax.fori_loop(0, 31, bit_body, thr)
        quota = n_sel - count(lambda k, off: jnp.where(k > thr, 1, 0))

        def idx_body(i, x):
            trial = x | jnp.left_shift(jnp.int32(1), idx_bits - 1 - i)
            below = count(lambda k, off: jnp.where(k == thr, jnp.where(lane + off < trial, 1, 0), 0))
            return jnp.where(below < quota, trial, x)

        xthr = lax.fori_loop(0, idx_bits, idx_body, jnp.zeros((ds, 1), I32))
        thr_scr[0] = jnp.broadcast_to(thr, (ds, LANE))
        thr_scr[1] = jnp.broadcast_to(xthr, (ds, LANE))

        qrow = _iota2((ds, LANE), 0)
        for h in range(H_B):
            bias_scr[0, h * ds:(h + 1) * ds, :] = _bias_from_dist(PAGE_SIZE + qrow - lane, rb_ref, h)
            bias_scr[1, h * ds:(h + 1) * ds, :] = _bias_from_dist(qrow - lane, rb_ref, h)
            bias_scr[2, h * ds:(h + 1) * ds, :] = jnp.full((ds, LANE), rb_ref[N_BUCKETS - 1, h], F32)
        m_scr[...] = jnp.full(m_scr.shape, NEG, F32)
        l_scr[...] = jnp.zeros(l_scr.shape, F32)
        acc_scr[...] = jnp.zeros(acc_scr.shape, F32)

    qst = _stack_heads(qb_ref[...], DH_B)
    thr, xthr = thr_scr[0], thr_scr[1]

    def select(tile):
        k = key_scr[:, pl.ds(pl.multiple_of(tile * LANE, LANE), LANE)]
        idx = _iota2((ds, LANE), 1) + tile * LANE
        sel = _selected_i32(k, idx, thr, xthr)
        return jnp.concatenate([sel] * H_B, axis=0) > 0

    def update(lg, sel, v):
        m_old = m_scr[...]
        m_new = jnp.maximum(m_old, jnp.where(sel, lg, NEG).max(axis=1, keepdims=True))
        p = jnp.where(sel, jnp.exp(lg - m_new), 0.0)
        corr = jnp.exp(m_old - m_new)
        l_scr[...] = l_scr[...] * corr + p.sum(axis=1, keepdims=True)
        acc_scr[...] = acc_scr[...] * corr + _bdot(p, v)
        m_scr[...] = m_new

    @pl.when(g < ng)
    def _():
        lgs, sels = [], []
        for i in range(PG):
            lg = _bdot_nt(qst, k_refs[i][...]) * (DH_B ** -0.5)
            if i == PG - 1:
                lg = lg + jnp.where(g == ng - 1, bias_scr[0], bias_scr[2])
            else:
                lg = lg + bias_scr[2]
            lgs.append(lg)
            sels.append(select(g * PG + i))
        v = jnp.concatenate([v_refs[i][...] for i in range(PG)], axis=0)
        update(jnp.concatenate(lgs, axis=1), jnp.concatenate(sels, axis=1), v)

    @pl.when(g == ng)
    def _():
        kvn = _pad_rows(kvn_ref[...], PAGE_SIZE)
        lg = _bdot_nt(qst, kvn[:, 0:DH_B]) * (DH_B ** -0.5) + bias_scr[1]
        update(lg, select(ng * PG), kvn[:, DH_B:2 * DH_B])
        o = acc_scr[...] / l_scr[...]
        o_ref[...] = jnp.concatenate([o[h * ds:(h + 1) * ds] for h in range(H_B)], axis=1)


def _dsa_sample(P, rel_bias, cache_k, cache_v, cache_kidx, page_table, layer, B, DS):
    n_pages = page_table.shape[1]
    past = n_pages * PAGE_SIZE
    assert n_pages % PG == 0
    ng = n_pages // PG
    n_sel = min(TOPK_MAX, (past + DS) // 4)
    idx_bits = int(math.ceil(math.log2(past + LANE)))
    s_pad = (ng + 1) * GW

    def page_spec(width, i):
        return pl.BlockSpec((None, None, PAGE_SIZE, width),
                            lambda b, g, pt: (layer, pt[b, jnp.minimum(g, ng - 1) * PG + i], 0, 0))

    row = lambda ci: (lambda b, g, pt: (b, ci))
    cp = pltpu.CompilerParams(dimension_semantics=("arbitrary", "arbitrary"), vmem_limit_bytes=VMEM_LIMIT)
    scores = pl.pallas_call(
        functools.partial(_dsa_s_scores_kernel, ng=ng, ds=DS),
        grid_spec=pltpu.PrefetchScalarGridSpec(
            num_scalar_prefetch=1, grid=(B, ng + 1),
            in_specs=[pl.BlockSpec((DS, LANE), row(OFF_QI // LANE)), pl.BlockSpec((DS, LANE), row(OFF_SM // LANE))]
                     + [page_spec(D_IDX, i) for i in range(PG)],
            out_specs=pl.BlockSpec((None, DS, GW), lambda b, g, pt: (b, 0, g))),
        out_shape=jax.ShapeDtypeStruct((B, DS, s_pad), F32),
        compiler_params=cp, name="dsa_sample_scores",
    )(page_table, P, P, *([cache_kidx] * PG))

    return pl.pallas_call(
        functools.partial(_dsa_s_attn_kernel, ng=ng, ds=DS, n_sel=n_sel, idx_bits=idx_bits),
        grid_spec=pltpu.PrefetchScalarGridSpec(
            num_scalar_prefetch=1, grid=(B, ng + 1),
            in_specs=[pl.BlockSpec(memory_space=pltpu.SMEM),
                      pl.BlockSpec((None, DS, s_pad), lambda b, g, pt: (b, 0, 0)),
                      pl.BlockSpec((DS, W_B), row(OFF_QB // W_B)), pl.BlockSpec((DS, LANE), row(OFF_KV // LANE))]
                     + [page_spec(DH_B, i) for i in range(PG)] + [page_spec(DH_B, i) for i in range(PG)],
            out_specs=pl.BlockSpec((DS, W_B), lambda b, g, pt: (b, 0)),
            scratch_shapes=[pltpu.VMEM((DS, s_pad), I32), pltpu.VMEM((2, DS, LANE), I32), pltpu.VMEM((3, H_B * DS, LANE), F32),
                            pltpu.VMEM((H_B * DS, 1), F32), pltpu.VMEM((H_B * DS, 1), F32), pltpu.VMEM((H_B * DS, DH_B), F32)]),
        out_shape=jax.ShapeDtypeStruct((B * DS, W_B), F32),
        compiler_params=cp, name="dsa_sample_attn",
    )(page_table, rel_bias, scores, P, P, *([cache_k] * PG), *([cache_v] * PG))


def _merge_kernel(x_ref, oa_ref, ob_ref, oc_ref, od_ref, gate_ref, wb_ref, wo_ref, o_ref):
    m = None
    r0 = 0
    for br, ref in enumerate((oa_ref, ob_ref, oc_ref, od_ref)):
        w = ref.shape[1]
        t = _sigmoid(gate_ref[:, br * D_MODEL:(br + 1) * D_MODEL]) * jnp.dot(ref[...].astype(BF16), wb_ref[r0:r0 + w, :], preferred_element_type=F32)
        m = t if m is None else m + t
        r0 += w
    o_ref[...] = x_ref[...] + jnp.dot(m.astype(BF16), wo_ref[...], preferred_element_type=F32)


def _merge(x2d, oa, ob, oc, od, P, wb, wo):
    n = x2d.shape[0]
    tm = min(n, 256)
    row = lambda i: (i, 0)
    return pl.pallas_call(
        _merge_kernel,
        grid=(n // tm,),
        in_specs=[pl.BlockSpec((tm, D_MODEL), row), pl.BlockSpec((tm, W_A), row), pl.BlockSpec((tm, W_B), row),
                  pl.BlockSpec((tm, W_C), row), pl.BlockSpec((tm, W_D), row),
                  pl.BlockSpec((tm, N_BRANCH * D_MODEL), lambda i: (i, OFF_GATE // (N_BRANCH * D_MODEL))),
                  _const_spec((MIX_W, D_MODEL)), _const_spec((D_MODEL, D_MODEL))],
        out_specs=pl.BlockSpec((tm, D_MODEL), row),
        out_shape=jax.ShapeDtypeStruct((n, D_MODEL), F32),
        compiler_params=pltpu.CompilerParams(dimension_semantics=("arbitrary",), vmem_limit_bytes=VMEM_LIMIT),
        name="merge",
    )(x2d, oa, ob, oc, od, P, wb, wo)


def _mlp_kernel(x_ref, g_ref, wu_ref, wd_ref, gf_ref, *refs, final):
    if final:
        o_ref, y_ref, h_scr, acc_scr = refs
    else:
        o_ref, h_scr, acc_scr = refs
    f = pl.program_id(1)

    @pl.when(f == 0)
    def _():
        x = x_ref[...]
        r = lax.rsqrt(jnp.mean(x * x, axis=-1, keepdims=True) + EPS)
        h_scr[...] = ((x * r) * g_ref[...]).astype(BF16)
        acc_scr[...] = jnp.zeros(acc_scr.shape, F32)

    a = jnp.maximum(jnp.dot(h_scr[...], wu_ref[...], preferred_element_type=F32), 0.0)
    acc_scr[...] += jnp.dot((a * a).astype(BF16), wd_ref[...], preferred_element_type=F32)

    @pl.when(f == pl.num_programs(1) - 1)
    def _():
        out = x_ref[...] + acc_scr[...]
        o_ref[...] = out
        if final:
            r = lax.rsqrt(jnp.mean(out * out, axis=-1, keepdims=True) + EPS)
            y_ref[...] = (out * r) * gf_ref[...]


def _mlp(x2d, g, wu, wd, gf, final):
    n = x2d.shape[0]
    tm = min(n, 512)
    tf = 1024
    row = lambda i, f: (i, 0)
    out_spec = pl.BlockSpec((tm, D_MODEL), row)
    shp = jax.ShapeDtypeStruct((n, D_MODEL), F32)
    return pl.pallas_call(
        functools.partial(_mlp_kernel, final=final),
        grid=(n // tm, D_FF // tf),
        in_specs=[pl.BlockSpec((tm, D_MODEL), row), _const_spec((1, D_MODEL)),
                  pl.BlockSpec((D_MODEL, tf), lambda i, f: (0, f)), pl.BlockSpec((tf, D_MODEL), lambda i, f: (f, 0)),
                  _const_spec((1, D_MODEL))],
        out_specs=[out_spec, out_spec] if final else [out_spec],
        out_shape=[shp, shp] if final else [shp],
        scratch_shapes=[pltpu.VMEM((tm, D_MODEL), BF16), pltpu.VMEM((tm, D_MODEL), F32)],
        compiler_params=pltpu.CompilerParams(dimension_semantics=("arbitrary", "arbitrary"), vmem_limit_bytes=VMEM_LIMIT),
        name="mlp_final" if final else "mlp",
    )(x2d, g.reshape(1, D_MODEL), wu, wd, gf.reshape(1, D_MODEL))


def _layer(x, conv_state, s_delta, s_ssm, s_hgrn, lw, consts, sample_ctx, final):
    B, T, _ = x.shape
    x2 = x.reshape(B * T, D_MODEL)
    P = _in_proj(x2, lw['norm_mix'], lw['w_in'])

    def conv0(lo, hi):
        if conv_state is None:
            return jnp.zeros((B, SUB, hi - lo), F32)
        return jnp.concatenate([jnp.zeros((B, SUB - (CONV_W - 1), hi - lo), F32), conv_state[:, :, lo:hi]], axis=1)

    zeros_state = lambda n: jnp.zeros((B, n, 64, 64), F32)
    c = consts
    half = CONV_CH // 2
    assert CONV_A == half and OFF_PC % half == 0
    oa, s_a = _mix_call(
        _mix_a_kernel, "mix_a", P, B, T,
        [(OFF_PC, half), (OFF_ZA, W_A), (OFF_SM, LANE)],
        [conv0(0, CONV_A), zeros_state(H_A) if s_delta is None else s_delta],
        [lw['conv_w'][:, 0:CONV_A], lw['conv_b'][None, 0:CONV_A], lw['hp'], lw['hpt'], lw['gn_a'], c['tril2'], c['triu2'], c['bo64']],
        W_A, H_A,
        [pltpu.VMEM((TC + SUB, CONV_A), F32), pltpu.VMEM((H_A, DK_A, DV_A), F32), pltpu.VMEM((TC, W_A), F32)])
    oc, s_c = _mix_call(
        _mix_c_kernel, "mix_c", P, B, T,
        [(OFF_PC + half, half), (OFF_ZC, W_C), (OFF_SM, LANE)],
        [conv0(CONV_A, CONV_CH), zeros_state(H_C) if s_ssm is None else s_ssm],
        [lw['conv_w'][:, CONV_A:], lw['conv_b'][None, CONV_A:], lw['hp'], lw['hpt'], lw['gn_c'], c['tril1'], c['triu1'], c['bo256']],
        W_C, H_C,
        [pltpu.VMEM((TC + SUB, CONV_C), F32), pltpu.VMEM((H_C, N_C, P_C), F32), pltpu.VMEM((TC, W_C), F32)])
    od, s_dt = _mix_call(
        _mix_d_kernel, "mix_d", P, B, T,
        [(OFF_D, 4 * W_D)],
        [zeros_state(H_D) if s_hgrn is None else jnp.swapaxes(s_hgrn, -1, -2)],
        [lw['lb'], lw['gn_d'], c['tril2'], c['sel'], c['bo64']],
        W_D, H_D,
        [pltpu.VMEM((H_D, DV_D, DK_D), F32), pltpu.VMEM((TC, W_D), F32)])
    if sample_ctx is None:
        ob = _dsa_prompt(P, lw['rel_bias'], B, T)
    else:
        ob = _dsa_sample(P, lw['rel_bias'], sample_ctx['cache_k'], sample_ctx['cache_v'], sample_ctx['cache_kidx'],
                         sample_ctx['page_table'], sample_ctx['layer'], B, T)
    x1 = _merge(x2, oa, ob, oc, od, P, lw['w_branch'], lw['w_out'])
    outs = _mlp(x1, lw['norm_mlp'], lw['w_up'], lw['w_down'], lw['norm_final'], final)
    xo = outs[0].reshape(B, T, D_MODEL)
    y = outs[1].reshape(B, T, D_MODEL) if final else None
    P3 = P.reshape(B, T, P_PAD)
    states = (P3[:, :, OFF_KV:OFF_KV + DH_B], P3[:, :, OFF_KV + DH_B:OFF_KV + 2 * DH_B], P3[:, :, OFF_SM + SM_KI:OFF_SM + SM_KI + D_IDX],
              P3[:, T - (CONV_W - 1):, OFF_PC:OFF_PC + CONV_CH], s_a, s_c, jnp.swapaxes(s_dt, -1, -2))
    return xo, y, states


def kernel(x_prompt, x_sample, cache_k, cache_v, cache_kidx, state_conv, state_delta, state_ssm, state_hgrn, page_table, norm_mix, w_in, conv_w, conv_b, a_log_a, dt_bias_a, gnorm_a, rel_bias, a_log_c, dt_bias_c, d_skip_c, gnorm_c, hgrn_gamma, gnorm_d, w_branch, w_out, norm_mlp, w_up, w_down, norm_final):
    depth = w_in.shape[0]
    npc = _np_consts()
    consts = {k: jnp.asarray(v, BF16) for k, v in npc.items()}
    pg = jax.nn.softmax(hgrn_gamma.astype(F32), axis=0)
    lower_bounds = jnp.cumsum(pg, axis=0) - pg[0]
    yp, ys = x_prompt, x_sample
    new_p, new_s = [], []
    for l in range(depth):
        hp = jnp.zeros((SUB, LANE), F32)
        hp = hp.at[0, 0:H_A].set(a_log_a[l]).at[1, 0:H_A].set(dt_bias_a[l])
        hp = hp.at[2, 0:H_C].set(a_log_c[l]).at[3, 0:H_C].set(dt_bias_c[l]).at[4, 0:H_C].set(d_skip_c[l])
        lw = dict(norm_mix=norm_mix[l], w_in=_permute_w_in(w_in[l]), conv_w=conv_w[l], conv_b=conv_b[l],
                  hp=hp, hpt=hp.T, gn_a=jnp.tile(gnorm_a[l], H_A)[None, :], gn_c=gnorm_c[l][None, :],
                  gn_d=jnp.tile(gnorm_d[l], H_D)[None, :], lb=lower_bounds[l][None, :], rel_bias=rel_bias,
                  w_branch=w_branch[l].astype(BF16), w_out=w_out[l].astype(BF16), norm_mlp=norm_mlp[l],
                  w_up=w_up[l].astype(BF16), w_down=w_down[l].astype(BF16), norm_final=norm_final)
        final = l == depth - 1
        yp, yp_n, st_p = _layer(yp, None, None, None, None, lw, consts, None, final)
        ctx = dict(cache_k=cache_k, cache_v=cache_v, cache_kidx=cache_kidx, page_table=page_table, layer=l)
        ys, ys_n, st_s = _layer(ys, state_conv[l], state_delta[l], state_ssm[l], state_hgrn[l], lw, consts, ctx, final)
        new_p.append(st_p)
        new_s.append(st_s)
    stack = lambda per_layer: [jnp.stack(items) for items in zip(*per_layer)]
    return (yp_n, ys_n, *stack(new_p), *stack(new_s))
```

```python
import functools
import math

import numpy as np
import jax
import jax.numpy as jnp
from jax import lax
from jax.experimental import pallas as pl
from jax.experimental.pallas import tpu as pltpu

F32 = jnp.float32
BF16 = jnp.bfloat16
I32 = jnp.int32

D_MODEL = 1024
PAGE_SIZE = 128
EPS = 1e-6
NEG = -1e30
TINY = 1e-30
CONV_W = 4
D_FF = 4 * D_MODEL
N_BRANCH = 4
H_A, DK_A, DV_A = 4, 64, 64
H_B, DH_B = 4, 64
H_IDX, D_IDX = 4, 32
TOPK_MAX = 256
N_BUCKETS = 32
MAX_DISTANCE = 128
H_C, P_C, N_C, G_C = 8, 64, 64, 2
H_D, DK_D, DV_D = 4, 64, 64
QK_A = H_A * DK_A
CONV_A = 2 * QK_A + H_A * DV_A
D_INNER_C = H_C * P_C
CONV_C = D_INNER_C + 2 * G_C * N_C
CONV_CH = CONV_A + CONV_C
W_A, W_B, W_C, W_D = H_A * DV_A, H_B * DH_B, D_INNER_C, H_D * DV_D
MIX_W = W_A + W_B + W_C + W_D
IN_WIDTHS = (CONV_CH, W_A, H_A, H_A, W_B, DH_B, DH_B, H_IDX * D_IDX, D_IDX, H_IDX, D_INNER_C, H_C,
             H_D * DK_D, H_D * DK_D, W_D, W_D, N_BRANCH * D_MODEL)
P_IN = sum(IN_WIDTHS)

LANE = 128
SUB = 8
TC = 128
QB = 128
VMEM_LIMIT = 48 * 1024 * 1024

OFF_GATE, OFF_D, OFF_ZC, OFF_ZA, OFF_QB, OFF_PC, OFF_KV, OFF_QI, OFF_SM = 0, 4096, 5120, 5632, 5888, 6144, 7680, 7808, 7936
P_PAD = 8064
SM_KI, SM_WI, SM_AA, SM_BA, SM_DT = 0, 32, 36, 40, 44


def _src_offsets():
    offs, o = [], 0
    for w in IN_WIDTHS:
        offs.append(o)
        o += w
    return offs


def _permute_w_in(w):
    (o_pc, o_za, o_aa, o_ba, o_qb, o_kb, o_vb, o_qi, o_ki, o_wi, o_zc, o_dt, o_qd, o_fd, o_id, o_gd, o_gate) = _src_offsets()
    seg = lambda o, n: w[:, o:o + n]
    small = jnp.concatenate([seg(o_ki, D_IDX), seg(o_wi, H_IDX), seg(o_aa, H_A), seg(o_ba, H_A), seg(o_dt, H_C),
                             jnp.zeros((w.shape[0], LANE - (D_IDX + H_IDX + 2 * H_A + H_C)), w.dtype)], axis=1)
    out = jnp.concatenate([
        seg(o_gate, N_BRANCH * D_MODEL),
        seg(o_qd, 4 * W_D),
        seg(o_zc, D_INNER_C), seg(o_za, W_A), seg(o_qb, W_B), seg(o_pc, CONV_CH),
        seg(o_kb, 2 * DH_B),
        seg(o_qi, H_IDX * D_IDX), small], axis=1)
    assert out.shape[1] == P_PAD
    return out.astype(BF16)


def _bdot(a, b):
    return jnp.dot(a.astype(BF16), b.astype(BF16), preferred_element_type=F32)


def _bdot_nt(a, b):
    return lax.dot_general(a.astype(BF16), b.astype(BF16), (((1,), (1,)), ((), ())), preferred_element_type=F32)


def _split2(a):
    hi = a.astype(BF16)
    lo = (a - hi.astype(F32)).astype(BF16)
    return hi, lo


def _split3(a):
    hi = a.astype(BF16)
    r = a - hi.astype(F32)
    mid = r.astype(BF16)
    lo = (r - mid.astype(F32)).astype(BF16)
    return hi, mid, lo


def _dot01_left(m01, x):
    hi, mid, lo = _split3(x)
    d = lambda p: jnp.dot(m01, p, preferred_element_type=F32)
    return d(hi) + (d(mid) + d(lo))


def _dot01_right(x, m01):
    hi, mid, lo = _split3(x)
    d = lambda p: jnp.dot(p, m01, preferred_element_type=F32)
    return d(hi) + (d(mid) + d(lo))


def _xdot(a, b):
    ah, al = _split2(a)
    bh, bl = _split2(b)
    d = lambda p, q: jnp.dot(p, q, preferred_element_type=F32)
    return d(ah, bh) + (d(ah, bl) + d(al, bh))


def _sigmoid(x):
    return 1.0 / (1.0 + jnp.exp(-x))


def _silu(x):
    return x * _sigmoid(x)


def _softplus(x):
    return jnp.maximum(x, 0.0) + jnp.log(1.0 + jnp.exp(-jnp.abs(x)))


def _iota2(shape, dim):
    return lax.broadcasted_iota(I32, shape, dim)


def _pad_rows(x, rows):
    if x.shape[0] == rows:
        return x
    return jnp.concatenate([x, jnp.zeros((rows - x.shape[0],) + x.shape[1:], x.dtype)], axis=0)


def _tri_inv(L, top):
    n = L.shape[0]
    ii, jj = _iota2((n, n), 0), _iota2((n, n), 1)
    xor = ii ^ jj
    eye = jnp.where(ii == jj, 1.0, 0.0)
    N = jnp.where((xor >> 3) == 0, -L, 0.0)
    X = eye + N
    N2 = _bdot(N, N)
    X = X + _bdot(X, N2)
    N4 = _bdot(N2, N2)
    X = X + _bdot(X, N4)
    sh = 4
    while (1 << sh) <= top:
        Bm = jnp.where((xor >> (sh - 1)) == 1, L, 0.0)
        X = X - _bdot(_bdot(X, Bm), X)
        sh += 1
    return X + _bdot(X, eye - X - _xdot(L, X))


def _conv_silu(pc, ext_scr, cw_ref, cb_ref):
    ext_scr[SUB:SUB + TC, :] = pc
    y = cb_ref[...]
    for j in range(CONV_W):
        y = y + ext_scr[SUB - (CONV_W - 1) + j:SUB - (CONV_W - 1) + j + TC, :] * cw_ref[j:j + 1, :]
    ext_scr[0:SUB, :] = ext_scr[TC:TC + SUB, :]
    return _silu(y)


def _in_proj_kernel(x_ref, g_ref, w_ref, o_ref, h_scr):
    @pl.when(pl.program_id(1) == 0)
    def _():
        x = x_ref[...]
        r = lax.rsqrt(jnp.mean(x * x, axis=-1, keepdims=True) + EPS)
        h_scr[...] = ((x * r) * g_ref[...]).astype(BF16)

    o_ref[...] = jnp.dot(h_scr[...], w_ref[...], preferred_element_type=F32)


def _in_proj(x2d, g, w_perm):
    n = x2d.shape[0]
    tm = min(n, 512)
    tn = 1152
    return pl.pallas_call(
        _in_proj_kernel,
        grid=(n // tm, P_PAD // tn),
        in_specs=[pl.BlockSpec((tm, D_MODEL), lambda i, j: (i, 0)),
                  pl.BlockSpec((1, D_MODEL), lambda i, j: (0, 0)),
                  pl.BlockSpec((D_MODEL, tn), lambda i, j: (0, j))],
        out_specs=pl.BlockSpec((tm, tn), lambda i, j: (i, j)),
        out_shape=jax.ShapeDtypeStruct((n, P_PAD), F32),
        scratch_shapes=[pltpu.VMEM((tm, D_MODEL), BF16)],
        compiler_params=pltpu.CompilerParams(dimension_semantics=("arbitrary", "arbitrary"), vmem_limit_bytes=VMEM_LIMIT),
        name="in_proj",
    )(x2d, g.reshape(1, D_MODEL), w_perm)


def _mix_a_kernel(pc_ref, za_ref, sm_ref, conv0_ref, s0_ref, cw_ref, cb_ref, hp_ref, hpt_ref, gn_ref,
                  tril_ref, triu_ref, bo_ref, o_ref, sout_ref, ext_scr, s_scr, ob_scr, *, t_valid, rows):
    CA = 64

    @pl.when(pl.program_id(1) == 0)
    def _():
        ext_scr[0:SUB, :] = conv0_ref[...]
        s_scr[...] = s0_ref[...]

    u = _conv_silu(_pad_rows(pc_ref[...], TC), ext_scr, cw_ref, cb_ref)
    bo = bo_ref[...]
    q_raw, k_raw, va = u[:, 0:QK_A], u[:, QK_A:2 * QK_A], u[:, 2 * QK_A:CONV_A]
    qq = q_raw * lax.rsqrt(_dot01_right(q_raw * q_raw, bo) + EPS) * (DK_A ** -0.5)
    ka = k_raw * lax.rsqrt(_dot01_right(k_raw * k_raw, bo) + EPS)
    kat = ka.T

    sm = _pad_rows(sm_ref[...], TC)
    smt = sm.T
    hp, hpt = hp_ref[...], hpt_ref[...]
    g_col = -jnp.exp(hp[0:1, 0:H_A]) * _softplus(sm[:, SM_AA:SM_AA + H_A] + hp[1:2, 0:H_A])
    beta = _sigmoid(sm[:, SM_BA:SM_BA + H_A])
    g_row = -jnp.exp(hpt[0:H_A, 0:1]) * _softplus(smt[SM_AA:SM_AA + H_A, :] + hpt[0:H_A, 1:2])
    if t_valid < TC:
        g_col = jnp.where(_iota2(g_col.shape, 0) < t_valid, g_col, 0.0)
        beta = jnp.where(_iota2(beta.shape, 0) < t_valid, beta, 0.0)
        g_row = jnp.where(_iota2(g_row.shape, 1) < t_valid, g_row, 0.0)
    gcum_col = _dot01_left(tril_ref[...], g_col)
    gcum_row = _dot01_right(g_row, triu_ref[...])

    HS = H_A * CA
    ii, jj = _iota2((HS, HS), 0), _iota2((HS, HS), 1)
    same = (ii >> 6) == (jj >> 6)
    incl, strict = same & (ii >= jj), same & (ii > jj)
    for sc in range(-(-t_valid // CA)):
        r0 = sc * CA
        stack = lambda x, w: jnp.concatenate([x[r0:r0 + CA, h * w:(h + 1) * w] for h in range(H_A)], axis=0)
        ks, qs, vs = stack(ka, DK_A), stack(qq, DK_A), stack(va, DV_A)
        bcol, gc = stack(beta, 1), stack(gcum_col, 1)
        gr = jnp.concatenate([gcum_row[h:h + 1, r0:r0 + CA] for h in range(H_A)], axis=1)
        e = jnp.exp(jnp.where(incl, gc - gr, 0.0))
        kb = ks * bcol
        tinv = _tri_inv(jnp.where(strict, _bdot_nt(kb, ks) * e, 0.0), CA)
        eg = jnp.exp(gc)
        sol = _xdot(tinv, jnp.concatenate([vs * bcol, kb * eg], axis=1))
        attn = jnp.where(incl, _bdot_nt(qs, ks) * e, 0.0)
        qe = qs * eg
        states = [s_scr[h] for h in range(H_A)]
        hs = lambda x, h: x[h * CA:(h + 1) * CA]
        v_new = jnp.concatenate([hs(sol, h)[:, 0:DV_A] - _bdot(hs(sol, h)[:, DV_A:], states[h]) for h in range(H_A)], axis=0)
        o = jnp.concatenate([_bdot(hs(qe, h), states[h]) for h in range(H_A)], axis=0) + _bdot(attn, v_new)
        for h in range(H_A):
            gch = hs(gc, h)
            gl = gch[CA - 1:CA, :]
            s_scr[h] = states[h] * jnp.exp(gl) + _bdot(kat[h * DK_A:(h + 1) * DK_A, r0:r0 + CA], hs(v_new, h) * jnp.exp(gl - gch))
            ob_scr[r0:r0 + CA, h * DV_A:(h + 1) * DV_A] = hs(o, h)
    if t_valid <= CA:
        ob_scr[CA:TC, :] = jnp.zeros((TC - CA, W_A), F32)

    o = ob_scr[...]
    o = o * lax.rsqrt(_dot01_right(o * o, bo) * (1.0 / DV_A) + EPS) * gn_ref[...]
    res = o * _silu(_pad_rows(za_ref[...], TC))
    o_ref[...] = res[0:rows]
    sout_ref[...] = s_scr[...]


def _mix_c_kernel(pc_ref, zc_ref, sm_ref, conv0_ref, s0_ref, cw_ref, cb_ref, hp_ref, hpt_ref, gn_ref,
                  tril_ref, triu_ref, bo_ref, o_ref, sout_ref, ext_scr, s_scr, yb_scr, *, t_valid, rows):
    @pl.when(pl.program_id(1) == 0)
    def _():
        ext_scr[0:SUB, :] = conv0_ref[...]
        s_scr[...] = s0_ref[...]

    u = _conv_silu(_pad_rows(pc_ref[...], TC), ext_scr, cw_ref, cb_ref)
    xc = u[:, 0:D_INNER_C]
    bcm = u[:, D_INNER_C:D_INNER_C + G_C * N_C]
    ccm = u[:, D_INNER_C + G_C * N_C:CONV_C]
    bct = bcm.T
    zc = _pad_rows(zc_ref[...], TC)

    sm = _pad_rows(sm_ref[...], TC)
    smt = sm.T
    hp, hpt = hp_ref[...], hpt_ref[...]
    dt_col = _softplus(sm[:, SM_DT:SM_DT + H_C] + hp[3:4, 0:H_C])
    dt_row = _softplus(smt[SM_DT:SM_DT + H_C, :] + hpt[0:H_C, 3:4])
    if t_valid < TC:
        dt_col = jnp.where(_iota2(dt_col.shape, 0) < t_valid, dt_col, 0.0)
        dt_row = jnp.where(_iota2(dt_row.shape, 1) < t_valid, dt_row, 0.0)
    gcum_col = _dot01_left(tril_ref[...], -jnp.exp(hp[2:3, 0:H_C]) * dt_col)
    gcum_row = _dot01_right(-jnp.exp(hpt[0:H_C, 2:3]) * dt_row, triu_ref[...])

    incl = _iota2((TC, TC), 0) >= _iota2((TC, TC), 1)
    hg = H_C // G_C
    for g in range(G_C):
        cc_g = ccm[:, g * N_C:(g + 1) * N_C]
        gram = _bdot_nt(cc_g, bcm[:, g * N_C:(g + 1) * N_C])
        for hh in range(hg):
            h = g * hg + hh
            c0 = h * P_C
            gc = gcum_col[:, h:h + 1]
            gr = gcum_row[h:h + 1, :]
            dec = jnp.where(incl, jnp.exp(jnp.where(incl, gc - gr, 0.0)), 0.0)
            attn = gram * dt_row[h:h + 1, :] * dec
            xh = xc[:, c0:c0 + P_C]
            S = s_scr[h]
            o = jnp.exp(gc) * _bdot(cc_g, S) + _bdot(attn, xh)
            gl = gc[TC - 1:TC, :]
            s_scr[h] = S * jnp.exp(gl) + _bdot(bct[g * N_C:(g + 1) * N_C, :], xh * (dt_col[:, h:h + 1] * jnp.exp(gl - gc)))
            yb_scr[:, c0:c0 + P_C] = (o + hp[4:5, h:h + 1] * xh) * _silu(zc[:, c0:c0 + P_C])

    y = yb_scr[...]
    y = y * lax.rsqrt(_dot01_right(y * y, bo_ref[...]) * (1.0 / (D_INNER_C // G_C)) + EPS) * gn_ref[...]
    o_ref[...] = y[0:rows]
    sout_ref[...] = s_scr[...]


_D_LEVELS = 6


def _mix_d_kernel(dg_ref, s0_ref, lb_ref, gn_ref, tril_ref, sel_ref, bo_ref, o_ref, sout_ref, s_scr, ob_scr, *, t_valid, rows):
    CD = 64
    W = H_D * DK_D

    @pl.when(pl.program_id(1) == 0)
    def _():
        s_scr[...] = s0_ref[...]

    dg = _pad_rows(dg_ref[...], TC)
    q = _silu(dg[:, 0:W])
    fr = dg[:, W:2 * W]
    v = dg[:, 2 * W:3 * W]
    gd = dg[:, 3 * W:4 * W]
    lb = lb_ref[...]
    logf = jnp.log(jnp.maximum(lb + (1.0 - lb) * _sigmoid(fr), TINY))
    kd = (1.0 - lb) * _sigmoid(-fr)
    if t_valid < TC:
        ok = _iota2((TC, W), 0) < t_valid
        logf = jnp.where(ok, logf, 0.0)
        kd = jnp.where(ok, kd, 0.0)
    gcum = _dot01_left(tril_ref[...], logf)
    gm = _dot01_left(sel_ref[...], gcum)
    vt = v.T
    qdec = q * jnp.exp(gcum)

    ii, jj = _iota2((CD, CD), 0), _iota2((CD, CD), 1)
    lhs, rhs, msk = [q], [kd], [ii == jj]
    for lv in range(1, _D_LEVELS + 1):
        gml = gm[(lv - 1) * TC:lv * TC, :]
        lhs.append(q * jnp.exp(jnp.minimum(gcum - gml, 0.0)))
        rhs.append(kd * jnp.exp(jnp.minimum(gml - gcum, 0.0)))
        half = 1 << (lv - 1)
        msk.append(((ii >> lv) == (jj >> lv)) & ((ii & half) != 0) & ((jj & half) == 0))

    for sc in range(TC // CD):
        r0 = sc * CD
        for h in range(H_D):
            c0 = h * DK_D
            attn = jnp.zeros((CD, CD), F32)
            for a, b, m in zip(lhs, rhs, msk):
                attn = attn + jnp.where(m, _bdot_nt(a[r0:r0 + CD, c0:c0 + DK_D], b[r0:r0 + CD, c0:c0 + DK_D]), 0.0)
            st = s_scr[h]
            ob_scr[r0:r0 + CD, c0:c0 + DV_D] = _bdot_nt(qdec[r0:r0 + CD, c0:c0 + DK_D], st) + _bdot(attn, v[r0:r0 + CD, c0:c0 + DV_D])
            gch = gcum[r0:r0 + CD, c0:c0 + DK_D]
            gl = gch[CD - 1:CD, :]
            s_scr[h] = st * jnp.exp(gl) + _bdot(vt[c0:c0 + DV_D, r0:r0 + CD], kd[r0:r0 + CD, c0:c0 + DK_D] * jnp.exp(gl - gch))

    o = ob_scr[...]
    o = o * lax.rsqrt(_dot01_right(o * o, bo_ref[...]) * (1.0 / DV_D) + EPS) * gn_ref[...]
    res = o * _silu(gd)
    o_ref[...] = res[0:rows]
    sout_ref[...] = s_scr[...]


def _np_consts():
    i = np.arange(TC)
    same64 = (i[:, None] // 64) == (i[None, :] // 64)
    tril2 = (same64 & (i[:, None] >= i[None, :])).astype(np.float32)
    tril1 = (i[:, None] >= i[None, :]).astype(np.float32)
    sel = np.zeros((_D_LEVELS * TC, TC), np.float32)
    for lv in range(1, _D_LEVELS + 1):
        m = ((i >> lv) << lv) + (1 << (lv - 1)) - 1
        sel[(lv - 1) * TC + i, m] = 1.0
    bo = lambda n, w: ((np.arange(n)[:, None] // w) == (np.arange(n)[None, :] // w)).astype(np.float32)
    return dict(tril2=tril2, triu2=tril2.T.copy(), tril1=tril1, triu1=tril1.T.copy(), sel=sel,
                bo64=bo(256, 64), bo256=bo(D_INNER_C, D_INNER_C // G_C))


def _bucket_starts():
    max_exact = N_BUCKETS // 2
    d = np.arange(0, 4 * MAX_DISTANCE)
    lr = np.log(np.maximum(d, 1).astype(np.float32) / max_exact) / math.log(MAX_DISTANCE / max_exact)
    large = np.minimum(max_exact + (np.maximum(lr, 0.0) * (N_BUCKETS - max_exact)).astype(np.int32), N_BUCKETS - 1)
    bucket = np.where(d < max_exact, d, large)
    return [int(np.argmax(bucket == b)) for b in range(N_BUCKETS)]


_BUCKET_STARTS = _bucket_starts()
_SAT_DIST = _BUCKET_STARTS[N_BUCKETS - 1]


def _bias_from_dist(d, rb_ref, h):
    out = jnp.full(d.shape, rb_ref[N_BUCKETS - 1, h], F32)
    for b in range(N_BUCKETS - 2, -1, -1):
        out = jnp.where(d < _BUCKET_STARTS[b + 1], rb_ref[b, h], out)
    return out


def _sortable(x):
    b = lax.bitcast_convert_type(x + 0.0, I32)
    return jnp.where(b < 0, b ^ 0x7FFFFFFF, b)


_NEG_KEY = int(np.array([NEG], np.float32).view(np.int32)[0]) ^ 0x7FFFFFFF
_INT_MIN = -2 ** 31


def _selected_i32(k, idx, thr, xthr):
    s = jnp.where(k > thr, 1, jnp.where(k == thr, jnp.where(idx <= xthr, 1, 0), 0))
    return jnp.where(k == _NEG_KEY, 0, s)


def _selected(k, idx, thr, xthr):
    return _selected_i32(k, idx, thr, xthr) > 0


def _const_spec(shape):
    nd = len(shape)
    return pl.BlockSpec(shape, lambda *_: (0,) * nd)


def _mix_call(kernel_fn, name, P, B, T, seq_specs, batch_args, const_args, out_w, n_state, scratch):
    rows = min(T, TC)
    nc = max(T // TC, 1)
    assert rows * nc == T
    in_specs = [pl.BlockSpec((rows, w), functools.partial(lambda b, c, ci: (b * nc + c, ci), ci=off // w)) for off, w in seq_specs]
    args = [P] * len(seq_specs)
    for a in batch_args:
        nd = a.ndim
        in_specs.append(pl.BlockSpec((None,) + a.shape[1:], functools.partial(lambda b, c, nd: (b,) + (0,) * (nd - 1), nd=nd)))
        args.append(a)
    for a in const_args:
        in_specs.append(_const_spec(a.shape))
        args.append(a)
    st_shape = (B, n_state, 64, 64)
    return pl.pallas_call(
        functools.partial(kernel_fn, t_valid=rows, rows=rows),
        grid=(B, nc),
        in_specs=in_specs,
        out_specs=[pl.BlockSpec((rows, out_w), lambda b, c: (b * nc + c, 0)),
                   pl.BlockSpec((None, n_state, 64, 64), lambda b, c: (b, 0, 0, 0))],
        out_shape=[jax.ShapeDtypeStruct((B * T, out_w), F32), jax.ShapeDtypeStruct(st_shape, F32)],
        scratch_shapes=scratch,
        compiler_params=pltpu.CompilerParams(dimension_semantics=("arbitrary", "arbitrary"), vmem_limit_bytes=VMEM_LIMIT),
        name=name,
    )(*args)


CU = 4


def _dsa_prompt_kernel(rb_ref, qb_ref, qi_ref, smq_ref, kv_ref, sma_ref, o_ref,
                       kvt_scr, key_scr, lg_scr, bias_scr, acc_scr, *, n_sel, idx_bits):
    j = pl.program_id(1)
    nch = j + 1

    @pl.when((pl.program_id(0) == 0) & (j == 0))
    def _():
        srow, tcol = _iota2((QB, QB), 0), _iota2((QB, QB), 1)
        for h in range(H_B):
            bias_scr[h, 0] = jnp.full((QB, QB), rb_ref[N_BUCKETS - 1, h], F32)
            bias_scr[h, 1] = _bias_from_dist(tcol - srow + QB, rb_ref, h)
            bias_scr[h, 2] = _bias_from_dist(tcol - srow, rb_ref, h)

    @pl.when(j == 0)
    def _():
        def tr(c, _):
            r = pl.multiple_of(c * QB, QB)
            kvt_scr[c] = kv_ref[pl.ds(r, QB), :].T[DH_B:2 * DH_B, :].astype(BF16)
            return 0
        lax.fori_loop(0, kv_ref.shape[0] // QB, tr, 0)

    lanes = lambda rows, w: jnp.concatenate([rows[h * w:(h + 1) * w, :] for h in range(H_B)], axis=1)
    w4 = lanes(smq_ref[...].T[SM_WI:SM_WI + H_IDX, :], 1) * (H_IDX ** -0.5 * D_IDX ** -0.5)
    qi_rhs = lanes(qi_ref[...].T, D_IDX).astype(BF16)
    qb_rhs = (lanes(qb_ref[...].T, DH_B) * (DH_B ** -0.5)).astype(BF16)
    srow, tcol = _iota2((QB, QB), 0), _iota2((QB, QB), 1)
    heads = lambda x: [x[:, h * QB:(h + 1) * QB] for h in range(H_B)]

    def score_body(c, _):
        r = pl.multiple_of(c * QB, QB)
        ki = sma_ref[pl.ds(r, QB), :][:, SM_KI:SM_KI + D_IDX].astype(BF16)
        s = jnp.maximum(jnp.dot(ki, qi_rhs, preferred_element_type=F32), 0.0) * w4
        sh = heads(s)
        sc = (sh[0] + sh[1]) + (sh[2] + sh[3])
        sc = jnp.where(srow + r <= tcol + j * QB, sc, NEG)
        key_scr[pl.ds(r, QB), :] = _sortable(sc)
        return 0

    lax.fori_loop(0, nch, score_body, 0)

    ngrp = (nch + (CU - 1)) // CU

    def pad_body(c, _):
        key_scr[pl.ds(pl.multiple_of(c * QB, QB), QB), :] = jnp.full((QB, QB), _INT_MIN, I32)
        return 0

    lax.fori_loop(nch, ngrp * CU, pad_body, 0)

    def count(pred):
        def body(g, acc):
            for u in range(CU):
                r = pl.multiple_of((g * CU + u) * QB, QB)
                acc = acc + pred(key_scr[pl.ds(r, QB), :], r).reshape(QB // SUB, SUB, QB).sum(axis=0)
            return acc
        return lax.fori_loop(0, ngrp, body, jnp.zeros((SUB, QB), I32)).sum(axis=0, keepdims=True)

    cnt_ge = lambda t: count(lambda k, r: jnp.where(k >= t, 1, 0))
    c0 = cnt_ge(jnp.zeros((1, QB), I32))
    thr = jnp.where(c0 >= n_sel, 0, _INT_MIN).astype(I32)
    cthr = jnp.where(c0 >= n_sel, c0, nch * QB)

    def bit_body(i, carry):
        t, ct = carry
        trial = t | jnp.left_shift(jnp.int32(1), 30 - i)
        cnt = cnt_ge(trial)
        ok = cnt >= n_sel
        return jnp.where(ok, trial, t), jnp.where(ok, cnt, ct)

    thr, cthr = lax.fori_loop(0, 31, bit_body, (thr, cthr))

    def tie_search():
        quota = n_sel - count(lambda k, r: jnp.where(k > thr, 1, 0))

        def idx_body(i, x):
            trial = x | jnp.left_shift(jnp.int32(1), idx_bits - 1 - i)
            below = count(lambda k, r: jnp.where(k == thr, jnp.where(srow + r < trial, 1, 0), 0))
            return jnp.where(below < quota, trial, x)

        return lax.fori_loop(0, idx_bits, idx_body, jnp.zeros((1, QB), I32))

    xthr = lax.cond(jnp.max(cthr) > n_sel, tie_search, lambda: jnp.full((1, QB), (1 << idx_bits) - 1, I32))

    def logit_body(c, ms):
        r = pl.multiple_of(c * QB, QB)
        kc = kv_ref[pl.ds(r, QB), :][:, 0:DH_B].astype(BF16)
        l4 = heads(jnp.dot(kc, qb_rhs, preferred_element_type=F32))
        sel = _selected(key_scr[pl.ds(r, QB), :], srow + r, thr, xthr)
        bidx = jnp.clip(c - j + 2, 0, 2)
        out = []
        for h in range(H_B):
            l = jnp.where(sel, l4[h] + bias_scr[h, bidx], NEG)
            lg_scr[h, pl.ds(r, QB), :] = l
            out.append(jnp.maximum(ms[h], l.max(axis=0, keepdims=True)))
        return tuple(out)

    ms = lax.fori_loop(0, nch, logit_body, tuple(jnp.full((1, QB), NEG, F32) for _ in range(H_B)))
    acc_scr[...] = jnp.zeros(acc_scr.shape, F32)

    def pv_body(c, ss):
        r = pl.multiple_of(c * QB, QB)
        es = [jnp.exp(lg_scr[h, pl.ds(r, QB), :] - ms[h]) for h in range(H_B)]
        acc_scr[...] += jnp.dot(kvt_scr[c], jnp.concatenate(es, axis=1).astype(BF16), preferred_element_type=F32)
        return tuple(ss[h] + es[h].sum(axis=0, keepdims=True) for h in range(H_B))

    ss = lax.fori_loop(0, nch, pv_body, tuple(jnp.zeros((1, QB), F32) for _ in range(H_B)))
    acc = heads(acc_scr[...])
    ot = jnp.concatenate([acc[h] / ss[h] for h in range(H_B)], axis=0)
    o_ref[...] = ot.T


def _dsa_prompt(P, rel_bias, B, T):
    nb = T // QB
    assert nb % CU == 0
    n_sel = min(TOPK_MAX, T // 4)
    idx_bits = max(1, int(math.ceil(math.log2(T))))
    return pl.pallas_call(
        functools.partial(_dsa_prompt_kernel, n_sel=n_sel, idx_bits=idx_bits),
        grid=(B, nb),
        in_specs=[pl.BlockSpec(memory_space=pltpu.SMEM),
                  pl.BlockSpec((QB, W_B), lambda b, j: (b * nb + j, OFF_QB // W_B)),
                  pl.BlockSpec((QB, LANE), lambda b, j: (b * nb + j, OFF_QI // LANE)),
                  pl.BlockSpec((QB, LANE), lambda b, j: (b * nb + j, OFF_SM // LANE)),
                  pl.BlockSpec((T, LANE), lambda b, j: (b, OFF_KV // LANE)),
                  pl.BlockSpec((T, LANE), lambda b, j: (b, OFF_SM // LANE))],
        out_specs=pl.BlockSpec((QB, W_B), lambda b, j: (b * nb + j, 0)),
        out_shape=jax.ShapeDtypeStruct((B * T, W_B), F32),
        scratch_shapes=[pltpu.VMEM((nb, DH_B, QB), BF16), pltpu.VMEM((T, QB), I32), pltpu.VMEM((H_B, T, QB), F32),
                        pltpu.VMEM((H_B, 3, QB, QB), F32), pltpu.VMEM((DH_B, H_B * QB), F32)],
        compiler_params=pltpu.CompilerParams(dimension_semantics=("arbitrary", "arbitrary"), vmem_limit_bytes=VMEM_LIMIT),
        name="dsa_prompt",
    )(rel_bias, P, P, P, P, P)


PG = 8
GW = PG * PAGE_SIZE


def _stack_heads(x, w):
    return jnp.concatenate([x[:, h * w:(h + 1) * w] for h in range(x.shape[1] // w)], axis=0)


def _dsa_s_scores_kernel(pt_ref, qi_ref, sm_ref, *refs, ng, ds):
    page_refs, o_ref = refs[:PG], refs[PG]
    g = pl.program_id(1)
    sm = sm_ref[...]
    qst = _stack_heads(qi_ref[...], D_IDX)
    wcol = jnp.concatenate([sm[:, SM_WI + h:SM_WI + h + 1] for h in range(H_IDX)], axis=0) * (H_IDX ** -0.5)

    def tile_scores(kidx):
        s = jnp.maximum(_bdot_nt(qst, kidx) * (D_IDX ** -0.5), 0.0) * wcol
        out = s[0:ds]
        for h in range(1, H_IDX):
            out = out + s[h * ds:(h + 1) * ds]
        return out

    @pl.when(g < ng)
    def _():
        for i in range(PG):
            o_ref[:, i * PAGE_SIZE:(i + 1) * PAGE_SIZE] = tile_scores(page_refs[i][...])

    @pl.when(g == ng)
    def _():
        sc = tile_scores(_pad_rows(sm[:, SM_KI:SM_KI + D_IDX], PAGE_SIZE))
        lane, row = _iota2(sc.shape, 1), _iota2(sc.shape, 0)
        o_ref[:, 0:PAGE_SIZE] = jnp.where(lane <= row, sc, NEG)
        o_ref[:, PAGE_SIZE:GW] = jnp.full((ds, GW - PAGE_SIZE), NEG, F32)


def _dsa_s_attn_kernel(pt_ref, rb_ref, sc_ref, qb_ref, kvn_ref, *refs, ng, ds, n_sel, idx_bits):
    k_refs, v_refs = refs[:PG], refs[PG:2 * PG]
    o_ref, key_scr, thr_scr, bias_scr, m_scr, l_scr, acc_scr = refs[2 * PG:]
    g = pl.program_id(1)
    ntile = (ng + 1) * PG
    hr = H_B * ds

    @pl.when(g == 0)
    def _():
        for i in range(ntile):
            key_scr[:, i * LANE:(i + 1) * LANE] = _sortable(sc_ref[:, i * LANE:(i + 1) * LANE])
        lane = _iota2((ds, LANE), 1)

        def count(pred):
            acc = jnp.zeros((ds, LANE), I32)
            for i in range(ntile):
                acc = acc + pred(key_scr[:, i * LANE:(i + 1) * LANE], i * LANE)
            return acc.sum(axis=1, keepdims=True)

        cnt_ge = lambda t: count(lambda k, off: jnp.where(k >= t, 1, 0))
        thr = jnp.where(cnt_ge(jnp.zeros((ds, 1), I32)) >= n_sel, 0, _INT_MIN).astype(I32)

        def bit_body(i, t):
            trial = t | jnp.left_shift(jnp.int32(1), 30 - i)
            return jnp.where(cnt_ge(trial) >= n_sel, trial, t)

        thr = lax.fori_loop(0, 31, bit_body, thr)
        quota = n_sel - count(lambda k, off: jnp.where(k > thr, 1, 0))

        def idx_body(i, x):
            trial = x | jnp.left_shift(jnp.int32(1), idx_bits - 1 - i)
            below = count(lambda k, off: jnp.where(k == thr, jnp.where(lane + off < trial, 1, 0), 0))
            return jnp.where(below < quota, trial, x)

        xthr = lax.fori_loop(0, idx_bits, idx_body, jnp.zeros((ds, 1), I32))
        thr_scr[0] = jnp.broadcast_to(thr, (ds, LANE))
        thr_scr[1] = jnp.broadcast_to(xthr, (ds, LANE))

        qrow = _iota2((ds, LANE), 0)
        for h in range(H_B):
            bias_scr[0, h * ds:(h + 1) * ds, :] = _bias_from_dist(PAGE_SIZE + qrow - lane, rb_ref, h)
            bias_scr[1, h * ds:(h + 1) * ds, :] = _bias_from_dist(qrow - lane, rb_ref, h)
            bias_scr[2, h * ds:(h + 1) * ds, :] = jnp.full((ds, LANE), rb_ref[N_BUCKETS - 1, h], F32)
        m_scr[...] = jnp.full(m_scr.shape, NEG, F32)
        l_scr[...] = jnp.zeros(l_scr.shape, F32)
        acc_scr[...] = jnp.zeros(acc_scr.shape, F32)

    qst = _stack_heads(qb_ref[...], DH_B)
    thr, xthr = thr_scr[0], thr_scr[1]

    def select(tile):
        k = key_scr[:, pl.ds(pl.multiple_of(tile * LANE, LANE), LANE)]
        idx = _iota2((ds, LANE), 1) + tile * LANE
        sel = _selected_i32(k, idx, thr, xthr)
        return jnp.concatenate([sel] * H_B, axis=0) > 0

    def update(lg, sel, v):
        m_old = m_scr[...]
        m_new = jnp.maximum(m_old, jnp.where(sel, lg, NEG).max(axis=1, keepdims=True))
        p = jnp.where(sel, jnp.exp(lg - m_new), 0.0)
        corr = jnp.exp(m_old - m_new)
        l_scr[...] = l_scr[...] * corr + p.sum(axis=1, keepdims=True)
        acc_scr[...] = acc_scr[...] * corr + _bdot(p, v)
        m_scr[...] = m_new

    @pl.when(g < ng)
    def _():
        lgs, sels = [], []
        for i in range(PG):
            lg = _bdot_nt(qst, k_refs[i][...]) * (DH_B ** -0.5)
            if i == PG - 1:
                lg = lg + jnp.where(g == ng - 1, bias_scr[0], bias_scr[2])
            else:
                lg = lg + bias_scr[2]
            lgs.append(lg)
            sels.append(select(g * PG + i))
        v = jnp.concatenate([v_refs[i][...] for i in range(PG)], axis=0)
        update(jnp.concatenate(lgs, axis=1), jnp.concatenate(sels, axis=1), v)

    @pl.when(g == ng)
    def _():
        kvn = _pad_rows(kvn_ref[...], PAGE_SIZE)
        lg = _bdot_nt(qst, kvn[:, 0:DH_B]) * (DH_B ** -0.5) + bias_scr[1]
        update(lg, select(ng * PG), kvn[:, DH_B:2 * DH_B])
        o = acc_scr[...] / l_scr[...]
        o_ref[...] = jnp.concatenate([o[h * ds:(h + 1) * ds] for h in range(H_B)], axis=1)


def _dsa_sample(P, rel_bias, cache_k, cache_v, cache_kidx, page_table, layer, B, DS):
    n_pages = page_table.shape[1]
    past = n_pages * PAGE_SIZE
    assert n_pages % PG == 0
    ng = n_pages // PG
    n_sel = min(TOPK_MAX, (past + DS) // 4)
    idx_bits = int(math.ceil(math.log2(past + LANE)))
    s_pad = (ng + 1) * GW

    def page_spec(width, i):
        return pl.BlockSpec((None, None, PAGE_SIZE, width),
                            lambda b, g, pt: (layer, pt[b, jnp.minimum(g, ng - 1) * PG + i], 0, 0))

    row = lambda ci: (lambda b, g, pt: (b, ci))
    cp = pltpu.CompilerParams(dimension_semantics=("arbitrary", "arbitrary"), vmem_limit_bytes=VMEM_LIMIT)
    scores = pl.pallas_call(
        functools.partial(_dsa_s_scores_kernel, ng=ng, ds=DS),
        grid_spec=pltpu.PrefetchScalarGridSpec(
            num_scalar_prefetch=1, grid=(B, ng + 1),
            in_specs=[pl.BlockSpec((DS, LANE), row(OFF_QI // LANE)), pl.BlockSpec((DS, LANE), row(OFF_SM // LANE))]
                     + [page_spec(D_IDX, i) for i in range(PG)],
            out_specs=pl.BlockSpec((None, DS, GW), lambda b, g, pt: (b, 0, g))),
        out_shape=jax.ShapeDtypeStruct((B, DS, s_pad), F32),
        compiler_params=cp, name="dsa_sample_scores",
    )(page_table, P, P, *([cache_kidx] * PG))

    return pl.pallas_call(
        functools.partial(_dsa_s_attn_kernel, ng=ng, ds=DS, n_sel=n_sel, idx_bits=idx_bits),
        grid_spec=pltpu.PrefetchScalarGridSpec(
            num_scalar_prefetch=1, grid=(B, ng + 1),
            in_specs=[pl.BlockSpec(memory_space=pltpu.SMEM),
                      pl.BlockSpec((None, DS, s_pad), lambda b, g, pt: (b, 0, 0)),
                      pl.BlockSpec((DS, W_B), row(OFF_QB // W_B)), pl.BlockSpec((DS, LANE), row(OFF_KV // LANE))]
                     + [page_spec(DH_B, i) for i in range(PG)] + [page_spec(DH_B, i) for i in range(PG)],
            out_specs=pl.BlockSpec((DS, W_B), lambda b, g, pt: (b, 0)),
            scratch_shapes=[pltpu.VMEM((DS, s_pad), I32), pltpu.VMEM((2, DS, LANE), I32), pltpu.VMEM((3, H_B * DS, LANE), F32),
                            pltpu.VMEM((H_B * DS, 1), F32), pltpu.VMEM((H_B * DS, 1), F32), pltpu.VMEM((H_B * DS, DH_B), F32)]),
        out_shape=jax.ShapeDtypeStruct((B * DS, W_B), F32),
        compiler_params=cp, name="dsa_sample_attn",
    )(page_table, rel_bias, scores, P, P, *([cache_k] * PG), *([cache_v] * PG))


def _merge_kernel(x_ref, oa_ref, ob_ref, oc_ref, od_ref, gate_ref, wb_ref, wo_ref, o_ref):
    m = None
    r0 = 0
    for br, ref in enumerate((oa_ref, ob_ref, oc_ref, od_ref)):
        w = ref.shape[1]
        t = _sigmoid(gate_ref[:, br * D_MODEL:(br + 1) * D_MODEL]) * jnp.dot(ref[...].astype(BF16), wb_ref[r0:r0 + w, :], preferred_element_type=F32)
        m = t if m is None else m + t
        r0 += w
    o_ref[...] = x_ref[...] + jnp.dot(m.astype(BF16), wo_ref[...], preferred_element_type=F32)


def _merge(x2d, oa, ob, oc, od, P, wb, wo):
    n = x2d.shape[0]
    tm = min(n, 256)
    row = lambda i: (i, 0)
    return pl.pallas_call(
        _merge_kernel,
        grid=(n // tm,),
        in_specs=[pl.BlockSpec((tm, D_MODEL), row), pl.BlockSpec((tm, W_A), row), pl.BlockSpec((tm, W_B), row),
                  pl.BlockSpec((tm, W_C), row), pl.BlockSpec((tm, W_D), row),
                  pl.BlockSpec((tm, N_BRANCH * D_MODEL), lambda i: (i, OFF_GATE // (N_BRANCH * D_MODEL))),
                  _const_spec((MIX_W, D_MODEL)), _const_spec((D_MODEL, D_MODEL))],
        out_specs=pl.BlockSpec((tm, D_MODEL), row),
        out_shape=jax.ShapeDtypeStruct((n, D_MODEL), F32),
        compiler_params=pltpu.CompilerParams(dimension_semantics=("arbitrary",), vmem_limit_bytes=VMEM_LIMIT),
        name="merge",
    )(x2d, oa, ob, oc, od, P, wb, wo)


def _mlp_kernel(x_ref, g_ref, wu_ref, wd_ref, gf_ref, *refs, final):
    if final:
        o_ref, y_ref, h_scr, acc_scr = refs
    else:
        o_ref, h_scr, acc_scr = refs
    f = pl.program_id(1)

    @pl.when(f == 0)
    def _():
        x = x_ref[...]
        r = lax.rsqrt(jnp.mean(x * x, axis=-1, keepdims=True) + EPS)
        h_scr[...] = ((x * r) * g_ref[...]).astype(BF16)
        acc_scr[...] = jnp.zeros(acc_scr.shape, F32)

    a = jnp.maximum(jnp.dot(h_scr[...], wu_ref[...], preferred_element_type=F32), 0.0)
    acc_scr[...] += jnp.dot((a * a).astype(BF16), wd_ref[...], preferred_element_type=F32)

    @pl.when(f == pl.num_programs(1) - 1)
    def _():
        out = x_ref[...] + acc_scr[...]
        o_ref[...] = out
        if final:
            r = lax.rsqrt(jnp.mean(out * out, axis=-1, keepdims=True) + EPS)
            y_ref[...] = (out * r) * gf_ref[...]


def _mlp(x2d, g, wu, wd, gf, final):
    n = x2d.shape[0]
    tm = min(n, 512)
    tf = 1024
    row = lambda i, f: (i, 0)
    out_spec = pl.BlockSpec((tm, D_MODEL), row)
    shp = jax.ShapeDtypeStruct((n, D_MODEL), F32)
    return pl.pallas_call(
        functools.partial(_mlp_kernel, final=final),
        grid=(n // tm, D_FF // tf),
        in_specs=[pl.BlockSpec((tm, D_MODEL), row), _const_spec((1, D_MODEL)),
                  pl.BlockSpec((D_MODEL, tf), lambda i, f: (0, f)), pl.BlockSpec((tf, D_MODEL), lambda i, f: (f, 0)),
                  _const_spec((1, D_MODEL))],
        out_specs=[out_spec, out_spec] if final else [out_spec],
        out_shape=[shp, shp] if final else [shp],
        scratch_shapes=[pltpu.VMEM((tm, D_MODEL), BF16), pltpu.VMEM((tm, D_MODEL), F32)],
        compiler_params=pltpu.CompilerParams(dimension_semantics=("arbitrary", "arbitrary"), vmem_limit_bytes=VMEM_LIMIT),
        name="mlp_final" if final else "mlp",
    )(x2d, g.reshape(1, D_MODEL), wu, wd, gf.reshape(1, D_MODEL))


def _layer(x, conv_state, s_delta, s_ssm, s_hgrn, lw, consts, sample_ctx, final):
    B, T, _ = x.shape
    x2 = x.reshape(B * T, D_MODEL)
    P = _in_proj(x2, lw['norm_mix'], lw['w_in'])

    def conv0(lo, hi):
        if conv_state is None:
            return jnp.zeros((B, SUB, hi - lo), F32)
        return jnp.concatenate([jnp.zeros((B, SUB - (CONV_W - 1), hi - lo), F32), conv_state[:, :, lo:hi]], axis=1)

    zeros_state = lambda n: jnp.zeros((B, n, 64, 64), F32)
    c = consts
    half = CONV_CH // 2
    assert CONV_A == half and OFF_PC % half == 0
    oa, s_a = _mix_call(
        _mix_a_kernel, "mix_a", P, B, T,
        [(OFF_PC, half), (OFF_ZA, W_A), (OFF_SM, LANE)],
        [conv0(0, CONV_A), zeros_state(H_A) if s_delta is None else s_delta],
        [lw['conv_w'][:, 0:CONV_A], lw['conv_b'][None, 0:CONV_A], lw['hp'], lw['hpt'], lw['gn_a'], c['tril2'], c['triu2'], c['bo64']],
        W_A, H_A,
        [pltpu.VMEM((TC + SUB, CONV_A), F32), pltpu.VMEM((H_A, DK_A, DV_A), F32), pltpu.VMEM((TC, W_A), F32)])
    oc, s_c = _mix_call(
        _mix_c_kernel, "mix_c", P, B, T,
        [(OFF_PC + half, half), (OFF_ZC, W_C), (OFF_SM, LANE)],
        [conv0(CONV_A, CONV_CH), zeros_state(H_C) if s_ssm is None else s_ssm],
        [lw['conv_w'][:, CONV_A:], lw['conv_b'][None, CONV_A:], lw['hp'], lw['hpt'], lw['gn_c'], c['tril1'], c['triu1'], c['bo256']],
        W_C, H_C,
        [pltpu.VMEM((TC + SUB, CONV_C), F32), pltpu.VMEM((H_C, N_C, P_C), F32), pltpu.VMEM((TC, W_C), F32)])
    od, s_dt = _mix_call(
        _mix_d_kernel, "mix_d", P, B, T,
        [(OFF_D, 4 * W_D)],
        [zeros_state(H_D) if s_hgrn is None else jnp.swapaxes(s_hgrn, -1, -2)],
        [lw['lb'], lw['gn_d'], c['tril2'], c['sel'], c['bo64']],
        W_D, H_D,
        [pltpu.VMEM((H_D, DV_D, DK_D), F32), pltpu.VMEM((TC, W_D), F32)])
    if sample_ctx is None:
        ob = _dsa_prompt(P, lw['rel_bias'], B, T)
    else:
        ob = _dsa_sample(P, lw['rel_bias'], sample_ctx['cache_k'], sample_ctx['cache_v'], sample_ctx['cache_kidx'],
                         sample_ctx['page_table'], sample_ctx['layer'], B, T)
    x1 = _merge(x2, oa, ob, oc, od, P, lw['w_branch'], lw['w_out'])
    outs = _mlp(x1, lw['norm_mlp'], lw['w_up'], lw['w_down'], lw['norm_final'], final)
    xo = outs[0].reshape(B, T, D_MODEL)
    y = outs[1].reshape(B, T, D_MODEL) if final else None
    P3 = P.reshape(B, T, P_PAD)
    states = (P3[:, :, OFF_KV:OFF_KV + DH_B], P3[:, :, OFF_KV + DH_B:OFF_KV + 2 * DH_B], P3[:, :, OFF_SM + SM_KI:OFF_SM + SM_KI + D_IDX],
              P3[:, T - (CONV_W - 1):, OFF_PC:OFF_PC + CONV_CH], s_a, s_c, jnp.swapaxes(s_dt, -1, -2))
    return xo, y, states


def kernel(x_prompt, x_sample, cache_k, cache_v, cache_kidx, state_conv, state_delta, state_ssm, state_hgrn, page_table, norm_mix, w_in, conv_w, conv_b, a_log_a, dt_bias_a, gnorm_a, rel_bias, a_log_c, dt_bias_c, d_skip_c, gnorm_c, hgrn_gamma, gnorm_d, w_branch, w_out, norm_mlp, w_up, w_down, norm_final):
    depth = w_in.shape[0]
    npc = _np_consts()
    consts = {k: jnp.asarray(v, BF16) for k, v in npc.items()}
    pg = jax.nn.softmax(hgrn_gamma.astype(F32), axis=0)
    lower_bounds = jnp.cumsum(pg, axis=0) - pg[0]
    yp, ys = x_prompt, x_sample
    new_p, new_s = [], []
    for l in range(depth):
        hp = jnp.zeros((SUB, LANE), F32)
        hp = hp.at[0, 0:H_A].set(a_log_a[l]).at[1, 0:H_A].set(dt_bias_a[l])
        hp = hp.at[2, 0:H_C].set(a_log_c[l]).at[3, 0:H_C].set(dt_bias_c[l]).at[4, 0:H_C].set(d_skip_c[l])
        lw = dict(norm_mix=norm_mix[l], w_in=_permute_w_in(w_in[l]), conv_w=conv_w[l], conv_b=conv_b[l],
                  hp=hp, hpt=hp.T, gn_a=jnp.tile(gnorm_a[l], H_A)[None, :], gn_c=gnorm_c[l][None, :],
                  gn_d=jnp.tile(gnorm_d[l], H_D)[None, :], lb=lower_bounds[l][None, :], rel_bias=rel_bias,
                  w_branch=w_branch[l].astype(BF16), w_out=w_out[l].astype(BF16), norm_mlp=norm_mlp[l],
                  w_up=w_up[l].astype(BF16), w_down=w_down[l].astype(BF16), norm_final=norm_final)
        final = l == depth - 1
        yp, yp_n, st_p = _layer(yp, None, None, None, None, lw, consts, None, final)
        ctx = dict(cache_k=cache_k, cache_v=cache_v, cache_kidx=cache_kidx, page_table=page_table, layer=l)
        ys, ys_n, st_s = _layer(ys, state_conv[l], state_delta[l], state_ssm[l], state_hgrn[l], lw, consts, ctx, final)
        new_p.append(st_p)
        new_s.append(st_s)
    stack = lambda per_layer: [jnp.stack(items) for items in zip(*per_layer)]
    return (yp_n, ys_n, *stack(new_p), *stack(new_s))
```

```python
import functools
import math

import numpy as np
import jax
import jax.numpy as jnp
from jax import lax
from jax.experimental import pallas as pl
from jax.experimental.pallas import tpu as pltpu

F32 = jnp.float32
BF16 = jnp.bfloat16
I32 = jnp.int32

D_MODEL = 1024
PAGE_SIZE = 128
EPS = 1e-6
NEG = -1e30
TINY = 1e-30
CONV_W = 4
D_FF = 4 * D_MODEL
N_BRANCH = 4
H_A, DK_A, DV_A = 4, 64, 64
H_B, DH_B = 4, 64
H_IDX, D_IDX = 4, 32
TOPK_MAX = 256
N_BUCKETS = 32
MAX_DISTANCE = 128
H_C, P_C, N_C, G_C = 8, 64, 64, 2
H_D, DK_D, DV_D = 4, 64, 64
QK_A = H_A * DK_A
CONV_A = 2 * QK_A + H_A * DV_A
D_INNER_C = H_C * P_C
CONV_C = D_INNER_C + 2 * G_C * N_C
CONV_CH = CONV_A + CONV_C
W_A, W_B, W_C, W_D = H_A * DV_A, H_B * DH_B, D_INNER_C, H_D * DV_D
MIX_W = W_A + W_B + W_C + W_D
IN_WIDTHS = (CONV_CH, W_A, H_A, H_A, W_B, DH_B, DH_B, H_IDX * D_IDX, D_IDX, H_IDX, D_INNER_C, H_C,
             H_D * DK_D, H_D * DK_D, W_D, W_D, N_BRANCH * D_MODEL)
P_IN = sum(IN_WIDTHS)

LANE = 128
SUB = 8
TC = 128
QB = 128
VMEM_LIMIT = 48 * 1024 * 1024

OFF_GATE, OFF_D, OFF_ZC, OFF_ZA, OFF_QB, OFF_PC, OFF_KV, OFF_QI, OFF_SM = 0, 4096, 5120, 5632, 5888, 6144, 7680, 7808, 7936
P_PAD = 8064
SM_KI, SM_WI, SM_AA, SM_BA, SM_DT = 0, 32, 36, 40, 44


def _src_offsets():
    offs, o = [], 0
    for w in IN_WIDTHS:
        offs.append(o)
        o += w
    return offs


def _permute_w_in(w):
    (o_pc, o_za, o_aa, o_ba, o_qb, o_kb, o_vb, o_qi, o_ki, o_wi, o_zc, o_dt, o_qd, o_fd, o_id, o_gd, o_gate) = _src_offsets()
    wt = jnp.swapaxes(w, 0, 1)
    seg = lambda o, n: wt[o:o + n, :]
    small = jnp.concatenate([seg(o_ki, D_IDX), seg(o_wi, H_IDX), seg(o_aa, H_A), seg(o_ba, H_A), seg(o_dt, H_C),
                             jnp.zeros((LANE - (D_IDX + H_IDX + 2 * H_A + H_C), w.shape[0]), w.dtype)], axis=0)
    out = jnp.concatenate([
        seg(o_gate, N_BRANCH * D_MODEL),
        seg(o_qd, 4 * W_D),
        seg(o_zc, D_INNER_C), seg(o_za, W_A), seg(o_qb, W_B), seg(o_pc, CONV_CH),
        seg(o_kb, 2 * DH_B),
        seg(o_qi, H_IDX * D_IDX), small], axis=0)
    assert out.shape[0] == P_PAD
    return out.astype(BF16)


def _bdot(a, b):
    return jnp.dot(a.astype(BF16), b.astype(BF16), preferred_element_type=F32)


def _bdot_nt(a, b):
    return lax.dot_general(a.astype(BF16), b.astype(BF16), (((1,), (1,)), ((), ())), preferred_element_type=F32)


def _split2(a):
    hi = a.astype(BF16)
    lo = (a - hi.astype(F32)).astype(BF16)
    return hi, lo


def _split3(a):
    hi = a.astype(BF16)
    r = a - hi.astype(F32)
    mid = r.astype(BF16)
    lo = (r - mid.astype(F32)).astype(BF16)
    return hi, mid, lo


def _dot01_left(m01, x):
    hi, mid, lo = _split3(x)
    d = lambda p: jnp.dot(m01, p, preferred_element_type=F32)
    return d(hi) + (d(mid) + d(lo))


def _dot01_right(x, m01):
    hi, mid, lo = _split3(x)
    d = lambda p: jnp.dot(p, m01, preferred_element_type=F32)
    return d(hi) + (d(mid) + d(lo))


def _xdot(a, b):
    ah, al = _split2(a)
    bh, bl = _split2(b)
    d = lambda p, q: jnp.dot(p, q, preferred_element_type=F32)
    return d(ah, bh) + (d(ah, bl) + d(al, bh))


def _sigmoid(x):
    return 1.0 / (1.0 + jnp.exp(-x))


def _silu(x):
    return x * _sigmoid(x)


def _softplus(x):
    return jnp.maximum(x, 0.0) + jnp.log(1.0 + jnp.exp(-jnp.abs(x)))


def _iota2(shape, dim):
    return lax.broadcasted_iota(I32, shape, dim)


def _pad_rows(x, rows):
    if x.shape[0] == rows:
        return x
    return jnp.concatenate([x, jnp.zeros((rows - x.shape[0],) + x.shape[1:], x.dtype)], axis=0)


def _tri_inv(L, top):
    return _tri_inv_many([L], top)[0]


def _tri_inv_many(Ls, top):
    n = Ls[0].shape[0]
    ii, jj = _iota2((n, n), 0), _iota2((n, n), 1)
    xor = ii ^ jj
    eye = jnp.where(ii == jj, 1.0, 0.0)
    Ns = [jnp.where((xor >> 3) == 0, -L, 0.0) for L in Ls]
    Xs = [eye + N for N in Ns]
    N2s = [_bdot(N, N) for N in Ns]
    Xs = [X + _bdot(X, N2) for X, N2 in zip(Xs, N2s)]
    N4s = [_bdot(N2, N2) for N2 in N2s]
    Xs = [X + _bdot(X, N4) for X, N4 in zip(Xs, N4s)]
    sh = 4
    while (1 << sh) <= top:
        XBs = [_bdot(X, jnp.where((xor >> (sh - 1)) == 1, L, 0.0)) for X, L in zip(Xs, Ls)]
        Xs = [X - _bdot(XB, X) for X, XB in zip(Xs, XBs)]
        sh += 1
    Rs = [eye - X - _xdot(L, X) for X, L in zip(Xs, Ls)]
    return [X + _bdot(X, R) for X, R in zip(Xs, Rs)]


def _conv_silu(pc, ext_scr, cw_ref, cb_ref):
    ext_scr[SUB:SUB + TC, :] = pc
    y = cb_ref[...]
    for j in range(CONV_W):
        y = y + ext_scr[SUB - (CONV_W - 1) + j:SUB - (CONV_W - 1) + j + TC, :] * cw_ref[j:j + 1, :]
    ext_scr[0:SUB, :] = ext_scr[TC:TC + SUB, :]
    return _silu(y)


def _in_proj_kernel(x_ref, g_ref, w_ref, o_ref, h_scr):
    @pl.when(pl.program_id(1) == 0)
    def _():
        x = x_ref[...]
        r = lax.rsqrt(jnp.mean(x * x, axis=-1, keepdims=True) + EPS)
        h_scr[...] = ((x * r) * g_ref[...]).astype(BF16)

    o_ref[...] = lax.dot_general(h_scr[...], w_ref[...], (((1,), (1,)), ((), ())), preferred_element_type=F32)


def _in_proj(x2d, g, w_perm_t):
    n = x2d.shape[0]
    tm = min(n, 1024)
    tn = 1152
    return pl.pallas_call(
        _in_proj_kernel,
        grid=(n // tm, P_PAD // tn),
        in_specs=[pl.BlockSpec((tm, D_MODEL), lambda i, j: (i, 0)),
                  pl.BlockSpec((1, D_MODEL), lambda i, j: (0, 0)),
                  pl.BlockSpec((tn, D_MODEL), lambda i, j: (j, 0))],
        out_specs=pl.BlockSpec((tm, tn), lambda i, j: (i, j)),
        out_shape=jax.ShapeDtypeStruct((n, P_PAD), F32),
        scratch_shapes=[pltpu.VMEM((tm, D_MODEL), BF16)],
        compiler_params=pltpu.CompilerParams(dimension_semantics=("arbitrary", "arbitrary"), vmem_limit_bytes=VMEM_LIMIT),
        name="in_proj",
    )(x2d, g.reshape(1, D_MODEL), w_perm_t)


def _mix_a_kernel(pc_ref, za_ref, sm_ref, conv0_ref, s0_ref, cw_ref, cb_ref, hp_ref, hpt_ref, gn_ref,
                  tril_ref, triu_ref, bo_ref, o_ref, sout_ref, ext_scr, s_scr, ob_scr, *, t_valid, rows):
    CA = 64

    @pl.when(pl.program_id(1) == 0)
    def _():
        ext_scr[0:SUB, :] = conv0_ref[...]
        s_scr[...] = s0_ref[...]

    u = _conv_silu(_pad_rows(pc_ref[...], TC), ext_scr, cw_ref, cb_ref)
    bo = bo_ref[...]
    q_raw, k_raw, va = u[:, 0:QK_A], u[:, QK_A:2 * QK_A], u[:, 2 * QK_A:CONV_A]
    qq = q_raw * lax.rsqrt(_dot01_right(q_raw * q_raw, bo) + EPS) * (DK_A ** -0.5)
    ka = k_raw * lax.rsqrt(_dot01_right(k_raw * k_raw, bo) + EPS)
    kat = ka.T

    sm = _pad_rows(sm_ref[...], TC)
    smt = sm.T
    hp, hpt = hp_ref[...], hpt_ref[...]
    g_col = -jnp.exp(hp[0:1, 0:H_A]) * _softplus(sm[:, SM_AA:SM_AA + H_A] + hp[1:2, 0:H_A])
    beta = _sigmoid(sm[:, SM_BA:SM_BA + H_A])
    g_row = -jnp.exp(hpt[0:H_A, 0:1]) * _softplus(smt[SM_AA:SM_AA + H_A, :] + hpt[0:H_A, 1:2])
    if t_valid < TC:
        g_col = jnp.where(_iota2(g_col.shape, 0) < t_valid, g_col, 0.0)
        beta = jnp.where(_iota2(beta.shape, 0) < t_valid, beta, 0.0)
        g_row = jnp.where(_iota2(g_row.shape, 1) < t_valid, g_row, 0.0)
    gcum_col = _dot01_left(tril_ref[...], g_col)
    gcum_row = _dot01_right(g_row, triu_ref[...])

    HS = H_A * CA
    ii, jj = _iota2((HS, HS), 0), _iota2((HS, HS), 1)
    same = (ii >> 6) == (jj >> 6)
    incl, strict = same & (ii >= jj), same & (ii > jj)
    nsc = -(-t_valid // CA)
    pre = []
    for sc in range(nsc):
        r0 = sc * CA
        stack = lambda x, w: jnp.concatenate([x[r0:r0 + CA, h * w:(h + 1) * w] for h in range(H_A)], axis=0)
        ks, qs, vs = stack(ka, DK_A), stack(qq, DK_A), stack(va, DV_A)
        bcol, gc = stack(beta, 1), stack(gcum_col, 1)
        gr = jnp.concatenate([gcum_row[h:h + 1, r0:r0 + CA] for h in range(H_A)], axis=1)
        e = jnp.exp(jnp.where(incl, gc - gr, 0.0))
        kb = ks * bcol
        eg = jnp.exp(gc)
        pre.append(dict(r0=r0, gc=gc, qe=qs * eg, low=jnp.where(strict, _bdot_nt(kb, ks) * e, 0.0),
                        rhs=jnp.concatenate([vs * bcol, kb * eg], axis=1),
                        attn=jnp.where(incl, _bdot_nt(qs, ks) * e, 0.0)))
    tinvs = _tri_inv_many([p['low'] for p in pre], CA)
    sols = [_xdot(t, p['rhs']) for t, p in zip(tinvs, pre)]
    hs = lambda x, h: x[h * CA:(h + 1) * CA]
    for p, sol in zip(pre, sols):
        r0, gc = p['r0'], p['gc']
        states = [s_scr[h] for h in range(H_A)]
        v_new = jnp.concatenate([hs(sol, h)[:, 0:DV_A] - _bdot(hs(sol, h)[:, DV_A:], states[h]) for h in range(H_A)], axis=0)
        o = jnp.concatenate([_bdot(hs(p['qe'], h), states[h]) for h in range(H_A)], axis=0) + _bdot(p['attn'], v_new)
        for h in range(H_A):
            gch = hs(gc, h)
            gl = gch[CA - 1:CA, :]
            s_scr[h] = states[h] * jnp.exp(gl) + _bdot(kat[h * DK_A:(h + 1) * DK_A, r0:r0 + CA], hs(v_new, h) * jnp.exp(gl - gch))
            ob_scr[r0:r0 + CA, h * DV_A:(h + 1) * DV_A] = hs(o, h)
    if t_valid <= CA:
        ob_scr[CA:TC, :] = jnp.zeros((TC - CA, W_A), F32)

    o = ob_scr[...]
    o = o * lax.rsqrt(_dot01_right(o * o, bo) * (1.0 / DV_A) + EPS) * gn_ref[...]
    res = o * _silu(_pad_rows(za_ref[...], TC))
    o_ref[...] = res[0:rows]
    sout_ref[...] = s_scr[...]


def _mix_c_kernel(pc_ref, zc_ref, sm_ref, conv0_ref, s0_ref, cw_ref, cb_ref, hp_ref, hpt_ref, gn_ref,
                  tril_ref, triu_ref, bo_ref, o_ref, sout_ref, ext_scr, s_scr, yb_scr, *, t_valid, rows):
    @pl.when(pl.program_id(1) == 0)
    def _():
        ext_scr[0:SUB, :] = conv0_ref[...]
        s_scr[...] = s0_ref[...]

    u = _conv_silu(_pad_rows(pc_ref[...], TC), ext_scr, cw_ref, cb_ref)
    xc = u[:, 0:D_INNER_C]
    bcm = u[:, D_INNER_C:D_INNER_C + G_C * N_C]
    ccm = u[:, D_INNER_C + G_C * N_C:CONV_C]
    bct = bcm.T
    zc = _pad_rows(zc_ref[...], TC)

    sm = _pad_rows(sm_ref[...], TC)
    smt = sm.T
    hp, hpt = hp_ref[...], hpt_ref[...]
    dt_col = _softplus(sm[:, SM_DT:SM_DT + H_C] + hp[3:4, 0:H_C])
    dt_row = _softplus(smt[SM_DT:SM_DT + H_C, :] + hpt[0:H_C, 3:4])
    if t_valid < TC:
        dt_col = jnp.where(_iota2(dt_col.shape, 0) < t_valid, dt_col, 0.0)
        dt_row = jnp.where(_iota2(dt_row.shape, 1) < t_valid, dt_row, 0.0)
    gcum_col = _dot01_left(tril_ref[...], -jnp.exp(hp[2:3, 0:H_C]) * dt_col)
    gcum_row = _dot01_right(-jnp.exp(hpt[0:H_C, 2:3]) * dt_row, triu_ref[...])

    incl = _iota2((TC, TC), 0) >= _iota2((TC, TC), 1)
    hg = H_C // G_C
    for g in range(G_C):
        cc_g = ccm[:, g * N_C:(g + 1) * N_C]
        gram = _bdot_nt(cc_g, bcm[:, g * N_C:(g + 1) * N_C])
        for hh in range(hg):
            h = g * hg + hh
            c0 = h * P_C
            gc = gcum_col[:, h:h + 1]
            gr = gcum_row[h:h + 1, :]
            dec = jnp.where(incl, jnp.exp(jnp.where(incl, gc - gr, 0.0)), 0.0)
            attn = gram * dt_row[h:h + 1, :] * dec
            xh = xc[:, c0:c0 + P_C]
            S = s_scr[h]
            o = jnp.exp(gc) * _bdot(cc_g, S) + _bdot(attn, xh)
            gl = gc[TC - 1:TC, :]
            s_scr[h] = S * jnp.exp(gl) + _bdot(bct[g * N_C:(g + 1) * N_C, :], xh * (dt_col[:, h:h + 1] * jnp.exp(gl - gc)))
            yb_scr[:, c0:c0 + P_C] = (o + hp[4:5, h:h + 1] * xh) * _silu(zc[:, c0:c0 + P_C])

    y = yb_scr[...]
    y = y * lax.rsqrt(_dot01_right(y * y, bo_ref[...]) * (1.0 / (D_INNER_C // G_C)) + EPS) * gn_ref[...]
    o_ref[...] = y[0:rows]
    sout_ref[...] = s_scr[...]


_D_LEVELS = 6


def _mix_d_kernel(dg_ref, s0_ref, lb_ref, gn_ref, tril_ref, sel_ref, bo_ref, o_ref, sout_ref, s_scr, ob_scr, *, t_valid, rows):
    CD = 64
    W = H_D * DK_D

    @pl.when(pl.program_id(1) == 0)
    def _():
        s_scr[...] = s0_ref[...]

    dg = _pad_rows(dg_ref[...], TC)
    q = _silu(dg[:, 0:W])
    fr = dg[:, W:2 * W]
    v = dg[:, 2 * W:3 * W]
    gd = dg[:, 3 * W:4 * W]
    lb = lb_ref[...]
    logf = jnp.log(jnp.maximum(lb + (1.0 - lb) * _sigmoid(fr), TINY))
    kd = (1.0 - lb) * _sigmoid(-fr)
    if t_valid < TC:
        ok = _iota2((TC, W), 0) < t_valid
        logf = jnp.where(ok, logf, 0.0)
        kd = jnp.where(ok, kd, 0.0)
    gcum = _dot01_left(tril_ref[...], logf)
    gm = _dot01_left(sel_ref[...], gcum)
    vt = v.T
    qdec = q * jnp.exp(gcum)

    ii, jj = _iota2((CD, CD), 0), _iota2((CD, CD), 1)
    lhs, rhs, msk = [q], [kd], [ii == jj]
    for lv in range(1, _D_LEVELS + 1):
        gml = gm[(lv - 1) * TC:lv * TC, :]
        lhs.append(q * jnp.exp(jnp.minimum(gcum - gml, 0.0)))
        rhs.append(kd * jnp.exp(jnp.minimum(gml - gcum, 0.0)))
        half = 1 << (lv - 1)
        msk.append(((ii >> lv) == (jj >> lv)) & ((ii & half) != 0) & ((jj & half) == 0))

    for sc in range(TC // CD):
        r0 = sc * CD
        for h in range(H_D):
            c0 = h * DK_D
            attn = jnp.zeros((CD, CD), F32)
            for a, b, m in zip(lhs, rhs, msk):
                attn = attn + jnp.where(m, _bdot_nt(a[r0:r0 + CD, c0:c0 + DK_D], b[r0:r0 + CD, c0:c0 + DK_D]), 0.0)
            st = s_scr[h]
            ob_scr[r0:r0 + CD, c0:c0 + DV_D] = _bdot_nt(qdec[r0:r0 + CD, c0:c0 + DK_D], st) + _bdot(attn, v[r0:r0 + CD, c0:c0 + DV_D])
            gch = gcum[r0:r0 + CD, c0:c0 + DK_D]
            gl = gch[CD - 1:CD, :]
            s_scr[h] = st * jnp.exp(gl) + _bdot(vt[c0:c0 + DV_D, r0:r0 + CD], kd[r0:r0 + CD, c0:c0 + DK_D] * jnp.exp(gl - gch))

    o = ob_scr[...]
    o = o * lax.rsqrt(_dot01_right(o * o, bo_ref[...]) * (1.0 / DV_D) + EPS) * gn_ref[...]
    res = o * _silu(gd)
    o_ref[...] = res[0:rows]
    sout_ref[...] = s_scr[...]


def _np_consts():
    i = np.arange(TC)
    same64 = (i[:, None] // 64) == (i[None, :] // 64)
    tril2 = (same64 & (i[:, None] >= i[None, :])).astype(np.float32)
    tril1 = (i[:, None] >= i[None, :]).astype(np.float32)
    sel = np.zeros((_D_LEVELS * TC, TC), np.float32)
    for lv in range(1, _D_LEVELS + 1):
        m = ((i >> lv) << lv) + (1 << (lv - 1)) - 1
        sel[(lv - 1) * TC + i, m] = 1.0
    bo = lambda n, w: ((np.arange(n)[:, None] // w) == (np.arange(n)[None, :] // w)).astype(np.float32)
    return dict(tril2=tril2, triu2=tril2.T.copy(), tril1=tril1, triu1=tril1.T.copy(), sel=sel,
                bo64=bo(256, 64), bo256=bo(D_INNER_C, D_INNER_C // G_C))


def _bucket_starts():
    max_exact = N_BUCKETS // 2
    d = np.arange(0, 4 * MAX_DISTANCE)
    lr = np.log(np.maximum(d, 1).astype(np.float32) / max_exact) / math.log(MAX_DISTANCE / max_exact)
    large = np.minimum(max_exact + (np.maximum(lr, 0.0) * (N_BUCKETS - max_exact)).astype(np.int32), N_BUCKETS - 1)
    bucket = np.where(d < max_exact, d, large)
    return [int(np.argmax(bucket == b)) for b in range(N_BUCKETS)]


_BUCKET_STARTS = _bucket_starts()
_SAT_DIST = _BUCKET_STARTS[N_BUCKETS - 1]


def _bias_from_dist(d, rb_ref, h):
    out = jnp.full(d.shape, rb_ref[N_BUCKETS - 1, h], F32)
    for b in range(N_BUCKETS - 2, -1, -1):
        out = jnp.where(d < _BUCKET_STARTS[b + 1], rb_ref[b, h], out)
    return out


def _sortable(x):
    b = lax.bitcast_convert_type(x + 0.0, I32)
    return jnp.where(b < 0, b ^ 0x7FFFFFFF, b)


_NEG_KEY = int(np.array([NEG], np.float32).view(np.int32)[0]) ^ 0x7FFFFFFF
_INT_MIN = -2 ** 31


def _selected_i32(k, idx, thr, xthr):
    s = jnp.where(k > thr, 1, jnp.where(k == thr, jnp.where(idx <= xthr, 1, 0), 0))
    return jnp.where(k == _NEG_KEY, 0, s)


def _selected(k, idx, thr, xthr):
    return _selected_i32(k, idx, thr, xthr) > 0


def _const_spec(shape):
    nd = len(shape)
    return pl.BlockSpec(shape, lambda *_: (0,) * nd)


def _mix_call(kernel_fn, name, P, B, T, seq_specs, batch_args, const_args, out_w, n_state, scratch):
    rows = min(T, TC)
    nc = max(T // TC, 1)
    assert rows * nc == T
    in_specs = [pl.BlockSpec((rows, w), functools.partial(lambda b, c, ci: (b * nc + c, ci), ci=off // w)) for off, w in seq_specs]
    args = [P] * len(seq_specs)
    for a in batch_args:
        nd = a.ndim
        in_specs.append(pl.BlockSpec((None,) + a.shape[1:], functools.partial(lambda b, c, nd: (b,) + (0,) * (nd - 1), nd=nd)))
        args.append(a)
    for a in const_args:
        in_specs.append(_const_spec(a.shape))
        args.append(a)
    st_shape = (B, n_state, 64, 64)
    return pl.pallas_call(
        functools.partial(kernel_fn, t_valid=rows, rows=rows),
        grid=(B, nc),
        in_specs=in_specs,
        out_specs=[pl.BlockSpec((rows, out_w), lambda b, c: (b * nc + c, 0)),
                   pl.BlockSpec((None, n_state, 64, 64), lambda b, c: (b, 0, 0, 0))],
        out_shape=[jax.ShapeDtypeStruct((B * T, out_w), F32), jax.ShapeDtypeStruct(st_shape, F32)],
        scratch_shapes=scratch,
        compiler_params=pltpu.CompilerParams(dimension_semantics=("arbitrary", "arbitrary"), vmem_limit_bytes=VMEM_LIMIT),
        name=name,
    )(*args)


CU = 4


def _dsa_prompt_kernel(rb_ref, qb_ref, qi_ref, smq_ref, kv_ref, sma_ref, o_ref,
                       kvt_scr, key_scr, lg_scr, bias_scr, acc_scr, *, n_sel, idx_bits):
    j = pl.program_id(1)
    nch = j + 1

    @pl.when((pl.program_id(0) == 0) & (j == 0))
    def _():
        srow, tcol = _iota2((QB, QB), 0), _iota2((QB, QB), 1)
        for h in range(H_B):
            bias_scr[h, 0] = jnp.full((QB, QB), rb_ref[N_BUCKETS - 1, h], F32)
            bias_scr[h, 1] = _bias_from_dist(tcol - srow + QB, rb_ref, h)
            bias_scr[h, 2] = _bias_from_dist(tcol - srow, rb_ref, h)

    @pl.when(j == 0)
    def _():
        def tr(c, _):
            r = pl.multiple_of(c * QB, QB)
            kvt_scr[c] = kv_ref[pl.ds(r, QB), :].T[DH_B:2 * DH_B, :].astype(BF16)
            return 0
        lax.fori_loop(0, kv_ref.shape[0] // QB, tr, 0)

    lanes = lambda rows, w: jnp.concatenate([rows[h * w:(h + 1) * w, :] for h in range(H_B)], axis=1)
    w4 = lanes(smq_ref[...].T[SM_WI:SM_WI + H_IDX, :], 1) * (H_IDX ** -0.5 * D_IDX ** -0.5)
    qi_rhs = lanes(qi_ref[...].T, D_IDX).astype(BF16)
    qb_rhs = (lanes(qb_ref[...].T, DH_B) * (DH_B ** -0.5)).astype(BF16)
    srow, tcol = _iota2((QB, QB), 0), _iota2((QB, QB), 1)
    heads = lambda x: [x[:, h * QB:(h + 1) * QB] for h in range(H_B)]

    ngrp = (nch + (CU - 1)) // CU

    def score_body(g, _):
        for u in range(CU):
            r = pl.multiple_of((g * CU + u) * QB, QB)
            ki = sma_ref[pl.ds(r, QB), :][:, SM_KI:SM_KI + D_IDX].astype(BF16)
            s = jnp.maximum(jnp.dot(ki, qi_rhs, preferred_element_type=F32), 0.0) * w4
            sh = heads(s)
            sc = (sh[0] + sh[1]) + (sh[2] + sh[3])
            sc = jnp.where(srow + r <= tcol + j * QB, sc, NEG)
            key_scr[pl.ds(r, QB), :] = _sortable(sc)
        return 0

    lax.fori_loop(0, ngrp, score_body, 0)

    def count(pred):
        def body(g, acc):
            for u in range(CU):
                r = pl.multiple_of((g * CU + u) * QB, QB)
                acc = acc + pred(key_scr[pl.ds(r, QB), :], r).reshape(QB // SUB, SUB, QB).sum(axis=0)
            return acc
        return lax.fori_loop(0, ngrp, body, jnp.zeros((SUB, QB), I32)).sum(axis=0, keepdims=True)

    cnt_ge = lambda t: count(lambda k, r: jnp.where(k >= t, 1, 0))
    c0 = cnt_ge(jnp.zeros((1, QB), I32))
    thr = jnp.where(c0 >= n_sel, 0, _INT_MIN).astype(I32)
    cthr = jnp.where(c0 >= n_sel, c0, ngrp * (CU * QB))

    def bit_body(i, carry):
        t, ct = carry
        trial = t | jnp.left_shift(jnp.int32(1), 30 - i)
        cnt = cnt_ge(trial)
        ok = cnt >= n_sel
        return jnp.where(ok, trial, t), jnp.where(ok, cnt, ct)

    thr, cthr = lax.fori_loop(0, 31, bit_body, (thr, cthr))

    def tie_search():
        quota = n_sel - count(lambda k, r: jnp.where(k > thr, 1, 0))

        def idx_body(i, x):
            trial = x | jnp.left_shift(jnp.int32(1), idx_bits - 1 - i)
            below = count(lambda k, r: jnp.where(k == thr, jnp.where(srow + r < trial, 1, 0), 0))
            return jnp.where(below < quota, trial, x)

        return lax.fori_loop(0, idx_bits, idx_body, jnp.zeros((1, QB), I32))

    xthr = lax.cond(jnp.max(cthr) > n_sel, tie_search, lambda: jnp.full((1, QB), (1 << idx_bits) - 1, I32))

    def logit_body(g, ms):
        ms = list(ms)
        for u in range(CU):
            c = g * CU + u
            r = pl.multiple_of(c * QB, QB)
            kc = kv_ref[pl.ds(r, QB), :][:, 0:DH_B].astype(BF16)
            l4 = heads(jnp.dot(kc, qb_rhs, preferred_element_type=F32))
            sel = _selected(key_scr[pl.ds(r, QB), :], srow + r, thr, xthr)
            bidx = jnp.clip(c - j + 2, 0, 2)
            for h in range(H_B):
                l = jnp.where(sel, l4[h] + bias_scr[h, bidx], NEG)
                lg_scr[h, pl.ds(r, QB), :] = l
                ms[h] = jnp.maximum(ms[h], l.max(axis=0, keepdims=True))
        return tuple(ms)

    ms = lax.fori_loop(0, ngrp, logit_body, tuple(jnp.full((1, QB), NEG, F32) for _ in range(H_B)))
    acc_scr[...] = jnp.zeros(acc_scr.shape, F32)

    def pv_body(g, ss):
        ss = list(ss)
        acc = acc_scr[...]
        for u in range(CU):
            c = g * CU + u
            r = pl.multiple_of(c * QB, QB)
            es = [jnp.exp(lg_scr[h, pl.ds(r, QB), :] - ms[h]) for h in range(H_B)]
            acc = acc + jnp.dot(kvt_scr[c], jnp.concatenate(es, axis=1).astype(BF16), preferred_element_type=F32)
            for h in range(H_B):
                ss[h] = ss[h] + es[h].sum(axis=0, keepdims=True)
        acc_scr[...] = acc
        return tuple(ss)

    ss = lax.fori_loop(0, ngrp, pv_body, tuple(jnp.zeros((1, QB), F32) for _ in range(H_B)))
    acc = heads(acc_scr[...])
    ot = jnp.concatenate([acc[h] / ss[h] for h in range(H_B)], axis=0)
    o_ref[...] = ot.T


def _dsa_prompt(P, rel_bias, B, T):
    nb = T // QB
    assert nb % CU == 0
    n_sel = min(TOPK_MAX, T // 4)
    idx_bits = max(1, int(math.ceil(math.log2(T))))
    return pl.pallas_call(
        functools.partial(_dsa_prompt_kernel, n_sel=n_sel, idx_bits=idx_bits),
        grid=(B, nb),
        in_specs=[pl.BlockSpec(memory_space=pltpu.SMEM),
                  pl.BlockSpec((QB, W_B), lambda b, j: (b * nb + j, OFF_QB // W_B)),
                  pl.BlockSpec((QB, LANE), lambda b, j: (b * nb + j, OFF_QI // LANE)),
                  pl.BlockSpec((QB, LANE), lambda b, j: (b * nb + j, OFF_SM // LANE)),
                  pl.BlockSpec((T, LANE), lambda b, j: (b, OFF_KV // LANE)),
                  pl.BlockSpec((T, LANE), lambda b, j: (b, OFF_SM // LANE))],
        out_specs=pl.BlockSpec((QB, W_B), lambda b, j: (b * nb + j, 0)),
        out_shape=jax.ShapeDtypeStruct((B * T, W_B), F32),
        scratch_shapes=[pltpu.VMEM((nb, DH_B, QB), BF16), pltpu.VMEM((T, QB), I32), pltpu.VMEM((H_B, T, QB), F32),
                        pltpu.VMEM((H_B, 3, QB, QB), F32), pltpu.VMEM((DH_B, H_B * QB), F32)],
        compiler_params=pltpu.CompilerParams(dimension_semantics=("arbitrary", "arbitrary"), vmem_limit_bytes=VMEM_LIMIT),
        name="dsa_prompt",
    )(rel_bias, P, P, P, P, P)


PG = 16
GW = PG * PAGE_SIZE


def _stack_heads(x, w):
    return jnp.concatenate([x[:, h * w:(h + 1) * w] for h in range(x.shape[1] // w)], axis=0)


def _dsa_s_scores_kernel(pt_ref, qi_ref, sm_ref, *refs, ng, ds):
    page_refs, o_ref = refs[:PG], refs[PG]
    g = pl.program_id(1)
    sm = sm_ref[...]
    qst = _stack_heads(qi_ref[...], D_IDX)
    wcol = jnp.concatenate([sm[:, SM_WI + h:SM_WI + h + 1] for h in range(H_IDX)], axis=0) * (H_IDX ** -0.5 * D_IDX ** -0.5)

    def tile_scores(qk):
        s = jnp.maximum(qk, 0.0) * wcol
        out = s[0:ds]
        for h in range(1, H_IDX):
            out = out + s[h * ds:(h + 1) * ds]
        return out

    @pl.when(g < ng)
    def _():
        for i in range(PG):
            o_ref[:, i * PAGE_SIZE:(i + 1) * PAGE_SIZE] = tile_scores(_bdot(qst, page_refs[i][...]))

    @pl.when(g == ng)
    def _():
        sc = tile_scores(_bdot_nt(qst, _pad_rows(sm[:, SM_KI:SM_KI + D_IDX], PAGE_SIZE)))
        lane, row = _iota2(sc.shape, 1), _iota2(sc.shape, 0)
        o_ref[:, 0:PAGE_SIZE] = jnp.where(lane <= row, sc, NEG)
        o_ref[:, PAGE_SIZE:GW] = jnp.full((ds, GW - PAGE_SIZE), NEG, F32)


def _dsa_s_attn_kernel(pt_ref, rb_ref, sc_ref, qb_ref, kvn_ref, *refs, ng, ds, n_sel, idx_bits):
    k_refs, v_refs = refs[:PG], refs[PG:2 * PG]
    o_ref, key_scr, thr_scr, bias_scr, m_scr, l_scr, acc_scr = refs[2 * PG:]
    g = pl.program_id(1)
    ntile = (ng + 1) * PG
    hr = H_B * ds

    @pl.when(g == 0)
    def _():
        for i in range(ntile):
            key_scr[:, i * LANE:(i + 1) * LANE] = _sortable(sc_ref[:, i * LANE:(i + 1) * LANE])
        lane = _iota2((ds, LANE), 1)

        def count(pred):
            acc = jnp.zeros((ds, LANE), I32)
            for i in range(ntile):
                acc = acc + pred(key_scr[:, i * LANE:(i + 1) * LANE], i * LANE)
            return acc.sum(axis=1, keepdims=True)

        cnt_ge = lambda t: count(lambda k, off: jnp.where(k >= t, 1, 0))
        thr = jnp.where(cnt_ge(jnp.zeros((ds, 1), I32)) >= n_sel, 0, _INT_MIN).astype(I32)

        def bit_body(i, t):
            trial = t | jnp.left_shift(jnp.int32(1), 30 - i)
            return jnp.where(cnt_ge(trial) >= n_sel, trial, t)

        thr = lax.fori_loop(0, 31, bit_body, thr)
        quota = n_sel - count(lambda k, off: jnp.where(k > thr, 1, 0))

        def idx_body(i, x):
            trial = x | jnp.left_shift(jnp.int32(1), idx_bits - 1 - i)
            below = count(lambda k, off: jnp.where(k == thr, jnp.where(lane + off < trial, 1, 0), 0))
            return jnp.where(below < quota, trial, x)

        xthr = lax.fori_loop(0, idx_bits, idx_body, jnp.zeros((ds, 1), I32))
        thr_scr[0] = jnp.broadcast_to(thr, (ds, LANE))
        thr_scr[1] = jnp.broadcast_to(xthr, (ds, LANE))

        qrow = _iota2((ds, LANE), 0)
        for h in range(H_B):
            bias_scr[0, h * ds:(h + 1) * ds, :] = _bias_from_dist(PAGE_SIZE + qrow - lane, rb_ref, h)
            bias_scr[1, h * ds:(h + 1) * ds, :] = _bias_from_dist(qrow - lane, rb_ref, h)
            bias_scr[2, h * ds:(h + 1) * ds, :] = jnp.full((ds, LANE), rb_ref[N_BUCKETS - 1, h], F32)
        m_scr[...] = jnp.full(m_scr.shape, NEG, F32)
        l_scr[...] = jnp.zeros(l_scr.shape, F32)
        acc_scr[...] = jnp.zeros(acc_scr.shape, F32)

    qst = _stack_heads(qb_ref[...], DH_B) * (DH_B ** -0.5)
    thr, xthr = thr_scr[0], thr_scr[1]

    def select(tile):
        k = key_scr[:, pl.ds(pl.multiple_of(tile * LANE, LANE), LANE)]
        idx = _iota2((ds, LANE), 1) + tile * LANE
        sel = _selected_i32(k, idx, thr, xthr)
        return jnp.concatenate([sel] * H_B, axis=0) > 0

    def update(lg, sel, pv):
        m_old = m_scr[...]
        m_new = jnp.maximum(m_old, jnp.where(sel, lg, NEG).max(axis=1, keepdims=True))
        p = jnp.where(sel, jnp.exp(lg - m_new), 0.0)
        corr = jnp.exp(m_old - m_new)
        l_scr[...] = l_scr[...] * corr + p.sum(axis=1, keepdims=True)
        acc_scr[...] = acc_scr[...] * corr + pv(p)
        m_scr[...] = m_new

    @pl.when(g < ng)
    def _():
        lgs, sels = [], []
        for i in range(PG):
            lg = _bdot(qst, k_refs[i][...])
            if i == PG - 1:
                lg = lg + jnp.where(g == ng - 1, bias_scr[0], bias_scr[2])
            else:
                lg = lg + bias_scr[2]
            lgs.append(lg)
            sels.append(select(g * PG + i))
        vt = jnp.concatenate([v_refs[i][...] for i in range(PG)], axis=1)
        update(jnp.concatenate(lgs, axis=1), jnp.concatenate(sels, axis=1), lambda p: _bdot_nt(p, vt))

    @pl.when(g == ng)
    def _():
        kvn = _pad_rows(kvn_ref[...], PAGE_SIZE)
        lg = _bdot_nt(qst, kvn[:, 0:DH_B]) + bias_scr[1]
        update(lg, select(ng * PG), lambda p: _bdot(p, kvn[:, DH_B:2 * DH_B]))
        o = acc_scr[...] / l_scr[...]
        o_ref[...] = jnp.concatenate([o[h * ds:(h + 1) * ds] for h in range(H_B)], axis=1)


def _dsa_sample(P, rel_bias, cache_k, cache_v, cache_kidx, page_table, layer, B, DS):
    n_pages = page_table.shape[1]
    past = n_pages * PAGE_SIZE
    assert n_pages % PG == 0
    ng = n_pages // PG
    n_sel = min(TOPK_MAX, (past + DS) // 4)
    idx_bits = int(math.ceil(math.log2(past + LANE)))
    s_pad = (ng + 1) * GW

    def page_spec(width, i):
        return pl.BlockSpec((None, None, width, PAGE_SIZE),
                            lambda b, g, pt: (layer, pt[b, jnp.minimum(g, ng - 1) * PG + i], 0, 0))

    row = lambda ci: (lambda b, g, pt: (b, ci))
    cp = pltpu.CompilerParams(dimension_semantics=("arbitrary", "arbitrary"), vmem_limit_bytes=VMEM_LIMIT)
    scores = pl.pallas_call(
        functools.partial(_dsa_s_scores_kernel, ng=ng, ds=DS),
        grid_spec=pltpu.PrefetchScalarGridSpec(
            num_scalar_prefetch=1, grid=(B, ng + 1),
            in_specs=[pl.BlockSpec((DS, LANE), row(OFF_QI // LANE)), pl.BlockSpec((DS, LANE), row(OFF_SM // LANE))]
                     + [page_spec(D_IDX, i) for i in range(PG)],
            out_specs=pl.BlockSpec((None, DS, GW), lambda b, g, pt: (b, 0, g))),
        out_shape=jax.ShapeDtypeStruct((B, DS, s_pad), F32),
        compiler_params=cp, name="dsa_sample_scores",
    )(page_table, P, P, *([cache_kidx] * PG))

    return pl.pallas_call(
        functools.partial(_dsa_s_attn_kernel, ng=ng, ds=DS, n_sel=n_sel, idx_bits=idx_bits),
        grid_spec=pltpu.PrefetchScalarGridSpec(
            num_scalar_prefetch=1, grid=(B, ng + 1),
            in_specs=[pl.BlockSpec(memory_space=pltpu.SMEM),
                      pl.BlockSpec((None, DS, s_pad), lambda b, g, pt: (b, 0, 0)),
                      pl.BlockSpec((DS, W_B), row(OFF_QB // W_B)), pl.BlockSpec((DS, LANE), row(OFF_KV // LANE))]
                     + [page_spec(DH_B, i) for i in range(PG)] + [page_spec(DH_B, i) for i in range(PG)],
            out_specs=pl.BlockSpec((DS, W_B), lambda b, g, pt: (b, 0)),
            scratch_shapes=[pltpu.VMEM((DS, s_pad), I32), pltpu.VMEM((2, DS, LANE), I32), pltpu.VMEM((3, H_B * DS, LANE), F32),
                            pltpu.VMEM((H_B * DS, 1), F32), pltpu.VMEM((H_B * DS, 1), F32), pltpu.VMEM((H_B * DS, DH_B), F32)]),
        out_shape=jax.ShapeDtypeStruct((B * DS, W_B), F32),
        compiler_params=cp, name="dsa_sample_attn",
    )(page_table, rel_bias, scores, P, P, *([cache_k] * PG), *([cache_v] * PG))


def _merge_kernel(x_ref, oa_ref, ob_ref, oc_ref, od_ref, gate_ref, wb_ref, wo_ref, o_ref):
    m = None
    r0 = 0
    for br, ref in enumerate((oa_ref, ob_ref, oc_ref, od_ref)):
        w = ref.shape[1]
        t = _sigmoid(gate_ref[:, br * D_MODEL:(br + 1) * D_MODEL]) * jnp.dot(ref[...].astype(BF16), wb_ref[r0:r0 + w, :], preferred_element_type=F32)
        m = t if m is None else m + t
        r0 += w
    o_ref[...] = x_ref[...] + jnp.dot(m.astype(BF16), wo_ref[...], preferred_element_type=F32)


def _merge(x2d, oa, ob, oc, od, P, wb, wo):
    n = x2d.shape[0]
    tm = min(n, 256)
    row = lambda i: (i, 0)
    return pl.pallas_call(
        _merge_kernel,
        grid=(n // tm,),
        in_specs=[pl.BlockSpec((tm, D_MODEL), row), pl.BlockSpec((tm, W_A), row), pl.BlockSpec((tm, W_B), row),
                  pl.BlockSpec((tm, W_C), row), pl.BlockSpec((tm, W_D), row),
                  pl.BlockSpec((tm, N_BRANCH * D_MODEL), lambda i: (i, OFF_GATE // (N_BRANCH * D_MODEL))),
                  _const_spec((MIX_W, D_MODEL)), _const_spec((D_MODEL, D_MODEL))],
        out_specs=pl.BlockSpec((tm, D_MODEL), row),
        out_shape=jax.ShapeDtypeStruct((n, D_MODEL), F32),
        compiler_params=pltpu.CompilerParams(dimension_semantics=("arbitrary",), vmem_limit_bytes=VMEM_LIMIT),
        name="merge",
    )(x2d, oa, ob, oc, od, P, wb, wo)


def _mlp_kernel(x_ref, g_ref, wu_ref, wd_ref, gf_ref, *refs, final):
    if final:
        o_ref, y_ref, h_scr, acc_scr = refs
    else:
        o_ref, h_scr, acc_scr = refs
    f = pl.program_id(1)

    @pl.when(f == 0)
    def _():
        x = x_ref[...]
        r = lax.rsqrt(jnp.mean(x * x, axis=-1, keepdims=True) + EPS)
        h_scr[...] = ((x * r) * g_ref[...]).astype(BF16)
        acc_scr[...] = jnp.zeros(acc_scr.shape, F32)

    a = jnp.maximum(jnp.dot(h_scr[...], wu_ref[...], preferred_element_type=F32), 0.0)
    acc_scr[...] += jnp.dot((a * a).astype(BF16), wd_ref[...], preferred_element_type=F32)

    @pl.when(f == pl.num_programs(1) - 1)
    def _():
        out = x_ref[...] + acc_scr[...]
        o_ref[...] = out
        if final:
            r = lax.rsqrt(jnp.mean(out * out, axis=-1, keepdims=True) + EPS)
            y_ref[...] = (out * r) * gf_ref[...]


def _mlp(x2d, g, wu, wd, gf, final):
    n = x2d.shape[0]
    tm = min(n, 512)
    tf = 1024
    row = lambda i, f: (i, 0)
    out_spec = pl.BlockSpec((tm, D_MODEL), row)
    shp = jax.ShapeDtypeStruct((n, D_MODEL), F32)
    return pl.pallas_call(
        functools.partial(_mlp_kernel, final=final),
        grid=(n // tm, D_FF // tf),
        in_specs=[pl.BlockSpec((tm, D_MODEL), row), _const_spec((1, D_MODEL)),
                  pl.BlockSpec((D_MODEL, tf), lambda i, f: (0, f)), pl.BlockSpec((tf, D_MODEL), lambda i, f: (f, 0)),
                  _const_spec((1, D_MODEL))],
        out_specs=[out_spec, out_spec] if final else [out_spec],
        out_shape=[shp, shp] if final else [shp],
        scratch_shapes=[pltpu.VMEM((tm, D_MODEL), BF16), pltpu.VMEM((tm, D_MODEL), F32)],
        compiler_params=pltpu.CompilerParams(dimension_semantics=("arbitrary", "arbitrary"), vmem_limit_bytes=VMEM_LIMIT),
        name="mlp_final" if final else "mlp",
    )(x2d, g.reshape(1, D_MODEL), wu, wd, gf.reshape(1, D_MODEL))


def _layer(x, conv_state, s_delta, s_ssm, s_hgrn, lw, consts, sample_ctx, final):
    B, T, _ = x.shape
    x2 = x.reshape(B * T, D_MODEL)
    P = _in_proj(x2, lw['norm_mix'], lw['w_in'])

    def conv0(lo, hi):
        if conv_state is None:
            return jnp.zeros((B, SUB, hi - lo), F32)
        return jnp.concatenate([jnp.zeros((B, SUB - (CONV_W - 1), hi - lo), F32), conv_state[:, :, lo:hi]], axis=1)

    zeros_state = lambda n: jnp.zeros((B, n, 64, 64), F32)
    c = consts
    half = CONV_CH // 2
    assert CONV_A == half and OFF_PC % half == 0
    oa, s_a = _mix_call(
        _mix_a_kernel, "mix_a", P, B, T,
        [(OFF_PC, half), (OFF_ZA, W_A), (OFF_SM, LANE)],
        [conv0(0, CONV_A), zeros_state(H_A) if s_delta is None else s_delta],
        [lw['conv_w'][:, 0:CONV_A], lw['conv_b'][None, 0:CONV_A], lw['hp'], lw['hpt'], lw['gn_a'], c['tril2'], c['triu2'], c['bo64']],
        W_A, H_A,
        [pltpu.VMEM((TC + SUB, CONV_A), F32), pltpu.VMEM((H_A, DK_A, DV_A), F32), pltpu.VMEM((TC, W_A), F32)])
    oc, s_c = _mix_call(
        _mix_c_kernel, "mix_c", P, B, T,
        [(OFF_PC + half, half), (OFF_ZC, W_C), (OFF_SM, LANE)],
        [conv0(CONV_A, CONV_CH), zeros_state(H_C) if s_ssm is None else s_ssm],
        [lw['conv_w'][:, CONV_A:], lw['conv_b'][None, CONV_A:], lw['hp'], lw['hpt'], lw['gn_c'], c['tril1'], c['triu1'], c['bo256']],
        W_C, H_C,
        [pltpu.VMEM((TC + SUB, CONV_C), F32), pltpu.VMEM((H_C, N_C, P_C), F32), pltpu.VMEM((TC, W_C), F32)])
    od, s_dt = _mix_call(
        _mix_d_kernel, "mix_d", P, B, T,
        [(OFF_D, 4 * W_D)],
        [zeros_state(H_D) if s_hgrn is None else jnp.swapaxes(s_hgrn, -1, -2)],
        [lw['lb'], lw['gn_d'], c['tril2'], c['sel'], c['bo64']],
        W_D, H_D,
        [pltpu.VMEM((H_D, DV_D, DK_D), F32), pltpu.VMEM((TC, W_D), F32)])
    if sample_ctx is None:
        ob = _dsa_prompt(P, lw['rel_bias'], B, T)
    else:
        ob = _dsa_sample(P, lw['rel_bias'], sample_ctx['cache_k'], sample_ctx['cache_v'], sample_ctx['cache_kidx'],
                         sample_ctx['page_table'], sample_ctx['layer'], B, T)
    x1 = _merge(x2, oa, ob, oc, od, P, lw['w_branch'], lw['w_out'])
    outs = _mlp(x1, lw['norm_mlp'], lw['w_up'], lw['w_down'], lw['norm_final'], final)
    xo = outs[0].reshape(B, T, D_MODEL)
    y = outs[1].reshape(B, T, D_MODEL) if final else None
    P3 = P.reshape(B, T, P_PAD)
    states = (P3[:, :, OFF_KV:OFF_KV + DH_B], P3[:, :, OFF_KV + DH_B:OFF_KV + 2 * DH_B], P3[:, :, OFF_SM + SM_KI:OFF_SM + SM_KI + D_IDX],
              P3[:, T - (CONV_W - 1):, OFF_PC:OFF_PC + CONV_CH], s_a, s_c, jnp.swapaxes(s_dt, -1, -2))
    return xo, y, states


def kernel(x_prompt, x_sample, cache_k, cache_v, cache_kidx, state_conv, state_delta, state_ssm, state_hgrn, page_table, norm_mix, w_in, conv_w, conv_b, a_log_a, dt_bias_a, gnorm_a, rel_bias, a_log_c, dt_bias_c, d_skip_c, gnorm_c, hgrn_gamma, gnorm_d, w_branch, w_out, norm_mlp, w_up, w_down, norm_final):
    depth = w_in.shape[0]
    npc = _np_consts()
    consts = {k: jnp.asarray(v, BF16) for k, v in npc.items()}
    pg = jax.nn.softmax(hgrn_gamma.astype(F32), axis=0)
    lower_bounds = jnp.cumsum(pg, axis=0) - pg[0]
    cache_kt, cache_vt, cache_kit = (jnp.swapaxes(c, 2, 3) for c in (cache_k, cache_v, cache_kidx))
    yp, ys = x_prompt, x_sample
    new_p, new_s = [], []
    for l in range(depth):
        hp = jnp.zeros((SUB, LANE), F32)
        hp = hp.at[0, 0:H_A].set(a_log_a[l]).at[1, 0:H_A].set(dt_bias_a[l])
        hp = hp.at[2, 0:H_C].set(a_log_c[l]).at[3, 0:H_C].set(dt_bias_c[l]).at[4, 0:H_C].set(d_skip_c[l])
        lw = dict(norm_mix=norm_mix[l], w_in=_permute_w_in(w_in[l]), conv_w=conv_w[l], conv_b=conv_b[l],
                  hp=hp, hpt=hp.T, gn_a=jnp.tile(gnorm_a[l], H_A)[None, :], gn_c=gnorm_c[l][None, :],
                  gn_d=jnp.tile(gnorm_d[l], H_D)[None, :], lb=lower_bounds[l][None, :], rel_bias=rel_bias,
                  w_branch=w_branch[l].astype(BF16), w_out=w_out[l].astype(BF16), norm_mlp=norm_mlp[l],
                  w_up=w_up[l].astype(BF16), w_down=w_down[l].astype(BF16), norm_final=norm_final)
        final = l == depth - 1
        yp, yp_n, st_p = _layer(yp, None, None, None, None, lw, consts, None, final)
        ctx = dict(cache_k=cache_kt, cache_v=cache_vt, cache_kidx=cache_kit, page_table=page_table, layer=l)
        ys, ys_n, st_s = _layer(ys, state_conv[l], state_delta[l], state_ssm[l], state_hgrn[l], lw, consts, ctx, final)
        new_p.append(st_p)
        new_s.append(st_s)
    stack = lambda per_layer: [jnp.stack(items) for items in zip(*per_layer)]
    return (yp_n, ys_n, *stack(new_p), *stack(new_s))
```

```python
import functools
import math

import numpy as np
import jax
import jax.numpy as jnp
from jax import lax
from jax.experimental import pallas as pl
from jax.experimental.pallas import tpu as pltpu

F32 = jnp.float32
BF16 = jnp.bfloat16
I32 = jnp.int32
I16 = jnp.int16

D_MODEL = 1024
PAGE_SIZE = 128
EPS = 1e-6
NEG = -1e30
TINY = 1e-30
CONV_W = 4
D_FF = 4 * D_MODEL
N_BRANCH = 4
H_A, DK_A, DV_A = 4, 64, 64
H_B, DH_B = 4, 64
H_IDX, D_IDX = 4, 32
TOPK_MAX = 256
N_BUCKETS = 32
MAX_DISTANCE = 128
H_C, P_C, N_C, G_C = 8, 64, 64, 2
H_D, DK_D, DV_D = 4, 64, 64
QK_A = H_A * DK_A
CONV_A = 2 * QK_A + H_A * DV_A
D_INNER_C = H_C * P_C
CONV_C = D_INNER_C + 2 * G_C * N_C
CONV_CH = CONV_A + CONV_C
W_A, W_B, W_C, W_D = H_A * DV_A, H_B * DH_B, D_INNER_C, H_D * DV_D
MIX_W = W_A + W_B + W_C + W_D
IN_WIDTHS = (CONV_CH, W_A, H_A, H_A, W_B, DH_B, DH_B, H_IDX * D_IDX, D_IDX, H_IDX, D_INNER_C, H_C,
             H_D * DK_D, H_D * DK_D, W_D, W_D, N_BRANCH * D_MODEL)
P_IN = sum(IN_WIDTHS)

LANE = 128
SUB = 8
TC = 128
QB = 128
VMEM_LIMIT = 48 * 1024 * 1024

OFF_GATE, OFF_D, OFF_ZC, OFF_ZA, OFF_QB, OFF_PC, OFF_KV, OFF_QI, OFF_SM = 0, 4096, 5120, 5632, 5888, 6144, 7680, 7808, 7936
P_PAD = 8064
SM_KI, SM_WI, SM_AA, SM_BA, SM_DT = 0, 32, 36, 40, 44


def _src_offsets():
    offs, o = [], 0
    for w in IN_WIDTHS:
        offs.append(o)
        o += w
    return offs


def _permute_w_in(w):
    (o_pc, o_za, o_aa, o_ba, o_qb, o_kb, o_vb, o_qi, o_ki, o_wi, o_zc, o_dt, o_qd, o_fd, o_id, o_gd, o_gate) = _src_offsets()
    wt = jnp.swapaxes(w, 0, 1)
    seg = lambda o, n: wt[o:o + n, :]
    small = jnp.concatenate([seg(o_ki, D_IDX), seg(o_wi, H_IDX), seg(o_aa, H_A), seg(o_ba, H_A), seg(o_dt, H_C),
                             jnp.zeros((LANE - (D_IDX + H_IDX + 2 * H_A + H_C), w.shape[0]), w.dtype)], axis=0)
    out = jnp.concatenate([
        seg(o_gate, N_BRANCH * D_MODEL),
        seg(o_qd, 4 * W_D),
        seg(o_zc, D_INNER_C), seg(o_za, W_A), seg(o_qb, W_B), seg(o_pc, CONV_CH),
        seg(o_kb, 2 * DH_B),
        seg(o_qi, H_IDX * D_IDX), small], axis=0)
    assert out.shape[0] == P_PAD
    return out.astype(BF16)


def _bdot(a, b):
    return jnp.dot(a.astype(BF16), b.astype(BF16), preferred_element_type=F32)


def _bdot_nt(a, b):
    return lax.dot_general(a.astype(BF16), b.astype(BF16), (((1,), (1,)), ((), ())), preferred_element_type=F32)


def _split2(a):
    hi = a.astype(BF16)
    lo = (a - hi.astype(F32)).astype(BF16)
    return hi, lo


def _split3(a):
    hi = a.astype(BF16)
    r = a - hi.astype(F32)
    mid = r.astype(BF16)
    lo = (r - mid.astype(F32)).astype(BF16)
    return hi, mid, lo


def _dot01_left(m01, x):
    hi, mid, lo = _split3(x)
    d = lambda p: jnp.dot(m01, p, preferred_element_type=F32)
    return d(hi) + (d(mid) + d(lo))


def _dot01_right(x, m01):
    hi, mid, lo = _split3(x)
    d = lambda p: jnp.dot(p, m01, preferred_element_type=F32)
    return d(hi) + (d(mid) + d(lo))


def _xdot(a, b):
    ah, al = _split2(a)
    bh, bl = _split2(b)
    d = lambda p, q: jnp.dot(p, q, preferred_element_type=F32)
    return d(ah, bh) + (d(ah, bl) + d(al, bh))


def _sigmoid(x):
    return 1.0 / (1.0 + jnp.exp(-x))


def _silu(x):
    return x * _sigmoid(x)


def _softplus(x):
    return jnp.maximum(x, 0.0) + jnp.log(1.0 + jnp.exp(-jnp.abs(x)))


def _iota2(shape, dim):
    return lax.broadcasted_iota(I32, shape, dim)


def _pad_rows(x, rows):
    if x.shape[0] == rows:
        return x
    return jnp.concatenate([x, jnp.zeros((rows - x.shape[0],) + x.shape[1:], x.dtype)], axis=0)


def _tri_inv(L, top):
    return _tri_inv_many([L], top)[0]


def _tri_inv_many(Ls, top):
    n = Ls[0].shape[0]
    ii, jj = _iota2((n, n), 0), _iota2((n, n), 1)
    xor = ii ^ jj
    eye = jnp.where(ii == jj, 1.0, 0.0)
    Ns = [jnp.where((xor >> 3) == 0, -L, 0.0) for L in Ls]
    Xs = [eye + N for N in Ns]
    N2s = [_bdot(N, N) for N in Ns]
    Xs = [X + _bdot(X, N2) for X, N2 in zip(Xs, N2s)]
    N4s = [_bdot(N2, N2) for N2 in N2s]
    Xs = [X + _bdot(X, N4) for X, N4 in zip(Xs, N4s)]
    sh = 4
    while (1 << sh) <= top:
        XBs = [_bdot(X, jnp.where((xor >> (sh - 1)) == 1, L, 0.0)) for X, L in zip(Xs, Ls)]
        Xs = [X - _bdot(XB, X) for X, XB in zip(Xs, XBs)]
        sh += 1
    Rs = [eye - X - _xdot(L, X) for X, L in zip(Xs, Ls)]
    return [X + _bdot(X, R) for X, R in zip(Xs, Rs)]


def _conv_silu(pc, ext_scr, cw_ref, cb_ref):
    ext_scr[SUB:SUB + TC, :] = pc
    y = cb_ref[...]
    for j in range(CONV_W):
        y = y + ext_scr[SUB - (CONV_W - 1) + j:SUB - (CONV_W - 1) + j + TC, :] * cw_ref[j:j + 1, :]
    ext_scr[0:SUB, :] = ext_scr[TC:TC + SUB, :]
    return _silu(y)


def _in_proj_kernel(x_ref, g_ref, w_ref, o_ref, h_scr):
    @pl.when(pl.program_id(1) == 0)
    def _():
        x = x_ref[...]
        r = lax.rsqrt(jnp.mean(x * x, axis=-1, keepdims=True) + EPS)
        h_scr[...] = ((x * r) * g_ref[...]).astype(BF16)

    o_ref[...] = lax.dot_general(h_scr[...], w_ref[...], (((1,), (1,)), ((), ())), preferred_element_type=F32)


def _in_proj(x2d, g, w_perm_t):
    n = x2d.shape[0]
    tm = min(n, 1024)
    tn = 1152
    return pl.pallas_call(
        _in_proj_kernel,
        grid=(n // tm, P_PAD // tn),
        in_specs=[pl.BlockSpec((tm, D_MODEL), lambda i, j: (i, 0)),
                  pl.BlockSpec((1, D_MODEL), lambda i, j: (0, 0)),
                  pl.BlockSpec((tn, D_MODEL), lambda i, j: (j, 0))],
        out_specs=pl.BlockSpec((tm, tn), lambda i, j: (i, j)),
        out_shape=jax.ShapeDtypeStruct((n, P_PAD), F32),
        scratch_shapes=[pltpu.VMEM((tm, D_MODEL), BF16)],
        compiler_params=pltpu.CompilerParams(dimension_semantics=("arbitrary", "arbitrary"), vmem_limit_bytes=VMEM_LIMIT),
        name="in_proj",
    )(x2d, g.reshape(1, D_MODEL), w_perm_t)


def _mix_a_kernel(pc_ref, za_ref, sm_ref, conv0_ref, s0_ref, cw_ref, cb_ref, hp_ref, hpt_ref, gn_ref,
                  tril_ref, triu_ref, bo_ref, o_ref, sout_ref, ext_scr, s_scr, ob_scr, *, t_valid, rows):
    CA = 64

    @pl.when(pl.program_id(1) == 0)
    def _():
        ext_scr[0:SUB, :] = conv0_ref[...]
        s_scr[...] = s0_ref[...]

    u = _conv_silu(_pad_rows(pc_ref[...], TC), ext_scr, cw_ref, cb_ref)
    bo = bo_ref[...]
    q_raw, k_raw, va = u[:, 0:QK_A], u[:, QK_A:2 * QK_A], u[:, 2 * QK_A:CONV_A]
    qq = q_raw * lax.rsqrt(_dot01_right(q_raw * q_raw, bo) + EPS) * (DK_A ** -0.5)
    ka = k_raw * lax.rsqrt(_dot01_right(k_raw * k_raw, bo) + EPS)
    kat = ka.T

    sm = _pad_rows(sm_ref[...], TC)
    smt = sm.T
    hp, hpt = hp_ref[...], hpt_ref[...]
    g_col = -jnp.exp(hp[0:1, 0:H_A]) * _softplus(sm[:, SM_AA:SM_AA + H_A] + hp[1:2, 0:H_A])
    beta = _sigmoid(sm[:, SM_BA:SM_BA + H_A])
    g_row = -jnp.exp(hpt[0:H_A, 0:1]) * _softplus(smt[SM_AA:SM_AA + H_A, :] + hpt[0:H_A, 1:2])
    if t_valid < TC:
        g_col = jnp.where(_iota2(g_col.shape, 0) < t_valid, g_col, 0.0)
        beta = jnp.where(_iota2(beta.shape, 0) < t_valid, beta, 0.0)
        g_row = jnp.where(_iota2(g_row.shape, 1) < t_valid, g_row, 0.0)
    gcum_col = _dot01_left(tril_ref[...], g_col)
    gcum_row = _dot01_right(g_row, triu_ref[...])

    HS = H_A * CA
    ii, jj = _iota2((HS, HS), 0), _iota2((HS, HS), 1)
    same = (ii >> 6) == (jj >> 6)
    incl, strict = same & (ii >= jj), same & (ii > jj)
    nsc = -(-t_valid // CA)
    pre = []
    for sc in range(nsc):
        r0 = sc * CA
        stack = lambda x, w: jnp.concatenate([x[r0:r0 + CA, h * w:(h + 1) * w] for h in range(H_A)], axis=0)
        ks, qs, vs = stack(ka, DK_A), stack(qq, DK_A), stack(va, DV_A)
        bcol, gc = stack(beta, 1), stack(gcum_col, 1)
        gr = jnp.concatenate([gcum_row[h:h + 1, r0:r0 + CA] for h in range(H_A)], axis=1)
        e = jnp.exp(jnp.where(incl, gc - gr, 0.0))
        kb = ks * bcol
        eg = jnp.exp(gc)
        pre.append(dict(r0=r0, gc=gc, qe=qs * eg, low=jnp.where(strict, _bdot_nt(kb, ks) * e, 0.0),
                        rhs=jnp.concatenate([vs * bcol, kb * eg], axis=1),
                        attn=jnp.where(incl, _bdot_nt(qs, ks) * e, 0.0)))
    tinvs = _tri_inv_many([p['low'] for p in pre], CA)
    sols = [_xdot(t, p['rhs']) for t, p in zip(tinvs, pre)]
    hs = lambda x, h: x[h * CA:(h + 1) * CA]
    for p, sol in zip(pre, sols):
        r0, gc = p['r0'], p['gc']
        states = [s_scr[h] for h in range(H_A)]
        v_new = jnp.concatenate([hs(sol, h)[:, 0:DV_A] - _bdot(hs(sol, h)[:, DV_A:], states[h]) for h in range(H_A)], axis=0)
        o = jnp.concatenate([_bdot(hs(p['qe'], h), states[h]) for h in range(H_A)], axis=0) + _bdot(p['attn'], v_new)
        for h in range(H_A):
            gch = hs(gc, h)
            gl = gch[CA - 1:CA, :]
            s_scr[h] = states[h] * jnp.exp(gl) + _bdot(kat[h * DK_A:(h + 1) * DK_A, r0:r0 + CA], hs(v_new, h) * jnp.exp(gl - gch))
            ob_scr[r0:r0 + CA, h * DV_A:(h + 1) * DV_A] = hs(o, h)
    if t_valid <= CA:
        ob_scr[CA:TC, :] = jnp.zeros((TC - CA, W_A), F32)

    o = ob_scr[...]
    o = o * lax.rsqrt(_dot01_right(o * o, bo) * (1.0 / DV_A) + EPS) * gn_ref[...]
    res = o * _silu(_pad_rows(za_ref[...], TC))
    o_ref[...] = res[0:rows]
    sout_ref[...] = s_scr[...]


def _mix_c_kernel(pc_ref, zc_ref, sm_ref, conv0_ref, s0_ref, cw_ref, cb_ref, hp_ref, hpt_ref, gn_ref,
                  tril_ref, triu_ref, bo_ref, o_ref, sout_ref, ext_scr, s_scr, yb_scr, *, t_valid, rows):
    @pl.when(pl.program_id(1) == 0)
    def _():
        ext_scr[0:SUB, :] = conv0_ref[...]
        s_scr[...] = s0_ref[...]

    u = _conv_silu(_pad_rows(pc_ref[...], TC), ext_scr, cw_ref, cb_ref)
    xc = u[:, 0:D_INNER_C]
    bcm = u[:, D_INNER_C:D_INNER_C + G_C * N_C]
    ccm = u[:, D_INNER_C + G_C * N_C:CONV_C]
    bct = bcm.T
    zc = _pad_rows(zc_ref[...], TC)

    sm = _pad_rows(sm_ref[...], TC)
    smt = sm.T
    hp, hpt = hp_ref[...], hpt_ref[...]
    dt_col = _softplus(sm[:, SM_DT:SM_DT + H_C] + hp[3:4, 0:H_C])
    dt_row = _softplus(smt[SM_DT:SM_DT + H_C, :] + hpt[0:H_C, 3:4])
    if t_valid < TC:
        dt_col = jnp.where(_iota2(dt_col.shape, 0) < t_valid, dt_col, 0.0)
        dt_row = jnp.where(_iota2(dt_row.shape, 1) < t_valid, dt_row, 0.0)
    gcum_col = _dot01_left(tril_ref[...], -jnp.exp(hp[2:3, 0:H_C]) * dt_col)
    gcum_row = _dot01_right(-jnp.exp(hpt[0:H_C, 2:3]) * dt_row, triu_ref[...])

    incl = _iota2((TC, TC), 0) >= _iota2((TC, TC), 1)
    hg = H_C // G_C
    for g in range(G_C):
        cc_g = ccm[:, g * N_C:(g + 1) * N_C]
        gram = _bdot_nt(cc_g, bcm[:, g * N_C:(g + 1) * N_C])
        for hh in range(hg):
            h = g * hg + hh
            c0 = h * P_C
            gc = gcum_col[:, h:h + 1]
            gr = gcum_row[h:h + 1, :]
            dec = jnp.where(incl, jnp.exp(jnp.where(incl, gc - gr, 0.0)), 0.0)
            attn = gram * dt_row[h:h + 1, :] * dec
            xh = xc[:, c0:c0 + P_C]
            S = s_scr[h]
            o = jnp.exp(gc) * _bdot(cc_g, S) + _bdot(attn, xh)
            gl = gc[TC - 1:TC, :]
            s_scr[h] = S * jnp.exp(gl) + _bdot(bct[g * N_C:(g + 1) * N_C, :], xh * (dt_col[:, h:h + 1] * jnp.exp(gl - gc)))
            yb_scr[:, c0:c0 + P_C] = (o + hp[4:5, h:h + 1] * xh) * _silu(zc[:, c0:c0 + P_C])

    y = yb_scr[...]
    y = y * lax.rsqrt(_dot01_right(y * y, bo_ref[...]) * (1.0 / (D_INNER_C // G_C)) + EPS) * gn_ref[...]
    o_ref[...] = y[0:rows]
    sout_ref[...] = s_scr[...]


_D_LEVELS = 6


def _mix_d_kernel(dg_ref, s0_ref, lb_ref, gn_ref, tril_ref, sel_ref, bo_ref, o_ref, sout_ref, s_scr, ob_scr, *, t_valid, rows):
    CD = 64
    W = H_D * DK_D

    @pl.when(pl.program_id(1) == 0)
    def _():
        s_scr[...] = s0_ref[...]

    dg = _pad_rows(dg_ref[...], TC)
    q = _silu(dg[:, 0:W])
    fr = dg[:, W:2 * W]
    v = dg[:, 2 * W:3 * W]
    gd = dg[:, 3 * W:4 * W]
    lb = lb_ref[...]
    logf = jnp.log(jnp.maximum(lb + (1.0 - lb) * _sigmoid(fr), TINY))
    kd = (1.0 - lb) * _sigmoid(-fr)
    if t_valid < TC:
        ok = _iota2((TC, W), 0) < t_valid
        logf = jnp.where(ok, logf, 0.0)
        kd = jnp.where(ok, kd, 0.0)
    gcum = _dot01_left(tril_ref[...], logf)
    gm = _dot01_left(sel_ref[...], gcum)
    qdec = q * jnp.exp(gcum)
    nsc = -(-t_valid // CD)
    gl_rows = jnp.concatenate([jnp.broadcast_to(gcum[(s + 1) * CD - 1:(s + 1) * CD, :], (CD, W)) for s in range(TC // CD)], axis=0)
    kdt = (kd * jnp.exp(gl_rows - gcum)).T
    gct = gcum.T

    HS = H_D * CD
    head_lanes = (_iota2((HS, W), 0) >> 6) == (_iota2((HS, W), 1) >> 6)
    stack = lambda x: jnp.where(head_lanes, jnp.concatenate([x] * H_D, axis=0), 0.0)
    ci, jj = _iota2((HS, CD), 0) & (CD - 1), _iota2((HS, CD), 1)
    later = jnp.where(ci > jj, ci ^ jj, 0)
    lhs, rhs, msk = [q], [kd], [ci == jj]
    for lv in range(1, _D_LEVELS + 1):
        gml = gm[(lv - 1) * TC:lv * TC, :]
        lhs.append(q * jnp.exp(jnp.minimum(gcum - gml, 0.0)))
        rhs.append(kd * jnp.exp(jnp.minimum(gml - gcum, 0.0)))
        msk.append((later >> (lv - 1)) == 1)

    lane_head = _iota2((CD, W), 1) >> 6
    intra = []
    for sc in range(nsc):
        r0 = sc * CD
        attn = jnp.zeros((HS, CD), F32)
        for a, b, m in zip(lhs, rhs, msk):
            attn = attn + jnp.where(m, _bdot_nt(stack(a[r0:r0 + CD]), b[r0:r0 + CD]), 0.0)
        full = _bdot(attn, v[r0:r0 + CD])
        o = jnp.zeros((CD, W), F32)
        for h in range(H_D):
            o = o + jnp.where(lane_head == h, full[h * CD:(h + 1) * CD], 0.0)
        intra.append(o)
    for sc in range(nsc):
        r0 = sc * CD
        sbd = s_scr[...]
        ob_scr[r0:r0 + CD, :] = _bdot(qdec[r0:r0 + CD], sbd) + intra[sc]
        decay = jnp.exp(gct[:, r0 + CD - 1:r0 + CD])
        s_scr[...] = jnp.where(head_lanes, sbd * decay + _bdot(kdt[:, r0:r0 + CD], v[r0:r0 + CD]), 0.0)
    if nsc < TC // CD:
        ob_scr[nsc * CD:TC, :] = jnp.zeros((TC - nsc * CD, W), F32)

    o = ob_scr[...]
    o = o * lax.rsqrt(_dot01_right(o * o, bo_ref[...]) * (1.0 / DV_D) + EPS) * gn_ref[...]
    res = o * _silu(gd)
    o_ref[...] = res[0:rows]
    sout_ref[...] = s_scr[...]


def _np_consts():
    i = np.arange(TC)
    same64 = (i[:, None] // 64) == (i[None, :] // 64)
    tril2 = (same64 & (i[:, None] >= i[None, :])).astype(np.float32)
    tril1 = (i[:, None] >= i[None, :]).astype(np.float32)
    sel = np.zeros((_D_LEVELS * TC, TC), np.float32)
    for lv in range(1, _D_LEVELS + 1):
        m = ((i >> lv) << lv) + (1 << (lv - 1)) - 1
        sel[(lv - 1) * TC + i, m] = 1.0
    bo = lambda n, w: ((np.arange(n)[:, None] // w) == (np.arange(n)[None, :] // w)).astype(np.float32)
    return dict(tril2=tril2, triu2=tril2.T.copy(), tril1=tril1, triu1=tril1.T.copy(), sel=sel,
                bo64=bo(256, 64), bo256=bo(D_INNER_C, D_INNER_C // G_C))


def _bucket_starts():
    max_exact = N_BUCKETS // 2
    d = np.arange(0, 4 * MAX_DISTANCE)
    lr = np.log(np.maximum(d, 1).astype(np.float32) / max_exact) / math.log(MAX_DISTANCE / max_exact)
    large = np.minimum(max_exact + (np.maximum(lr, 0.0) * (N_BUCKETS - max_exact)).astype(np.int32), N_BUCKETS - 1)
    bucket = np.where(d < max_exact, d, large)
    return [int(np.argmax(bucket == b)) for b in range(N_BUCKETS)]


_BUCKET_STARTS = _bucket_starts()
_SAT_DIST = _BUCKET_STARTS[N_BUCKETS - 1]


def _bias_from_dist(d, rb_ref, h):
    out = jnp.full(d.shape, rb_ref[N_BUCKETS - 1, h], F32)
    for b in range(N_BUCKETS - 2, -1, -1):
        out = jnp.where(d < _BUCKET_STARTS[b + 1], rb_ref[b, h], out)
    return out


def _sortable(x):
    b = lax.bitcast_convert_type(x + 0.0, I32)
    return jnp.where(b < 0, b ^ 0x7FFFFFFF, b)


_NEG_KEY = int(np.array([NEG], np.float32).view(np.int32)[0]) ^ 0x7FFFFFFF
_INT_MIN = -2 ** 31


def _selected_i32(k, idx, thr, xthr):
    s = jnp.where(k > thr, 1, jnp.where(k == thr, jnp.where(idx <= xthr, 1, 0), 0))
    return jnp.where(k == _NEG_KEY, 0, s)


def _selected(k, idx, thr, xthr):
    return _selected_i32(k, idx, thr, xthr) > 0


def _const_spec(shape):
    nd = len(shape)
    return pl.BlockSpec(shape, lambda *_: (0,) * nd)


def _mix_call(kernel_fn, name, P, B, T, seq_specs, batch_args, const_args, out_w, st_tail, scratch):
    rows = min(T, TC)
    nc = max(T // TC, 1)
    assert rows * nc == T
    in_specs = [pl.BlockSpec((rows, w), functools.partial(lambda b, c, ci: (b * nc + c, ci), ci=off // w)) for off, w in seq_specs]
    args = [P] * len(seq_specs)
    for a in batch_args:
        nd = a.ndim
        in_specs.append(pl.BlockSpec((None,) + a.shape[1:], functools.partial(lambda b, c, nd: (b,) + (0,) * (nd - 1), nd=nd)))
        args.append(a)
    for a in const_args:
        in_specs.append(_const_spec(a.shape))
        args.append(a)
    st_nd = len(st_tail)
    return pl.pallas_call(
        functools.partial(kernel_fn, t_valid=rows, rows=rows),
        grid=(B, nc),
        in_specs=in_specs,
        out_specs=[pl.BlockSpec((rows, out_w), lambda b, c: (b * nc + c, 0)),
                   pl.BlockSpec((None,) + st_tail, lambda b, c: (b,) + (0,) * st_nd)],
        out_shape=[jax.ShapeDtypeStruct((B * T, out_w), F32), jax.ShapeDtypeStruct((B,) + st_tail, F32)],
        scratch_shapes=scratch,
        compiler_params=pltpu.CompilerParams(dimension_semantics=("arbitrary", "arbitrary"), vmem_limit_bytes=VMEM_LIMIT),
        name=name,
    )(*args)


CU = 4


def _dsa_prompt_kernel(rb_ref, qb_ref, qi_ref, smq_ref, kv_ref, sma_ref, o_ref,
                       kvt_scr, key_scr, lg_scr, bias_scr, acc_scr, hi_scr, lo_scr, *, n_sel, idx_bits):
    j = pl.program_id(1)
    nch = j + 1

    @pl.when((pl.program_id(0) == 0) & (j == 0))
    def _():
        srow, tcol = _iota2((QB, QB), 0), _iota2((QB, QB), 1)
        for h in range(H_B):
            bias_scr[h, 0] = jnp.full((QB, QB), rb_ref[N_BUCKETS - 1, h], F32)
            bias_scr[h, 1] = _bias_from_dist(tcol - srow + QB, rb_ref, h)
            bias_scr[h, 2] = _bias_from_dist(tcol - srow, rb_ref, h)

    @pl.when(j == 0)
    def _():
        def tr(c, _):
            r = pl.multiple_of(c * QB, QB)
            kvt_scr[c] = kv_ref[pl.ds(r, QB), :].T[DH_B:2 * DH_B, :].astype(BF16)
            return 0
        lax.fori_loop(0, kv_ref.shape[0] // QB, tr, 0)

    lanes = lambda rows, w: jnp.concatenate([rows[h * w:(h + 1) * w, :] for h in range(H_B)], axis=1)
    w4 = lanes(smq_ref[...].T[SM_WI:SM_WI + H_IDX, :], 1) * (H_IDX ** -0.5 * D_IDX ** -0.5)
    qi_rhs = lanes(qi_ref[...].T, D_IDX).astype(BF16)
    qb_rhs = (lanes(qb_ref[...].T, DH_B) * (DH_B ** -0.5)).astype(BF16)
    srow, tcol = _iota2((QB, QB), 0), _iota2((QB, QB), 1)
    heads = lambda x: [x[:, h * QB:(h + 1) * QB] for h in range(H_B)]

    ngrp = (nch + (CU - 1)) // CU

    def score_body(g, _):
        for u in range(CU):
            r = pl.multiple_of((g * CU + u) * QB, QB)
            ki = sma_ref[pl.ds(r, QB), :][:, SM_KI:SM_KI + D_IDX].astype(BF16)
            s = jnp.maximum(jnp.dot(ki, qi_rhs, preferred_element_type=F32), 0.0) * w4
            sh = heads(s)
            sc = (sh[0] + sh[1]) + (sh[2] + sh[3])
            sc = jnp.where(srow + r <= tcol + j * QB, sc, NEG)
            key = _sortable(sc)
            key_scr[pl.ds(r, QB), :] = key
            hi_scr[pl.ds(r, QB), :] = (key >> 16).astype(I16)
            lo_scr[pl.ds(r, QB), :] = ((key & 0xFFFF) - 32768).astype(I16)
        return 0

    lax.fori_loop(0, ngrp, score_body, 0)

    def count16(ref, pred):
        def body(g, acc):
            parts = []
            for u in range(CU):
                w = pred(ref[pl.ds(pl.multiple_of((g * CU + u) * QB, QB), QB), :])
                parts += [w[q * 2 * SUB:(q + 1) * 2 * SUB] for q in range(QB // (2 * SUB))]
            while len(parts) > 1:
                parts = [parts[i] + parts[i + 1] for i in range(0, len(parts), 2)]
            return acc + parts[0]
        acc = lax.fori_loop(0, ngrp, body, jnp.zeros((2 * SUB, QB), I16))
        return acc.astype(I32).sum(axis=0, keepdims=True)

    one16, zero16 = jnp.int16(1), jnp.int16(0)

    def search16(ref, need, c_start):
        cnt_ge = lambda t: count16(ref, lambda x: jnp.where(x >= t.astype(I16), one16, zero16))
        c0 = cnt_ge(jnp.zeros((1, QB), I32))
        t = jnp.where(c0 >= need, 0, -32768).astype(I32)
        ct = jnp.where(c0 >= need, c0, c_start)

        def bit_body(i, carry):
            t, ct = carry
            trial = t | jnp.left_shift(jnp.int32(1), 14 - i)
            cnt = cnt_ge(trial)
            ok = cnt >= need
            return jnp.where(ok, trial, t), jnp.where(ok, cnt, ct)

        return lax.fori_loop(0, 15, bit_body, (t, ct))

    thi, c_hi = search16(hi_scr, n_sel, ngrp * (CU * QB))
    thi16 = thi.astype(I16)
    c_above = count16(hi_scr, lambda x: jnp.where(x > thi16, one16, zero16))

    def mask_body(g, _):
        for u in range(CU):
            sl = pl.ds(pl.multiple_of((g * CU + u) * QB, QB), QB)
            lo_scr[sl, :] = jnp.where(hi_scr[sl, :] == thi16, lo_scr[sl, :], jnp.int16(-32768))
        return 0

    lax.fori_loop(0, ngrp, mask_body, 0)
    tlo, c_lo = search16(lo_scr, n_sel - c_above, c_hi - c_above)
    thr = (thi << 16) | ((tlo + 32768) & 0xFFFF)
    cthr = c_above + c_lo

    def count(pred):
        def body(g, acc):
            for u in range(CU):
                r = pl.multiple_of((g * CU + u) * QB, QB)
                acc = acc + pred(key_scr[pl.ds(r, QB), :], r).reshape(QB // SUB, SUB, QB).sum(axis=0)
            return acc
        return lax.fori_loop(0, ngrp, body, jnp.zeros((SUB, QB), I32)).sum(axis=0, keepdims=True)

    @pl.when(jnp.max(cthr) > n_sel)
    def _():
        quota = n_sel - count(lambda k, r: jnp.where(k > thr, 1, 0))

        def idx_body(i, x):
            trial = x | jnp.left_shift(jnp.int32(1), idx_bits - 1 - i)
            below = count(lambda k, r: jnp.where(k == thr, jnp.where(srow + r < trial, 1, 0), 0))
            return jnp.where(below < quota, trial, x)

        xthr = lax.fori_loop(0, idx_bits, idx_body, jnp.zeros((1, QB), I32))

        def demote_body(g, _):
            for u in range(CU):
                r = pl.multiple_of((g * CU + u) * QB, QB)
                k = key_scr[pl.ds(r, QB), :]
                key_scr[pl.ds(r, QB), :] = jnp.where(k == thr, jnp.where(srow + r > xthr, k - 1, k), k)
            return 0

        lax.fori_loop(0, ngrp, demote_body, 0)

    thr_sel = jnp.maximum(thr, _NEG_KEY + 1)

    def logit_body(g, ms):
        ms = list(ms)
        for u in range(CU):
            c = g * CU + u
            r = pl.multiple_of(c * QB, QB)
            kc = kv_ref[pl.ds(r, QB), :][:, 0:DH_B].astype(BF16)
            l4 = heads(jnp.dot(kc, qb_rhs, preferred_element_type=F32))
            sel = key_scr[pl.ds(r, QB), :] >= thr_sel
            bidx = jnp.clip(c - j + 2, 0, 2)
            for h in range(H_B):
                l = jnp.where(sel, l4[h] + bias_scr[h, bidx], NEG)
                lg_scr[h, pl.ds(r, QB), :] = l
                ms[h] = jnp.maximum(ms[h], l.max(axis=0, keepdims=True))
        return tuple(ms)

    ms = lax.fori_loop(0, ngrp, logit_body, tuple(jnp.full((1, QB), NEG, F32) for _ in range(H_B)))
    acc_scr[...] = jnp.zeros(acc_scr.shape, F32)

    def pv_body(g, ss):
        ss = list(ss)
        acc = acc_scr[...]
        for u in range(CU):
            c = g * CU + u
            r = pl.multiple_of(c * QB, QB)
            es = [jnp.exp(lg_scr[h, pl.ds(r, QB), :] - ms[h]) for h in range(H_B)]
            acc = acc + jnp.dot(kvt_scr[c], jnp.concatenate(es, axis=1).astype(BF16), preferred_element_type=F32)
            for h in range(H_B):
                ss[h] = ss[h] + es[h].sum(axis=0, keepdims=True)
        acc_scr[...] = acc
        return tuple(ss)

    ss = lax.fori_loop(0, ngrp, pv_body, tuple(jnp.zeros((1, QB), F32) for _ in range(H_B)))
    acc = heads(acc_scr[...])
    ot = jnp.concatenate([acc[h] / ss[h] for h in range(H_B)], axis=0)
    o_ref[...] = ot.T


def _dsa_prompt(P, rel_bias, B, T):
    nb = T // QB
    assert nb % CU == 0
    n_sel = min(TOPK_MAX, T // 4)
    idx_bits = max(1, int(math.ceil(math.log2(T))))
    return pl.pallas_call(
        functools.partial(_dsa_prompt_kernel, n_sel=n_sel, idx_bits=idx_bits),
        grid=(B, nb),
        in_specs=[pl.BlockSpec(memory_space=pltpu.SMEM),
                  pl.BlockSpec((QB, W_B), lambda b, j: (b * nb + j, OFF_QB // W_B)),
                  pl.BlockSpec((QB, LANE), lambda b, j: (b * nb + j, OFF_QI // LANE)),
                  pl.BlockSpec((QB, LANE), lambda b, j: (b * nb + j, OFF_SM // LANE)),
                  pl.BlockSpec((T, LANE), lambda b, j: (b, OFF_KV // LANE)),
                  pl.BlockSpec((T, LANE), lambda b, j: (b, OFF_SM // LANE))],
        out_specs=pl.BlockSpec((QB, W_B), lambda b, j: (b * nb + j, 0)),
        out_shape=jax.ShapeDtypeStruct((B * T, W_B), F32),
        scratch_shapes=[pltpu.VMEM((nb, DH_B, QB), BF16), pltpu.VMEM((T, QB), I32), pltpu.VMEM((H_B, T, QB), F32),
                        pltpu.VMEM((H_B, 3, QB, QB), F32), pltpu.VMEM((DH_B, H_B * QB), F32),
                        pltpu.VMEM((T, QB), I16), pltpu.VMEM((T, QB), I16)],
        compiler_params=pltpu.CompilerParams(dimension_semantics=("arbitrary", "arbitrary"), vmem_limit_bytes=VMEM_LIMIT),
        name="dsa_prompt",
    )(rel_bias, P, P, P, P, P)


PG = 16
GW = PG * PAGE_SIZE


def _stack_heads(x, w):
    return jnp.concatenate([x[:, h * w:(h + 1) * w] for h in range(x.shape[1] // w)], axis=0)


def _dsa_s_scores_kernel(pt_ref, qi_ref, sm_ref, *refs, ng, ds):
    page_refs, o_ref = refs[:PG], refs[PG]
    g = pl.program_id(1)
    sm = sm_ref[...]
    qst = _stack_heads(qi_ref[...], D_IDX)
    wcol = jnp.concatenate([sm[:, SM_WI + h:SM_WI + h + 1] for h in range(H_IDX)], axis=0) * (H_IDX ** -0.5 * D_IDX ** -0.5)

    def tile_scores(qk):
        s = jnp.maximum(qk, 0.0) * wcol
        out = s[0:ds]
        for h in range(1, H_IDX):
            out = out + s[h * ds:(h + 1) * ds]
        return out

    @pl.when(g < ng)
    def _():
        for i in range(PG):
            o_ref[:, i * PAGE_SIZE:(i + 1) * PAGE_SIZE] = tile_scores(_bdot(qst, page_refs[i][...]))

    @pl.when(g == ng)
    def _():
        sc = tile_scores(_bdot_nt(qst, _pad_rows(sm[:, SM_KI:SM_KI + D_IDX], PAGE_SIZE)))
        lane, row = _iota2(sc.shape, 1), _iota2(sc.shape, 0)
        o_ref[:, 0:PAGE_SIZE] = jnp.where(lane <= row, sc, NEG)
        o_ref[:, PAGE_SIZE:GW] = jnp.full((ds, GW - PAGE_SIZE), NEG, F32)


def _dsa_s_attn_kernel(pt_ref, rb_ref, sc_ref, qb_ref, kvn_ref, *refs, ng, ds, n_sel, idx_bits):
    k_refs, v_refs = refs[:PG], refs[PG:2 * PG]
    o_ref, key_scr, thr_scr, bias_scr, m_scr, l_scr, acc_scr = refs[2 * PG:]
    g = pl.program_id(1)
    ntile = (ng + 1) * PG
    hr = H_B * ds

    @pl.when(g == 0)
    def _():
        for i in range(ntile):
            key_scr[:, i * LANE:(i + 1) * LANE] = _sortable(sc_ref[:, i * LANE:(i + 1) * LANE])
        lane = _iota2((ds, LANE), 1)

        def count(pred):
            acc = jnp.zeros((ds, LANE), I32)
            for i in range(ntile):
                acc = acc + pred(key_scr[:, i * LANE:(i + 1) * LANE], i * LANE)
            return acc.sum(axis=1, keepdims=True)

        cnt_ge = lambda t: count(lambda k, off: jnp.where(k >= t, 1, 0))
        thr = jnp.where(cnt_ge(jnp.zeros((ds, 1), I32)) >= n_sel, 0, _INT_MIN).astype(I32)

        def bit_body(i, t):
            trial = t | jnp.left_shift(jnp.int32(1), 30 - i)
            return jnp.where(cnt_ge(trial) >= n_sel, trial, t)

        thr = lax.fori_loop(0, 31, bit_body, thr)
        quota = n_sel - count(lambda k, off: jnp.where(k > thr, 1, 0))

        def idx_body(i, x):
            trial = x | jnp.left_shift(jnp.int32(1), idx_bits - 1 - i)
            below = count(lambda k, off: jnp.where(k == thr, jnp.where(lane + off < trial, 1, 0), 0))
            return jnp.where(below < quota, trial, x)

        xthr = lax.fori_loop(0, idx_bits, idx_body, jnp.zeros((ds, 1), I32))
        thr_scr[0] = jnp.broadcast_to(thr, (ds, LANE))
        thr_scr[1] = jnp.broadcast_to(xthr, (ds, LANE))

        qrow = _iota2((ds, LANE), 0)
        for h in range(H_B):
            bias_scr[0, h * ds:(h + 1) * ds, :] = _bias_from_dist(PAGE_SIZE + qrow - lane, rb_ref, h)
            bias_scr[1, h * ds:(h + 1) * ds, :] = _bias_from_dist(qrow - lane, rb_ref, h)
            bias_scr[2, h * ds:(h + 1) * ds, :] = jnp.full((ds, LANE), rb_ref[N_BUCKETS - 1, h], F32)
        m_scr[...] = jnp.full(m_scr.shape, NEG, F32)
        l_scr[...] = jnp.zeros(l_scr.shape, F32)
        acc_scr[...] = jnp.zeros(acc_scr.shape, F32)

    qst = _stack_heads(qb_ref[...], DH_B) * (DH_B ** -0.5)
    thr, xthr = thr_scr[0], thr_scr[1]

    def select(tile):
        k = key_scr[:, pl.ds(pl.multiple_of(tile * LANE, LANE), LANE)]
        idx = _iota2((ds, LANE), 1) + tile * LANE
        sel = _selected_i32(k, idx, thr, xthr)
        return jnp.concatenate([sel] * H_B, axis=0) > 0

    def update(lg, sel, pv):
        m_old = m_scr[...]
        m_new = jnp.maximum(m_old, jnp.where(sel, lg, NEG).max(axis=1, keepdims=True))
        p = jnp.where(sel, jnp.exp(lg - m_new), 0.0)
        corr = jnp.exp(m_old - m_new)
        l_scr[...] = l_scr[...] * corr + p.sum(axis=1, keepdims=True)
        acc_scr[...] = acc_scr[...] * corr + pv(p)
        m_scr[...] = m_new

    @pl.when(g < ng)
    def _():
        lgs, sels = [], []
        for i in range(PG):
            lg = _bdot(qst, k_refs[i][...])
            if i == PG - 1:
                lg = lg + jnp.where(g == ng - 1, bias_scr[0], bias_scr[2])
            else:
                lg = lg + bias_scr[2]
            lgs.append(lg)
            sels.append(select(g * PG + i))
        vt = jnp.concatenate([v_refs[i][...] for i in range(PG)], axis=1)
        update(jnp.concatenate(lgs, axis=1), jnp.concatenate(sels, axis=1), lambda p: _bdot_nt(p, vt))

    @pl.when(g == ng)
    def _():
        kvn = _pad_rows(kvn_ref[...], PAGE_SIZE)
        lg = _bdot_nt(qst, kvn[:, 0:DH_B]) + bias_scr[1]
        update(lg, select(ng * PG), lambda p: _bdot(p, kvn[:, DH_B:2 * DH_B]))
        o = acc_scr[...] / l_scr[...]
        o_ref[...] = jnp.concatenate([o[h * ds:(h + 1) * ds] for h in range(H_B)], axis=1)


def _dsa_sample(P, rel_bias, cache_k, cache_v, cache_kidx, page_table, layer, B, DS):
    n_pages = page_table.shape[1]
    past = n_pages * PAGE_SIZE
    assert n_pages % PG == 0
    ng = n_pages // PG
    n_sel = min(TOPK_MAX, (past + DS) // 4)
    idx_bits = int(math.ceil(math.log2(past + LANE)))
    s_pad = (ng + 1) * GW

    def page_spec(width, i):
        return pl.BlockSpec((None, None, width, PAGE_SIZE),
                            lambda b, g, pt: (layer, pt[b, jnp.minimum(g, ng - 1) * PG + i], 0, 0))

    row = lambda ci: (lambda b, g, pt: (b, ci))
    cp = pltpu.CompilerParams(dimension_semantics=("arbitrary", "arbitrary"), vmem_limit_bytes=VMEM_LIMIT)
    scores = pl.pallas_call(
        functools.partial(_dsa_s_scores_kernel, ng=ng, ds=DS),
        grid_spec=pltpu.PrefetchScalarGridSpec(
            num_scalar_prefetch=1, grid=(B, ng + 1),
            in_specs=[pl.BlockSpec((DS, LANE), row(OFF_QI // LANE)), pl.BlockSpec((DS, LANE), row(OFF_SM // LANE))]
                     + [page_spec(D_IDX, i) for i in range(PG)],
            out_specs=pl.BlockSpec((None, DS, GW), lambda b, g, pt: (b, 0, g))),
        out_shape=jax.ShapeDtypeStruct((B, DS, s_pad), F32),
        compiler_params=cp, name="dsa_sample_scores",
    )(page_table, P, P, *([cache_kidx] * PG))

    return pl.pallas_call(
        functools.partial(_dsa_s_attn_kernel, ng=ng, ds=DS, n_sel=n_sel, idx_bits=idx_bits),
        grid_spec=pltpu.PrefetchScalarGridSpec(
            num_scalar_prefetch=1, grid=(B, ng + 1),
            in_specs=[pl.BlockSpec(memory_space=pltpu.SMEM),
                      pl.BlockSpec((None, DS, s_pad), lambda b, g, pt: (b, 0, 0)),
                      pl.BlockSpec((DS, W_B), row(OFF_QB // W_B)), pl.BlockSpec((DS, LANE), row(OFF_KV // LANE))]
                     + [page_spec(DH_B, i) for i in range(PG)] + [page_spec(DH_B, i) for i in range(PG)],
            out_specs=pl.BlockSpec((DS, W_B), lambda b, g, pt: (b, 0)),
            scratch_shapes=[pltpu.VMEM((DS, s_pad), I32), pltpu.VMEM((2, DS, LANE), I32), pltpu.VMEM((3, H_B * DS, LANE), F32),
                            pltpu.VMEM((H_B * DS, 1), F32), pltpu.VMEM((H_B * DS, 1), F32), pltpu.VMEM((H_B * DS, DH_B), F32)]),
        out_shape=jax.ShapeDtypeStruct((B * DS, W_B), F32),
        compiler_params=cp, name="dsa_sample_attn",
    )(page_table, rel_bias, scores, P, P, *([cache_k] * PG), *([cache_v] * PG))


def _merge_kernel(x_ref, oa_ref, ob_ref, oc_ref, od_ref, gate_ref, wb_ref, wo_ref, o_ref):
    m = None
    r0 = 0
    for br, ref in enumerate((oa_ref, ob_ref, oc_ref, od_ref)):
        w = ref.shape[1]
        t = _sigmoid(gate_ref[:, br * D_MODEL:(br + 1) * D_MODEL]) * jnp.dot(ref[...].astype(BF16), wb_ref[r0:r0 + w, :], preferred_element_type=F32)
        m = t if m is None else m + t
        r0 += w
    o_ref[...] = x_ref[...] + jnp.dot(m.astype(BF16), wo_ref[...], preferred_element_type=F32)


def _merge(x2d, oa, ob, oc, od, P, wb, wo):
    n = x2d.shape[0]
    tm = min(n, 256)
    row = lambda i: (i, 0)
    return pl.pallas_call(
        _merge_kernel,
        grid=(n // tm,),
        in_specs=[pl.BlockSpec((tm, D_MODEL), row), pl.BlockSpec((tm, W_A), row), pl.BlockSpec((tm, W_B), row),
                  pl.BlockSpec((tm, W_C), row), pl.BlockSpec((tm, W_D), row),
                  pl.BlockSpec((tm, N_BRANCH * D_MODEL), lambda i: (i, OFF_GATE // (N_BRANCH * D_MODEL))),
                  _const_spec((MIX_W, D_MODEL)), _const_spec((D_MODEL, D_MODEL))],
        out_specs=pl.BlockSpec((tm, D_MODEL), row),
        out_shape=jax.ShapeDtypeStruct((n, D_MODEL), F32),
        compiler_params=pltpu.CompilerParams(dimension_semantics=("arbitrary",), vmem_limit_bytes=VMEM_LIMIT),
        name="merge",
    )(x2d, oa, ob, oc, od, P, wb, wo)


def _mlp_kernel(x_ref, g_ref, wu_ref, wd_ref, gf_ref, *refs, final):
    if final:
        o_ref, y_ref, h_scr, acc_scr = refs
    else:
        o_ref, h_scr, acc_scr = refs
    f = pl.program_id(1)

    @pl.when(f == 0)
    def _():
        x = x_ref[...]
        r = lax.rsqrt(jnp.mean(x * x, axis=-1, keepdims=True) + EPS)
        h_scr[...] = ((x * r) * g_ref[...]).astype(BF16)
        acc_scr[...] = jnp.zeros(acc_scr.shape, F32)

    a = jnp.maximum(jnp.dot(h_scr[...], wu_ref[...], preferred_element_type=F32), 0.0)
    acc_scr[...] += jnp.dot((a * a).astype(BF16), wd_ref[...], preferred_element_type=F32)

    @pl.when(f == pl.num_programs(1) - 1)
    def _():
        out = x_ref[...] + acc_scr[...]
        o_ref[...] = out
        if final:
            r = lax.rsqrt(jnp.mean(out * out, axis=-1, keepdims=True) + EPS)
            y_ref[...] = (out * r) * gf_ref[...]


def _mlp(x2d, g, wu, wd, gf, final):
    n = x2d.shape[0]
    tm = min(n, 512)
    tf = 1024
    row = lambda i, f: (i, 0)
    out_spec = pl.BlockSpec((tm, D_MODEL), row)
    shp = jax.ShapeDtypeStruct((n, D_MODEL), F32)
    return pl.pallas_call(
        functools.partial(_mlp_kernel, final=final),
        grid=(n // tm, D_FF // tf),
        in_specs=[pl.BlockSpec((tm, D_MODEL), row), _const_spec((1, D_MODEL)),
                  pl.BlockSpec((D_MODEL, tf), lambda i, f: (0, f)), pl.BlockSpec((tf, D_MODEL), lambda i, f: (f, 0)),
                  _const_spec((1, D_MODEL))],
        out_specs=[out_spec, out_spec] if final else [out_spec],
        out_shape=[shp, shp] if final else [shp],
        scratch_shapes=[pltpu.VMEM((tm, D_MODEL), BF16), pltpu.VMEM((tm, D_MODEL), F32)],
        compiler_params=pltpu.CompilerParams(dimension_semantics=("arbitrary", "arbitrary"), vmem_limit_bytes=VMEM_LIMIT),
        name="mlp_final" if final else "mlp",
    )(x2d, g.reshape(1, D_MODEL), wu, wd, gf.reshape(1, D_MODEL))


def _layer(x, conv_state, s_delta, s_ssm, s_hgrn, lw, consts, sample_ctx, final):
    B, T, _ = x.shape
    x2 = x.reshape(B * T, D_MODEL)
    P = _in_proj(x2, lw['norm_mix'], lw['w_in'])

    def conv0(lo, hi):
        if conv_state is None:
            return jnp.zeros((B, SUB, hi - lo), F32)
        return jnp.concatenate([jnp.zeros((B, SUB - (CONV_W - 1), hi - lo), F32), conv_state[:, :, lo:hi]], axis=1)

    zeros_state = lambda n: jnp.zeros((B, n, 64, 64), F32)
    c = consts
    half = CONV_CH // 2
    assert CONV_A == half and OFF_PC % half == 0
    oa, s_a = _mix_call(
        _mix_a_kernel, "mix_a", P, B, T,
        [(OFF_PC, half), (OFF_ZA, W_A), (OFF_SM, LANE)],
        [conv0(0, CONV_A), zeros_state(H_A) if s_delta is None else s_delta],
        [lw['conv_w'][:, 0:CONV_A], lw['conv_b'][None, 0:CONV_A], lw['hp'], lw['hpt'], lw['gn_a'], c['tril2'], c['triu2'], c['bo64']],
        W_A, (H_A, DK_A, DV_A),
        [pltpu.VMEM((TC + SUB, CONV_A), F32), pltpu.VMEM((H_A, DK_A, DV_A), F32), pltpu.VMEM((TC, W_A), F32)])
    oc, s_c = _mix_call(
        _mix_c_kernel, "mix_c", P, B, T,
        [(OFF_PC + half, half), (OFF_ZC, W_C), (OFF_SM, LANE)],
        [conv0(CONV_A, CONV_CH), zeros_state(H_C) if s_ssm is None else s_ssm],
        [lw['conv_w'][:, CONV_A:], lw['conv_b'][None, CONV_A:], lw['hp'], lw['hpt'], lw['gn_c'], c['tril1'], c['triu1'], c['bo256']],
        W_C, (H_C, N_C, P_C),
        [pltpu.VMEM((TC + SUB, CONV_C), F32), pltpu.VMEM((H_C, N_C, P_C), F32), pltpu.VMEM((TC, W_C), F32)])
    sbd0 = jnp.zeros((B, H_D, DK_D, H_D, DV_D), F32)
    if s_hgrn is not None:
        for h in range(H_D):
            sbd0 = sbd0.at[:, h, :, h, :].set(s_hgrn[:, h])
    sbd0 = sbd0.reshape(B, H_D * DK_D, W_D)
    od, sbd = _mix_call(
        _mix_d_kernel, "mix_d", P, B, T,
        [(OFF_D, 4 * W_D)],
        [sbd0],
        [lw['lb'], lw['gn_d'], c['tril2'], c['sel'], c['bo64']],
        W_D, (H_D * DK_D, W_D),
        [pltpu.VMEM((H_D * DK_D, W_D), F32), pltpu.VMEM((TC, W_D), F32)])
    sbd5 = sbd.reshape(B, H_D, DK_D, H_D, DV_D)
    s_d = jnp.stack([sbd5[:, h, :, h, :] for h in range(H_D)], axis=1)
    if sample_ctx is None:
        ob = _dsa_prompt(P, lw['rel_bias'], B, T)
    else:
        ob = _dsa_sample(P, lw['rel_bias'], sample_ctx['cache_k'], sample_ctx['cache_v'], sample_ctx['cache_kidx'],
                         sample_ctx['page_table'], sample_ctx['layer'], B, T)
    x1 = _merge(x2, oa, ob, oc, od, P, lw['w_branch'], lw['w_out'])
    outs = _mlp(x1, lw['norm_mlp'], lw['w_up'], lw['w_down'], lw['norm_final'], final)
    xo = outs[0].reshape(B, T, D_MODEL)
    y = outs[1].reshape(B, T, D_MODEL) if final else None
    P3 = P.reshape(B, T, P_PAD)
    states = (P3[:, :, OFF_KV:OFF_KV + DH_B], P3[:, :, OFF_KV + DH_B:OFF_KV + 2 * DH_B], P3[:, :, OFF_SM + SM_KI:OFF_SM + SM_KI + D_IDX],
              P3[:, T - (CONV_W - 1):, OFF_PC:OFF_PC + CONV_CH], s_a, s_c, s_d)
    return xo, y, states


def kernel(x_prompt, x_sample, cache_k, cache_v, cache_kidx, state_conv, state_delta, state_ssm, state_hgrn, page_table, norm_mix, w_in, conv_w, conv_b, a_log_a, dt_bias_a, gnorm_a, rel_bias, a_log_c, dt_bias_c, d_skip_c, gnorm_c, hgrn_gamma, gnorm_d, w_branch, w_out, norm_mlp, w_up, w_down, norm_final):
    depth = w_in.shape[0]
    npc = _np_consts()
    consts = {k: jnp.asarray(v, BF16) for k, v in npc.items()}
    pg = jax.nn.softmax(hgrn_gamma.astype(F32), axis=0)
    lower_bounds = jnp.cumsum(pg, axis=0) - pg[0]
    cache_kt, cache_vt, cache_kit = (jnp.swapaxes(c, 2, 3) for c in (cache_k, cache_v, cache_kidx))
    yp, ys = x_prompt, x_sample
    new_p, new_s = [], []
    for l in range(depth):
        hp = jnp.zeros((SUB, LANE), F32)
        hp = hp.at[0, 0:H_A].set(a_log_a[l]).at[1, 0:H_A].set(dt_bias_a[l])
        hp = hp.at[2, 0:H_C].set(a_log_c[l]).at[3, 0:H_C].set(dt_bias_c[l]).at[4, 0:H_C].set(d_skip_c[l])
        lw = dict(norm_mix=norm_mix[l], w_in=_permute_w_in(w_in[l]), conv_w=conv_w[l], conv_b=conv_b[l],
                  hp=hp, hpt=hp.T, gn_a=jnp.tile(gnorm_a[l], H_A)[None, :], gn_c=gnorm_c[l][None, :],
                  gn_d=jnp.tile(gnorm_d[l], H_D)[None, :], lb=lower_bounds[l][None, :], rel_bias=rel_bias,
                  w_branch=w_branch[l].astype(BF16), w_out=w_out[l].astype(BF16), norm_mlp=norm_mlp[l],
                  w_up=w_up[l].astype(BF16), w_down=w_down[l].astype(BF16), norm_final=norm_final)
        final = l == depth - 1
        yp, yp_n, st_p = _layer(yp, None, None, None, None, lw, consts, None, final)
        ctx = dict(cache_k=cache_kt, cache_v=cache_vt, cache_kidx=cache_kit, page_table=page_table, layer=l)
        ys, ys_n, st_s = _layer(ys, state_conv[l], state_delta[l], state_ssm[l], state_hgrn[l], lw, consts, ctx, final)
        new_p.append(st_p)
        new_s.append(st_s)
    stack = lambda per_layer: [jnp.stack(items) for items in zip(*per_layer)]
    return (yp_n, ys_n, *stack(new_p), *stack(new_s))
```

```python
import functools
import math

import numpy as np
import jax
import jax.numpy as jnp
from jax import lax
from jax.experimental import pallas as pl
from jax.experimental.pallas import tpu as pltpu

F32 = jnp.float32
BF16 = jnp.bfloat16
I32 = jnp.int32

D_MODEL = 1024
PAGE_SIZE = 128
EPS = 1e-6
NEG = -1e30
TINY = 1e-30
CONV_W = 4
D_FF = 4 * D_MODEL
N_BRANCH = 4
H_A, DK_A, DV_A = 4, 64, 64
H_B, DH_B = 4, 64
H_IDX, D_IDX = 4, 32
TOPK_MAX = 256
N_BUCKETS = 32
MAX_DISTANCE = 128
H_C, P_C, N_C, G_C = 8, 64, 64, 2
H_D, DK_D, DV_D = 4, 64, 64
QK_A = H_A * DK_A
CONV_A = 2 * QK_A + H_A * DV_A
D_INNER_C = H_C * P_C
CONV_C = D_INNER_C + 2 * G_C * N_C
CONV_CH = CONV_A + CONV_C
W_A, W_B, W_C, W_D = H_A * DV_A, H_B * DH_B, D_INNER_C, H_D * DV_D
MIX_W = W_A + W_B + W_C + W_D
IN_WIDTHS = (CONV_CH, W_A, H_A, H_A, W_B, DH_B, DH_B, H_IDX * D_IDX, D_IDX, H_IDX, D_INNER_C, H_C,
             H_D * DK_D, H_D * DK_D, W_D, W_D, N_BRANCH * D_MODEL)
P_IN = sum(IN_WIDTHS)

LANE = 128
SUB = 8
TC = 128
QB = 128
VMEM_LIMIT = 48 * 1024 * 1024

OFF_GATE, OFF_D, OFF_ZC, OFF_ZA, OFF_QB, OFF_PC, OFF_KV, OFF_QI, OFF_SM = 0, 4096, 5120, 5632, 5888, 6144, 7680, 7808, 7936
P_PAD = 8064
SM_KI, SM_WI, SM_AA, SM_BA, SM_DT = 0, 32, 36, 40, 44


def _src_offsets():
    offs, o = [], 0
    for w in IN_WIDTHS:
        offs.append(o)
        o += w
    return offs


def _permute_w_in(w):
    (o_pc, o_za, o_aa, o_ba, o_qb, o_kb, o_vb, o_qi, o_ki, o_wi, o_zc, o_dt, o_qd, o_fd, o_id, o_gd, o_gate) = _src_offsets()
    wt = jnp.swapaxes(w, 0, 1)
    seg = lambda o, n: wt[o:o + n, :]
    small = jnp.concatenate([seg(o_ki, D_IDX), seg(o_wi, H_IDX), seg(o_aa, H_A), seg(o_ba, H_A), seg(o_dt, H_C),
                             jnp.zeros((LANE - (D_IDX + H_IDX + 2 * H_A + H_C), w.shape[0]), w.dtype)], axis=0)
    out = jnp.concatenate([
        seg(o_gate, N_BRANCH * D_MODEL),
        seg(o_qd, 4 * W_D),
        seg(o_zc, D_INNER_C), seg(o_za, W_A), seg(o_qb, W_B), seg(o_pc, CONV_CH),
        seg(o_kb, 2 * DH_B),
        seg(o_qi, H_IDX * D_IDX), small], axis=0)
    assert out.shape[0] == P_PAD
    return out.astype(BF16)


def _bdot(a, b):
    return jnp.dot(a.astype(BF16), b.astype(BF16), preferred_element_type=F32)


def _bdot_nt(a, b):
    return lax.dot_general(a.astype(BF16), b.astype(BF16), (((1,), (1,)), ((), ())), preferred_element_type=F32)


def _split2(a):
    hi = a.astype(BF16)
    lo = (a - hi.astype(F32)).astype(BF16)
    return hi, lo


def _split3(a):
    hi = a.astype(BF16)
    r = a - hi.astype(F32)
    mid = r.astype(BF16)
    lo = (r - mid.astype(F32)).astype(BF16)
    return hi, mid, lo


def _dot01_left(m01, x):
    hi, mid, lo = _split3(x)
    d = lambda p: jnp.dot(m01, p, preferred_element_type=F32)
    return d(hi) + (d(mid) + d(lo))


def _dot01_right(x, m01):
    hi, mid, lo = _split3(x)
    d = lambda p: jnp.dot(p, m01, preferred_element_type=F32)
    return d(hi) + (d(mid) + d(lo))


def _xdot(a, b):
    ah, al = _split2(a)
    bh, bl = _split2(b)
    d = lambda p, q: jnp.dot(p, q, preferred_element_type=F32)
    return d(ah, bh) + (d(ah, bl) + d(al, bh))


def _sigmoid(x):
    return 1.0 / (1.0 + jnp.exp(-x))


def _silu(x):
    return x * _sigmoid(x)


def _softplus(x):
    return jnp.maximum(x, 0.0) + jnp.log(1.0 + jnp.exp(-jnp.abs(x)))


def _iota2(shape, dim):
    return lax.broadcasted_iota(I32, shape, dim)


def _pad_rows(x, rows):
    if x.shape[0] == rows:
        return x
    return jnp.concatenate([x, jnp.zeros((rows - x.shape[0],) + x.shape[1:], x.dtype)], axis=0)


def _tri_inv(L, top):
    return _tri_inv_many([L], top)[0]


def _tri_inv_many(Ls, top):
    n = Ls[0].shape[0]
    ii, jj = _iota2((n, n), 0), _iota2((n, n), 1)
    xor = ii ^ jj
    eye = jnp.where(ii == jj, 1.0, 0.0)
    Ns = [jnp.where((xor >> 3) == 0, -L, 0.0) for L in Ls]
    Xs = [eye + N for N in Ns]
    N2s = [_bdot(N, N) for N in Ns]
    Xs = [X + _bdot(X, N2) for X, N2 in zip(Xs, N2s)]
    N4s = [_bdot(N2, N2) for N2 in N2s]
    Xs = [X + _bdot(X, N4) for X, N4 in zip(Xs, N4s)]
    sh = 4
    while (1 << sh) <= top:
        XBs = [_bdot(X, jnp.where((xor >> (sh - 1)) == 1, L, 0.0)) for X, L in zip(Xs, Ls)]
        Xs = [X - _bdot(XB, X) for X, XB in zip(Xs, XBs)]
        sh += 1
    Rs = [eye - X - _xdot(L, X) for X, L in zip(Xs, Ls)]
    return [X + _bdot(X, R) for X, R in zip(Xs, Rs)]


def _conv_silu(pc, ext_scr, cw_ref, cb_ref):
    ext_scr[SUB:SUB + TC, :] = pc
    y = cb_ref[...]
    for j in range(CONV_W):
        y = y + ext_scr[SUB - (CONV_W - 1) + j:SUB - (CONV_W - 1) + j + TC, :] * cw_ref[j:j + 1, :]
    ext_scr[0:SUB, :] = ext_scr[TC:TC + SUB, :]
    return _silu(y)


TN_IN = 1152
PQ_W = W_B + 2 * LANE
PK_W = 2 * LANE
PQ_QB, PQ_QI, PQ_SM = 0, W_B, W_B + LANE
PK_KV, PK_SM = 0, LANE
_COMPACT_COPIES = (("pq", PQ_QB, OFF_QB, W_B), ("pq", PQ_QI, OFF_QI, LANE), ("pq", PQ_SM, OFF_SM, LANE),
                   ("pk", PK_KV, OFF_KV, LANE), ("pk", PK_SM, OFF_SM, LANE))


def _in_proj_kernel(x_ref, g_ref, w_ref, o_ref, pq_ref, pk_ref, h_scr):
    j = pl.program_id(1)

    @pl.when(j == 0)
    def _():
        x = x_ref[...]
        r = lax.rsqrt(jnp.mean(x * x, axis=-1, keepdims=True) + EPS)
        h_scr[...] = ((x * r) * g_ref[...]).astype(BF16)

    o_ref[...] = lax.dot_general(h_scr[...], w_ref[...], (((1,), (1,)), ((), ())), preferred_element_type=F32)

    def copy(dst_ref, dst, src, width):
        tile, lo = divmod(src, TN_IN)
        assert lo + width <= TN_IN

        @pl.when(j == tile)
        def _():
            dst_ref[:, dst:dst + width] = o_ref[:, lo:lo + width]

    for name, dst, src, width in _COMPACT_COPIES:
        copy(pq_ref if name == "pq" else pk_ref, dst, src, width)


def _in_proj(x2d, g, w_perm_t):
    n = x2d.shape[0]
    tm = min(n, 1024)
    tn = TN_IN
    return pl.pallas_call(
        _in_proj_kernel,
        grid=(n // tm, P_PAD // tn),
        in_specs=[pl.BlockSpec((tm, D_MODEL), lambda i, j: (i, 0)),
                  pl.BlockSpec((1, D_MODEL), lambda i, j: (0, 0)),
                  pl.BlockSpec((tn, D_MODEL), lambda i, j: (j, 0))],
        out_specs=[pl.BlockSpec((tm, tn), lambda i, j: (i, j)),
                   pl.BlockSpec((tm, PQ_W), lambda i, j: (i, 0)), pl.BlockSpec((tm, PK_W), lambda i, j: (i, 0))],
        out_shape=[jax.ShapeDtypeStruct((n, P_PAD), F32), jax.ShapeDtypeStruct((n, PQ_W), F32),
                   jax.ShapeDtypeStruct((n, PK_W), F32)],
        scratch_shapes=[pltpu.VMEM((tm, D_MODEL), BF16)],
        compiler_params=pltpu.CompilerParams(dimension_semantics=("arbitrary", "arbitrary"), vmem_limit_bytes=VMEM_LIMIT),
        name="in_proj",
    )(x2d, g.reshape(1, D_MODEL), w_perm_t)


def _mix_a_kernel(pc_ref, za_ref, sm_ref, conv0_ref, s0_ref, cw_ref, cb_ref, hp_ref, hpt_ref, gn_ref,
                  tril_ref, triu_ref, bo_ref, o_ref, sout_ref, ext_scr, s_scr, ob_scr, *, t_valid, rows):
    CA = 64

    @pl.when(pl.program_id(1) == 0)
    def _():
        ext_scr[0:SUB, :] = conv0_ref[...]
        s_scr[...] = s0_ref[...]

    u = _conv_silu(_pad_rows(pc_ref[...], TC), ext_scr, cw_ref, cb_ref)
    bo = bo_ref[...]
    q_raw, k_raw, va = u[:, 0:QK_A], u[:, QK_A:2 * QK_A], u[:, 2 * QK_A:CONV_A]
    qq = q_raw * lax.rsqrt(_dot01_right(q_raw * q_raw, bo) + EPS) * (DK_A ** -0.5)
    ka = k_raw * lax.rsqrt(_dot01_right(k_raw * k_raw, bo) + EPS)
    kat = ka.T

    sm = _pad_rows(sm_ref[...], TC)
    smt = sm.T
    hp, hpt = hp_ref[...], hpt_ref[...]
    g_col = -jnp.exp(hp[0:1, 0:H_A]) * _softplus(sm[:, SM_AA:SM_AA + H_A] + hp[1:2, 0:H_A])
    beta = _sigmoid(sm[:, SM_BA:SM_BA + H_A])
    g_row = -jnp.exp(hpt[0:H_A, 0:1]) * _softplus(smt[SM_AA:SM_AA + H_A, :] + hpt[0:H_A, 1:2])
    if t_valid < TC:
        g_col = jnp.where(_iota2(g_col.shape, 0) < t_valid, g_col, 0.0)
        beta = jnp.where(_iota2(beta.shape, 0) < t_valid, beta, 0.0)
        g_row = jnp.where(_iota2(g_row.shape, 1) < t_valid, g_row, 0.0)
    gcum_col = _dot01_left(tril_ref[...], g_col)
    gcum_row = _dot01_right(g_row, triu_ref[...])

    HS = H_A * CA
    ii, jj = _iota2((HS, HS), 0), _iota2((HS, HS), 1)
    same = (ii >> 6) == (jj >> 6)
    incl, strict = same & (ii >= jj), same & (ii > jj)
    nsc = -(-t_valid // CA)
    pre = []
    for sc in range(nsc):
        r0 = sc * CA
        stack = lambda x, w: jnp.concatenate([x[r0:r0 + CA, h * w:(h + 1) * w] for h in range(H_A)], axis=0)
        ks, qs, vs = stack(ka, DK_A), stack(qq, DK_A), stack(va, DV_A)
        bcol, gc = stack(beta, 1), stack(gcum_col, 1)
        gr = jnp.concatenate([gcum_row[h:h + 1, r0:r0 + CA] for h in range(H_A)], axis=1)
        e = jnp.exp(jnp.where(incl, gc - gr, 0.0))
        kb = ks * bcol
        eg = jnp.exp(gc)
        pre.append(dict(r0=r0, gc=gc, qe=qs * eg, low=jnp.where(strict, _bdot_nt(kb, ks) * e, 0.0),
                        rhs=jnp.concatenate([vs * bcol, kb * eg], axis=1),
                        attn=jnp.where(incl, _bdot_nt(qs, ks) * e, 0.0)))
    tinvs = _tri_inv_many([p['low'] for p in pre], CA)
    sols = [_xdot(t, p['rhs']) for t, p in zip(tinvs, pre)]
    hs = lambda x, h: x[h * CA:(h + 1) * CA]
    for p, sol in zip(pre, sols):
        r0, gc = p['r0'], p['gc']
        states = [s_scr[h] for h in range(H_A)]
        v_new = jnp.concatenate([hs(sol, h)[:, 0:DV_A] - _bdot(hs(sol, h)[:, DV_A:], states[h]) for h in range(H_A)], axis=0)
        o = jnp.concatenate([_bdot(hs(p['qe'], h), states[h]) for h in range(H_A)], axis=0) + _bdot(p['attn'], v_new)
        for h in range(H_A):
            gch = hs(gc, h)
            gl = gch[CA - 1:CA, :]
            s_scr[h] = states[h] * jnp.exp(gl) + _bdot(kat[h * DK_A:(h + 1) * DK_A, r0:r0 + CA], hs(v_new, h) * jnp.exp(gl - gch))
            ob_scr[r0:r0 + CA, h * DV_A:(h + 1) * DV_A] = hs(o, h)
    if t_valid <= CA:
        ob_scr[CA:TC, :] = jnp.zeros((TC - CA, W_A), F32)

    o = ob_scr[...]
    o = o * lax.rsqrt(_dot01_right(o * o, bo) * (1.0 / DV_A) + EPS) * gn_ref[...]
    res = o * _silu(_pad_rows(za_ref[...], TC))
    o_ref[...] = res[0:rows]
    sout_ref[...] = s_scr[...]


def _mix_c_kernel(pc_ref, zc_ref, sm_ref, conv0_ref, s0_ref, cw_ref, cb_ref, hp_ref, hpt_ref, gn_ref,
                  tril_ref, triu_ref, bo_ref, o_ref, sout_ref, ext_scr, s_scr, yb_scr, *, t_valid, rows):
    @pl.when(pl.program_id(1) == 0)
    def _():
        ext_scr[0:SUB, :] = conv0_ref[...]
        s_scr[...] = s0_ref[...]

    u = _conv_silu(_pad_rows(pc_ref[...], TC), ext_scr, cw_ref, cb_ref)
    xc = u[:, 0:D_INNER_C]
    bcm = u[:, D_INNER_C:D_INNER_C + G_C * N_C]
    ccm = u[:, D_INNER_C + G_C * N_C:CONV_C]
    bct = bcm.T
    zc = _pad_rows(zc_ref[...], TC)

    sm = _pad_rows(sm_ref[...], TC)
    smt = sm.T
    hp, hpt = hp_ref[...], hpt_ref[...]
    dt_col = _softplus(sm[:, SM_DT:SM_DT + H_C] + hp[3:4, 0:H_C])
    dt_row = _softplus(smt[SM_DT:SM_DT + H_C, :] + hpt[0:H_C, 3:4])
    if t_valid < TC:
        dt_col = jnp.where(_iota2(dt_col.shape, 0) < t_valid, dt_col, 0.0)
        dt_row = jnp.where(_iota2(dt_row.shape, 1) < t_valid, dt_row, 0.0)
    gcum_col = _dot01_left(tril_ref[...], -jnp.exp(hp[2:3, 0:H_C]) * dt_col)
    gcum_row = _dot01_right(-jnp.exp(hpt[0:H_C, 2:3]) * dt_row, triu_ref[...])

    incl = _iota2((TC, TC), 0) >= _iota2((TC, TC), 1)
    hg = H_C // G_C
    for g in range(G_C):
        cc_g = ccm[:, g * N_C:(g + 1) * N_C]
        gram = _bdot_nt(cc_g, bcm[:, g * N_C:(g + 1) * N_C])
        for hh in range(hg):
            h = g * hg + hh
            c0 = h * P_C
            gc = gcum_col[:, h:h + 1]
            gr = gcum_row[h:h + 1, :]
            dec = jnp.where(incl, jnp.exp(jnp.where(incl, gc - gr, 0.0)), 0.0)
            attn = gram * dt_row[h:h + 1, :] * dec
            xh = xc[:, c0:c0 + P_C]
            S = s_scr[h]
            o = jnp.exp(gc) * _bdot(cc_g, S) + _bdot(attn, xh)
            gl = gc[TC - 1:TC, :]
            s_scr[h] = S * jnp.exp(gl) + _bdot(bct[g * N_C:(g + 1) * N_C, :], xh * (dt_col[:, h:h + 1] * jnp.exp(gl - gc)))
            yb_scr[:, c0:c0 + P_C] = (o + hp[4:5, h:h + 1] * xh) * _silu(zc[:, c0:c0 + P_C])

    y = yb_scr[...]
    y = y * lax.rsqrt(_dot01_right(y * y, bo_ref[...]) * (1.0 / (D_INNER_C // G_C)) + EPS) * gn_ref[...]
    o_ref[...] = y[0:rows]
    sout_ref[...] = s_scr[...]


_D_LEVELS = 6


def _mix_d_kernel(dg_ref, s0_ref, lb_ref, gn_ref, tril_ref, sel_ref, bo_ref, o_ref, sout_ref, s_scr, ob_scr, *, t_valid, rows):
    CD = 64
    W = H_D * DK_D

    @pl.when(pl.program_id(1) == 0)
    def _():
        s_scr[...] = s0_ref[...]

    dg = _pad_rows(dg_ref[...], TC)
    q = _silu(dg[:, 0:W])
    fr = dg[:, W:2 * W]
    v = dg[:, 2 * W:3 * W]
    gd = dg[:, 3 * W:4 * W]
    lb = lb_ref[...]
    logf = jnp.log(jnp.maximum(lb + (1.0 - lb) * _sigmoid(fr), TINY))
    kd = (1.0 - lb) * _sigmoid(-fr)
    if t_valid < TC:
        ok = _iota2((TC, W), 0) < t_valid
        logf = jnp.where(ok, logf, 0.0)
        kd = jnp.where(ok, kd, 0.0)
    gcum = _dot01_left(tril_ref[...], logf)
    gm = _dot01_left(sel_ref[...], gcum)
    qdec = q * jnp.exp(gcum)
    nsc = -(-t_valid // CD)
    gl_rows = jnp.concatenate([jnp.broadcast_to(gcum[(s + 1) * CD - 1:(s + 1) * CD, :], (CD, W)) for s in range(TC // CD)], axis=0)
    kdt = (kd * jnp.exp(gl_rows - gcum)).T
    gct = gcum.T

    HS = H_D * CD
    head_lanes = (_iota2((HS, W), 0) >> 6) == (_iota2((HS, W), 1) >> 6)
    stack = lambda x: jnp.where(head_lanes, jnp.concatenate([x] * H_D, axis=0), 0.0)
    ci, jj = _iota2((HS, CD), 0) & (CD - 1), _iota2((HS, CD), 1)
    later = jnp.where(ci > jj, ci ^ jj, 0)
    lhs, rhs, msk = [q], [kd], [ci == jj]
    for lv in range(1, _D_LEVELS + 1):
        gml = gm[(lv - 1) * TC:lv * TC, :]
        lhs.append(q * jnp.exp(jnp.minimum(gcum - gml, 0.0)))
        rhs.append(kd * jnp.exp(jnp.minimum(gml - gcum, 0.0)))
        msk.append((later >> (lv - 1)) == 1)

    lane_head = _iota2((CD, W), 1) >> 6
    intra = []
    for sc in range(nsc):
        r0 = sc * CD
        attn = jnp.zeros((HS, CD), F32)
        for a, b, m in zip(lhs, rhs, msk):
            attn = attn + jnp.where(m, _bdot_nt(stack(a[r0:r0 + CD]), b[r0:r0 + CD]), 0.0)
        full = _bdot(attn, v[r0:r0 + CD])
        o = jnp.zeros((CD, W), F32)
        for h in range(H_D):
            o = o + jnp.where(lane_head == h, full[h * CD:(h + 1) * CD], 0.0)
        intra.append(o)
    for sc in range(nsc):
        r0 = sc * CD
        sbd = s_scr[...]
        ob_scr[r0:r0 + CD, :] = _bdot(qdec[r0:r0 + CD], sbd) + intra[sc]
        decay = jnp.exp(gct[:, r0 + CD - 1:r0 + CD])
        s_scr[...] = jnp.where(head_lanes, sbd * decay + _bdot(kdt[:, r0:r0 + CD], v[r0:r0 + CD]), 0.0)
    if nsc < TC // CD:
        ob_scr[nsc * CD:TC, :] = jnp.zeros((TC - nsc * CD, W), F32)

    o = ob_scr[...]
    o = o * lax.rsqrt(_dot01_right(o * o, bo_ref[...]) * (1.0 / DV_D) + EPS) * gn_ref[...]
    res = o * _silu(gd)
    o_ref[...] = res[0:rows]
    sout_ref[...] = s_scr[...]


def _np_consts():
    i = np.arange(TC)
    same64 = (i[:, None] // 64) == (i[None, :] // 64)
    tril2 = (same64 & (i[:, None] >= i[None, :])).astype(np.float32)
    tril1 = (i[:, None] >= i[None, :]).astype(np.float32)
    sel = np.zeros((_D_LEVELS * TC, TC), np.float32)
    for lv in range(1, _D_LEVELS + 1):
        m = ((i >> lv) << lv) + (1 << (lv - 1)) - 1
        sel[(lv - 1) * TC + i, m] = 1.0
    bo = lambda n, w: ((np.arange(n)[:, None] // w) == (np.arange(n)[None, :] // w)).astype(np.float32)
    return dict(tril2=tril2, triu2=tril2.T.copy(), tril1=tril1, triu1=tril1.T.copy(), sel=sel,
                bo64=bo(256, 64), bo256=bo(D_INNER_C, D_INNER_C // G_C))


def _bucket_starts():
    max_exact = N_BUCKETS // 2
    d = np.arange(0, 4 * MAX_DISTANCE)
    lr = np.log(np.maximum(d, 1).astype(np.float32) / max_exact) / math.log(MAX_DISTANCE / max_exact)
    large = np.minimum(max_exact + (np.maximum(lr, 0.0) * (N_BUCKETS - max_exact)).astype(np.int32), N_BUCKETS - 1)
    bucket = np.where(d < max_exact, d, large)
    return [int(np.argmax(bucket == b)) for b in range(N_BUCKETS)]


_BUCKET_STARTS = _bucket_starts()
_SAT_DIST = _BUCKET_STARTS[N_BUCKETS - 1]


def _bias_from_dist(d, rb_ref, h):
    out = jnp.full(d.shape, rb_ref[N_BUCKETS - 1, h], F32)
    for b in range(N_BUCKETS - 2, -1, -1):
        out = jnp.where(d < _BUCKET_STARTS[b + 1], rb_ref[b, h], out)
    return out


def _sortable(x):
    b = lax.bitcast_convert_type(x + 0.0, I32)
    return jnp.where(b < 0, b ^ 0x7FFFFFFF, b)


_NEG_KEY = int(np.array([NEG], np.float32).view(np.int32)[0]) ^ 0x7FFFFFFF
_INT_MIN = -2 ** 31


def _selected_i32(k, idx, thr, xthr):
    s = jnp.where(k > thr, 1, jnp.where(k == thr, jnp.where(idx <= xthr, 1, 0), 0))
    return jnp.where(k == _NEG_KEY, 0, s)


def _selected(k, idx, thr, xthr):
    return _selected_i32(k, idx, thr, xthr) > 0


def _const_spec(shape):
    nd = len(shape)
    return pl.BlockSpec(shape, lambda *_: (0,) * nd)


def _mix_call(kernel_fn, name, P, B, T, seq_specs, batch_args, const_args, out_w, st_tail, scratch):
    rows = min(T, TC)
    nc = max(T // TC, 1)
    assert rows * nc == T
    in_specs = [pl.BlockSpec((rows, w), functools.partial(lambda b, c, ci: (b * nc + c, ci), ci=off // w)) for off, w in seq_specs]
    args = [P] * len(seq_specs)
    for a in batch_args:
        nd = a.ndim
        in_specs.append(pl.BlockSpec((None,) + a.shape[1:], functools.partial(lambda b, c, nd: (b,) + (0,) * (nd - 1), nd=nd)))
        args.append(a)
    for a in const_args:
        in_specs.append(_const_spec(a.shape))
        args.append(a)
    st_nd = len(st_tail)
    return pl.pallas_call(
        functools.partial(kernel_fn, t_valid=rows, rows=rows),
        grid=(B, nc),
        in_specs=in_specs,
        out_specs=[pl.BlockSpec((rows, out_w), lambda b, c: (b * nc + c, 0)),
                   pl.BlockSpec((None,) + st_tail, lambda b, c: (b,) + (0,) * st_nd)],
        out_shape=[jax.ShapeDtypeStruct((B * T, out_w), F32), jax.ShapeDtypeStruct((B,) + st_tail, F32)],
        scratch_shapes=scratch,
        compiler_params=pltpu.CompilerParams(dimension_semantics=("arbitrary", "arbitrary"), vmem_limit_bytes=VMEM_LIMIT),
        name=name,
    )(*args)


CU = 4


def _dsa_prompt_kernel(rb_ref, pq_ref, pk_ref, o_ref,
                       kvt_scr, key_scr, lg_scr, bias_scr, acc_scr, *, n_sel, idx_bits):
    j = pl.program_id(1)
    nch = j + 1

    @pl.when((pl.program_id(0) == 0) & (j == 0))
    def _():
        srow, tcol = _iota2((QB, QB), 0), _iota2((QB, QB), 1)
        for h in range(H_B):
            bias_scr[h, 0] = jnp.full((QB, QB), rb_ref[N_BUCKETS - 1, h], F32)
            bias_scr[h, 1] = _bias_from_dist(tcol - srow + QB, rb_ref, h)
            bias_scr[h, 2] = _bias_from_dist(tcol - srow, rb_ref, h)

    @pl.when(j == 0)
    def _():
        def tr(c, _):
            r = pl.multiple_of(c * QB, QB)
            kvt_scr[c] = pk_ref[pl.ds(r, QB), PK_KV:PK_KV + LANE].T[DH_B:2 * DH_B, :].astype(BF16)
            return 0
        lax.fori_loop(0, pk_ref.shape[0] // QB, tr, 0)

    lanes = lambda rows, w: jnp.concatenate([rows[h * w:(h + 1) * w, :] for h in range(H_B)], axis=1)
    w4 = lanes(pq_ref[:, PQ_SM:PQ_SM + LANE].T[SM_WI:SM_WI + H_IDX, :], 1) * (H_IDX ** -0.5 * D_IDX ** -0.5)
    qi_rhs = lanes(pq_ref[:, PQ_QI:PQ_QI + LANE].T, D_IDX).astype(BF16)
    qb_rhs = (lanes(pq_ref[:, PQ_QB:PQ_QB + W_B].T, DH_B) * (DH_B ** -0.5)).astype(BF16)
    srow, tcol = _iota2((QB, QB), 0), _iota2((QB, QB), 1)
    heads = lambda x: [x[:, h * QB:(h + 1) * QB] for h in range(H_B)]

    ngrp = (nch + (CU - 1)) // CU

    def score_body(g, _):
        for u in range(CU):
            r = pl.multiple_of((g * CU + u) * QB, QB)
            ki = pk_ref[pl.ds(r, QB), PK_SM:PK_SM + LANE][:, SM_KI:SM_KI + D_IDX].astype(BF16)
            s = jnp.maximum(jnp.dot(ki, qi_rhs, preferred_element_type=F32), 0.0) * w4
            sh = heads(s)
            sc = (sh[0] + sh[1]) + (sh[2] + sh[3])
            sc = jnp.where(srow + r <= tcol + j * QB, sc, NEG)
            key_scr[pl.ds(r, QB), :] = _sortable(sc)
        return 0

    lax.fori_loop(0, ngrp, score_body, 0)

    def count(pred):
        def body(g, acc):
            for u in range(CU):
                r = pl.multiple_of((g * CU + u) * QB, QB)
                acc = acc + pred(key_scr[pl.ds(r, QB), :], r).reshape(QB // SUB, SUB, QB).sum(axis=0)
            return acc
        return lax.fori_loop(0, ngrp, body, jnp.zeros((SUB, QB), I32)).sum(axis=0, keepdims=True)

    cnt_ge = lambda t: count(lambda k, r: jnp.where(k >= t, 1, 0))
    c0 = cnt_ge(jnp.zeros((1, QB), I32))
    thr = jnp.where(c0 >= n_sel, 0, _INT_MIN).astype(I32)
    cthr = jnp.where(c0 >= n_sel, c0, ngrp * (CU * QB))

    def bit_body(i, carry):
        t, ct = carry
        trial = t | jnp.left_shift(jnp.int32(1), 30 - i)
        cnt = cnt_ge(trial)
        ok = cnt >= n_sel
        return jnp.where(ok, trial, t), jnp.where(ok, cnt, ct)

    thr, cthr = lax.fori_loop(0, 31, bit_body, (thr, cthr))

    @pl.when(jnp.max(cthr) > n_sel)
    def _():
        quota = n_sel - count(lambda k, r: jnp.where(k > thr, 1, 0))

        def idx_body(i, x):
            trial = x | jnp.left_shift(jnp.int32(1), idx_bits - 1 - i)
            below = count(lambda k, r: jnp.where(k == thr, jnp.where(srow + r < trial, 1, 0), 0))
            return jnp.where(below < quota, trial, x)

        xthr = lax.fori_loop(0, idx_bits, idx_body, jnp.zeros((1, QB), I32))

        def demote_body(g, _):
            for u in range(CU):
                r = pl.multiple_of((g * CU + u) * QB, QB)
                k = key_scr[pl.ds(r, QB), :]
                key_scr[pl.ds(r, QB), :] = jnp.where(k == thr, jnp.where(srow + r > xthr, k - 1, k), k)
            return 0

        lax.fori_loop(0, ngrp, demote_body, 0)

    thr_sel = jnp.maximum(thr, _NEG_KEY + 1)

    def logit_body(g, ms):
        ms = list(ms)
        for u in range(CU):
            c = g * CU + u
            r = pl.multiple_of(c * QB, QB)
            kc = pk_ref[pl.ds(r, QB), PK_KV:PK_KV + LANE][:, 0:DH_B].astype(BF16)
            l4 = heads(jnp.dot(kc, qb_rhs, preferred_element_type=F32))
            sel = key_scr[pl.ds(r, QB), :] >= thr_sel
            bidx = jnp.clip(c - j + 2, 0, 2)
            for h in range(H_B):
                l = jnp.where(sel, l4[h] + bias_scr[h, bidx], NEG)
                lg_scr[h, pl.ds(r, QB), :] = l
                ms[h] = jnp.maximum(ms[h], l.max(axis=0, keepdims=True))
        return tuple(ms)

    ms = lax.fori_loop(0, ngrp, logit_body, tuple(jnp.full((1, QB), NEG, F32) for _ in range(H_B)))
    acc_scr[...] = jnp.zeros(acc_scr.shape, F32)

    def pv_body(g, ss):
        ss = list(ss)
        acc = acc_scr[...]
        for u in range(CU):
            c = g * CU + u
            r = pl.multiple_of(c * QB, QB)
            es = [jnp.exp(lg_scr[h, pl.ds(r, QB), :] - ms[h]) for h in range(H_B)]
            acc = acc + jnp.dot(kvt_scr[c], jnp.concatenate(es, axis=1).astype(BF16), preferred_element_type=F32)
            for h in range(H_B):
                ss[h] = ss[h] + es[h].sum(axis=0, keepdims=True)
        acc_scr[...] = acc
        return tuple(ss)

    ss = lax.fori_loop(0, ngrp, pv_body, tuple(jnp.zeros((1, QB), F32) for _ in range(H_B)))
    acc = heads(acc_scr[...])
    ot = jnp.concatenate([acc[h] / ss[h] for h in range(H_B)], axis=0)
    o_ref[...] = ot.T


def _dsa_prompt(Pq, Pk, rel_bias, B, T):
    nb = T // QB
    assert nb % CU == 0
    n_sel = min(TOPK_MAX, T // 4)
    idx_bits = max(1, int(math.ceil(math.log2(T))))
    return pl.pallas_call(
        functools.partial(_dsa_prompt_kernel, n_sel=n_sel, idx_bits=idx_bits),
        grid=(B, nb),
        in_specs=[pl.BlockSpec(memory_space=pltpu.SMEM),
                  pl.BlockSpec((QB, PQ_W), lambda b, j: (b * nb + j, 0)),
                  pl.BlockSpec((T, PK_W), lambda b, j: (b, 0))],
        out_specs=pl.BlockSpec((QB, W_B), lambda b, j: (b * nb + j, 0)),
        out_shape=jax.ShapeDtypeStruct((B * T, W_B), F32),
        scratch_shapes=[pltpu.VMEM((nb, DH_B, QB), BF16), pltpu.VMEM((T, QB), I32), pltpu.VMEM((H_B, T, QB), F32),
                        pltpu.VMEM((H_B, 3, QB, QB), F32), pltpu.VMEM((DH_B, H_B * QB), F32)],
        compiler_params=pltpu.CompilerParams(dimension_semantics=("arbitrary", "arbitrary"), vmem_limit_bytes=VMEM_LIMIT),
        name="dsa_prompt",
    )(rel_bias, Pq, Pk)


PG = 16
GW = PG * PAGE_SIZE


def _stack_heads(x, w):
    return jnp.concatenate([x[:, h * w:(h + 1) * w] for h in range(x.shape[1] // w)], axis=0)


def _dsa_s_scores_kernel(pt_ref, pq_ref, *refs, ng, ds):
    page_refs, o_ref = refs[:PG], refs[PG]
    g = pl.program_id(1)
    sm = pq_ref[:, PQ_SM:PQ_SM + LANE]
    qst = _stack_heads(pq_ref[:, PQ_QI:PQ_QI + LANE], D_IDX)
    wcol = jnp.concatenate([sm[:, SM_WI + h:SM_WI + h + 1] for h in range(H_IDX)], axis=0) * (H_IDX ** -0.5 * D_IDX ** -0.5)

    def tile_scores(qk):
        s = jnp.maximum(qk, 0.0) * wcol
        out = s[0:ds]
        for h in range(1, H_IDX):
            out = out + s[h * ds:(h + 1) * ds]
        return out

    @pl.when(g < ng)
    def _():
        for i in range(PG):
            o_ref[:, i * PAGE_SIZE:(i + 1) * PAGE_SIZE] = tile_scores(_bdot(qst, page_refs[i][...]))

    @pl.when(g == ng)
    def _():
        sc = tile_scores(_bdot_nt(qst, _pad_rows(sm[:, SM_KI:SM_KI + D_IDX], PAGE_SIZE)))
        lane, row = _iota2(sc.shape, 1), _iota2(sc.shape, 0)
        o_ref[:, 0:PAGE_SIZE] = jnp.where(lane <= row, sc, NEG)
        o_ref[:, PAGE_SIZE:GW] = jnp.full((ds, GW - PAGE_SIZE), NEG, F32)


def _dsa_s_attn_kernel(pt_ref, rb_ref, sc_ref, pq_ref, pk_ref, *refs, ng, ds, n_sel, idx_bits):
    k_refs, v_refs = refs[:PG], refs[PG:2 * PG]
    o_ref, key_scr, thr_scr, bias_scr, m_scr, l_scr, acc_scr = refs[2 * PG:]
    g = pl.program_id(1)
    ntile = (ng + 1) * PG
    hr = H_B * ds

    @pl.when(g == 0)
    def _():
        for i in range(ntile):
            key_scr[:, i * LANE:(i + 1) * LANE] = _sortable(sc_ref[:, i * LANE:(i + 1) * LANE])
        lane = _iota2((ds, LANE), 1)

        def count(pred):
            acc = jnp.zeros((ds, LANE), I32)
            for i in range(ntile):
                acc = acc + pred(key_scr[:, i * LANE:(i + 1) * LANE], i * LANE)
            return acc.sum(axis=1, keepdims=True)

        cnt_ge = lambda t: count(lambda k, off: jnp.where(k >= t, 1, 0))
        thr = jnp.where(cnt_ge(jnp.zeros((ds, 1), I32)) >= n_sel, 0, _INT_MIN).astype(I32)

        def bit_body(i, t):
            trial = t | jnp.left_shift(jnp.int32(1), 30 - i)
            return jnp.where(cnt_ge(trial) >= n_sel, trial, t)

        thr = lax.fori_loop(0, 31, bit_body, thr)
        quota = n_sel - count(lambda k, off: jnp.where(k > thr, 1, 0))

        def idx_body(i, x):
            trial = x | jnp.left_shift(jnp.int32(1), idx_bits - 1 - i)
            below = count(lambda k, off: jnp.where(k == thr, jnp.where(lane + off < trial, 1, 0), 0))
            return jnp.where(below < quota, trial, x)

        xthr = lax.fori_loop(0, idx_bits, idx_body, jnp.zeros((ds, 1), I32))
        thr_scr[0] = jnp.broadcast_to(thr, (ds, LANE))
        thr_scr[1] = jnp.broadcast_to(xthr, (ds, LANE))

        qrow = _iota2((ds, LANE), 0)
        for h in range(H_B):
            bias_scr[0, h * ds:(h + 1) * ds, :] = _bias_from_dist(PAGE_SIZE + qrow - lane, rb_ref, h)
            bias_scr[1, h * ds:(h + 1) * ds, :] = _bias_from_dist(qrow - lane, rb_ref, h)
            bias_scr[2, h * ds:(h + 1) * ds, :] = jnp.full((ds, LANE), rb_ref[N_BUCKETS - 1, h], F32)
        m_scr[...] = jnp.full(m_scr.shape, NEG, F32)
        l_scr[...] = jnp.zeros(l_scr.shape, F32)
        acc_scr[...] = jnp.zeros(acc_scr.shape, F32)

    qst = _stack_heads(pq_ref[:, PQ_QB:PQ_QB + W_B], DH_B) * (DH_B ** -0.5)
    thr, xthr = thr_scr[0], thr_scr[1]

    def select(tile):
        k = key_scr[:, pl.ds(pl.multiple_of(tile * LANE, LANE), LANE)]
        idx = _iota2((ds, LANE), 1) + tile * LANE
        sel = _selected_i32(k, idx, thr, xthr)
        return jnp.concatenate([sel] * H_B, axis=0) > 0

    def update(lg, sel, pv):
        m_old = m_scr[...]
        m_new = jnp.maximum(m_old, jnp.where(sel, lg, NEG).max(axis=1, keepdims=True))
        p = jnp.where(sel, jnp.exp(lg - m_new), 0.0)
        corr = jnp.exp(m_old - m_new)
        l_scr[...] = l_scr[...] * corr + p.sum(axis=1, keepdims=True)
        acc_scr[...] = acc_scr[...] * corr + pv(p)
        m_scr[...] = m_new

    @pl.when(g < ng)
    def _():
        lgs, sels = [], []
        for i in range(PG):
            lg = _bdot(qst, k_refs[i][...])
            if i == PG - 1:
                lg = lg + jnp.where(g == ng - 1, bias_scr[0], bias_scr[2])
            else:
                lg = lg + bias_scr[2]
            lgs.append(lg)
            sels.append(select(g * PG + i))
        vt = jnp.concatenate([v_refs[i][...] for i in range(PG)], axis=1)
        update(jnp.concatenate(lgs, axis=1), jnp.concatenate(sels, axis=1), lambda p: _bdot_nt(p, vt))

    @pl.when(g == ng)
    def _():
        kvn = _pad_rows(pk_ref[:, PK_KV:PK_KV + LANE], PAGE_SIZE)
        lg = _bdot_nt(qst, kvn[:, 0:DH_B]) + bias_scr[1]
        update(lg, select(ng * PG), lambda p: _bdot(p, kvn[:, DH_B:2 * DH_B]))
        o = acc_scr[...] / l_scr[...]
        o_ref[...] = jnp.concatenate([o[h * ds:(h + 1) * ds] for h in range(H_B)], axis=1)


def _dsa_sample(Pq, Pk, rel_bias, cache_k, cache_v, cache_kidx, page_table, layer, B, DS):
    n_pages = page_table.shape[1]
    past = n_pages * PAGE_SIZE
    assert n_pages % PG == 0
    ng = n_pages // PG
    n_sel = min(TOPK_MAX, (past + DS) // 4)
    idx_bits = int(math.ceil(math.log2(past + LANE)))
    s_pad = (ng + 1) * GW

    def page_spec(width, i):
        return pl.BlockSpec((None, None, width, PAGE_SIZE),
                            lambda b, g, pt: (layer, pt[b, jnp.minimum(g, ng - 1) * PG + i], 0, 0))

    row = lambda b, g, pt: (b, 0)
    cp = pltpu.CompilerParams(dimension_semantics=("arbitrary", "arbitrary"), vmem_limit_bytes=VMEM_LIMIT)
    scores = pl.pallas_call(
        functools.partial(_dsa_s_scores_kernel, ng=ng, ds=DS),
        grid_spec=pltpu.PrefetchScalarGridSpec(
            num_scalar_prefetch=1, grid=(B, ng + 1),
            in_specs=[pl.BlockSpec((DS, PQ_W), row)] + [page_spec(D_IDX, i) for i in range(PG)],
            out_specs=pl.BlockSpec((None, DS, GW), lambda b, g, pt: (b, 0, g))),
        out_shape=jax.ShapeDtypeStruct((B, DS, s_pad), F32),
        compiler_params=cp, name="dsa_sample_scores",
    )(page_table, Pq, *([cache_kidx] * PG))

    return pl.pallas_call(
        functools.partial(_dsa_s_attn_kernel, ng=ng, ds=DS, n_sel=n_sel, idx_bits=idx_bits),
        grid_spec=pltpu.PrefetchScalarGridSpec(
            num_scalar_prefetch=1, grid=(B, ng + 1),
            in_specs=[pl.BlockSpec(memory_space=pltpu.SMEM),
                      pl.BlockSpec((None, DS, s_pad), lambda b, g, pt: (b, 0, 0)),
                      pl.BlockSpec((DS, PQ_W), row), pl.BlockSpec((DS, PK_W), row)]
                     + [page_spec(DH_B, i) for i in range(PG)] + [page_spec(DH_B, i) for i in range(PG)],
            out_specs=pl.BlockSpec((DS, W_B), lambda b, g, pt: (b, 0)),
            scratch_shapes=[pltpu.VMEM((DS, s_pad), I32), pltpu.VMEM((2, DS, LANE), I32), pltpu.VMEM((3, H_B * DS, LANE), F32),
                            pltpu.VMEM((H_B * DS, 1), F32), pltpu.VMEM((H_B * DS, 1), F32), pltpu.VMEM((H_B * DS, DH_B), F32)]),
        out_shape=jax.ShapeDtypeStruct((B * DS, W_B), F32),
        compiler_params=cp, name="dsa_sample_attn",
    )(page_table, rel_bias, scores, Pq, Pk, *([cache_k] * PG), *([cache_v] * PG))


def _merge_kernel(x_ref, oa_ref, ob_ref, oc_ref, od_ref, gate_ref, wb_ref, wo_ref, o_ref):
    m = None
    r0 = 0
    for br, ref in enumerate((oa_ref, ob_ref, oc_ref, od_ref)):
        w = ref.shape[1]
        t = _sigmoid(gate_ref[:, br * D_MODEL:(br + 1) * D_MODEL]) * jnp.dot(ref[...].astype(BF16), wb_ref[r0:r0 + w, :], preferred_element_type=F32)
        m = t if m is None else m + t
        r0 += w
    o_ref[...] = x_ref[...] + jnp.dot(m.astype(BF16), wo_ref[...], preferred_element_type=F32)


def _merge(x2d, oa, ob, oc, od, P, wb, wo):
    n = x2d.shape[0]
    tm = min(n, 256)
    row = lambda i: (i, 0)
    return pl.pallas_call(
        _merge_kernel,
        grid=(n // tm,),
        in_specs=[pl.BlockSpec((tm, D_MODEL), row), pl.BlockSpec((tm, W_A), row), pl.BlockSpec((tm, W_B), row),
                  pl.BlockSpec((tm, W_C), row), pl.BlockSpec((tm, W_D), row),
                  pl.BlockSpec((tm, N_BRANCH * D_MODEL), lambda i: (i, OFF_GATE // (N_BRANCH * D_MODEL))),
                  _const_spec((MIX_W, D_MODEL)), _const_spec((D_MODEL, D_MODEL))],
        out_specs=pl.BlockSpec((tm, D_MODEL), row),
        out_shape=jax.ShapeDtypeStruct((n, D_MODEL), F32),
        compiler_params=pltpu.CompilerParams(dimension_semantics=("arbitrary",), vmem_limit_bytes=VMEM_LIMIT),
        name="merge",
    )(x2d, oa, ob, oc, od, P, wb, wo)


def _mlp_kernel(x_ref, g_ref, wu_ref, wd_ref, gf_ref, *refs, final):
    if final:
        o_ref, y_ref, h_scr, acc_scr = refs
    else:
        o_ref, h_scr, acc_scr = refs
    f = pl.program_id(1)

    @pl.when(f == 0)
    def _():
        x = x_ref[...]
        r = lax.rsqrt(jnp.mean(x * x, axis=-1, keepdims=True) + EPS)
        h_scr[...] = ((x * r) * g_ref[...]).astype(BF16)
        acc_scr[...] = jnp.zeros(acc_scr.shape, F32)

    a = jnp.maximum(jnp.dot(h_scr[...], wu_ref[...], preferred_element_type=F32), 0.0)
    acc_scr[...] += jnp.dot((a * a).astype(BF16), wd_ref[...], preferred_element_type=F32)

    @pl.when(f == pl.num_programs(1) - 1)
    def _():
        out = x_ref[...] + acc_scr[...]
        o_ref[...] = out
        if final:
            r = lax.rsqrt(jnp.mean(out * out, axis=-1, keepdims=True) + EPS)
            y_ref[...] = (out * r) * gf_ref[...]


def _mlp(x2d, g, wu, wd, gf, final):
    n = x2d.shape[0]
    tm = min(n, 512)
    tf = 1024
    row = lambda i, f: (i, 0)
    out_spec = pl.BlockSpec((tm, D_MODEL), row)
    shp = jax.ShapeDtypeStruct((n, D_MODEL), F32)
    return pl.pallas_call(
        functools.partial(_mlp_kernel, final=final),
        grid=(n // tm, D_FF // tf),
        in_specs=[pl.BlockSpec((tm, D_MODEL), row), _const_spec((1, D_MODEL)),
                  pl.BlockSpec((D_MODEL, tf), lambda i, f: (0, f)), pl.BlockSpec((tf, D_MODEL), lambda i, f: (f, 0)),
                  _const_spec((1, D_MODEL))],
        out_specs=[out_spec, out_spec] if final else [out_spec],
        out_shape=[shp, shp] if final else [shp],
        scratch_shapes=[pltpu.VMEM((tm, D_MODEL), BF16), pltpu.VMEM((tm, D_MODEL), F32)],
        compiler_params=pltpu.CompilerParams(dimension_semantics=("arbitrary", "arbitrary"), vmem_limit_bytes=VMEM_LIMIT),
        name="mlp_final" if final else "mlp",
    )(x2d, g.reshape(1, D_MODEL), wu, wd, gf.reshape(1, D_MODEL))


def _layer(x, conv_state, s_delta, s_ssm, s_hgrn, lw, consts, sample_ctx, final):
    B, T, _ = x.shape
    x2 = x.reshape(B * T, D_MODEL)
    P, Pq, Pk = _in_proj(x2, lw['norm_mix'], lw['w_in'])

    def conv0(lo, hi):
        if conv_state is None:
            return jnp.zeros((B, SUB, hi - lo), F32)
        return jnp.concatenate([jnp.zeros((B, SUB - (CONV_W - 1), hi - lo), F32), conv_state[:, :, lo:hi]], axis=1)

    zeros_state = lambda n: jnp.zeros((B, n, 64, 64), F32)
    c = consts
    half = CONV_CH // 2
    assert CONV_A == half and OFF_PC % half == 0
    oa, s_a = _mix_call(
        _mix_a_kernel, "mix_a", P, B, T,
        [(OFF_PC, half), (OFF_ZA, W_A), (OFF_SM, LANE)],
        [conv0(0, CONV_A), zeros_state(H_A) if s_delta is None else s_delta],
        [lw['conv_w'][:, 0:CONV_A], lw['conv_b'][None, 0:CONV_A], lw['hp'], lw['hpt'], lw['gn_a'], c['tril2'], c['triu2'], c['bo64']],
        W_A, (H_A, DK_A, DV_A),
        [pltpu.VMEM((TC + SUB, CONV_A), F32), pltpu.VMEM((H_A, DK_A, DV_A), F32), pltpu.VMEM((TC, W_A), F32)])
    oc, s_c = _mix_call(
        _mix_c_kernel, "mix_c", P, B, T,
        [(OFF_PC + half, half), (OFF_ZC, W_C), (OFF_SM, LANE)],
        [conv0(CONV_A, CONV_CH), zeros_state(H_C) if s_ssm is None else s_ssm],
        [lw['conv_w'][:, CONV_A:], lw['conv_b'][None, CONV_A:], lw['hp'], lw['hpt'], lw['gn_c'], c['tril1'], c['triu1'], c['bo256']],
        W_C, (H_C, N_C, P_C),
        [pltpu.VMEM((TC + SUB, CONV_C), F32), pltpu.VMEM((H_C, N_C, P_C), F32), pltpu.VMEM((TC, W_C), F32)])
    sbd0 = jnp.zeros((B, H_D, DK_D, H_D, DV_D), F32)
    if s_hgrn is not None:
        for h in range(H_D):
            sbd0 = sbd0.at[:, h, :, h, :].set(s_hgrn[:, h])
    sbd0 = sbd0.reshape(B, H_D * DK_D, W_D)
    od, sbd = _mix_call(
        _mix_d_kernel, "mix_d", P, B, T,
        [(OFF_D, 4 * W_D)],
        [sbd0],
        [lw['lb'], lw['gn_d'], c['tril2'], c['sel'], c['bo64']],
        W_D, (H_D * DK_D, W_D),
        [pltpu.VMEM((H_D * DK_D, W_D), F32), pltpu.VMEM((TC, W_D), F32)])
    sbd5 = sbd.reshape(B, H_D, DK_D, H_D, DV_D)
    s_d = jnp.stack([sbd5[:, h, :, h, :] for h in range(H_D)], axis=1)
    if sample_ctx is None:
        ob = _dsa_prompt(Pq, Pk, lw['rel_bias'], B, T)
    else:
        ob = _dsa_sample(Pq, Pk, lw['rel_bias'], sample_ctx['cache_k'], sample_ctx['cache_v'], sample_ctx['cache_kidx'],
                         sample_ctx['page_table'], sample_ctx['layer'], B, T)
    x1 = _merge(x2, oa, ob, oc, od, P, lw['w_branch'], lw['w_out'])
    outs = _mlp(x1, lw['norm_mlp'], lw['w_up'], lw['w_down'], lw['norm_final'], final)
    xo = outs[0].reshape(B, T, D_MODEL)
    y = outs[1].reshape(B, T, D_MODEL) if final else None
    P3 = P.reshape(B, T, P_PAD)
    Pk3 = Pk.reshape(B, T, PK_W)
    states = (Pk3[:, :, PK_KV:PK_KV + DH_B], Pk3[:, :, PK_KV + DH_B:PK_KV + 2 * DH_B], Pk3[:, :, PK_SM + SM_KI:PK_SM + SM_KI + D_IDX],
              P3[:, T - (CONV_W - 1):, OFF_PC:OFF_PC + CONV_CH], s_a, s_c, s_d)
    return xo, y, states


def kernel(x_prompt, x_sample, cache_k, cache_v, cache_kidx, state_conv, state_delta, state_ssm, state_hgrn, page_table, norm_mix, w_in, conv_w, conv_b, a_log_a, dt_bias_a, gnorm_a, rel_bias, a_log_c, dt_bias_c, d_skip_c, gnorm_c, hgrn_gamma, gnorm_d, w_branch, w_out, norm_mlp, w_up, w_down, norm_final):
    depth = w_in.shape[0]
    npc = _np_consts()
    consts = {k: jnp.asarray(v, BF16) for k, v in npc.items()}
    pg = jax.nn.softmax(hgrn_gamma.astype(F32), axis=0)
    lower_bounds = jnp.cumsum(pg, axis=0) - pg[0]
    cache_kt, cache_vt, cache_kit = (jnp.swapaxes(c, 2, 3) for c in (cache_k, cache_v, cache_kidx))
    yp, ys = x_prompt, x_sample
    new_p, new_s = [], []
    for l in range(depth):
        hp = jnp.zeros((SUB, LANE), F32)
        hp = hp.at[0, 0:H_A].set(a_log_a[l]).at[1, 0:H_A].set(dt_bias_a[l])
        hp = hp.at[2, 0:H_C].set(a_log_c[l]).at[3, 0:H_C].set(dt_bias_c[l]).at[4, 0:H_C].set(d_skip_c[l])
        lw = dict(norm_mix=norm_mix[l], w_in=_permute_w_in(w_in[l]), conv_w=conv_w[l], conv_b=conv_b[l],
                  hp=hp, hpt=hp.T, gn_a=jnp.tile(gnorm_a[l], H_A)[None, :], gn_c=gnorm_c[l][None, :],
                  gn_d=jnp.tile(gnorm_d[l], H_D)[None, :], lb=lower_bounds[l][None, :], rel_bias=rel_bias,
                  w_branch=w_branch[l].astype(BF16), w_out=w_out[l].astype(BF16), norm_mlp=norm_mlp[l],
                  w_up=w_up[l].astype(BF16), w_down=w_down[l].astype(BF16), norm_final=norm_final)
        final = l == depth - 1
        yp, yp_n, st_p = _layer(yp, None, None, None, None, lw, consts, None, final)
        ctx = dict(cache_k=cache_kt, cache_v=cache_vt, cache_kidx=cache_kit, page_table=page_table, layer=l)
        ys, ys_n, st_s = _layer(ys, state_conv[l], state_delta[l], state_ssm[l], state_hgrn[l], lw, consts, ctx, final)
        new_p.append(st_p)
        new_s.append(st_s)
    stack = lambda per_layer: [jnp.stack(items) for items in zip(*per_layer)]
    return (yp_n, ys_n, *stack(new_p), *stack(new_s))
```

```python
import functools
import math

import numpy as np
import jax
import jax.numpy as jnp
from jax import lax
from jax.experimental import pallas as pl
from jax.experimental.pallas import tpu as pltpu

F32 = jnp.float32
BF16 = jnp.bfloat16
I32 = jnp.int32

D_MODEL = 1024
PAGE_SIZE = 128
EPS = 1e-6
NEG = -1e30
TINY = 1e-30
CONV_W = 4
D_FF = 4 * D_MODEL
N_BRANCH = 4
H_A, DK_A, DV_A = 4, 64, 64
H_B, DH_B = 4, 64
H_IDX, D_IDX = 4, 32
TOPK_MAX = 256
N_BUCKETS = 32
MAX_DISTANCE = 128
H_C, P_C, N_C, G_C = 8, 64, 64, 2
H_D, DK_D, DV_D = 4, 64, 64
QK_A = H_A * DK_A
CONV_A = 2 * QK_A + H_A * DV_A
D_INNER_C = H_C * P_C
CONV_C = D_INNER_C + 2 * G_C * N_C
CONV_CH = CONV_A + CONV_C
W_A, W_B, W_C, W_D = H_A * DV_A, H_B * DH_B, D_INNER_C, H_D * DV_D
MIX_W = W_A + W_B + W_C + W_D
IN_WIDTHS = (CONV_CH, W_A, H_A, H_A, W_B, DH_B, DH_B, H_IDX * D_IDX, D_IDX, H_IDX, D_INNER_C, H_C,
             H_D * DK_D, H_D * DK_D, W_D, W_D, N_BRANCH * D_MODEL)
P_IN = sum(IN_WIDTHS)

LANE = 128
SUB = 8
TC = 128
TC_A = 256
QB = 128
VMEM_LIMIT = 48 * 1024 * 1024

OFF_GATE, OFF_D, OFF_ZC, OFF_ZA, OFF_QB, OFF_PC, OFF_KV, OFF_QI, OFF_SM = 0, 4096, 5120, 5632, 5888, 6144, 7680, 7808, 7936
P_PAD = 8064
SM_KI, SM_WI, SM_AA, SM_BA, SM_DT = 0, 32, 36, 40, 44


def _src_offsets():
    offs, o = [], 0
    for w in IN_WIDTHS:
        offs.append(o)
        o += w
    return offs


def _permute_w_in(w):
    (o_pc, o_za, o_aa, o_ba, o_qb, o_kb, o_vb, o_qi, o_ki, o_wi, o_zc, o_dt, o_qd, o_fd, o_id, o_gd, o_gate) = _src_offsets()
    wt = jnp.swapaxes(w, 0, 1)
    seg = lambda o, n: wt[o:o + n, :]
    small = jnp.concatenate([seg(o_ki, D_IDX), seg(o_wi, H_IDX), seg(o_aa, H_A), seg(o_ba, H_A), seg(o_dt, H_C),
                             jnp.zeros((LANE - (D_IDX + H_IDX + 2 * H_A + H_C), w.shape[0]), w.dtype)], axis=0)
    out = jnp.concatenate([
        seg(o_gate, N_BRANCH * D_MODEL),
        seg(o_qd, 4 * W_D),
        seg(o_zc, D_INNER_C), seg(o_za, W_A), seg(o_qb, W_B), seg(o_pc, CONV_CH),
        seg(o_kb, 2 * DH_B),
        seg(o_qi, H_IDX * D_IDX), small], axis=0)
    assert out.shape[0] == P_PAD
    return out.astype(BF16)


def _bdot(a, b):
    return jnp.dot(a.astype(BF16), b.astype(BF16), preferred_element_type=F32)


def _bdot_nt(a, b):
    return lax.dot_general(a.astype(BF16), b.astype(BF16), (((1,), (1,)), ((), ())), preferred_element_type=F32)


def _split2(a):
    hi = a.astype(BF16)
    lo = (a - hi.astype(F32)).astype(BF16)
    return hi, lo


def _split3(a):
    hi = a.astype(BF16)
    r = a - hi.astype(F32)
    mid = r.astype(BF16)
    lo = (r - mid.astype(F32)).astype(BF16)
    return hi, mid, lo


def _dot01_left(m01, x):
    hi, mid, lo = _split3(x)
    d = lambda p: jnp.dot(m01, p, preferred_element_type=F32)
    return d(hi) + (d(mid) + d(lo))


def _dot01_right(x, m01):
    hi, mid, lo = _split3(x)
    d = lambda p: jnp.dot(p, m01, preferred_element_type=F32)
    return d(hi) + (d(mid) + d(lo))


def _xdot(a, b):
    ah, al = _split2(a)
    bh, bl = _split2(b)
    d = lambda p, q: jnp.dot(p, q, preferred_element_type=F32)
    return d(ah, bh) + (d(ah, bl) + d(al, bh))


def _sigmoid(x):
    return 1.0 / (1.0 + jnp.exp(-x))


def _silu(x):
    return x * _sigmoid(x)


def _softplus(x):
    return jnp.maximum(x, 0.0) + jnp.log(1.0 + jnp.exp(-jnp.abs(x)))


def _iota2(shape, dim):
    return lax.broadcasted_iota(I32, shape, dim)


def _pad_rows(x, rows):
    if x.shape[0] == rows:
        return x
    return jnp.concatenate([x, jnp.zeros((rows - x.shape[0],) + x.shape[1:], x.dtype)], axis=0)


def _tri_inv_many(Ls, top):
    n = Ls[0].shape[0]
    ii, jj = _iota2((n, n), 0), _iota2((n, n), 1)
    xor = ii ^ jj
    eye = jnp.where(ii == jj, 1.0, 0.0)
    Ns = [jnp.where((xor >> 3) == 0, -L, 0.0) for L in Ls]
    Xs = [eye + N for N in Ns]
    N2s = [_bdot(N, N) for N in Ns]
    Xs = [X + _bdot(X, N2) for X, N2 in zip(Xs, N2s)]
    N4s = [_bdot(N2, N2) for N2 in N2s]
    Xs = [X + _bdot(X, N4) for X, N4 in zip(Xs, N4s)]
    sh = 4
    while (1 << sh) <= top:
        XBs = [_bdot(X, jnp.where((xor >> (sh - 1)) == 1, L, 0.0)) for X, L in zip(Xs, Ls)]
        Xs = [X - _bdot(XB, X) for X, XB in zip(Xs, XBs)]
        sh += 1
    return Xs


def _conv_silu(pc, ext_scr, cw_ref, cb_ref):
    tc = pc.shape[0]
    ext_scr[SUB:SUB + tc, :] = pc
    y = cb_ref[...]
    for j in range(CONV_W):
        y = y + ext_scr[SUB - (CONV_W - 1) + j:SUB - (CONV_W - 1) + j + tc, :] * cw_ref[j:j + 1, :]
    ext_scr[0:SUB, :] = ext_scr[tc:tc + SUB, :]
    return _silu(y)


TN_IN = 1152
PQ_W = W_B + 2 * LANE
PK_W = 2 * LANE
PQ_QB, PQ_QI, PQ_SM = 0, W_B, W_B + LANE
PK_KV, PK_SM = 0, LANE
_COMPACT_COPIES = (("pq", PQ_QB, OFF_QB, W_B), ("pq", PQ_QI, OFF_QI, LANE), ("pq", PQ_SM, OFF_SM, LANE),
                   ("pk", PK_KV, OFF_KV, LANE), ("pk", PK_SM, OFF_SM, LANE))


def _in_proj_kernel(x_ref, g_ref, w_ref, o_ref, pq_ref, pk_ref, h_scr):
    j = pl.program_id(1)

    @pl.when(j == 0)
    def _():
        x = x_ref[...]
        r = lax.rsqrt(jnp.mean(x * x, axis=-1, keepdims=True) + EPS)
        h_scr[...] = ((x * r) * g_ref[...]).astype(BF16)

    o_ref[...] = lax.dot_general(h_scr[...], w_ref[...], (((1,), (1,)), ((), ())), preferred_element_type=F32)

    def copy(dst_ref, dst, src, width):
        tile, lo = divmod(src, TN_IN)
        assert lo + width <= TN_IN

        @pl.when(j == tile)
        def _():
            dst_ref[:, dst:dst + width] = o_ref[:, lo:lo + width]

    for name, dst, src, width in _COMPACT_COPIES:
        copy(pq_ref if name == "pq" else pk_ref, dst, src, width)


def _in_proj(x2d, g, w_perm_t):
    n = x2d.shape[0]
    tm = min(n, 1024)
    tn = TN_IN
    return pl.pallas_call(
        _in_proj_kernel,
        grid=(n // tm, P_PAD // tn),
        in_specs=[pl.BlockSpec((tm, D_MODEL), lambda i, j: (i, 0)),
                  pl.BlockSpec((1, D_MODEL), lambda i, j: (0, 0)),
                  pl.BlockSpec((tn, D_MODEL), lambda i, j: (j, 0))],
        out_specs=[pl.BlockSpec((tm, tn), lambda i, j: (i, j)),
                   pl.BlockSpec((tm, PQ_W), lambda i, j: (i, 0)), pl.BlockSpec((tm, PK_W), lambda i, j: (i, 0))],
        out_shape=[jax.ShapeDtypeStruct((n, P_PAD), F32), jax.ShapeDtypeStruct((n, PQ_W), F32),
                   jax.ShapeDtypeStruct((n, PK_W), F32)],
        scratch_shapes=[pltpu.VMEM((tm, D_MODEL), BF16)],
        compiler_params=pltpu.CompilerParams(dimension_semantics=("arbitrary", "arbitrary"), vmem_limit_bytes=VMEM_LIMIT),
        name="in_proj",
    )(x2d, g.reshape(1, D_MODEL), w_perm_t)


def _mix_a_kernel(pc_ref, za_ref, sm_ref, conv0_ref, s0_ref, cw_ref, cb_ref, hp_ref, hpt_ref, gn_ref,
                  tril_ref, triu_ref, bo_ref, o_ref, sout_ref, ext_scr, s_scr, ob_scr, *, t_valid, rows):
    CA = 64
    tc = ob_scr.shape[0]

    @pl.when(pl.program_id(1) == 0)
    def _():
        ext_scr[0:SUB, :] = conv0_ref[...]
        s_scr[...] = s0_ref[...]

    u = _conv_silu(_pad_rows(pc_ref[...], tc), ext_scr, cw_ref, cb_ref)
    bo = bo_ref[...]
    q_raw, k_raw, va = u[:, 0:QK_A], u[:, QK_A:2 * QK_A], u[:, 2 * QK_A:CONV_A]
    qq = q_raw * lax.rsqrt(_dot01_right(q_raw * q_raw, bo) + EPS) * (DK_A ** -0.5)
    ka = k_raw * lax.rsqrt(_dot01_right(k_raw * k_raw, bo) + EPS)
    kat = ka.T

    sm = _pad_rows(sm_ref[...], tc)
    smt = sm.T
    hp, hpt = hp_ref[...], hpt_ref[...]
    g_col = -jnp.exp(hp[0:1, 0:H_A]) * _softplus(sm[:, SM_AA:SM_AA + H_A] + hp[1:2, 0:H_A])
    beta = _sigmoid(sm[:, SM_BA:SM_BA + H_A])
    g_row = -jnp.exp(hpt[0:H_A, 0:1]) * _softplus(smt[SM_AA:SM_AA + H_A, :] + hpt[0:H_A, 1:2])
    if t_valid < tc:
        g_col = jnp.where(_iota2(g_col.shape, 0) < t_valid, g_col, 0.0)
        beta = jnp.where(_iota2(beta.shape, 0) < t_valid, beta, 0.0)
        g_row = jnp.where(_iota2(g_row.shape, 1) < t_valid, g_row, 0.0)
    gcum_col = _dot01_left(tril_ref[...], g_col)
    gcum_row = _dot01_right(g_row, triu_ref[...])

    HS = H_A * CA
    ii, jj = _iota2((HS, HS), 0), _iota2((HS, HS), 1)
    same = (ii >> 6) == (jj >> 6)
    incl, strict = same & (ii >= jj), same & (ii > jj)
    nsc = -(-t_valid // CA)
    pre = []
    for sc in range(nsc):
        r0 = sc * CA
        stack = lambda x, w: jnp.concatenate([x[r0:r0 + CA, h * w:(h + 1) * w] for h in range(H_A)], axis=0)
        ks, qs, vs = stack(ka, DK_A), stack(qq, DK_A), stack(va, DV_A)
        bcol, gc = stack(beta, 1), stack(gcum_col, 1)
        gr = jnp.concatenate([gcum_row[h:h + 1, r0:r0 + CA] for h in range(H_A)], axis=1)
        e = jnp.exp(jnp.where(incl, gc - gr, 0.0))
        kb = ks * bcol
        eg = jnp.exp(gc)
        pre.append(dict(r0=r0, gc=gc, qe=qs * eg, low=jnp.where(strict, _bdot_nt(kb, ks) * e, 0.0),
                        rhs=jnp.concatenate([vs * bcol, kb * eg], axis=1),
                        attn=jnp.where(incl, _bdot_nt(qs, ks) * e, 0.0)))
    tinvs = _tri_inv_many([p['low'] for p in pre], CA)
    sols = [_xdot(t, p['rhs']) for t, p in zip(tinvs, pre)]
    hs = lambda x, h: x[h * CA:(h + 1) * CA]
    for p, sol in zip(pre, sols):
        r0, gc = p['r0'], p['gc']
        states = [s_scr[h] for h in range(H_A)]
        v_new = jnp.concatenate([hs(sol, h)[:, 0:DV_A] - _bdot(hs(sol, h)[:, DV_A:], states[h]) for h in range(H_A)], axis=0)
        o = jnp.concatenate([_bdot(hs(p['qe'], h), states[h]) for h in range(H_A)], axis=0) + _bdot(p['attn'], v_new)
        for h in range(H_A):
            gch = hs(gc, h)
            gl = gch[CA - 1:CA, :]
            s_scr[h] = states[h] * jnp.exp(gl) + _bdot(kat[h * DK_A:(h + 1) * DK_A, r0:r0 + CA], hs(v_new, h) * jnp.exp(gl - gch))
            ob_scr[r0:r0 + CA, h * DV_A:(h + 1) * DV_A] = hs(o, h)
    if nsc * CA < tc:
        ob_scr[nsc * CA:tc, :] = jnp.zeros((tc - nsc * CA, W_A), F32)

    o = ob_scr[...]
    o = o * lax.rsqrt(_dot01_right(o * o, bo) * (1.0 / DV_A) + EPS) * gn_ref[...]
    res = o * _silu(_pad_rows(za_ref[...], tc))
    o_ref[...] = res[0:rows]
    sout_ref[...] = s_scr[...]


def _mix_c_kernel(pc_ref, zc_ref, sm_ref, conv0_ref, s0_ref, cw_ref, cb_ref, hp_ref, hpt_ref, gn_ref,
                  tril_ref, triu_ref, bo_ref, o_ref, sout_ref, ext_scr, s_scr, yb_scr, *, t_valid, rows):
    @pl.when(pl.program_id(1) == 0)
    def _():
        ext_scr[0:SUB, :] = conv0_ref[...]
        s_scr[...] = s0_ref[...]

    u = _conv_silu(_pad_rows(pc_ref[...], TC), ext_scr, cw_ref, cb_ref)
    xc = u[:, 0:D_INNER_C]
    bcm = u[:, D_INNER_C:D_INNER_C + G_C * N_C]
    ccm = u[:, D_INNER_C + G_C * N_C:CONV_C]
    bct = bcm.T
    zc = _pad_rows(zc_ref[...], TC)

    sm = _pad_rows(sm_ref[...], TC)
    smt = sm.T
    hp, hpt = hp_ref[...], hpt_ref[...]
    dt_col = _softplus(sm[:, SM_DT:SM_DT + H_C] + hp[3:4, 0:H_C])
    dt_row = _softplus(smt[SM_DT:SM_DT + H_C, :] + hpt[0:H_C, 3:4])
    if t_valid < TC:
        dt_col = jnp.where(_iota2(dt_col.shape, 0) < t_valid, dt_col, 0.0)
        dt_row = jnp.where(_iota2(dt_row.shape, 1) < t_valid, dt_row, 0.0)
    gcum_col = _dot01_left(tril_ref[...], -jnp.exp(hp[2:3, 0:H_C]) * dt_col)
    gcum_row = _dot01_right(-jnp.exp(hpt[0:H_C, 2:3]) * dt_row, triu_ref[...])

    incl = _iota2((TC, TC), 0) >= _iota2((TC, TC), 1)
    hg = H_C // G_C
    for g in range(G_C):
        cc_g = ccm[:, g * N_C:(g + 1) * N_C]
        gram = _bdot_nt(cc_g, bcm[:, g * N_C:(g + 1) * N_C])
        for hh in range(hg):
            h = g * hg + hh
            c0 = h * P_C
            gc = gcum_col[:, h:h + 1]
            gr = gcum_row[h:h + 1, :]
            dec = jnp.where(incl, jnp.exp(jnp.where(incl, gc - gr, 0.0)), 0.0)
            attn = gram * dt_row[h:h + 1, :] * dec
            xh = xc[:, c0:c0 + P_C]
            S = s_scr[h]
            o = jnp.exp(gc) * _bdot(cc_g, S) + _bdot(attn, xh)
            gl = gc[TC - 1:TC, :]
            s_scr[h] = S * jnp.exp(gl) + _bdot(bct[g * N_C:(g + 1) * N_C, :], xh * (dt_col[:, h:h + 1] * jnp.exp(gl - gc)))
            yb_scr[:, c0:c0 + P_C] = (o + hp[4:5, h:h + 1] * xh) * _silu(zc[:, c0:c0 + P_C])

    y = yb_scr[...]
    y = y * lax.rsqrt(_dot01_right(y * y, bo_ref[...]) * (1.0 / (D_INNER_C // G_C)) + EPS) * gn_ref[...]
    o_ref[...] = y[0:rows]
    sout_ref[...] = s_scr[...]


_D_LEVELS = 6


def _mix_d_kernel(dg_ref, s0_ref, lb_ref, gn_ref, tril_ref, sel_ref, bo_ref, o_ref, sout_ref, s_scr, ob_scr, *, t_valid, rows):
    CD = 64
    W = H_D * DK_D

    @pl.when(pl.program_id(1) == 0)
    def _():
        s_scr[...] = s0_ref[...]

    dg = _pad_rows(dg_ref[...], TC)
    q = _silu(dg[:, 0:W])
    fr = dg[:, W:2 * W]
    v = dg[:, 2 * W:3 * W]
    gd = dg[:, 3 * W:4 * W]
    lb = lb_ref[...]
    logf = jnp.log(jnp.maximum(lb + (1.0 - lb) * _sigmoid(fr), TINY))
    kd = (1.0 - lb) * _sigmoid(-fr)
    if t_valid < TC:
        ok = _iota2((TC, W), 0) < t_valid
        logf = jnp.where(ok, logf, 0.0)
        kd = jnp.where(ok, kd, 0.0)
    gcum = _dot01_left(tril_ref[...], logf)
    gm = _dot01_left(sel_ref[...], gcum)
    qdec = q * jnp.exp(gcum)
    nsc = -(-t_valid // CD)
    gl_rows = jnp.concatenate([jnp.broadcast_to(gcum[(s + 1) * CD - 1:(s + 1) * CD, :], (CD, W)) for s in range(TC // CD)], axis=0)
    kdt = (kd * jnp.exp(gl_rows - gcum)).T
    gct = gcum.T

    HS = H_D * CD
    head_lanes = (_iota2((HS, W), 0) >> 6) == (_iota2((HS, W), 1) >> 6)
    stack = lambda x: jnp.where(head_lanes, jnp.concatenate([x] * H_D, axis=0), 0.0)
    ci, jj = _iota2((HS, CD), 0) & (CD - 1), _iota2((HS, CD), 1)
    later = jnp.where(ci > jj, ci ^ jj, 0)
    lhs, rhs, msk = [q], [kd], [ci == jj]
    for lv in range(1, _D_LEVELS + 1):
        gml = gm[(lv - 1) * TC:lv * TC, :]
        lhs.append(q * jnp.exp(jnp.minimum(gcum - gml, 0.0)))
        rhs.append(kd * jnp.exp(jnp.minimum(gml - gcum, 0.0)))
        msk.append((later >> (lv - 1)) == 1)

    lane_head = _iota2((CD, W), 1) >> 6
    intra = []
    for sc in range(nsc):
        r0 = sc * CD
        attn = jnp.zeros((HS, CD), F32)
        for a, b, m in zip(lhs, rhs, msk):
            attn = attn + jnp.where(m, _bdot_nt(stack(a[r0:r0 + CD]), b[r0:r0 + CD]), 0.0)
        full = _bdot(attn, v[r0:r0 + CD])
        o = jnp.zeros((CD, W), F32)
        for h in range(H_D):
            o = o + jnp.where(lane_head == h, full[h * CD:(h + 1) * CD], 0.0)
        intra.append(o)
    for sc in range(nsc):
        r0 = sc * CD
        sbd = s_scr[...]
        ob_scr[r0:r0 + CD, :] = _bdot(qdec[r0:r0 + CD], sbd) + intra[sc]
        decay = jnp.exp(gct[:, r0 + CD - 1:r0 + CD])
        s_scr[...] = jnp.where(head_lanes, sbd * decay + _bdot(kdt[:, r0:r0 + CD], v[r0:r0 + CD]), 0.0)
    if nsc < TC // CD:
        ob_scr[nsc * CD:TC, :] = jnp.zeros((TC - nsc * CD, W), F32)

    o = ob_scr[...]
    o = o * lax.rsqrt(_dot01_right(o * o, bo_ref[...]) * (1.0 / DV_D) + EPS) * gn_ref[...]
    res = o * _silu(gd)
    o_ref[...] = res[0:rows]
    sout_ref[...] = s_scr[...]


def _np_consts():
    i = np.arange(TC)
    same64 = (i[:, None] // 64) == (i[None, :] // 64)
    tril2 = (same64 & (i[:, None] >= i[None, :])).astype(np.float32)
    tril1 = (i[:, None] >= i[None, :]).astype(np.float32)
    sel = np.zeros((_D_LEVELS * TC, TC), np.float32)
    for lv in range(1, _D_LEVELS + 1):
        m = ((i >> lv) << lv) + (1 << (lv - 1)) - 1
        sel[(lv - 1) * TC + i, m] = 1.0
    bo = lambda n, w: ((np.arange(n)[:, None] // w) == (np.arange(n)[None, :] // w)).astype(np.float32)
    ia = np.arange(TC_A)
    tril2a = (((ia[:, None] // 64) == (ia[None, :] // 64)) & (ia[:, None] >= ia[None, :])).astype(np.float32)
    return dict(tril2=tril2, triu2=tril2.T.copy(), tril1=tril1, triu1=tril1.T.copy(), sel=sel,
                tril2a=tril2a, triu2a=tril2a.T.copy(),
                bo64=bo(256, 64), bo256=bo(D_INNER_C, D_INNER_C // G_C))


def _bucket_starts():
    max_exact = N_BUCKETS // 2
    d = np.arange(0, 4 * MAX_DISTANCE)
    lr = np.log(np.maximum(d, 1).astype(np.float32) / max_exact) / math.log(MAX_DISTANCE / max_exact)
    large = np.minimum(max_exact + (np.maximum(lr, 0.0) * (N_BUCKETS - max_exact)).astype(np.int32), N_BUCKETS - 1)
    bucket = np.where(d < max_exact, d, large)
    return [int(np.argmax(bucket == b)) for b in range(N_BUCKETS)]


_BUCKET_STARTS = _bucket_starts()
_SAT_DIST = _BUCKET_STARTS[N_BUCKETS - 1]


def _bias_from_dist(d, rb_ref, h):
    out = jnp.full(d.shape, rb_ref[N_BUCKETS - 1, h], F32)
    for b in range(N_BUCKETS - 2, -1, -1):
        out = jnp.where(d < _BUCKET_STARTS[b + 1], rb_ref[b, h], out)
    return out


def _sortable(x):
    b = lax.bitcast_convert_type(x + 0.0, I32)
    return jnp.where(b < 0, b ^ 0x7FFFFFFF, b)


_NEG_KEY = int(np.array([NEG], np.float32).view(np.int32)[0]) ^ 0x7FFFFFFF
_INT_MIN = -2 ** 31


def _selected_i32(k, idx, thr, xthr):
    s = jnp.where(k > thr, 1, jnp.where(k == thr, jnp.where(idx <= xthr, 1, 0), 0))
    return jnp.where(k == _NEG_KEY, 0, s)


def _selected(k, idx, thr, xthr):
    return _selected_i32(k, idx, thr, xthr) > 0


def _const_spec(shape):
    nd = len(shape)
    return pl.BlockSpec(shape, lambda *_: (0,) * nd)


def _mix_call(kernel_fn, name, P, B, T, seq_specs, batch_args, const_args, out_w, st_tail, scratch, tc=TC):
    rows = min(T, tc)
    nc = max(T // tc, 1)
    assert rows * nc == T
    in_specs = [pl.BlockSpec((rows, w), functools.partial(lambda b, c, ci: (b * nc + c, ci), ci=off // w)) for off, w in seq_specs]
    args = [P] * len(seq_specs)
    for a in batch_args:
        nd = a.ndim
        in_specs.append(pl.BlockSpec((None,) + a.shape[1:], functools.partial(lambda b, c, nd: (b,) + (0,) * (nd - 1), nd=nd)))
        args.append(a)
    for a in const_args:
        in_specs.append(_const_spec(a.shape))
        args.append(a)
    st_nd = len(st_tail)
    return pl.pallas_call(
        functools.partial(kernel_fn, t_valid=rows, rows=rows),
        grid=(B, nc),
        in_specs=in_specs,
        out_specs=[pl.BlockSpec((rows, out_w), lambda b, c: (b * nc + c, 0)),
                   pl.BlockSpec((None,) + st_tail, lambda b, c: (b,) + (0,) * st_nd)],
        out_shape=[jax.ShapeDtypeStruct((B * T, out_w), F32), jax.ShapeDtypeStruct((B,) + st_tail, F32)],
        scratch_shapes=scratch,
        compiler_params=pltpu.CompilerParams(dimension_semantics=("arbitrary", "arbitrary"), vmem_limit_bytes=VMEM_LIMIT),
        name=name,
    )(*args)


CU = 4
SEARCH_FIXED_BITS = 16


def _dsa_prompt_kernel(rb_ref, pq_ref, pk_ref, o_ref,
                       kvt_scr, key_scr, lg_scr, bias_scr, acc_scr, *, n_sel, idx_bits):
    j = pl.program_id(1)
    nch = j + 1

    @pl.when((pl.program_id(0) == 0) & (j == 0))
    def _():
        srow, tcol = _iota2((QB, QB), 0), _iota2((QB, QB), 1)
        for h in range(H_B):
            bias_scr[h, 0] = jnp.full((QB, QB), rb_ref[N_BUCKETS - 1, h], F32)
            bias_scr[h, 1] = _bias_from_dist(tcol - srow + QB, rb_ref, h)
            bias_scr[h, 2] = _bias_from_dist(tcol - srow, rb_ref, h)

    @pl.when(j == 0)
    def _():
        def tr(c, _):
            r = pl.multiple_of(c * QB, QB)
            kvt_scr[c] = pk_ref[pl.ds(r, QB), PK_KV:PK_KV + LANE].T[DH_B:2 * DH_B, :].astype(BF16)
            return 0
        lax.fori_loop(0, pk_ref.shape[0] // QB, tr, 0)

    lanes = lambda rows, w: jnp.concatenate([rows[h * w:(h + 1) * w, :] for h in range(H_B)], axis=1)
    w4 = lanes(pq_ref[:, PQ_SM:PQ_SM + LANE].T[SM_WI:SM_WI + H_IDX, :], 1) * (H_IDX ** -0.5 * D_IDX ** -0.5)
    qi_rhs = lanes(pq_ref[:, PQ_QI:PQ_QI + LANE].T, D_IDX).astype(BF16)
    qb_rhs = (lanes(pq_ref[:, PQ_QB:PQ_QB + W_B].T, DH_B) * (DH_B ** -0.5)).astype(BF16)
    srow, tcol = _iota2((QB, QB), 0), _iota2((QB, QB), 1)
    heads = lambda x: [x[:, h * QB:(h + 1) * QB] for h in range(H_B)]

    ngrp = (nch + (CU - 1)) // CU

    def score_body(g, _):
        for u in range(CU):
            r = pl.multiple_of((g * CU + u) * QB, QB)
            ki = pk_ref[pl.ds(r, QB), PK_SM:PK_SM + LANE][:, SM_KI:SM_KI + D_IDX].astype(BF16)
            s = jnp.maximum(jnp.dot(ki, qi_rhs, preferred_element_type=F32), 0.0) * w4
            sh = heads(s)
            sc = (sh[0] + sh[1]) + (sh[2] + sh[3])
            sc = jnp.where(srow + r <= tcol + j * QB, sc, NEG)
            key_scr[pl.ds(r, QB), :] = _sortable(sc)
        return 0

    lax.fori_loop(0, ngrp, score_body, 0)

    def count(pred):
        def body(g, acc):
            for u in range(CU):
                r = pl.multiple_of((g * CU + u) * QB, QB)
                acc = acc + pred(key_scr[pl.ds(r, QB), :], r).reshape(QB // SUB, SUB, QB).sum(axis=0)
            return acc
        return lax.fori_loop(0, ngrp, body, jnp.zeros((SUB, QB), I32)).sum(axis=0, keepdims=True)

    cnt_ge = lambda t: count(lambda k, r: jnp.where(k >= t, 1, 0))
    c0 = cnt_ge(jnp.zeros((1, QB), I32))
    thr = jnp.where(c0 >= n_sel, 0, _INT_MIN).astype(I32)
    cthr = jnp.where(c0 >= n_sel, c0, ngrp * (CU * QB))

    def bit_body(i, carry):
        t, ct = carry
        trial = t | jnp.left_shift(jnp.int32(1), 30 - i)
        cnt = cnt_ge(trial)
        ok = cnt >= n_sel
        return jnp.where(ok, trial, t), jnp.where(ok, cnt, ct)

    thr, cthr = lax.fori_loop(0, SEARCH_FIXED_BITS, bit_body, (thr, cthr))

    def more(c):
        return (c[0] < 31) & c[3]

    def refine(c):
        i, t, ct, _ = c
        unresolved = jnp.max(ct) > n_sel
        t, ct = bit_body(i, (t, ct))
        return i + 1, t, ct, unresolved

    _, thr, cthr, _ = lax.while_loop(more, refine, (jnp.int32(SEARCH_FIXED_BITS), thr, cthr, True))

    @pl.when(jnp.max(cthr) > n_sel)
    def _():
        quota = n_sel - count(lambda k, r: jnp.where(k > thr, 1, 0))

        def idx_body(i, x):
            trial = x | jnp.left_shift(jnp.int32(1), idx_bits - 1 - i)
            below = count(lambda k, r: jnp.where(k == thr, jnp.where(srow + r < trial, 1, 0), 0))
            return jnp.where(below < quota, trial, x)

        xthr = lax.fori_loop(0, idx_bits, idx_body, jnp.zeros((1, QB), I32))

        def demote_body(g, _):
            for u in range(CU):
                r = pl.multiple_of((g * CU + u) * QB, QB)
                k = key_scr[pl.ds(r, QB), :]
                key_scr[pl.ds(r, QB), :] = jnp.where(k == thr, jnp.where(srow + r > xthr, k - 1, k), k)
            return 0

        lax.fori_loop(0, ngrp, demote_body, 0)

    thr_sel = jnp.maximum(thr, _NEG_KEY + 1)

    def logit_body(g, ms):
        ms = list(ms)
        for u in range(CU):
            c = g * CU + u
            r = pl.multiple_of(c * QB, QB)
            kc = pk_ref[pl.ds(r, QB), PK_KV:PK_KV + LANE][:, 0:DH_B].astype(BF16)
            l4 = heads(jnp.dot(kc, qb_rhs, preferred_element_type=F32))
            sel = key_scr[pl.ds(r, QB), :] >= thr_sel
            bidx = jnp.clip(c - j + 2, 0, 2)
            for h in range(H_B):
                l = jnp.where(sel, l4[h] + bias_scr[h, bidx], NEG)
                lg_scr[h, pl.ds(r, QB), :] = l
                ms[h] = jnp.maximum(ms[h], l.max(axis=0, keepdims=True))
        return tuple(ms)

    ms = lax.fori_loop(0, ngrp, logit_body, tuple(jnp.full((1, QB), NEG, F32) for _ in range(H_B)))
    acc_scr[...] = jnp.zeros(acc_scr.shape, F32)

    def pv_body(g, ss):
        ss = list(ss)
        acc = acc_scr[...]
        for u in range(CU):
            c = g * CU + u
            r = pl.multiple_of(c * QB, QB)
            es = [jnp.exp(lg_scr[h, pl.ds(r, QB), :] - ms[h]) for h in range(H_B)]
            acc = acc + jnp.dot(kvt_scr[c], jnp.concatenate(es, axis=1).astype(BF16), preferred_element_type=F32)
            for h in range(H_B):
                ss[h] = ss[h] + es[h].sum(axis=0, keepdims=True)
        acc_scr[...] = acc
        return tuple(ss)

    ss = lax.fori_loop(0, ngrp, pv_body, tuple(jnp.zeros((1, QB), F32) for _ in range(H_B)))
    acc = heads(acc_scr[...])
    ot = jnp.concatenate([acc[h] / ss[h] for h in range(H_B)], axis=0)
    o_ref[...] = ot.T


def _dsa_prompt(Pq, Pk, rel_bias, B, T):
    nb = T // QB
    assert nb % CU == 0
    n_sel = min(TOPK_MAX, T // 4)
    idx_bits = max(1, int(math.ceil(math.log2(T))))
    return pl.pallas_call(
        functools.partial(_dsa_prompt_kernel, n_sel=n_sel, idx_bits=idx_bits),
        grid=(B, nb),
        in_specs=[pl.BlockSpec(memory_space=pltpu.SMEM),
                  pl.BlockSpec((QB, PQ_W), lambda b, j: (b * nb + j, 0)),
                  pl.BlockSpec((T, PK_W), lambda b, j: (b, 0))],
        out_specs=pl.BlockSpec((QB, W_B), lambda b, j: (b * nb + j, 0)),
        out_shape=jax.ShapeDtypeStruct((B * T, W_B), F32),
        scratch_shapes=[pltpu.VMEM((nb, DH_B, QB), BF16), pltpu.VMEM((T, QB), I32), pltpu.VMEM((H_B, T, QB), F32),
                        pltpu.VMEM((H_B, 3, QB, QB), F32), pltpu.VMEM((DH_B, H_B * QB), F32)],
        compiler_params=pltpu.CompilerParams(dimension_semantics=("arbitrary", "arbitrary"), vmem_limit_bytes=VMEM_LIMIT),
        name="dsa_prompt",
    )(rel_bias, Pq, Pk)


PG = 16
GW = PG * PAGE_SIZE


def _stack_heads(x, w):
    return jnp.concatenate([x[:, h * w:(h + 1) * w] for h in range(x.shape[1] // w)], axis=0)


def _dsa_s_scores_kernel(pt_ref, pq_ref, *refs, ng, ds):
    page_refs, o_ref = refs[:PG], refs[PG]
    g = pl.program_id(1)
    sm = pq_ref[:, PQ_SM:PQ_SM + LANE]
    qst = _stack_heads(pq_ref[:, PQ_QI:PQ_QI + LANE], D_IDX)
    wcol = jnp.concatenate([sm[:, SM_WI + h:SM_WI + h + 1] for h in range(H_IDX)], axis=0) * (H_IDX ** -0.5 * D_IDX ** -0.5)

    def tile_scores(qk):
        s = jnp.maximum(qk, 0.0) * wcol
        out = s[0:ds]
        for h in range(1, H_IDX):
            out = out + s[h * ds:(h + 1) * ds]
        return out

    @pl.when(g < ng)
    def _():
        for i in range(PG):
            o_ref[:, i * PAGE_SIZE:(i + 1) * PAGE_SIZE] = tile_scores(_bdot(qst, page_refs[i][...]))

    @pl.when(g == ng)
    def _():
        sc = tile_scores(_bdot_nt(qst, _pad_rows(sm[:, SM_KI:SM_KI + D_IDX], PAGE_SIZE)))
        lane, row = _iota2(sc.shape, 1), _iota2(sc.shape, 0)
        o_ref[:, 0:PAGE_SIZE] = jnp.where(lane <= row, sc, NEG)
        o_ref[:, PAGE_SIZE:GW] = jnp.full((ds, GW - PAGE_SIZE), NEG, F32)


def _dsa_s_attn_kernel(pt_ref, rb_ref, sc_ref, pq_ref, pk_ref, *refs, ng, ds, n_sel, idx_bits):
    k_refs, v_refs = refs[:PG], refs[PG:2 * PG]
    o_ref, key_scr, thr_scr, bias_scr, m_scr, l_scr, acc_scr = refs[2 * PG:]
    g = pl.program_id(1)
    ntile = (ng + 1) * PG
    hr = H_B * ds

    @pl.when(g == 0)
    def _():
        for i in range(ntile):
            key_scr[:, i * LANE:(i + 1) * LANE] = _sortable(sc_ref[:, i * LANE:(i + 1) * LANE])
        lane = _iota2((ds, LANE), 1)

        def count(pred):
            accs = [jnp.zeros((ds, LANE), I32) for _ in range(4)]
            for i in range(ng * PG + 1):
                accs[i % 4] = accs[i % 4] + pred(key_scr[:, i * LANE:(i + 1) * LANE], i * LANE)
            return ((accs[0] + accs[1]) + (accs[2] + accs[3])).sum(axis=1, keepdims=True)

        cnt_ge = lambda t: count(lambda k, off: jnp.where(k >= t, 1, 0))
        c0 = cnt_ge(jnp.zeros((ds, 1), I32))
        thr = jnp.where(c0 >= n_sel, 0, _INT_MIN).astype(I32)
        cthr = jnp.where(c0 >= n_sel, c0, ng * GW + LANE)

        def bit_body(i, carry):
            t, ct = carry
            trial = t | jnp.left_shift(jnp.int32(1), 30 - i)
            cnt = cnt_ge(trial)
            ok = cnt >= n_sel
            return jnp.where(ok, trial, t), jnp.where(ok, cnt, ct)

        thr, cthr = lax.fori_loop(0, 31, bit_body, (thr, cthr))

        def tie_search():
            quota = n_sel - count(lambda k, off: jnp.where(k > thr, 1, 0))

            def idx_body(i, x):
                trial = x | jnp.left_shift(jnp.int32(1), idx_bits - 1 - i)
                below = count(lambda k, off: jnp.where(k == thr, jnp.where(lane + off < trial, 1, 0), 0))
                return jnp.where(below < quota, trial, x)

            return lax.fori_loop(0, idx_bits, idx_body, jnp.zeros((ds, 1), I32))

        xthr = lax.cond(jnp.max(cthr) > n_sel, tie_search, lambda: jnp.full((ds, 1), (1 << idx_bits) - 1, I32))
        thr_scr[0] = jnp.broadcast_to(thr, (ds, LANE))
        thr_scr[1] = jnp.broadcast_to(xthr, (ds, LANE))

        qrow = _iota2((ds, LANE), 0)
        for h in range(H_B):
            bias_scr[0, h * ds:(h + 1) * ds, :] = _bias_from_dist(PAGE_SIZE + qrow - lane, rb_ref, h)
            bias_scr[1, h * ds:(h + 1) * ds, :] = _bias_from_dist(qrow - lane, rb_ref, h)
            bias_scr[2, h * ds:(h + 1) * ds, :] = jnp.full((ds, LANE), rb_ref[N_BUCKETS - 1, h], F32)
        m_scr[...] = jnp.full(m_scr.shape, NEG, F32)
        l_scr[...] = jnp.zeros(l_scr.shape, F32)
        acc_scr[...] = jnp.zeros(acc_scr.shape, F32)

    qst = _stack_heads(pq_ref[:, PQ_QB:PQ_QB + W_B], DH_B) * (DH_B ** -0.5)
    thr, xthr = thr_scr[0], thr_scr[1]

    def select(tile):
        k = key_scr[:, pl.ds(pl.multiple_of(tile * LANE, LANE), LANE)]
        idx = _iota2((ds, LANE), 1) + tile * LANE
        sel = _selected_i32(k, idx, thr, xthr)
        return jnp.concatenate([sel] * H_B, axis=0) > 0

    def update(lg, sel, pv):
        m_old = m_scr[...]
        m_new = jnp.maximum(m_old, jnp.where(sel, lg, NEG).max(axis=1, keepdims=True))
        p = jnp.where(sel, jnp.exp(lg - m_new), 0.0)
        corr = jnp.exp(m_old - m_new)
        l_scr[...] = l_scr[...] * corr + p.sum(axis=1, keepdims=True)
        acc_scr[...] = acc_scr[...] * corr + pv(p)
        m_scr[...] = m_new

    @pl.when(g < ng)
    def _():
        lgs, sels = [], []
        for i in range(PG):
            lg = _bdot(qst, k_refs[i][...])
            if i == PG - 1:
                lg = lg + jnp.where(g == ng - 1, bias_scr[0], bias_scr[2])
            else:
                lg = lg + bias_scr[2]
            lgs.append(lg)
            sels.append(select(g * PG + i))
        vt = jnp.concatenate([v_refs[i][...] for i in range(PG)], axis=1)
        update(jnp.concatenate(lgs, axis=1), jnp.concatenate(sels, axis=1), lambda p: _bdot_nt(p, vt))

    @pl.when(g == ng)
    def _():
        kvn = _pad_rows(pk_ref[:, PK_KV:PK_KV + LANE], PAGE_SIZE)
        lg = _bdot_nt(qst, kvn[:, 0:DH_B]) + bias_scr[1]
        update(lg, select(ng * PG), lambda p: _bdot(p, kvn[:, DH_B:2 * DH_B]))
        o = acc_scr[...] / l_scr[...]
        o_ref[...] = jnp.concatenate([o[h * ds:(h + 1) * ds] for h in range(H_B)], axis=1)


def _dsa_sample(Pq, Pk, rel_bias, cache_k, cache_v, cache_kidx, page_table, layer, B, DS):
    n_pages = page_table.shape[1]
    past = n_pages * PAGE_SIZE
    assert n_pages % PG == 0
    ng = n_pages // PG
    n_sel = min(TOPK_MAX, (past + DS) // 4)
    idx_bits = int(math.ceil(math.log2(past + LANE)))
    s_pad = (ng + 1) * GW

    def page_spec(width, i):
        return pl.BlockSpec((None, None, width, PAGE_SIZE),
                            lambda b, g, pt: (layer, pt[b, jnp.minimum(g, ng - 1) * PG + i], 0, 0))

    row = lambda b, g, pt: (b, 0)
    cp = pltpu.CompilerParams(dimension_semantics=("arbitrary", "arbitrary"), vmem_limit_bytes=VMEM_LIMIT)
    scores = pl.pallas_call(
        functools.partial(_dsa_s_scores_kernel, ng=ng, ds=DS),
        grid_spec=pltpu.PrefetchScalarGridSpec(
            num_scalar_prefetch=1, grid=(B, ng + 1),
            in_specs=[pl.BlockSpec((DS, PQ_W), row)] + [page_spec(D_IDX, i) for i in range(PG)],
            out_specs=pl.BlockSpec((None, DS, GW), lambda b, g, pt: (b, 0, g))),
        out_shape=jax.ShapeDtypeStruct((B, DS, s_pad), F32),
        compiler_params=cp, name="dsa_sample_scores",
    )(page_table, Pq, *([cache_kidx] * PG))

    return pl.pallas_call(
        functools.partial(_dsa_s_attn_kernel, ng=ng, ds=DS, n_sel=n_sel, idx_bits=idx_bits),
        grid_spec=pltpu.PrefetchScalarGridSpec(
            num_scalar_prefetch=1, grid=(B, ng + 1),
            in_specs=[pl.BlockSpec(memory_space=pltpu.SMEM),
                      pl.BlockSpec((None, DS, s_pad), lambda b, g, pt: (b, 0, 0)),
                      pl.BlockSpec((DS, PQ_W), row), pl.BlockSpec((DS, PK_W), row)]
                     + [page_spec(DH_B, i) for i in range(PG)] + [page_spec(DH_B, i) for i in range(PG)],
            out_specs=pl.BlockSpec((DS, W_B), lambda b, g, pt: (b, 0)),
            scratch_shapes=[pltpu.VMEM((DS, s_pad), I32), pltpu.VMEM((2, DS, LANE), I32), pltpu.VMEM((3, H_B * DS, LANE), F32),
                            pltpu.VMEM((H_B * DS, 1), F32), pltpu.VMEM((H_B * DS, 1), F32), pltpu.VMEM((H_B * DS, DH_B), F32)]),
        out_shape=jax.ShapeDtypeStruct((B * DS, W_B), F32),
        compiler_params=cp, name="dsa_sample_attn",
    )(page_table, rel_bias, scores, Pq, Pk, *([cache_k] * PG), *([cache_v] * PG))


def _merge_kernel(x_ref, oa_ref, ob_ref, oc_ref, od_ref, gate_ref, wb_ref, wo_ref, o_ref):
    m = None
    r0 = 0
    for br, ref in enumerate((oa_ref, ob_ref, oc_ref, od_ref)):
        w = ref.shape[1]
        t = _sigmoid(gate_ref[:, br * D_MODEL:(br + 1) * D_MODEL]) * jnp.dot(ref[...].astype(BF16), wb_ref[r0:r0 + w, :], preferred_element_type=F32)
        m = t if m is None else m + t
        r0 += w
    o_ref[...] = x_ref[...] + jnp.dot(m.astype(BF16), wo_ref[...], preferred_element_type=F32)


def _merge(x2d, oa, ob, oc, od, P, wb, wo):
    n = x2d.shape[0]
    tm = min(n, 256)
    row = lambda i: (i, 0)
    return pl.pallas_call(
        _merge_kernel,
        grid=(n // tm,),
        in_specs=[pl.BlockSpec((tm, D_MODEL), row), pl.BlockSpec((tm, W_A), row), pl.BlockSpec((tm, W_B), row),
                  pl.BlockSpec((tm, W_C), row), pl.BlockSpec((tm, W_D), row),
                  pl.BlockSpec((tm, N_BRANCH * D_MODEL), lambda i: (i, OFF_GATE // (N_BRANCH * D_MODEL))),
                  _const_spec((MIX_W, D_MODEL)), _const_spec((D_MODEL, D_MODEL))],
        out_specs=pl.BlockSpec((tm, D_MODEL), row),
        out_shape=jax.ShapeDtypeStruct((n, D_MODEL), F32),
        compiler_params=pltpu.CompilerParams(dimension_semantics=("arbitrary",), vmem_limit_bytes=VMEM_LIMIT),
        name="merge",
    )(x2d, oa, ob, oc, od, P, wb, wo)


def _mlp_kernel(x_ref, g_ref, wu_ref, wd_ref, gf_ref, *refs, final):
    if final:
        o_ref, y_ref, h_scr, acc_scr = refs
    else:
        o_ref, h_scr, acc_scr = refs
    f = pl.program_id(1)

    @pl.when(f == 0)
    def _():
        x = x_ref[...]
        r = lax.rsqrt(jnp.mean(x * x, axis=-1, keepdims=True) + EPS)
        h_scr[...] = ((x * r) * g_ref[...]).astype(BF16)
        acc_scr[...] = jnp.zeros(acc_scr.shape, F32)

    a = jnp.maximum(jnp.dot(h_scr[...], wu_ref[...], preferred_element_type=F32), 0.0)
    acc_scr[...] += jnp.dot((a * a).astype(BF16), wd_ref[...], preferred_element_type=F32)

    @pl.when(f == pl.num_programs(1) - 1)
    def _():
        out = x_ref[...] + acc_scr[...]
        o_ref[...] = out
        if final:
            r = lax.rsqrt(jnp.mean(out * out, axis=-1, keepdims=True) + EPS)
            y_ref[...] = (out * r) * gf_ref[...]


def _mlp(x2d, g, wu, wd, gf, final):
    n = x2d.shape[0]
    tm = min(n, 512)
    tf = 1024
    row = lambda i, f: (i, 0)
    out_spec = pl.BlockSpec((tm, D_MODEL), row)
    shp = jax.ShapeDtypeStruct((n, D_MODEL), F32)
    return pl.pallas_call(
        functools.partial(_mlp_kernel, final=final),
        grid=(n // tm, D_FF // tf),
        in_specs=[pl.BlockSpec((tm, D_MODEL), row), _const_spec((1, D_MODEL)),
                  pl.BlockSpec((D_MODEL, tf), lambda i, f: (0, f)), pl.BlockSpec((tf, D_MODEL), lambda i, f: (f, 0)),
                  _const_spec((1, D_MODEL))],
        out_specs=[out_spec, out_spec] if final else [out_spec],
        out_shape=[shp, shp] if final else [shp],
        scratch_shapes=[pltpu.VMEM((tm, D_MODEL), BF16), pltpu.VMEM((tm, D_MODEL), F32)],
        compiler_params=pltpu.CompilerParams(dimension_semantics=("arbitrary", "arbitrary"), vmem_limit_bytes=VMEM_LIMIT),
        name="mlp_final" if final else "mlp",
    )(x2d, g.reshape(1, D_MODEL), wu, wd, gf.reshape(1, D_MODEL))


def _layer(x, conv_state, s_delta, s_ssm, s_hgrn, lw, consts, sample_ctx, final):
    B, T, _ = x.shape
    x2 = x.reshape(B * T, D_MODEL)
    P, Pq, Pk = _in_proj(x2, lw['norm_mix'], lw['w_in'])

    def conv0(lo, hi):
        if conv_state is None:
            return jnp.zeros((B, SUB, hi - lo), F32)
        return jnp.concatenate([jnp.zeros((B, SUB - (CONV_W - 1), hi - lo), F32), conv_state[:, :, lo:hi]], axis=1)

    zeros_state = lambda n: jnp.zeros((B, n, 64, 64), F32)
    c = consts
    half = CONV_CH // 2
    assert CONV_A == half and OFF_PC % half == 0
    oa, s_a = _mix_call(
        _mix_a_kernel, "mix_a", P, B, T,
        [(OFF_PC, half), (OFF_ZA, W_A), (OFF_SM, LANE)],
        [conv0(0, CONV_A), zeros_state(H_A) if s_delta is None else s_delta],
        [lw['conv_w'][:, 0:CONV_A], lw['conv_b'][None, 0:CONV_A], lw['hp'], lw['hpt'], lw['gn_a'], c['tril2a'], c['triu2a'], c['bo64']],
        W_A, (H_A, DK_A, DV_A),
        [pltpu.VMEM((TC_A + SUB, CONV_A), F32), pltpu.VMEM((H_A, DK_A, DV_A), F32), pltpu.VMEM((TC_A, W_A), F32)],
        tc=TC_A)
    oc, s_c = _mix_call(
        _mix_c_kernel, "mix_c", P, B, T,
        [(OFF_PC + half, half), (OFF_ZC, W_C), (OFF_SM, LANE)],
        [conv0(CONV_A, CONV_CH), zeros_state(H_C) if s_ssm is None else s_ssm],
        [lw['conv_w'][:, CONV_A:], lw['conv_b'][None, CONV_A:], lw['hp'], lw['hpt'], lw['gn_c'], c['tril1'], c['triu1'], c['bo256']],
        W_C, (H_C, N_C, P_C),
        [pltpu.VMEM((TC + SUB, CONV_C), F32), pltpu.VMEM((H_C, N_C, P_C), F32), pltpu.VMEM((TC, W_C), F32)])
    sbd0 = jnp.zeros((B, H_D, DK_D, H_D, DV_D), F32)
    if s_hgrn is not None:
        for h in range(H_D):
            sbd0 = sbd0.at[:, h, :, h, :].set(s_hgrn[:, h])
    sbd0 = sbd0.reshape(B, H_D * DK_D, W_D)
    od, sbd = _mix_call(
        _mix_d_kernel, "mix_d", P, B, T,
        [(OFF_D, 4 * W_D)],
        [sbd0],
        [lw['lb'], lw['gn_d'], c['tril2'], c['sel'], c['bo64']],
        W_D, (H_D * DK_D, W_D),
        [pltpu.VMEM((H_D * DK_D, W_D), F32), pltpu.VMEM((TC, W_D), F32)])
    sbd5 = sbd.reshape(B, H_D, DK_D, H_D, DV_D)
    s_d = jnp.stack([sbd5[:, h, :, h, :] for h in range(H_D)], axis=1)
    if sample_ctx is None:
        ob = _dsa_prompt(Pq, Pk, lw['rel_bias'], B, T)
    else:
        ob = _dsa_sample(Pq, Pk, lw['rel_bias'], sample_ctx['cache_k'], sample_ctx['cache_v'], sample_ctx['cache_kidx'],
                         sample_ctx['page_table'], sample_ctx['layer'], B, T)
    x1 = _merge(x2, oa, ob, oc, od, P, lw['w_branch'], lw['w_out'])
    outs = _mlp(x1, lw['norm_mlp'], lw['w_up'], lw['w_down'], lw['norm_final'], final)
    xo = outs[0].reshape(B, T, D_MODEL)
    y = outs[1].reshape(B, T, D_MODEL) if final else None
    P3 = P.reshape(B, T, P_PAD)
    Pk3 = Pk.reshape(B, T, PK_W)
    states = (Pk3[:, :, PK_KV:PK_KV + DH_B], Pk3[:, :, PK_KV + DH_B:PK_KV + 2 * DH_B], Pk3[:, :, PK_SM + SM_KI:PK_SM + SM_KI + D_IDX],
              P3[:, T - (CONV_W - 1):, OFF_PC:OFF_PC + CONV_CH], s_a, s_c, s_d)
    return xo, y, states


def kernel(x_prompt, x_sample, cache_k, cache_v, cache_kidx, state_conv, state_delta, state_ssm, state_hgrn, page_table, norm_mix, w_in, conv_w, conv_b, a_log_a, dt_bias_a, gnorm_a, rel_bias, a_log_c, dt_bias_c, d_skip_c, gnorm_c, hgrn_gamma, gnorm_d, w_branch, w_out, norm_mlp, w_up, w_down, norm_final):
    depth = w_in.shape[0]
    npc = _np_consts()
    consts = {k: jnp.asarray(v, BF16) for k, v in npc.items()}
    pg = jax.nn.softmax(hgrn_gamma.astype(F32), axis=0)
    lower_bounds = jnp.cumsum(pg, axis=0) - pg[0]
    cache_kt, cache_vt, cache_kit = (jnp.swapaxes(c, 2, 3) for c in (cache_k, cache_v, cache_kidx))
    yp, ys = x_prompt, x_sample
    new_p, new_s = [], []
    for l in range(depth):
        hp = jnp.zeros((SUB, LANE), F32)
        hp = hp.at[0, 0:H_A].set(a_log_a[l]).at[1, 0:H_A].set(dt_bias_a[l])
        hp = hp.at[2, 0:H_C].set(a_log_c[l]).at[3, 0:H_C].set(dt_bias_c[l]).at[4, 0:H_C].set(d_skip_c[l])
        lw = dict(norm_mix=norm_mix[l], w_in=_permute_w_in(w_in[l]), conv_w=conv_w[l], conv_b=conv_b[l],
                  hp=hp, hpt=hp.T, gn_a=jnp.tile(gnorm_a[l], H_A)[None, :], gn_c=gnorm_c[l][None, :],
                  gn_d=jnp.tile(gnorm_d[l], H_D)[None, :], lb=lower_bounds[l][None, :], rel_bias=rel_bias,
                  w_branch=w_branch[l].astype(BF16), w_out=w_out[l].astype(BF16), norm_mlp=norm_mlp[l],
                  w_up=w_up[l].astype(BF16), w_down=w_down[l].astype(BF16), norm_final=norm_final)
        final = l == depth - 1
        yp, yp_n, st_p = _layer(yp, None, None, None, None, lw, consts, None, final)
        ctx = dict(cache_k=cache_kt, cache_v=cache_vt, cache_kidx=cache_kit, page_table=page_table, layer=l)
        ys, ys_n, st_s = _layer(ys, state_conv[l], state_delta[l], state_ssm[l], state_hgrn[l], lw, consts, ctx, final)
        new_p.append(st_p)
        new_s.append(st_s)
    stack = lambda per_layer: [jnp.stack(items) for items in zip(*per_layer)]
    return (yp_n, ys_n, *stack(new_p), *stack(new_s))
```

```python
import functools
import math

import numpy as np
import jax
import jax.numpy as jnp
from jax import lax
from jax.experimental import pallas as pl
from jax.experimental.pallas import tpu as pltpu

F32 = jnp.float32
BF16 = jnp.bfloat16
I32 = jnp.int32

D_MODEL = 1024
PAGE_SIZE = 128
EPS = 1e-6
NEG = -1e30
TINY = 1e-30
CONV_W = 4
D_FF = 4 * D_MODEL
N_BRANCH = 4
H_A, DK_A, DV_A = 4, 64, 64
H_B, DH_B = 4, 64
H_IDX, D_IDX = 4, 32
TOPK_MAX = 256
N_BUCKETS = 32
MAX_DISTANCE = 128
H_C, P_C, N_C, G_C = 8, 64, 64, 2
H_D, DK_D, DV_D = 4, 64, 64
QK_A = H_A * DK_A
CONV_A = 2 * QK_A + H_A * DV_A
D_INNER_C = H_C * P_C
CONV_C = D_INNER_C + 2 * G_C * N_C
CONV_CH = CONV_A + CONV_C
W_A, W_B, W_C, W_D = H_A * DV_A, H_B * DH_B, D_INNER_C, H_D * DV_D
MIX_W = W_A + W_B + W_C + W_D
IN_WIDTHS = (CONV_CH, W_A, H_A, H_A, W_B, DH_B, DH_B, H_IDX * D_IDX, D_IDX, H_IDX, D_INNER_C, H_C,
             H_D * DK_D, H_D * DK_D, W_D, W_D, N_BRANCH * D_MODEL)
P_IN = sum(IN_WIDTHS)

LANE = 128
SUB = 8
TC = 128
TC_A = 256
QB = 128
VMEM_LIMIT = 48 * 1024 * 1024

OFF_GATE, OFF_D, OFF_ZC, OFF_ZA, OFF_QB, OFF_PC, OFF_KV, OFF_QI, OFF_SM = 0, 4096, 5120, 5632, 5888, 6144, 7680, 7808, 7936
P_PAD = 8064
SM_KI, SM_WI, SM_AA, SM_BA, SM_DT = 0, 32, 36, 40, 44


def _src_offsets():
    offs, o = [], 0
    for w in IN_WIDTHS:
        offs.append(o)
        o += w
    return offs


def _permute_w_in(w):
    (o_pc, o_za, o_aa, o_ba, o_qb, o_kb, o_vb, o_qi, o_ki, o_wi, o_zc, o_dt, o_qd, o_fd, o_id, o_gd, o_gate) = _src_offsets()
    wt = jnp.swapaxes(w, 0, 1)
    seg = lambda o, n: wt[o:o + n, :]
    small = jnp.concatenate([seg(o_ki, D_IDX), seg(o_wi, H_IDX), seg(o_aa, H_A), seg(o_ba, H_A), seg(o_dt, H_C),
                             jnp.zeros((LANE - (D_IDX + H_IDX + 2 * H_A + H_C), w.shape[0]), w.dtype)], axis=0)
    out = jnp.concatenate([
        seg(o_gate, N_BRANCH * D_MODEL),
        seg(o_qd, 4 * W_D),
        seg(o_zc, D_INNER_C), seg(o_za, W_A), seg(o_qb, W_B), seg(o_pc, CONV_CH),
        seg(o_kb, 2 * DH_B),
        seg(o_qi, H_IDX * D_IDX), small], axis=0)
    assert out.shape[0] == P_PAD
    return out.astype(BF16)


def _bdot(a, b):
    return jnp.dot(a.astype(BF16), b.astype(BF16), preferred_element_type=F32)


def _bdot_nt(a, b):
    return lax.dot_general(a.astype(BF16), b.astype(BF16), (((1,), (1,)), ((), ())), preferred_element_type=F32)


def _split2(a):
    hi = a.astype(BF16)
    lo = (a - hi.astype(F32)).astype(BF16)
    return hi, lo


def _split3(a):
    hi = a.astype(BF16)
    r = a - hi.astype(F32)
    mid = r.astype(BF16)
    lo = (r - mid.astype(F32)).astype(BF16)
    return hi, mid, lo


def _dot01_left(m01, x):
    hi, mid, lo = _split3(x)
    d = lambda p: jnp.dot(m01, p, preferred_element_type=F32)
    return d(hi) + (d(mid) + d(lo))


def _dot01_right(x, m01):
    hi, mid, lo = _split3(x)
    d = lambda p: jnp.dot(p, m01, preferred_element_type=F32)
    return d(hi) + (d(mid) + d(lo))


def _xdot(a, b):
    ah, al = _split2(a)
    bh, bl = _split2(b)
    d = lambda p, q: jnp.dot(p, q, preferred_element_type=F32)
    return d(ah, bh) + (d(ah, bl) + d(al, bh))


def _sigmoid(x):
    return 1.0 / (1.0 + jnp.exp(-x))


def _silu(x):
    return x * _sigmoid(x)


def _softplus(x):
    return jnp.maximum(x, 0.0) + jnp.log(1.0 + jnp.exp(-jnp.abs(x)))


def _iota2(shape, dim):
    return lax.broadcasted_iota(I32, shape, dim)


def _pad_rows(x, rows):
    if x.shape[0] == rows:
        return x
    return jnp.concatenate([x, jnp.zeros((rows - x.shape[0],) + x.shape[1:], x.dtype)], axis=0)


def _tri_inv_many(Ls, top):
    n = Ls[0].shape[0]
    ii, jj = _iota2((n, n), 0), _iota2((n, n), 1)
    xor = ii ^ jj
    eye = jnp.where(ii == jj, 1.0, 0.0)
    Ns = [jnp.where((xor >> 3) == 0, -L, 0.0) for L in Ls]
    Xs = [eye + N for N in Ns]
    N2s = [_bdot(N, N) for N in Ns]
    Xs = [X + _bdot(X, N2) for X, N2 in zip(Xs, N2s)]
    N4s = [_bdot(N2, N2) for N2 in N2s]
    Xs = [X + _bdot(X, N4) for X, N4 in zip(Xs, N4s)]
    sh = 4
    while (1 << sh) <= top:
        XBs = [_bdot(X, jnp.where((xor >> (sh - 1)) == 1, L, 0.0)) for X, L in zip(Xs, Ls)]
        Xs = [X - _bdot(XB, X) for X, XB in zip(Xs, XBs)]
        sh += 1
    return Xs


def _conv_silu(pc, ext_scr, cw_ref, cb_ref):
    tc = pc.shape[0]
    ext_scr[SUB:SUB + tc, :] = pc
    y = cb_ref[...]
    for j in range(CONV_W):
        y = y + ext_scr[SUB - (CONV_W - 1) + j:SUB - (CONV_W - 1) + j + tc, :] * cw_ref[j:j + 1, :]
    ext_scr[0:SUB, :] = ext_scr[tc:tc + SUB, :]
    return _silu(y)


TN_IN = 1152
PQ_W = W_B + 2 * LANE
PK_W = 2 * LANE
PQ_QB, PQ_QI, PQ_SM = 0, W_B, W_B + LANE
PK_KV, PK_SM = 0, LANE
_COMPACT_COPIES = (("pq", PQ_QB, OFF_QB, W_B), ("pq", PQ_QI, OFF_QI, LANE), ("pq", PQ_SM, OFF_SM, LANE),
                   ("pk", PK_KV, OFF_KV, LANE), ("pk", PK_SM, OFF_SM, LANE))


def _in_proj_kernel(x_ref, g_ref, w_ref, o_ref, pq_ref, pk_ref, h_scr):
    j = pl.program_id(1)

    @pl.when(j == 0)
    def _():
        x = x_ref[...]
        r = lax.rsqrt(jnp.mean(x * x, axis=-1, keepdims=True) + EPS)
        h_scr[...] = ((x * r) * g_ref[...]).astype(BF16)

    o_ref[...] = lax.dot_general(h_scr[...], w_ref[...], (((1,), (1,)), ((), ())), preferred_element_type=F32)

    def copy(dst_ref, dst, src, width):
        tile, lo = divmod(src, TN_IN)
        assert lo + width <= TN_IN

        @pl.when(j == tile)
        def _():
            dst_ref[:, dst:dst + width] = o_ref[:, lo:lo + width]

    for name, dst, src, width in _COMPACT_COPIES:
        copy(pq_ref if name == "pq" else pk_ref, dst, src, width)


def _in_proj(x2d, g, w_perm_t):
    n = x2d.shape[0]
    tm = min(n, 1024)
    tn = TN_IN
    return pl.pallas_call(
        _in_proj_kernel,
        grid=(n // tm, P_PAD // tn),
        in_specs=[pl.BlockSpec((tm, D_MODEL), lambda i, j: (i, 0)),
                  pl.BlockSpec((1, D_MODEL), lambda i, j: (0, 0)),
                  pl.BlockSpec((tn, D_MODEL), lambda i, j: (j, 0))],
        out_specs=[pl.BlockSpec((tm, tn), lambda i, j: (i, j)),
                   pl.BlockSpec((tm, PQ_W), lambda i, j: (i, 0)), pl.BlockSpec((tm, PK_W), lambda i, j: (i, 0))],
        out_shape=[jax.ShapeDtypeStruct((n, P_PAD), F32), jax.ShapeDtypeStruct((n, PQ_W), F32),
                   jax.ShapeDtypeStruct((n, PK_W), F32)],
        scratch_shapes=[pltpu.VMEM((tm, D_MODEL), BF16)],
        compiler_params=pltpu.CompilerParams(dimension_semantics=("arbitrary", "arbitrary"), vmem_limit_bytes=VMEM_LIMIT),
        name="in_proj",
    )(x2d, g.reshape(1, D_MODEL), w_perm_t)


def _mix_a_kernel(pc_ref, za_ref, sm_ref, conv0_ref, s0_ref, cw_ref, cb_ref, hp_ref, hpt_ref, gn_ref,
                  tril_ref, triu_ref, bo_ref, o_ref, sout_ref, ext_scr, s_scr, ob_scr, *, t_valid, rows):
    CA = 64
    tc = ob_scr.shape[0]

    @pl.when(pl.program_id(1) == 0)
    def _():
        ext_scr[0:SUB, :] = conv0_ref[...]
        s_scr[...] = s0_ref[...]

    u = _conv_silu(_pad_rows(pc_ref[...], tc), ext_scr, cw_ref, cb_ref)
    bo = bo_ref[...]
    q_raw, k_raw, va = u[:, 0:QK_A], u[:, QK_A:2 * QK_A], u[:, 2 * QK_A:CONV_A]
    qq = q_raw * lax.rsqrt(_dot01_right(q_raw * q_raw, bo) + EPS) * (DK_A ** -0.5)
    ka = k_raw * lax.rsqrt(_dot01_right(k_raw * k_raw, bo) + EPS)
    kat = ka.T

    sm = _pad_rows(sm_ref[...], tc)
    smt = sm.T
    hp, hpt = hp_ref[...], hpt_ref[...]
    g_col = -jnp.exp(hp[0:1, 0:H_A]) * _softplus(sm[:, SM_AA:SM_AA + H_A] + hp[1:2, 0:H_A])
    beta = _sigmoid(sm[:, SM_BA:SM_BA + H_A])
    g_row = -jnp.exp(hpt[0:H_A, 0:1]) * _softplus(smt[SM_AA:SM_AA + H_A, :] + hpt[0:H_A, 1:2])
    if t_valid < tc:
        g_col = jnp.where(_iota2(g_col.shape, 0) < t_valid, g_col, 0.0)
        beta = jnp.where(_iota2(beta.shape, 0) < t_valid, beta, 0.0)
        g_row = jnp.where(_iota2(g_row.shape, 1) < t_valid, g_row, 0.0)
    gcum_col = _dot01_left(tril_ref[...], g_col)
    gcum_row = _dot01_right(g_row, triu_ref[...])

    HS = H_A * CA
    ii, jj = _iota2((HS, HS), 0), _iota2((HS, HS), 1)
    same = (ii >> 6) == (jj >> 6)
    incl, strict = same & (ii >= jj), same & (ii > jj)
    nsc = -(-t_valid // CA)
    pre = []
    for sc in range(nsc):
        r0 = sc * CA
        stack = lambda x, w: jnp.concatenate([x[r0:r0 + CA, h * w:(h + 1) * w] for h in range(H_A)], axis=0)
        ks, qs, vs = stack(ka, DK_A), stack(qq, DK_A), stack(va, DV_A)
        bcol, gc = stack(beta, 1), stack(gcum_col, 1)
        gr = jnp.concatenate([gcum_row[h:h + 1, r0:r0 + CA] for h in range(H_A)], axis=1)
        e = jnp.exp(jnp.where(incl, gc - gr, 0.0))
        kb = ks * bcol
        eg = jnp.exp(gc)
        pre.append(dict(r0=r0, gc=gc, qe=qs * eg, low=jnp.where(strict, _bdot_nt(kb, ks) * e, 0.0),
                        rhs=jnp.concatenate([vs * bcol, kb * eg], axis=1),
                        attn=jnp.where(incl, _bdot_nt(qs, ks) * e, 0.0)))
    tinvs = _tri_inv_many([p['low'] for p in pre], CA)
    sols = [_xdot(t, p['rhs']) for t, p in zip(tinvs, pre)]
    hs = lambda x, h: x[h * CA:(h + 1) * CA]
    for p, sol in zip(pre, sols):
        r0, gc = p['r0'], p['gc']
        states = [s_scr[h] for h in range(H_A)]
        v_new = jnp.concatenate([hs(sol, h)[:, 0:DV_A] - _bdot(hs(sol, h)[:, DV_A:], states[h]) for h in range(H_A)], axis=0)
        o = jnp.concatenate([_bdot(hs(p['qe'], h), states[h]) for h in range(H_A)], axis=0) + _bdot(p['attn'], v_new)
        for h in range(H_A):
            gch = hs(gc, h)
            gl = gch[CA - 1:CA, :]
            s_scr[h] = states[h] * jnp.exp(gl) + _bdot(kat[h * DK_A:(h + 1) * DK_A, r0:r0 + CA], hs(v_new, h) * jnp.exp(gl - gch))
            ob_scr[r0:r0 + CA, h * DV_A:(h + 1) * DV_A] = hs(o, h)
    if nsc * CA < tc:
        ob_scr[nsc * CA:tc, :] = jnp.zeros((tc - nsc * CA, W_A), F32)

    o = ob_scr[...]
    o = o * lax.rsqrt(_dot01_right(o * o, bo) * (1.0 / DV_A) + EPS) * gn_ref[...]
    res = o * _silu(_pad_rows(za_ref[...], tc))
    o_ref[...] = res[0:rows]
    sout_ref[...] = s_scr[...]


def _mix_c_kernel(pc_ref, zc_ref, sm_ref, conv0_ref, s0_ref, cw_ref, cb_ref, hp_ref, hpt_ref, gn_ref,
                  tril_ref, triu_ref, bo_ref, o_ref, sout_ref, ext_scr, s_scr, yb_scr, *, t_valid, rows):
    @pl.when(pl.program_id(1) == 0)
    def _():
        ext_scr[0:SUB, :] = conv0_ref[...]
        s_scr[...] = s0_ref[...]

    u = _conv_silu(_pad_rows(pc_ref[...], TC), ext_scr, cw_ref, cb_ref)
    xc = u[:, 0:D_INNER_C]
    bcm = u[:, D_INNER_C:D_INNER_C + G_C * N_C]
    ccm = u[:, D_INNER_C + G_C * N_C:CONV_C]
    bct = bcm.T
    zc = _pad_rows(zc_ref[...], TC)

    sm = _pad_rows(sm_ref[...], TC)
    smt = sm.T
    hp, hpt = hp_ref[...], hpt_ref[...]
    dt_col = _softplus(sm[:, SM_DT:SM_DT + H_C] + hp[3:4, 0:H_C])
    dt_row = _softplus(smt[SM_DT:SM_DT + H_C, :] + hpt[0:H_C, 3:4])
    if t_valid < TC:
        dt_col = jnp.where(_iota2(dt_col.shape, 0) < t_valid, dt_col, 0.0)
        dt_row = jnp.where(_iota2(dt_row.shape, 1) < t_valid, dt_row, 0.0)
    gcum_col = _dot01_left(tril_ref[...], -jnp.exp(hp[2:3, 0:H_C]) * dt_col)
    gcum_row = _dot01_right(-jnp.exp(hpt[0:H_C, 2:3]) * dt_row, triu_ref[...])

    incl = _iota2((TC, TC), 0) >= _iota2((TC, TC), 1)
    hg = H_C // G_C
    for g in range(G_C):
        cc_g = ccm[:, g * N_C:(g + 1) * N_C]
        gram = _bdot_nt(cc_g, bcm[:, g * N_C:(g + 1) * N_C])
        for hh in range(hg):
            h = g * hg + hh
            c0 = h * P_C
            gc = gcum_col[:, h:h + 1]
            gr = gcum_row[h:h + 1, :]
            dec = jnp.where(incl, jnp.exp(jnp.where(incl, gc - gr, 0.0)), 0.0)
            attn = gram * dt_row[h:h + 1, :] * dec
            xh = xc[:, c0:c0 + P_C]
            S = s_scr[h]
            o = jnp.exp(gc) * _bdot(cc_g, S) + _bdot(attn, xh)
            gl = gc[TC - 1:TC, :]
            s_scr[h] = S * jnp.exp(gl) + _bdot(bct[g * N_C:(g + 1) * N_C, :], xh * (dt_col[:, h:h + 1] * jnp.exp(gl - gc)))
            yb_scr[:, c0:c0 + P_C] = (o + hp[4:5, h:h + 1] * xh) * _silu(zc[:, c0:c0 + P_C])

    y = yb_scr[...]
    y = y * lax.rsqrt(_dot01_right(y * y, bo_ref[...]) * (1.0 / (D_INNER_C // G_C)) + EPS) * gn_ref[...]
    o_ref[...] = y[0:rows]
    sout_ref[...] = s_scr[...]


_D_LEVELS = 6


def _mix_d_kernel(dg_ref, s0_ref, lb_ref, gn_ref, tril_ref, sel_ref, bo_ref, o_ref, sout_ref, s_scr, ob_scr, *, t_valid, rows):
    CD = 64
    W = H_D * DK_D

    @pl.when(pl.program_id(1) == 0)
    def _():
        s_scr[...] = s0_ref[...]

    dg = _pad_rows(dg_ref[...], TC)
    q = _silu(dg[:, 0:W])
    fr = dg[:, W:2 * W]
    v = dg[:, 2 * W:3 * W]
    gd = dg[:, 3 * W:4 * W]
    lb = lb_ref[...]
    logf = jnp.log(jnp.maximum(lb + (1.0 - lb) * _sigmoid(fr), TINY))
    kd = (1.0 - lb) * _sigmoid(-fr)
    if t_valid < TC:
        ok = _iota2((TC, W), 0) < t_valid
        logf = jnp.where(ok, logf, 0.0)
        kd = jnp.where(ok, kd, 0.0)
    gcum = _dot01_left(tril_ref[...], logf)
    gm = _dot01_left(sel_ref[...], gcum)
    qdec = q * jnp.exp(gcum)
    nsc = -(-t_valid // CD)
    gl_rows = jnp.concatenate([jnp.broadcast_to(gcum[(s + 1) * CD - 1:(s + 1) * CD, :], (CD, W)) for s in range(TC // CD)], axis=0)
    kdt = (kd * jnp.exp(gl_rows - gcum)).T
    gct = gcum.T

    HS = H_D * CD
    head_lanes = (_iota2((HS, W), 0) >> 6) == (_iota2((HS, W), 1) >> 6)
    stack = lambda x: jnp.where(head_lanes, jnp.concatenate([x] * H_D, axis=0), 0.0)
    ci, jj = _iota2((HS, CD), 0) & (CD - 1), _iota2((HS, CD), 1)
    later = jnp.where(ci > jj, ci ^ jj, 0)
    lhs, rhs, msk = [q], [kd], [ci == jj]
    for lv in range(1, _D_LEVELS + 1):
        gml = gm[(lv - 1) * TC:lv * TC, :]
        lhs.append(q * jnp.exp(jnp.minimum(gcum - gml, 0.0)))
        rhs.append(kd * jnp.exp(jnp.minimum(gml - gcum, 0.0)))
        msk.append((later >> (lv - 1)) == 1)

    lane_head = _iota2((CD, W), 1) >> 6
    intra = []
    for sc in range(nsc):
        r0 = sc * CD
        attn = jnp.zeros((HS, CD), F32)
        for a, b, m in zip(lhs, rhs, msk):
            attn = attn + jnp.where(m, _bdot_nt(stack(a[r0:r0 + CD]), b[r0:r0 + CD]), 0.0)
        full = _bdot(attn, v[r0:r0 + CD])
        o = jnp.zeros((CD, W), F32)
        for h in range(H_D):
            o = o + jnp.where(lane_head == h, full[h * CD:(h + 1) * CD], 0.0)
        intra.append(o)
    for sc in range(nsc):
        r0 = sc * CD
        sbd = s_scr[...]
        ob_scr[r0:r0 + CD, :] = _bdot(qdec[r0:r0 + CD], sbd) + intra[sc]
        decay = jnp.exp(gct[:, r0 + CD - 1:r0 + CD])
        s_scr[...] = jnp.where(head_lanes, sbd * decay + _bdot(kdt[:, r0:r0 + CD], v[r0:r0 + CD]), 0.0)
    if nsc < TC // CD:
        ob_scr[nsc * CD:TC, :] = jnp.zeros((TC - nsc * CD, W), F32)

    o = ob_scr[...]
    o = o * lax.rsqrt(_dot01_right(o * o, bo_ref[...]) * (1.0 / DV_D) + EPS) * gn_ref[...]
    res = o * _silu(gd)
    o_ref[...] = res[0:rows]
    sout_ref[...] = s_scr[...]


def _np_consts():
    i = np.arange(TC)
    same64 = (i[:, None] // 64) == (i[None, :] // 64)
    tril2 = (same64 & (i[:, None] >= i[None, :])).astype(np.float32)
    tril1 = (i[:, None] >= i[None, :]).astype(np.float32)
    sel = np.zeros((_D_LEVELS * TC, TC), np.float32)
    for lv in range(1, _D_LEVELS + 1):
        m = ((i >> lv) << lv) + (1 << (lv - 1)) - 1
        sel[(lv - 1) * TC + i, m] = 1.0
    bo = lambda n, w: ((np.arange(n)[:, None] // w) == (np.arange(n)[None, :] // w)).astype(np.float32)
    ia = np.arange(TC_A)
    tril2a = (((ia[:, None] // 64) == (ia[None, :] // 64)) & (ia[:, None] >= ia[None, :])).astype(np.float32)
    return dict(tril2=tril2, triu2=tril2.T.copy(), tril1=tril1, triu1=tril1.T.copy(), sel=sel,
                tril2a=tril2a, triu2a=tril2a.T.copy(),
                bo64=bo(256, 64), bo256=bo(D_INNER_C, D_INNER_C // G_C))


def _bucket_starts():
    max_exact = N_BUCKETS // 2
    d = np.arange(0, 4 * MAX_DISTANCE)
    lr = np.log(np.maximum(d, 1).astype(np.float32) / max_exact) / math.log(MAX_DISTANCE / max_exact)
    large = np.minimum(max_exact + (np.maximum(lr, 0.0) * (N_BUCKETS - max_exact)).astype(np.int32), N_BUCKETS - 1)
    bucket = np.where(d < max_exact, d, large)
    return [int(np.argmax(bucket == b)) for b in range(N_BUCKETS)]


_BUCKET_STARTS = _bucket_starts()
_SAT_DIST = _BUCKET_STARTS[N_BUCKETS - 1]


def _bias_from_dist(d, rb_ref, h):
    out = jnp.full(d.shape, rb_ref[N_BUCKETS - 1, h], F32)
    for b in range(N_BUCKETS - 2, -1, -1):
        out = jnp.where(d < _BUCKET_STARTS[b + 1], rb_ref[b, h], out)
    return out


def _sortable(x):
    b = lax.bitcast_convert_type(x + 0.0, I32)
    return jnp.where(b < 0, b ^ 0x7FFFFFFF, b)


_NEG_KEY = int(np.array([NEG], np.float32).view(np.int32)[0]) ^ 0x7FFFFFFF
_INT_MIN = -2 ** 31


def _selected_i32(k, idx, thr, xthr):
    s = jnp.where(k > thr, 1, jnp.where(k == thr, jnp.where(idx <= xthr, 1, 0), 0))
    return jnp.where(k == _NEG_KEY, 0, s)


def _selected(k, idx, thr, xthr):
    return _selected_i32(k, idx, thr, xthr) > 0


def _const_spec(shape):
    nd = len(shape)
    return pl.BlockSpec(shape, lambda *_: (0,) * nd)


def _mix_call(kernel_fn, name, P, B, T, seq_specs, batch_args, const_args, out_w, st_tail, scratch, tc=TC):
    rows = min(T, tc)
    nc = max(T // tc, 1)
    assert rows * nc == T
    in_specs = [pl.BlockSpec((rows, w), functools.partial(lambda b, c, ci: (b * nc + c, ci), ci=off // w)) for off, w in seq_specs]
    args = [P] * len(seq_specs)
    for a in batch_args:
        nd = a.ndim
        in_specs.append(pl.BlockSpec((None,) + a.shape[1:], functools.partial(lambda b, c, nd: (b,) + (0,) * (nd - 1), nd=nd)))
        args.append(a)
    for a in const_args:
        in_specs.append(_const_spec(a.shape))
        args.append(a)
    st_nd = len(st_tail)
    return pl.pallas_call(
        functools.partial(kernel_fn, t_valid=rows, rows=rows),
        grid=(B, nc),
        in_specs=in_specs,
        out_specs=[pl.BlockSpec((rows, out_w), lambda b, c: (b * nc + c, 0)),
                   pl.BlockSpec((None,) + st_tail, lambda b, c: (b,) + (0,) * st_nd)],
        out_shape=[jax.ShapeDtypeStruct((B * T, out_w), F32), jax.ShapeDtypeStruct((B,) + st_tail, F32)],
        scratch_shapes=scratch,
        compiler_params=pltpu.CompilerParams(dimension_semantics=("arbitrary", "arbitrary"), vmem_limit_bytes=VMEM_LIMIT),
        name=name,
    )(*args)


CU = 4


def _dsa_prompt_kernel(rb_ref, pq_ref, pk_ref, o_ref,
                       kvt_scr, key_scr, lg_scr, bias_scr, acc_scr, *, n_sel, idx_bits):
    j = pl.program_id(1)
    nch = j + 1

    @pl.when((pl.program_id(0) == 0) & (j == 0))
    def _():
        srow, tcol = _iota2((QB, QB), 0), _iota2((QB, QB), 1)
        for h in range(H_B):
            bias_scr[h, 0] = jnp.full((QB, QB), rb_ref[N_BUCKETS - 1, h], F32)
            bias_scr[h, 1] = _bias_from_dist(tcol - srow + QB, rb_ref, h)
            bias_scr[h, 2] = _bias_from_dist(tcol - srow, rb_ref, h)

    @pl.when(j == 0)
    def _():
        def tr(c, _):
            r = pl.multiple_of(c * QB, QB)
            kvt_scr[c] = pk_ref[pl.ds(r, QB), PK_KV:PK_KV + LANE].T[DH_B:2 * DH_B, :].astype(BF16)
            return 0
        lax.fori_loop(0, pk_ref.shape[0] // QB, tr, 0)

    lanes = lambda rows, w: jnp.concatenate([rows[h * w:(h + 1) * w, :] for h in range(H_B)], axis=1)
    w4 = lanes(pq_ref[:, PQ_SM:PQ_SM + LANE].T[SM_WI:SM_WI + H_IDX, :], 1) * (H_IDX ** -0.5 * D_IDX ** -0.5)
    qi_rhs = lanes(pq_ref[:, PQ_QI:PQ_QI + LANE].T, D_IDX).astype(BF16)
    qb_rhs = (lanes(pq_ref[:, PQ_QB:PQ_QB + W_B].T, DH_B) * (DH_B ** -0.5)).astype(BF16)
    srow, tcol = _iota2((QB, QB), 0), _iota2((QB, QB), 1)
    heads = lambda x: [x[:, h * QB:(h + 1) * QB] for h in range(H_B)]

    ngrp = (nch + (CU - 1)) // CU

    def score_body(g, _):
        for u in range(CU):
            r = pl.multiple_of((g * CU + u) * QB, QB)
            ki = pk_ref[pl.ds(r, QB), PK_SM:PK_SM + LANE][:, SM_KI:SM_KI + D_IDX].astype(BF16)
            s = jnp.maximum(jnp.dot(ki, qi_rhs, preferred_element_type=F32), 0.0) * w4
            sh = heads(s)
            sc = (sh[0] + sh[1]) + (sh[2] + sh[3])
            sc = jnp.where(srow + r <= tcol + j * QB, sc, NEG)
            key_scr[pl.ds(r, QB), :] = _sortable(sc)
        return 0

    lax.fori_loop(0, ngrp, score_body, 0)

    def count(pred):
        def body(g, acc):
            for u in range(CU):
                r = pl.multiple_of((g * CU + u) * QB, QB)
                acc = acc + pred(key_scr[pl.ds(r, QB), :], r).reshape(QB // SUB, SUB, QB).sum(axis=0)
            return acc
        return lax.fori_loop(0, ngrp, body, jnp.zeros((SUB, QB), I32)).sum(axis=0, keepdims=True)

    def count_ge_static(n_groups):
        def f(t):
            accs = [jnp.zeros((SUB, QB), I32) for _ in range(2)]
            for c in range(n_groups * CU):
                w = jnp.where(key_scr[c * QB:(c + 1) * QB, :] >= t, 1, 0)
                accs[c % 2] = accs[c % 2] + w.reshape(QB // SUB, SUB, QB).sum(axis=0)
            return (accs[0] + accs[1]).sum(axis=0, keepdims=True)
        return f

    count_variants = [count_ge_static(n) for n in range(1, key_scr.shape[0] // (CU * QB) + 1)]
    cnt_ge = lambda t: lax.switch(ngrp - 1, count_variants, t)
    c0 = cnt_ge(jnp.zeros((1, QB), I32))
    thr = jnp.where(c0 >= n_sel, 0, _INT_MIN).astype(I32)
    cthr = jnp.where(c0 >= n_sel, c0, ngrp * (CU * QB))

    def bit_body(i, carry):
        t, ct = carry
        trial = t | jnp.left_shift(jnp.int32(1), 30 - i)
        cnt = cnt_ge(trial)
        ok = cnt >= n_sel
        return jnp.where(ok, trial, t), jnp.where(ok, cnt, ct)

    thr, cthr = lax.fori_loop(0, 31, bit_body, (thr, cthr))

    @pl.when(jnp.max(cthr) > n_sel)
    def _():
        quota = n_sel - count(lambda k, r: jnp.where(k > thr, 1, 0))

        def idx_body(i, x):
            trial = x | jnp.left_shift(jnp.int32(1), idx_bits - 1 - i)
            below = count(lambda k, r: jnp.where(k == thr, jnp.where(srow + r < trial, 1, 0), 0))
            return jnp.where(below < quota, trial, x)

        xthr = lax.fori_loop(0, idx_bits, idx_body, jnp.zeros((1, QB), I32))

        def demote_body(g, _):
            for u in range(CU):
                r = pl.multiple_of((g * CU + u) * QB, QB)
                k = key_scr[pl.ds(r, QB), :]
                key_scr[pl.ds(r, QB), :] = jnp.where(k == thr, jnp.where(srow + r > xthr, k - 1, k), k)
            return 0

        lax.fori_loop(0, ngrp, demote_body, 0)

    thr_sel = jnp.maximum(thr, _NEG_KEY + 1)

    def logit_body(g, ms):
        ms = list(ms)
        for u in range(CU):
            c = g * CU + u
            r = pl.multiple_of(c * QB, QB)
            kc = pk_ref[pl.ds(r, QB), PK_KV:PK_KV + LANE][:, 0:DH_B].astype(BF16)
            l4 = heads(jnp.dot(kc, qb_rhs, preferred_element_type=F32))
            sel = key_scr[pl.ds(r, QB), :] >= thr_sel
            bidx = jnp.clip(c - j + 2, 0, 2)
            for h in range(H_B):
                l = jnp.where(sel, l4[h] + bias_scr[h, bidx], NEG)
                lg_scr[h, pl.ds(r, QB), :] = l
                ms[h] = jnp.maximum(ms[h], l.max(axis=0, keepdims=True))
        return tuple(ms)

    ms = lax.fori_loop(0, ngrp, logit_body, tuple(jnp.full((1, QB), NEG, F32) for _ in range(H_B)))
    acc_scr[...] = jnp.zeros(acc_scr.shape, F32)

    def pv_body(g, ss):
        ss = list(ss)
        acc = acc_scr[...]
        for u in range(CU):
            c = g * CU + u
            r = pl.multiple_of(c * QB, QB)
            es = [jnp.exp(lg_scr[h, pl.ds(r, QB), :] - ms[h]) for h in range(H_B)]
            acc = acc + jnp.dot(kvt_scr[c], jnp.concatenate(es, axis=1).astype(BF16), preferred_element_type=F32)
            for h in range(H_B):
                ss[h] = ss[h] + es[h].sum(axis=0, keepdims=True)
        acc_scr[...] = acc
        return tuple(ss)

    ss = lax.fori_loop(0, ngrp, pv_body, tuple(jnp.zeros((1, QB), F32) for _ in range(H_B)))
    acc = heads(acc_scr[...])
    ot = jnp.concatenate([acc[h] / ss[h] for h in range(H_B)], axis=0)
    o_ref[...] = ot.T


def _dsa_prompt(Pq, Pk, rel_bias, B, T):
    nb = T // QB
    assert nb % CU == 0
    n_sel = min(TOPK_MAX, T // 4)
    idx_bits = max(1, int(math.ceil(math.log2(T))))
    return pl.pallas_call(
        functools.partial(_dsa_prompt_kernel, n_sel=n_sel, idx_bits=idx_bits),
        grid=(B, nb),
        in_specs=[pl.BlockSpec(memory_space=pltpu.SMEM),
                  pl.BlockSpec((QB, PQ_W), lambda b, j: (b * nb + j, 0)),
                  pl.BlockSpec((T, PK_W), lambda b, j: (b, 0))],
        out_specs=pl.BlockSpec((QB, W_B), lambda b, j: (b * nb + j, 0)),
        out_shape=jax.ShapeDtypeStruct((B * T, W_B), F32),
        scratch_shapes=[pltpu.VMEM((nb, DH_B, QB), BF16), pltpu.VMEM((T, QB), I32), pltpu.VMEM((H_B, T, QB), F32),
                        pltpu.VMEM((H_B, 3, QB, QB), F32), pltpu.VMEM((DH_B, H_B * QB), F32)],
        compiler_params=pltpu.CompilerParams(dimension_semantics=("arbitrary", "arbitrary"), vmem_limit_bytes=VMEM_LIMIT),
        name="dsa_prompt",
    )(rel_bias, Pq, Pk)


PG = 16
GW = PG * PAGE_SIZE


def _stack_heads(x, w):
    return jnp.concatenate([x[:, h * w:(h + 1) * w] for h in range(x.shape[1] // w)], axis=0)


def _dsa_s_scores_kernel(pt_ref, pq_ref, *refs, ng, ds):
    page_refs, o_ref = refs[:PG], refs[PG]
    g = pl.program_id(1)
    sm = pq_ref[:, PQ_SM:PQ_SM + LANE]
    qst = _stack_heads(pq_ref[:, PQ_QI:PQ_QI + LANE], D_IDX)
    wcol = jnp.concatenate([sm[:, SM_WI + h:SM_WI + h + 1] for h in range(H_IDX)], axis=0) * (H_IDX ** -0.5 * D_IDX ** -0.5)

    def tile_scores(qk):
        s = jnp.maximum(qk, 0.0) * wcol
        out = s[0:ds]
        for h in range(1, H_IDX):
            out = out + s[h * ds:(h + 1) * ds]
        return out

    @pl.when(g < ng)
    def _():
        for i in range(PG):
            o_ref[:, i * PAGE_SIZE:(i + 1) * PAGE_SIZE] = tile_scores(_bdot(qst, page_refs[i][...]))

    @pl.when(g == ng)
    def _():
        sc = tile_scores(_bdot_nt(qst, _pad_rows(sm[:, SM_KI:SM_KI + D_IDX], PAGE_SIZE)))
        lane, row = _iota2(sc.shape, 1), _iota2(sc.shape, 0)
        o_ref[:, 0:PAGE_SIZE] = jnp.where(lane <= row, sc, NEG)
        o_ref[:, PAGE_SIZE:GW] = jnp.full((ds, GW - PAGE_SIZE), NEG, F32)


def _dsa_s_attn_kernel(pt_ref, rb_ref, sc_ref, pq_ref, pk_ref, *refs, ng, ds, n_sel, idx_bits):
    k_refs, v_refs = refs[:PG], refs[PG:2 * PG]
    o_ref, key_scr, thr_scr, bias_scr, m_scr, l_scr, acc_scr = refs[2 * PG:]
    g = pl.program_id(1)
    ntile = (ng + 1) * PG
    hr = H_B * ds

    @pl.when(g == 0)
    def _():
        for i in range(ntile):
            key_scr[:, i * LANE:(i + 1) * LANE] = _sortable(sc_ref[:, i * LANE:(i + 1) * LANE])
        lane = _iota2((ds, LANE), 1)

        def count(pred):
            accs = [jnp.zeros((ds, LANE), I32) for _ in range(4)]
            for i in range(ng * PG + 1):
                accs[i % 4] = accs[i % 4] + pred(key_scr[:, i * LANE:(i + 1) * LANE], i * LANE)
            return ((accs[0] + accs[1]) + (accs[2] + accs[3])).sum(axis=1, keepdims=True)

        cnt_ge = lambda t: count(lambda k, off: jnp.where(k >= t, 1, 0))
        c0 = cnt_ge(jnp.zeros((ds, 1), I32))
        thr = jnp.where(c0 >= n_sel, 0, _INT_MIN).astype(I32)
        cthr = jnp.where(c0 >= n_sel, c0, ng * GW + LANE)

        def bit_body(i, carry):
            t, ct = carry
            trial = t | jnp.left_shift(jnp.int32(1), 30 - i)
            cnt = cnt_ge(trial)
            ok = cnt >= n_sel
            return jnp.where(ok, trial, t), jnp.where(ok, cnt, ct)

        thr, cthr = lax.fori_loop(0, 31, bit_body, (thr, cthr))

        def tie_search():
            quota = n_sel - count(lambda k, off: jnp.where(k > thr, 1, 0))

            def idx_body(i, x):
                trial = x | jnp.left_shift(jnp.int32(1), idx_bits - 1 - i)
                below = count(lambda k, off: jnp.where(k == thr, jnp.where(lane + off < trial, 1, 0), 0))
                return jnp.where(below < quota, trial, x)

            return lax.fori_loop(0, idx_bits, idx_body, jnp.zeros((ds, 1), I32))

        xthr = lax.cond(jnp.max(cthr) > n_sel, tie_search, lambda: jnp.full((ds, 1), (1 << idx_bits) - 1, I32))
        thr_scr[0] = jnp.broadcast_to(thr, (ds, LANE))
        thr_scr[1] = jnp.broadcast_to(xthr, (ds, LANE))

        qrow = _iota2((ds, LANE), 0)
        for h in range(H_B):
            bias_scr[0, h * ds:(h + 1) * ds, :] = _bias_from_dist(PAGE_SIZE + qrow - lane, rb_ref, h)
            bias_scr[1, h * ds:(h + 1) * ds, :] = _bias_from_dist(qrow - lane, rb_ref, h)
            bias_scr[2, h * ds:(h + 1) * ds, :] = jnp.full((ds, LANE), rb_ref[N_BUCKETS - 1, h], F32)
        m_scr[...] = jnp.full(m_scr.shape, NEG, F32)
        l_scr[...] = jnp.zeros(l_scr.shape, F32)
        acc_scr[...] = jnp.zeros(acc_scr.shape, F32)

    qst = _stack_heads(pq_ref[:, PQ_QB:PQ_QB + W_B], DH_B) * (DH_B ** -0.5)
    thr, xthr = thr_scr[0], thr_scr[1]

    def select(tile):
        k = key_scr[:, pl.ds(pl.multiple_of(tile * LANE, LANE), LANE)]
        idx = _iota2((ds, LANE), 1) + tile * LANE
        sel = _selected_i32(k, idx, thr, xthr)
        return jnp.concatenate([sel] * H_B, axis=0) > 0

    def update(lg, sel, pv):
        m_old = m_scr[...]
        m_new = jnp.maximum(m_old, jnp.where(sel, lg, NEG).max(axis=1, keepdims=True))
        p = jnp.where(sel, jnp.exp(lg - m_new), 0.0)
        corr = jnp.exp(m_old - m_new)
        l_scr[...] = l_scr[...] * corr + p.sum(axis=1, keepdims=True)
        acc_scr[...] = acc_scr[...] * corr + pv(p)
        m_scr[...] = m_new

    @pl.when(g < ng)
    def _():
        lgs, sels = [], []
        for i in range(PG):
            lg = _bdot(qst, k_refs[i][...])
            if i == PG - 1:
                lg = lg + jnp.where(g == ng - 1, bias_scr[0], bias_scr[2])
            else:
                lg = lg + bias_scr[2]
            lgs.append(lg)
            sels.append(select(g * PG + i))
        vt = jnp.concatenate([v_refs[i][...] for i in range(PG)], axis=1)
        update(jnp.concatenate(lgs, axis=1), jnp.concatenate(sels, axis=1), lambda p: _bdot_nt(p, vt))

    @pl.when(g == ng)
    def _():
        kvn = _pad_rows(pk_ref[:, PK_KV:PK_KV + LANE], PAGE_SIZE)
        lg = _bdot_nt(qst, kvn[:, 0:DH_B]) + bias_scr[1]
        update(lg, select(ng * PG), lambda p: _bdot(p, kvn[:, DH_B:2 * DH_B]))
        o = acc_scr[...] / l_scr[...]
        o_ref[...] = jnp.concatenate([o[h * ds:(h + 1) * ds] for h in range(H_B)], axis=1)


def _dsa_sample(Pq, Pk, rel_bias, cache_k, cache_v, cache_kidx, page_table, layer, B, DS):
    n_pages = page_table.shape[1]
    past = n_pages * PAGE_SIZE
    assert n_pages % PG == 0
    ng = n_pages // PG
    n_sel = min(TOPK_MAX, (past + DS) // 4)
    idx_bits = int(math.ceil(math.log2(past + LANE)))
    s_pad = (ng + 1) * GW

    def page_spec(width, i):
        return pl.BlockSpec((None, None, width, PAGE_SIZE),
                            lambda b, g, pt: (layer, pt[b, jnp.minimum(g, ng - 1) * PG + i], 0, 0))

    row = lambda b, g, pt: (b, 0)
    cp = pltpu.CompilerParams(dimension_semantics=("arbitrary", "arbitrary"), vmem_limit_bytes=VMEM_LIMIT)
    scores = pl.pallas_call(
        functools.partial(_dsa_s_scores_kernel, ng=ng, ds=DS),
        grid_spec=pltpu.PrefetchScalarGridSpec(
            num_scalar_prefetch=1, grid=(B, ng + 1),
            in_specs=[pl.BlockSpec((DS, PQ_W), row)] + [page_spec(D_IDX, i) for i in range(PG)],
            out_specs=pl.BlockSpec((None, DS, GW), lambda b, g, pt: (b, 0, g))),
        out_shape=jax.ShapeDtypeStruct((B, DS, s_pad), F32),
        compiler_params=cp, name="dsa_sample_scores",
    )(page_table, Pq, *([cache_kidx] * PG))

    return pl.pallas_call(
        functools.partial(_dsa_s_attn_kernel, ng=ng, ds=DS, n_sel=n_sel, idx_bits=idx_bits),
        grid_spec=pltpu.PrefetchScalarGridSpec(
            num_scalar_prefetch=1, grid=(B, ng + 1),
            in_specs=[pl.BlockSpec(memory_space=pltpu.SMEM),
                      pl.BlockSpec((None, DS, s_pad), lambda b, g, pt: (b, 0, 0)),
                      pl.BlockSpec((DS, PQ_W), row), pl.BlockSpec((DS, PK_W), row)]
                     + [page_spec(DH_B, i) for i in range(PG)] + [page_spec(DH_B, i) for i in range(PG)],
            out_specs=pl.BlockSpec((DS, W_B), lambda b, g, pt: (b, 0)),
            scratch_shapes=[pltpu.VMEM((DS, s_pad), I32), pltpu.VMEM((2, DS, LANE), I32), pltpu.VMEM((3, H_B * DS, LANE), F32),
                            pltpu.VMEM((H_B * DS, 1), F32), pltpu.VMEM((H_B * DS, 1), F32), pltpu.VMEM((H_B * DS, DH_B), F32)]),
        out_shape=jax.ShapeDtypeStruct((B * DS, W_B), F32),
        compiler_params=cp, name="dsa_sample_attn",
    )(page_table, rel_bias, scores, Pq, Pk, *([cache_k] * PG), *([cache_v] * PG))


def _merge_kernel(x_ref, oa_ref, ob_ref, oc_ref, od_ref, gate_ref, wb_ref, wo_ref, o_ref):
    m = None
    r0 = 0
    for br, ref in enumerate((oa_ref, ob_ref, oc_ref, od_ref)):
        w = ref.shape[1]
        t = _sigmoid(gate_ref[:, br * D_MODEL:(br + 1) * D_MODEL]) * jnp.dot(ref[...].astype(BF16), wb_ref[r0:r0 + w, :], preferred_element_type=F32)
        m = t if m is None else m + t
        r0 += w
    o_ref[...] = x_ref[...] + jnp.dot(m.astype(BF16), wo_ref[...], preferred_element_type=F32)


def _merge(x2d, oa, ob, oc, od, P, wb, wo):
    n = x2d.shape[0]
    tm = min(n, 256)
    row = lambda i: (i, 0)
    return pl.pallas_call(
        _merge_kernel,
        grid=(n // tm,),
        in_specs=[pl.BlockSpec((tm, D_MODEL), row), pl.BlockSpec((tm, W_A), row), pl.BlockSpec((tm, W_B), row),
                  pl.BlockSpec((tm, W_C), row), pl.BlockSpec((tm, W_D), row),
                  pl.BlockSpec((tm, N_BRANCH * D_MODEL), lambda i: (i, OFF_GATE // (N_BRANCH * D_MODEL))),
                  _const_spec((MIX_W, D_MODEL)), _const_spec((D_MODEL, D_MODEL))],
        out_specs=pl.BlockSpec((tm, D_MODEL), row),
        out_shape=jax.ShapeDtypeStruct((n, D_MODEL), F32),
        compiler_params=pltpu.CompilerParams(dimension_semantics=("arbitrary",), vmem_limit_bytes=VMEM_LIMIT),
        name="merge",
    )(x2d, oa, ob, oc, od, P, wb, wo)


def _mlp_kernel(x_ref, g_ref, wu_ref, wd_ref, gf_ref, *refs, final):
    if final:
        o_ref, y_ref, h_scr, acc_scr = refs
    else:
        o_ref, h_scr, acc_scr = refs
    f = pl.program_id(1)

    @pl.when(f == 0)
    def _():
        x = x_ref[...]
        r = lax.rsqrt(jnp.mean(x * x, axis=-1, keepdims=True) + EPS)
        h_scr[...] = ((x * r) * g_ref[...]).astype(BF16)
        acc_scr[...] = jnp.zeros(acc_scr.shape, F32)

    a = jnp.maximum(jnp.dot(h_scr[...], wu_ref[...], preferred_element_type=F32), 0.0)
    acc_scr[...] += jnp.dot((a * a).astype(BF16), wd_ref[...], preferred_element_type=F32)

    @pl.when(f == pl.num_programs(1) - 1)
    def _():
        out = x_ref[...] + acc_scr[...]
        o_ref[...] = out
        if final:
            r = lax.rsqrt(jnp.mean(out * out, axis=-1, keepdims=True) + EPS)
            y_ref[...] = (out * r) * gf_ref[...]


def _mlp(x2d, g, wu, wd, gf, final):
    n = x2d.shape[0]
    tm = min(n, 512)
    tf = 1024
    row = lambda i, f: (i, 0)
    out_spec = pl.BlockSpec((tm, D_MODEL), row)
    shp = jax.ShapeDtypeStruct((n, D_MODEL), F32)
    return pl.pallas_call(
        functools.partial(_mlp_kernel, final=final),
        grid=(n // tm, D_FF // tf),
        in_specs=[pl.BlockSpec((tm, D_MODEL), row), _const_spec((1, D_MODEL)),
                  pl.BlockSpec((D_MODEL, tf), lambda i, f: (0, f)), pl.BlockSpec((tf, D_MODEL), lambda i, f: (f, 0)),
                  _const_spec((1, D_MODEL))],
        out_specs=[out_spec, out_spec] if final else [out_spec],
        out_shape=[shp, shp] if final else [shp],
        scratch_shapes=[pltpu.VMEM((tm, D_MODEL), BF16), pltpu.VMEM((tm, D_MODEL), F32)],
        compiler_params=pltpu.CompilerParams(dimension_semantics=("arbitrary", "arbitrary"), vmem_limit_bytes=VMEM_LIMIT),
        name="mlp_final" if final else "mlp",
    )(x2d, g.reshape(1, D_MODEL), wu, wd, gf.reshape(1, D_MODEL))


def _layer(x, conv_state, s_delta, s_ssm, s_hgrn, lw, consts, sample_ctx, final):
    B, T, _ = x.shape
    x2 = x.reshape(B * T, D_MODEL)
    P, Pq, Pk = _in_proj(x2, lw['norm_mix'], lw['w_in'])

    def conv0(lo, hi):
        if conv_state is None:
            return jnp.zeros((B, SUB, hi - lo), F32)
        return jnp.concatenate([jnp.zeros((B, SUB - (CONV_W - 1), hi - lo), F32), conv_state[:, :, lo:hi]], axis=1)

    zeros_state = lambda n: jnp.zeros((B, n, 64, 64), F32)
    c = consts
    half = CONV_CH // 2
    assert CONV_A == half and OFF_PC % half == 0
    oa, s_a = _mix_call(
        _mix_a_kernel, "mix_a", P, B, T,
        [(OFF_PC, half), (OFF_ZA, W_A), (OFF_SM, LANE)],
        [conv0(0, CONV_A), zeros_state(H_A) if s_delta is None else s_delta],
        [lw['conv_w'][:, 0:CONV_A], lw['conv_b'][None, 0:CONV_A], lw['hp'], lw['hpt'], lw['gn_a'], c['tril2a'], c['triu2a'], c['bo64']],
        W_A, (H_A, DK_A, DV_A),
        [pltpu.VMEM((TC_A + SUB, CONV_A), F32), pltpu.VMEM((H_A, DK_A, DV_A), F32), pltpu.VMEM((TC_A, W_A), F32)],
        tc=TC_A)
    oc, s_c = _mix_call(
        _mix_c_kernel, "mix_c", P, B, T,
        [(OFF_PC + half, half), (OFF_ZC, W_C), (OFF_SM, LANE)],
        [conv0(CONV_A, CONV_CH), zeros_state(H_C) if s_ssm is None else s_ssm],
        [lw['conv_w'][:, CONV_A:], lw['conv_b'][None, CONV_A:], lw['hp'], lw['hpt'], lw['gn_c'], c['tril1'], c['triu1'], c['bo256']],
        W_C, (H_C, N_C, P_C),
        [pltpu.VMEM((TC + SUB, CONV_C), F32), pltpu.VMEM((H_C, N_C, P_C), F32), pltpu.VMEM((TC, W_C), F32)])
    sbd0 = jnp.zeros((B, H_D, DK_D, H_D, DV_D), F32)
    if s_hgrn is not None:
        for h in range(H_D):
            sbd0 = sbd0.at[:, h, :, h, :].set(s_hgrn[:, h])
    sbd0 = sbd0.reshape(B, H_D * DK_D, W_D)
    od, sbd = _mix_call(
        _mix_d_kernel, "mix_d", P, B, T,
        [(OFF_D, 4 * W_D)],
        [sbd0],
        [lw['lb'], lw['gn_d'], c['tril2'], c['sel'], c['bo64']],
        W_D, (H_D * DK_D, W_D),
        [pltpu.VMEM((H_D * DK_D, W_D), F32), pltpu.VMEM((TC, W_D), F32)])
    sbd5 = sbd.reshape(B, H_D, DK_D, H_D, DV_D)
    s_d = jnp.stack([sbd5[:, h, :, h, :] for h in range(H_D)], axis=1)
    if sample_ctx is None:
        ob = _dsa_prompt(Pq, Pk, lw['rel_bias'], B, T)
    else:
        ob = _dsa_sample(Pq, Pk, lw['rel_bias'], sample_ctx['cache_k'], sample_ctx['cache_v'], sample_ctx['cache_kidx'],
                         sample_ctx['page_table'], sample_ctx['layer'], B, T)
    x1 = _merge(x2, oa, ob, oc, od, P, lw['w_branch'], lw['w_out'])
    outs = _mlp(x1, lw['norm_mlp'], lw['w_up'], lw['w_down'], lw['norm_final'], final)
    xo = outs[0].reshape(B, T, D_MODEL)
    y = outs[1].reshape(B, T, D_MODEL) if final else None
    P3 = P.reshape(B, T, P_PAD)
    Pk3 = Pk.reshape(B, T, PK_W)
    states = (Pk3[:, :, PK_KV:PK_KV + DH_B], Pk3[:, :, PK_KV + DH_B:PK_KV + 2 * DH_B], Pk3[:, :, PK_SM + SM_KI:PK_SM + SM_KI + D_IDX],
              P3[:, T - (CONV_W - 1):, OFF_PC:OFF_PC + CONV_CH], s_a, s_c, s_d)
    return xo, y, states


def kernel(x_prompt, x_sample, cache_k, cache_v, cache_kidx, state_conv, state_delta, state_ssm, state_hgrn, page_table, norm_mix, w_in, conv_w, conv_b, a_log_a, dt_bias_a, gnorm_a, rel_bias, a_log_c, dt_bias_c, d_skip_c, gnorm_c, hgrn_gamma, gnorm_d, w_branch, w_out, norm_mlp, w_up, w_down, norm_final):
    depth = w_in.shape[0]
    npc = _np_consts()
    consts = {k: jnp.asarray(v, BF16) for k, v in npc.items()}
    pg = jax.nn.softmax(hgrn_gamma.astype(F32), axis=0)
    lower_bounds = jnp.cumsum(pg, axis=0) - pg[0]
    cache_kt, cache_vt, cache_kit = (jnp.swapaxes(c, 2, 3) for c in (cache_k, cache_v, cache_kidx))
    yp, ys = x_prompt, x_sample
    new_p, new_s = [], []
    for l in range(depth):
        hp = jnp.zeros((SUB, LANE), F32)
        hp = hp.at[0, 0:H_A].set(a_log_a[l]).at[1, 0:H_A].set(dt_bias_a[l])
        hp = hp.at[2, 0:H_C].set(a_log_c[l]).at[3, 0:H_C].set(dt_bias_c[l]).at[4, 0:H_C].set(d_skip_c[l])
        lw = dict(norm_mix=norm_mix[l], w_in=_permute_w_in(w_in[l]), conv_w=conv_w[l], conv_b=conv_b[l],
                  hp=hp, hpt=hp.T, gn_a=jnp.tile(gnorm_a[l], H_A)[None, :], gn_c=gnorm_c[l][None, :],
                  gn_d=jnp.tile(gnorm_d[l], H_D)[None, :], lb=lower_bounds[l][None, :], rel_bias=rel_bias,
                  w_branch=w_branch[l].astype(BF16), w_out=w_out[l].astype(BF16), norm_mlp=norm_mlp[l],
                  w_up=w_up[l].astype(BF16), w_down=w_down[l].astype(BF16), norm_final=norm_final)
        final = l == depth - 1
        yp, yp_n, st_p = _layer(yp, None, None, None, None, lw, consts, None, final)
        ctx = dict(cache_k=cache_kt, cache_v=cache_vt, cache_kidx=cache_kit, page_table=page_table, layer=l)
        ys, ys_n, st_s = _layer(ys, state_conv[l], state_delta[l], state_ssm[l], state_hgrn[l], lw, consts, ctx, final)
        new_p.append(st_p)
        new_s.append(st_s)
    stack = lambda per_layer: [jnp.stack(items) for items in zip(*per_layer)]
    return (yp_n, ys_n, *stack(new_p), *stack(new_s))
```

```python
import functools
import math

import numpy as np
import jax
import jax.numpy as jnp
from jax import lax
from jax.experimental import pallas as pl
from jax.experimental.pallas import tpu as pltpu

F32 = jnp.float32
BF16 = jnp.bfloat16
I32 = jnp.int32

D_MODEL = 1024
PAGE_SIZE = 128
EPS = 1e-6
NEG = -1e30
TINY = 1e-30
CONV_W = 4
D_FF = 4 * D_MODEL
N_BRANCH = 4
H_A, DK_A, DV_A = 4, 64, 64
H_B, DH_B = 4, 64
H_IDX, D_IDX = 4, 32
TOPK_MAX = 256
N_BUCKETS = 32
MAX_DISTANCE = 128
H_C, P_C, N_C, G_C = 8, 64, 64, 2
H_D, DK_D, DV_D = 4, 64, 64
QK_A = H_A * DK_A
CONV_A = 2 * QK_A + H_A * DV_A
D_INNER_C = H_C * P_C
CONV_C = D_INNER_C + 2 * G_C * N_C
CONV_CH = CONV_A + CONV_C
W_A, W_B, W_C, W_D = H_A * DV_A, H_B * DH_B, D_INNER_C, H_D * DV_D
MIX_W = W_A + W_B + W_C + W_D
IN_WIDTHS = (CONV_CH, W_A, H_A, H_A, W_B, DH_B, DH_B, H_IDX * D_IDX, D_IDX, H_IDX, D_INNER_C, H_C,
             H_D * DK_D, H_D * DK_D, W_D, W_D, N_BRANCH * D_MODEL)
P_IN = sum(IN_WIDTHS)

LANE = 128
SUB = 8
TC = 128
TC_A = 256
QB = 128
VMEM_LIMIT = 48 * 1024 * 1024

OFF_GATE, OFF_D, OFF_ZC, OFF_ZA, OFF_QB, OFF_PC, OFF_KV, OFF_QI, OFF_SM = 0, 4096, 5120, 5632, 5888, 6144, 7680, 7808, 7936
P_PAD = 8064
SM_KI, SM_WI, SM_AA, SM_BA, SM_DT = 0, 32, 36, 40, 44


def _src_offsets():
    offs, o = [], 0
    for w in IN_WIDTHS:
        offs.append(o)
        o += w
    return offs


def _permute_w_in(w):
    (o_pc, o_za, o_aa, o_ba, o_qb, o_kb, o_vb, o_qi, o_ki, o_wi, o_zc, o_dt, o_qd, o_fd, o_id, o_gd, o_gate) = _src_offsets()
    wt = jnp.swapaxes(w, 0, 1)
    seg = lambda o, n: wt[o:o + n, :]
    small = jnp.concatenate([seg(o_ki, D_IDX), seg(o_wi, H_IDX), seg(o_aa, H_A), seg(o_ba, H_A), seg(o_dt, H_C),
                             jnp.zeros((LANE - (D_IDX + H_IDX + 2 * H_A + H_C), w.shape[0]), w.dtype)], axis=0)
    out = jnp.concatenate([
        seg(o_gate, N_BRANCH * D_MODEL),
        seg(o_qd, 4 * W_D),
        seg(o_zc, D_INNER_C), seg(o_za, W_A), seg(o_qb, W_B), seg(o_pc, CONV_CH),
        seg(o_kb, 2 * DH_B),
        seg(o_qi, H_IDX * D_IDX), small], axis=0)
    assert out.shape[0] == P_PAD
    return out.astype(BF16)


def _bdot(a, b):
    return jnp.dot(a.astype(BF16), b.astype(BF16), preferred_element_type=F32)


def _bdot_nt(a, b):
    return lax.dot_general(a.astype(BF16), b.astype(BF16), (((1,), (1,)), ((), ())), preferred_element_type=F32)


def _split2(a):
    hi = a.astype(BF16)
    lo = (a - hi.astype(F32)).astype(BF16)
    return hi, lo


def _split3(a):
    hi = a.astype(BF16)
    r = a - hi.astype(F32)
    mid = r.astype(BF16)
    lo = (r - mid.astype(F32)).astype(BF16)
    return hi, mid, lo


def _dot01_left(m01, x):
    hi, mid, lo = _split3(x)
    d = lambda p: jnp.dot(m01, p, preferred_element_type=F32)
    return d(hi) + (d(mid) + d(lo))


def _dot01_right(x, m01):
    hi, mid, lo = _split3(x)
    d = lambda p: jnp.dot(p, m01, preferred_element_type=F32)
    return d(hi) + (d(mid) + d(lo))


def _xdot(a, b):
    ah, al = _split2(a)
    bh, bl = _split2(b)
    d = lambda p, q: jnp.dot(p, q, preferred_element_type=F32)
    return d(ah, bh) + (d(ah, bl) + d(al, bh))


def _sigmoid(x):
    return 1.0 / (1.0 + jnp.exp(-x))


def _silu(x):
    return x * _sigmoid(x)


def _softplus(x):
    return jnp.maximum(x, 0.0) + jnp.log(1.0 + jnp.exp(-jnp.abs(x)))


def _iota2(shape, dim):
    return lax.broadcasted_iota(I32, shape, dim)


def _pad_rows(x, rows):
    if x.shape[0] == rows:
        return x
    return jnp.concatenate([x, jnp.zeros((rows - x.shape[0],) + x.shape[1:], x.dtype)], axis=0)


def _tri_inv_many(Ls, top):
    n = Ls[0].shape[0]
    ii, jj = _iota2((n, n), 0), _iota2((n, n), 1)
    xor = ii ^ jj
    eye = jnp.where(ii == jj, 1.0, 0.0)
    Ns = [jnp.where((xor >> 3) == 0, -L, 0.0) for L in Ls]
    Xs = [eye + N for N in Ns]
    N2s = [_bdot(N, N) for N in Ns]
    Xs = [X + _bdot(X, N2) for X, N2 in zip(Xs, N2s)]
    N4s = [_bdot(N2, N2) for N2 in N2s]
    Xs = [X + _bdot(X, N4) for X, N4 in zip(Xs, N4s)]
    sh = 4
    while (1 << sh) <= top:
        XBs = [_bdot(X, jnp.where((xor >> (sh - 1)) == 1, L, 0.0)) for X, L in zip(Xs, Ls)]
        Xs = [X - _bdot(XB, X) for X, XB in zip(Xs, XBs)]
        sh += 1
    return Xs


def _conv_silu(pc, ext_scr, cw_ref, cb_ref):
    tc = pc.shape[0]
    ext_scr[SUB:SUB + tc, :] = pc
    y = cb_ref[...]
    for j in range(CONV_W):
        y = y + ext_scr[SUB - (CONV_W - 1) + j:SUB - (CONV_W - 1) + j + tc, :] * cw_ref[j:j + 1, :]
    ext_scr[0:SUB, :] = ext_scr[tc:tc + SUB, :]
    return _silu(y)


TN_IN = 1152
PQ_W = W_B + 2 * LANE
PK_W = 2 * LANE
PQ_QB, PQ_QI, PQ_SM = 0, W_B, W_B + LANE
PK_KV, PK_SM = 0, LANE
_COMPACT_COPIES = (("pq", PQ_QB, OFF_QB, W_B), ("pq", PQ_QI, OFF_QI, LANE), ("pq", PQ_SM, OFF_SM, LANE),
                   ("pk", PK_KV, OFF_KV, LANE), ("pk", PK_SM, OFF_SM, LANE))


def _in_proj_kernel(x_ref, g_ref, w_ref, o_ref, pq_ref, pk_ref, h_scr):
    j = pl.program_id(1)

    @pl.when(j == 0)
    def _():
        x = x_ref[...]
        r = lax.rsqrt(jnp.mean(x * x, axis=-1, keepdims=True) + EPS)
        h_scr[...] = ((x * r) * g_ref[...]).astype(BF16)

    o_ref[...] = lax.dot_general(h_scr[...], w_ref[...], (((1,), (1,)), ((), ())), preferred_element_type=F32)

    def copy(dst_ref, dst, src, width):
        tile, lo = divmod(src, TN_IN)
        assert lo + width <= TN_IN

        @pl.when(j == tile)
        def _():
            dst_ref[:, dst:dst + width] = o_ref[:, lo:lo + width]

    for name, dst, src, width in _COMPACT_COPIES:
        copy(pq_ref if name == "pq" else pk_ref, dst, src, width)


def _in_proj(x2d, g, w_perm_t):
    n = x2d.shape[0]
    tm = min(n, 1024)
    tn = TN_IN
    return pl.pallas_call(
        _in_proj_kernel,
        grid=(n // tm, P_PAD // tn),
        in_specs=[pl.BlockSpec((tm, D_MODEL), lambda i, j: (i, 0)),
                  pl.BlockSpec((1, D_MODEL), lambda i, j: (0, 0)),
                  pl.BlockSpec((tn, D_MODEL), lambda i, j: (j, 0))],
        out_specs=[pl.BlockSpec((tm, tn), lambda i, j: (i, j)),
                   pl.BlockSpec((tm, PQ_W), lambda i, j: (i, 0)), pl.BlockSpec((tm, PK_W), lambda i, j: (i, 0))],
        out_shape=[jax.ShapeDtypeStruct((n, P_PAD), F32), jax.ShapeDtypeStruct((n, PQ_W), F32),
                   jax.ShapeDtypeStruct((n, PK_W), F32)],
        scratch_shapes=[pltpu.VMEM((tm, D_MODEL), BF16)],
        compiler_params=pltpu.CompilerParams(dimension_semantics=("arbitrary", "arbitrary"), vmem_limit_bytes=VMEM_LIMIT),
        name="in_proj",
    )(x2d, g.reshape(1, D_MODEL), w_perm_t)


def _mix_a_kernel(pc_ref, za_ref, sm_ref, conv0_ref, s0_ref, cw_ref, cb_ref, hp_ref, hpt_ref, gn_ref,
                  tril_ref, triu_ref, bo_ref, o_ref, sout_ref, ext_scr, s_scr, ob_scr, *, t_valid, rows):
    CA = 64
    tc = ob_scr.shape[0]

    @pl.when(pl.program_id(1) == 0)
    def _():
        ext_scr[0:SUB, :] = conv0_ref[...]
        s_scr[...] = s0_ref[...]

    u = _conv_silu(_pad_rows(pc_ref[...], tc), ext_scr, cw_ref, cb_ref)
    bo = bo_ref[...]
    q_raw, k_raw, va = u[:, 0:QK_A], u[:, QK_A:2 * QK_A], u[:, 2 * QK_A:CONV_A]
    qq = q_raw * lax.rsqrt(_dot01_right(q_raw * q_raw, bo) + EPS) * (DK_A ** -0.5)
    ka = k_raw * lax.rsqrt(_dot01_right(k_raw * k_raw, bo) + EPS)
    kat = ka.T

    sm = _pad_rows(sm_ref[...], tc)
    smt = sm.T
    hp, hpt = hp_ref[...], hpt_ref[...]
    g_col = -jnp.exp(hp[0:1, 0:H_A]) * _softplus(sm[:, SM_AA:SM_AA + H_A] + hp[1:2, 0:H_A])
    beta = _sigmoid(sm[:, SM_BA:SM_BA + H_A])
    g_row = -jnp.exp(hpt[0:H_A, 0:1]) * _softplus(smt[SM_AA:SM_AA + H_A, :] + hpt[0:H_A, 1:2])
    if t_valid < tc:
        g_col = jnp.where(_iota2(g_col.shape, 0) < t_valid, g_col, 0.0)
        beta = jnp.where(_iota2(beta.shape, 0) < t_valid, beta, 0.0)
        g_row = jnp.where(_iota2(g_row.shape, 1) < t_valid, g_row, 0.0)
    gcum_col = _dot01_left(tril_ref[...], g_col)
    gcum_row = _dot01_right(g_row, triu_ref[...])

    HS = H_A * CA
    ii, jj = _iota2((HS, HS), 0), _iota2((HS, HS), 1)
    same = (ii >> 6) == (jj >> 6)
    incl, strict = same & (ii >= jj), same & (ii > jj)
    nsc = -(-t_valid // CA)
    pre = []
    for sc in range(nsc):
        r0 = sc * CA
        stack = lambda x, w: jnp.concatenate([x[r0:r0 + CA, h * w:(h + 1) * w] for h in range(H_A)], axis=0)
        ks, qs, vs = stack(ka, DK_A), stack(qq, DK_A), stack(va, DV_A)
        bcol, gc = stack(beta, 1), stack(gcum_col, 1)
        gr = jnp.concatenate([gcum_row[h:h + 1, r0:r0 + CA] for h in range(H_A)], axis=1)
        e = jnp.exp(jnp.where(incl, gc - gr, 0.0))
        kb = ks * bcol
        eg = jnp.exp(gc)
        pre.append(dict(r0=r0, gc=gc, qe=qs * eg, low=jnp.where(strict, _bdot_nt(kb, ks) * e, 0.0),
                        rhs=jnp.concatenate([vs * bcol, kb * eg], axis=1),
                        attn=jnp.where(incl, _bdot_nt(qs, ks) * e, 0.0)))
    tinvs = _tri_inv_many([p['low'] for p in pre], CA)
    sols = [_xdot(t, p['rhs']) for t, p in zip(tinvs, pre)]
    hs = lambda x, h: x[h * CA:(h + 1) * CA]
    for p, sol in zip(pre, sols):
        r0, gc = p['r0'], p['gc']
        states = [s_scr[h] for h in range(H_A)]
        v_new = jnp.concatenate([hs(sol, h)[:, 0:DV_A] - _bdot(hs(sol, h)[:, DV_A:], states[h]) for h in range(H_A)], axis=0)
        o = jnp.concatenate([_bdot(hs(p['qe'], h), states[h]) for h in range(H_A)], axis=0) + _bdot(p['attn'], v_new)
        for h in range(H_A):
            gch = hs(gc, h)
            gl = gch[CA - 1:CA, :]
            s_scr[h] = states[h] * jnp.exp(gl) + _bdot(kat[h * DK_A:(h + 1) * DK_A, r0:r0 + CA], hs(v_new, h) * jnp.exp(gl - gch))
            ob_scr[r0:r0 + CA, h * DV_A:(h + 1) * DV_A] = hs(o, h)
    if nsc * CA < tc:
        ob_scr[nsc * CA:tc, :] = jnp.zeros((tc - nsc * CA, W_A), F32)

    o = ob_scr[...]
    o = o * lax.rsqrt(_dot01_right(o * o, bo) * (1.0 / DV_A) + EPS) * gn_ref[...]
    res = o * _silu(_pad_rows(za_ref[...], tc))
    o_ref[...] = res[0:rows]
    sout_ref[...] = s_scr[...]


def _mix_c_kernel(pc_ref, zc_ref, sm_ref, conv0_ref, s0_ref, cw_ref, cb_ref, hp_ref, hpt_ref, gn_ref,
                  tril_ref, triu_ref, bo_ref, o_ref, sout_ref, ext_scr, s_scr, yb_scr, *, t_valid, rows):
    @pl.when(pl.program_id(1) == 0)
    def _():
        ext_scr[0:SUB, :] = conv0_ref[...]
        s_scr[...] = s0_ref[...]

    u = _conv_silu(_pad_rows(pc_ref[...], TC), ext_scr, cw_ref, cb_ref)
    xc = u[:, 0:D_INNER_C]
    bcm = u[:, D_INNER_C:D_INNER_C + G_C * N_C]
    ccm = u[:, D_INNER_C + G_C * N_C:CONV_C]
    bct = bcm.T
    zc = _pad_rows(zc_ref[...], TC)

    sm = _pad_rows(sm_ref[...], TC)
    smt = sm.T
    hp, hpt = hp_ref[...], hpt_ref[...]
    dt_col = _softplus(sm[:, SM_DT:SM_DT + H_C] + hp[3:4, 0:H_C])
    dt_row = _softplus(smt[SM_DT:SM_DT + H_C, :] + hpt[0:H_C, 3:4])
    if t_valid < TC:
        dt_col = jnp.where(_iota2(dt_col.shape, 0) < t_valid, dt_col, 0.0)
        dt_row = jnp.where(_iota2(dt_row.shape, 1) < t_valid, dt_row, 0.0)
    gcum_col = _dot01_left(tril_ref[...], -jnp.exp(hp[2:3, 0:H_C]) * dt_col)
    gcum_row = _dot01_right(-jnp.exp(hpt[0:H_C, 2:3]) * dt_row, triu_ref[...])

    incl = _iota2((TC, TC), 0) >= _iota2((TC, TC), 1)
    hg = H_C // G_C
    for g in range(G_C):
        cc_g = ccm[:, g * N_C:(g + 1) * N_C]
        gram = _bdot_nt(cc_g, bcm[:, g * N_C:(g + 1) * N_C])
        for hh in range(hg):
            h = g * hg + hh
            c0 = h * P_C
            gc = gcum_col[:, h:h + 1]
            gr = gcum_row[h:h + 1, :]
            dec = jnp.where(incl, jnp.exp(jnp.where(incl, gc - gr, 0.0)), 0.0)
            attn = gram * dt_row[h:h + 1, :] * dec
            xh = xc[:, c0:c0 + P_C]
            S = s_scr[h]
            o = jnp.exp(gc) * _bdot(cc_g, S) + _bdot(attn, xh)
            gl = gc[TC - 1:TC, :]
            s_scr[h] = S * jnp.exp(gl) + _bdot(bct[g * N_C:(g + 1) * N_C, :], xh * (dt_col[:, h:h + 1] * jnp.exp(gl - gc)))
            yb_scr[:, c0:c0 + P_C] = (o + hp[4:5, h:h + 1] * xh) * _silu(zc[:, c0:c0 + P_C])

    y = yb_scr[...]
    y = y * lax.rsqrt(_dot01_right(y * y, bo_ref[...]) * (1.0 / (D_INNER_C // G_C)) + EPS) * gn_ref[...]
    o_ref[...] = y[0:rows]
    sout_ref[...] = s_scr[...]


_D_LEVELS = 6


def _mix_d_kernel(dg_ref, s0_ref, lb_ref, gn_ref, tril_ref, sel_ref, bo_ref, o_ref, sout_ref, s_scr, ob_scr, *, t_valid, rows):
    CD = 64
    W = H_D * DK_D

    @pl.when(pl.program_id(1) == 0)
    def _():
        s_scr[...] = s0_ref[...]

    dg = _pad_rows(dg_ref[...], TC)
    q = _silu(dg[:, 0:W])
    fr = dg[:, W:2 * W]
    v = dg[:, 2 * W:3 * W]
    gd = dg[:, 3 * W:4 * W]
    lb = lb_ref[...]
    logf = jnp.log(jnp.maximum(lb + (1.0 - lb) * _sigmoid(fr), TINY))
    kd = (1.0 - lb) * _sigmoid(-fr)
    if t_valid < TC:
        ok = _iota2((TC, W), 0) < t_valid
        logf = jnp.where(ok, logf, 0.0)
        kd = jnp.where(ok, kd, 0.0)
    gcum = _dot01_left(tril_ref[...], logf)
    gm = _dot01_left(sel_ref[...], gcum)
    qdec = q * jnp.exp(gcum)
    nsc = -(-t_valid // CD)
    gl_rows = jnp.concatenate([jnp.broadcast_to(gcum[(s + 1) * CD - 1:(s + 1) * CD, :], (CD, W)) for s in range(TC // CD)], axis=0)
    kdt = (kd * jnp.exp(gl_rows - gcum)).T
    gct = gcum.T

    HS = H_D * CD
    head_lanes = (_iota2((HS, W), 0) >> 6) == (_iota2((HS, W), 1) >> 6)
    stack = lambda x: jnp.where(head_lanes, jnp.concatenate([x] * H_D, axis=0), 0.0)
    ci, jj = _iota2((HS, CD), 0) & (CD - 1), _iota2((HS, CD), 1)
    later = jnp.where(ci > jj, ci ^ jj, 0)
    lhs, rhs, msk = [q], [kd], [ci == jj]
    for lv in range(1, _D_LEVELS + 1):
        gml = gm[(lv - 1) * TC:lv * TC, :]
        lhs.append(q * jnp.exp(jnp.minimum(gcum - gml, 0.0)))
        rhs.append(kd * jnp.exp(jnp.minimum(gml - gcum, 0.0)))
        msk.append((later >> (lv - 1)) == 1)

    lane_head = _iota2((CD, W), 1) >> 6
    intra = []
    for sc in range(nsc):
        r0 = sc * CD
        attn = jnp.zeros((HS, CD), F32)
        for a, b, m in zip(lhs, rhs, msk):
            attn = attn + jnp.where(m, _bdot_nt(stack(a[r0:r0 + CD]), b[r0:r0 + CD]), 0.0)
        full = _bdot(attn, v[r0:r0 + CD])
        o = jnp.zeros((CD, W), F32)
        for h in range(H_D):
            o = o + jnp.where(lane_head == h, full[h * CD:(h + 1) * CD], 0.0)
        intra.append(o)
    for sc in range(nsc):
        r0 = sc * CD
        sbd = s_scr[...]
        ob_scr[r0:r0 + CD, :] = _bdot(qdec[r0:r0 + CD], sbd) + intra[sc]
        decay = jnp.exp(gct[:, r0 + CD - 1:r0 + CD])
        s_scr[...] = jnp.where(head_lanes, sbd * decay + _bdot(kdt[:, r0:r0 + CD], v[r0:r0 + CD]), 0.0)
    if nsc < TC // CD:
        ob_scr[nsc * CD:TC, :] = jnp.zeros((TC - nsc * CD, W), F32)

    o = ob_scr[...]
    o = o * lax.rsqrt(_dot01_right(o * o, bo_ref[...]) * (1.0 / DV_D) + EPS) * gn_ref[...]
    res = o * _silu(gd)
    o_ref[...] = res[0:rows]
    sout_ref[...] = s_scr[...]


def _np_consts():
    i = np.arange(TC)
    same64 = (i[:, None] // 64) == (i[None, :] // 64)
    tril2 = (same64 & (i[:, None] >= i[None, :])).astype(np.float32)
    tril1 = (i[:, None] >= i[None, :]).astype(np.float32)
    sel = np.zeros((_D_LEVELS * TC, TC), np.float32)
    for lv in range(1, _D_LEVELS + 1):
        m = ((i >> lv) << lv) + (1 << (lv - 1)) - 1
        sel[(lv - 1) * TC + i, m] = 1.0
    bo = lambda n, w: ((np.arange(n)[:, None] // w) == (np.arange(n)[None, :] // w)).astype(np.float32)
    ia = np.arange(TC_A)
    tril2a = (((ia[:, None] // 64) == (ia[None, :] // 64)) & (ia[:, None] >= ia[None, :])).astype(np.float32)
    return dict(tril2=tril2, triu2=tril2.T.copy(), tril1=tril1, triu1=tril1.T.copy(), sel=sel,
                tril2a=tril2a, triu2a=tril2a.T.copy(),
                bo64=bo(256, 64), bo256=bo(D_INNER_C, D_INNER_C // G_C))


def _bucket_starts():
    max_exact = N_BUCKETS // 2
    d = np.arange(0, 4 * MAX_DISTANCE)
    lr = np.log(np.maximum(d, 1).astype(np.float32) / max_exact) / math.log(MAX_DISTANCE / max_exact)
    large = np.minimum(max_exact + (np.maximum(lr, 0.0) * (N_BUCKETS - max_exact)).astype(np.int32), N_BUCKETS - 1)
    bucket = np.where(d < max_exact, d, large)
    return [int(np.argmax(bucket == b)) for b in range(N_BUCKETS)]


_BUCKET_STARTS = _bucket_starts()
_SAT_DIST = _BUCKET_STARTS[N_BUCKETS - 1]


def _bias_from_dist(d, rb_ref, h):
    out = jnp.full(d.shape, rb_ref[N_BUCKETS - 1, h], F32)
    for b in range(N_BUCKETS - 2, -1, -1):
        out = jnp.where(d < _BUCKET_STARTS[b + 1], rb_ref[b, h], out)
    return out


def _sortable(x):
    b = lax.bitcast_convert_type(x + 0.0, I32)
    return jnp.where(b < 0, b ^ 0x7FFFFFFF, b)


_NEG_KEY = int(np.array([NEG], np.float32).view(np.int32)[0]) ^ 0x7FFFFFFF
_INT_MIN = -2 ** 31


def _selected_i32(k, idx, thr, xthr):
    s = jnp.where(k > thr, 1, jnp.where(k == thr, jnp.where(idx <= xthr, 1, 0), 0))
    return jnp.where(k == _NEG_KEY, 0, s)


def _selected(k, idx, thr, xthr):
    return _selected_i32(k, idx, thr, xthr) > 0


def _const_spec(shape):
    nd = len(shape)
    return pl.BlockSpec(shape, lambda *_: (0,) * nd)


def _mix_call(kernel_fn, name, P, B, T, seq_specs, batch_args, const_args, out_w, st_tail, scratch, tc=TC):
    rows = min(T, tc)
    nc = max(T // tc, 1)
    assert rows * nc == T
    in_specs = [pl.BlockSpec((rows, w), functools.partial(lambda b, c, ci: (b * nc + c, ci), ci=off // w)) for off, w in seq_specs]
    args = [P] * len(seq_specs)
    for a in batch_args:
        nd = a.ndim
        in_specs.append(pl.BlockSpec((None,) + a.shape[1:], functools.partial(lambda b, c, nd: (b,) + (0,) * (nd - 1), nd=nd)))
        args.append(a)
    for a in const_args:
        in_specs.append(_const_spec(a.shape))
        args.append(a)
    st_nd = len(st_tail)
    return pl.pallas_call(
        functools.partial(kernel_fn, t_valid=rows, rows=rows),
        grid=(B, nc),
        in_specs=in_specs,
        out_specs=[pl.BlockSpec((rows, out_w), lambda b, c: (b * nc + c, 0)),
                   pl.BlockSpec((None,) + st_tail, lambda b, c: (b,) + (0,) * st_nd)],
        out_shape=[jax.ShapeDtypeStruct((B * T, out_w), F32), jax.ShapeDtypeStruct((B,) + st_tail, F32)],
        scratch_shapes=scratch,
        compiler_params=pltpu.CompilerParams(dimension_semantics=("arbitrary", "arbitrary"), vmem_limit_bytes=VMEM_LIMIT),
        name=name,
    )(*args)


CU = 4


def _dsa_prompt_kernel(rb_ref, pq_ref, pk_ref, o_ref,
                       kvt_scr, key_scr, lg_scr, bias_scr, acc_scr, kh_scr, *, n_sel, idx_bits):
    j = pl.program_id(1)
    nch = j + 1

    @pl.when((pl.program_id(0) == 0) & (j == 0))
    def _():
        srow, tcol = _iota2((QB, QB), 0), _iota2((QB, QB), 1)
        for h in range(H_B):
            bias_scr[h, 0] = jnp.full((QB, QB), rb_ref[N_BUCKETS - 1, h], F32)
            bias_scr[h, 1] = _bias_from_dist(tcol - srow + QB, rb_ref, h)
            bias_scr[h, 2] = _bias_from_dist(tcol - srow, rb_ref, h)

    @pl.when(j == 0)
    def _():
        def tr(c, _):
            r = pl.multiple_of(c * QB, QB)
            kvt_scr[c] = pk_ref[pl.ds(r, QB), PK_KV:PK_KV + LANE].T[DH_B:2 * DH_B, :].astype(BF16)
            return 0
        lax.fori_loop(0, pk_ref.shape[0] // QB, tr, 0)

    lanes = lambda rows, w: jnp.concatenate([rows[h * w:(h + 1) * w, :] for h in range(H_B)], axis=1)
    w4 = lanes(pq_ref[:, PQ_SM:PQ_SM + LANE].T[SM_WI:SM_WI + H_IDX, :], 1) * (H_IDX ** -0.5 * D_IDX ** -0.5)
    qi_rhs = lanes(pq_ref[:, PQ_QI:PQ_QI + LANE].T, D_IDX).astype(BF16)
    qb_rhs = (lanes(pq_ref[:, PQ_QB:PQ_QB + W_B].T, DH_B) * (DH_B ** -0.5)).astype(BF16)
    srow, tcol = _iota2((QB, QB), 0), _iota2((QB, QB), 1)
    heads = lambda x: [x[:, h * QB:(h + 1) * QB] for h in range(H_B)]

    ngrp = (nch + (CU - 1)) // CU

    def score_body(g, _):
        for u in range(CU):
            r = pl.multiple_of((g * CU + u) * QB, QB)
            ki = pk_ref[pl.ds(r, QB), PK_SM:PK_SM + LANE][:, SM_KI:SM_KI + D_IDX].astype(BF16)
            s = jnp.maximum(jnp.dot(ki, qi_rhs, preferred_element_type=F32), 0.0) * w4
            sh = heads(s)
            sc = (sh[0] + sh[1]) + (sh[2] + sh[3])
            sc = jnp.where(srow + r <= tcol + j * QB, sc, NEG)
            key = _sortable(sc)
            key_scr[pl.ds(r, QB), :] = key
            kh_scr[pl.ds(r, QB), :] = key >> 1
        return 0

    lax.fori_loop(0, ngrp, score_body, 0)

    def count(pred):
        def body(g, acc):
            for u in range(CU):
                r = pl.multiple_of((g * CU + u) * QB, QB)
                acc = acc + pred(key_scr[pl.ds(r, QB), :], r).reshape(QB // SUB, SUB, QB).sum(axis=0)
            return acc
        return lax.fori_loop(0, ngrp, body, jnp.zeros((SUB, QB), I32)).sum(axis=0, keepdims=True)

    n_keys = ngrp * (CU * QB)

    def count_lt_static(n_groups):
        def f(t):
            accs = [jnp.zeros((SUB, QB), I32) for _ in range(2)]
            for c in range(n_groups * CU):
                w = (kh_scr[c * QB:(c + 1) * QB, :] - t) >> 31
                accs[c % 2] = accs[c % 2] + w.reshape(QB // SUB, SUB, QB).sum(axis=0)
            return (accs[0] + accs[1]).sum(axis=0, keepdims=True)
        return f

    count_variants = [count_lt_static(n) for n in range(1, key_scr.shape[0] // (CU * QB) + 1)]
    cnt_ge_half = lambda t: n_keys + lax.switch(ngrp - 1, count_variants, t)
    c0 = cnt_ge_half(jnp.zeros((1, QB), I32))
    th = jnp.where(c0 >= n_sel, 0, -(1 << 30)).astype(I32)
    cth = jnp.where(c0 >= n_sel, c0, n_keys)

    def bit_body(i, carry):
        t, ct = carry
        trial = t | jnp.left_shift(jnp.int32(1), 29 - i)
        cnt = cnt_ge_half(trial)
        ok = cnt >= n_sel
        return jnp.where(ok, trial, t), jnp.where(ok, cnt, ct)

    th, cth = lax.fori_loop(0, 30, bit_body, (th, cth))
    odd = (th << 1) | 1
    c_odd = count(lambda k, r: jnp.where(k >= odd, 1, 0))
    thr = jnp.where(c_odd >= n_sel, odd, th << 1)
    cthr = jnp.where(c_odd >= n_sel, c_odd, cth)

    @pl.when(jnp.max(cthr) > n_sel)
    def _():
        quota = n_sel - count(lambda k, r: jnp.where(k > thr, 1, 0))

        def idx_body(i, x):
            trial = x | jnp.left_shift(jnp.int32(1), idx_bits - 1 - i)
            below = count(lambda k, r: jnp.where(k == thr, jnp.where(srow + r < trial, 1, 0), 0))
            return jnp.where(below < quota, trial, x)

        xthr = lax.fori_loop(0, idx_bits, idx_body, jnp.zeros((1, QB), I32))

        def demote_body(g, _):
            for u in range(CU):
                r = pl.multiple_of((g * CU + u) * QB, QB)
                k = key_scr[pl.ds(r, QB), :]
                key_scr[pl.ds(r, QB), :] = jnp.where(k == thr, jnp.where(srow + r > xthr, k - 1, k), k)
            return 0

        lax.fori_loop(0, ngrp, demote_body, 0)

    thr_sel = jnp.maximum(thr, _NEG_KEY + 1)

    def logit_body(g, ms):
        ms = list(ms)
        for u in range(CU):
            c = g * CU + u
            r = pl.multiple_of(c * QB, QB)
            kc = pk_ref[pl.ds(r, QB), PK_KV:PK_KV + LANE][:, 0:DH_B].astype(BF16)
            l4 = heads(jnp.dot(kc, qb_rhs, preferred_element_type=F32))
            sel = key_scr[pl.ds(r, QB), :] >= thr_sel
            bidx = jnp.clip(c - j + 2, 0, 2)
            for h in range(H_B):
                l = jnp.where(sel, l4[h] + bias_scr[h, bidx], NEG)
                lg_scr[h, pl.ds(r, QB), :] = l
                ms[h] = jnp.maximum(ms[h], l.max(axis=0, keepdims=True))
        return tuple(ms)

    ms = lax.fori_loop(0, ngrp, logit_body, tuple(jnp.full((1, QB), NEG, F32) for _ in range(H_B)))
    acc_scr[...] = jnp.zeros(acc_scr.shape, F32)

    def pv_body(g, ss):
        ss = list(ss)
        acc = acc_scr[...]
        for u in range(CU):
            c = g * CU + u
            r = pl.multiple_of(c * QB, QB)
            es = [jnp.exp(lg_scr[h, pl.ds(r, QB), :] - ms[h]) for h in range(H_B)]
            acc = acc + jnp.dot(kvt_scr[c], jnp.concatenate(es, axis=1).astype(BF16), preferred_element_type=F32)
            for h in range(H_B):
                ss[h] = ss[h] + es[h].sum(axis=0, keepdims=True)
        acc_scr[...] = acc
        return tuple(ss)

    ss = lax.fori_loop(0, ngrp, pv_body, tuple(jnp.zeros((1, QB), F32) for _ in range(H_B)))
    acc = heads(acc_scr[...])
    ot = jnp.concatenate([acc[h] / ss[h] for h in range(H_B)], axis=0)
    o_ref[...] = ot.T


def _dsa_prompt(Pq, Pk, rel_bias, B, T):
    nb = T // QB
    assert nb % CU == 0
    n_sel = min(TOPK_MAX, T // 4)
    idx_bits = max(1, int(math.ceil(math.log2(T))))
    return pl.pallas_call(
        functools.partial(_dsa_prompt_kernel, n_sel=n_sel, idx_bits=idx_bits),
        grid=(B, nb),
        in_specs=[pl.BlockSpec(memory_space=pltpu.SMEM),
                  pl.BlockSpec((QB, PQ_W), lambda b, j: (b * nb + j, 0)),
                  pl.BlockSpec((T, PK_W), lambda b, j: (b, 0))],
        out_specs=pl.BlockSpec((QB, W_B), lambda b, j: (b * nb + j, 0)),
        out_shape=jax.ShapeDtypeStruct((B * T, W_B), F32),
        scratch_shapes=[pltpu.VMEM((nb, DH_B, QB), BF16), pltpu.VMEM((T, QB), I32), pltpu.VMEM((H_B, T, QB), F32),
                        pltpu.VMEM((H_B, 3, QB, QB), F32), pltpu.VMEM((DH_B, H_B * QB), F32), pltpu.VMEM((T, QB), I32)],
        compiler_params=pltpu.CompilerParams(dimension_semantics=("arbitrary", "arbitrary"), vmem_limit_bytes=VMEM_LIMIT),
        name="dsa_prompt",
    )(rel_bias, Pq, Pk)


PG = 16
GW = PG * PAGE_SIZE


def _stack_heads(x, w):
    return jnp.concatenate([x[:, h * w:(h + 1) * w] for h in range(x.shape[1] // w)], axis=0)


def _dsa_s_scores_kernel(pt_ref, pq_ref, *refs, ng, ds):
    page_refs, o_ref = refs[:PG], refs[PG]
    g = pl.program_id(1)
    sm = pq_ref[:, PQ_SM:PQ_SM + LANE]
    qst = _stack_heads(pq_ref[:, PQ_QI:PQ_QI + LANE], D_IDX)
    wcol = jnp.concatenate([sm[:, SM_WI + h:SM_WI + h + 1] for h in range(H_IDX)], axis=0) * (H_IDX ** -0.5 * D_IDX ** -0.5)

    def tile_scores(qk):
        s = jnp.maximum(qk, 0.0) * wcol
        out = s[0:ds]
        for h in range(1, H_IDX):
            out = out + s[h * ds:(h + 1) * ds]
        return out

    @pl.when(g < ng)
    def _():
        for i in range(PG):
            o_ref[:, i * PAGE_SIZE:(i + 1) * PAGE_SIZE] = tile_scores(_bdot(qst, page_refs[i][...]))

    @pl.when(g == ng)
    def _():
        sc = tile_scores(_bdot_nt(qst, _pad_rows(sm[:, SM_KI:SM_KI + D_IDX], PAGE_SIZE)))
        lane, row = _iota2(sc.shape, 1), _iota2(sc.shape, 0)
        o_ref[:, 0:PAGE_SIZE] = jnp.where(lane <= row, sc, NEG)
        o_ref[:, PAGE_SIZE:GW] = jnp.full((ds, GW - PAGE_SIZE), NEG, F32)


def _dsa_s_attn_kernel(pt_ref, rb_ref, sc_ref, pq_ref, pk_ref, *refs, ng, ds, n_sel, idx_bits):
    k_refs, v_refs = refs[:PG], refs[PG:2 * PG]
    o_ref, key_scr, thr_scr, bias_scr, m_scr, l_scr, acc_scr = refs[2 * PG:]
    g = pl.program_id(1)
    ntile = (ng + 1) * PG
    hr = H_B * ds

    @pl.when(g == 0)
    def _():
        for i in range(ntile):
            key_scr[:, i * LANE:(i + 1) * LANE] = _sortable(sc_ref[:, i * LANE:(i + 1) * LANE])
        lane = _iota2((ds, LANE), 1)

        def count(pred):
            accs = [jnp.zeros((ds, LANE), I32) for _ in range(4)]
            for i in range(ng * PG + 1):
                accs[i % 4] = accs[i % 4] + pred(key_scr[:, i * LANE:(i + 1) * LANE], i * LANE)
            return ((accs[0] + accs[1]) + (accs[2] + accs[3])).sum(axis=1, keepdims=True)

        cnt_ge = lambda t: count(lambda k, off: jnp.where(k >= t, 1, 0))
        c0 = cnt_ge(jnp.zeros((ds, 1), I32))
        thr = jnp.where(c0 >= n_sel, 0, _INT_MIN).astype(I32)
        cthr = jnp.where(c0 >= n_sel, c0, ng * GW + LANE)

        def bit_body(i, carry):
            t, ct = carry
            trial = t | jnp.left_shift(jnp.int32(1), 30 - i)
            cnt = cnt_ge(trial)
            ok = cnt >= n_sel
            return jnp.where(ok, trial, t), jnp.where(ok, cnt, ct)

        thr, cthr = lax.fori_loop(0, 31, bit_body, (thr, cthr))

        def tie_search():
            quota = n_sel - count(lambda k, off: jnp.where(k > thr, 1, 0))

            def idx_body(i, x):
                trial = x | jnp.left_shift(jnp.int32(1), idx_bits - 1 - i)
                below = count(lambda k, off: jnp.where(k == thr, jnp.where(lane + off < trial, 1, 0), 0))
                return jnp.where(below < quota, trial, x)

            return lax.fori_loop(0, idx_bits, idx_body, jnp.zeros((ds, 1), I32))

        xthr = lax.cond(jnp.max(cthr) > n_sel, tie_search, lambda: jnp.full((ds, 1), (1 << idx_bits) - 1, I32))
        thr_scr[0] = jnp.broadcast_to(thr, (ds, LANE))
        thr_scr[1] = jnp.broadcast_to(xthr, (ds, LANE))

        qrow = _iota2((ds, LANE), 0)
        for h in range(H_B):
            bias_scr[0, h * ds:(h + 1) * ds, :] = _bias_from_dist(PAGE_SIZE + qrow - lane, rb_ref, h)
            bias_scr[1, h * ds:(h + 1) * ds, :] = _bias_from_dist(qrow - lane, rb_ref, h)
            bias_scr[2, h * ds:(h + 1) * ds, :] = jnp.full((ds, LANE), rb_ref[N_BUCKETS - 1, h], F32)
        m_scr[...] = jnp.full(m_scr.shape, NEG, F32)
        l_scr[...] = jnp.zeros(l_scr.shape, F32)
        acc_scr[...] = jnp.zeros(acc_scr.shape, F32)

    qst = _stack_heads(pq_ref[:, PQ_QB:PQ_QB + W_B], DH_B) * (DH_B ** -0.5)
    thr, xthr = thr_scr[0], thr_scr[1]

    def select(tile):
        k = key_scr[:, pl.ds(pl.multiple_of(tile * LANE, LANE), LANE)]
        idx = _iota2((ds, LANE), 1) + tile * LANE
        sel = _selected_i32(k, idx, thr, xthr)
        return jnp.concatenate([sel] * H_B, axis=0) > 0

    def update(lg, sel, pv):
        m_old = m_scr[...]
        m_new = jnp.maximum(m_old, jnp.where(sel, lg, NEG).max(axis=1, keepdims=True))
        p = jnp.where(sel, jnp.exp(lg - m_new), 0.0)
        corr = jnp.exp(m_old - m_new)
        l_scr[...] = l_scr[...] * corr + p.sum(axis=1, keepdims=True)
        acc_scr[...] = acc_scr[...] * corr + pv(p)
        m_scr[...] = m_new

    @pl.when(g < ng)
    def _():
        lgs, sels = [], []
        for i in range(PG):
            lg = _bdot(qst, k_refs[i][...])
            if i == PG - 1:
                lg = lg + jnp.where(g == ng - 1, bias_scr[0], bias_scr[2])
            else:
                lg = lg + bias_scr[2]
            lgs.append(lg)
            sels.append(select(g * PG + i))
        vt = jnp.concatenate([v_refs[i][...] for i in range(PG)], axis=1)
        update(jnp.concatenate(lgs, axis=1), jnp.concatenate(sels, axis=1), lambda p: _bdot_nt(p, vt))

    @pl.when(g == ng)
    def _():
        kvn = _pad_rows(pk_ref[:, PK_KV:PK_KV + LANE], PAGE_SIZE)
        lg = _bdot_nt(qst, kvn[:, 0:DH_B]) + bias_scr[1]
        update(lg, select(ng * PG), lambda p: _bdot(p, kvn[:, DH_B:2 * DH_B]))
        o = acc_scr[...] / l_scr[...]
        o_ref[...] = jnp.concatenate([o[h * ds:(h + 1) * ds] for h in range(H_B)], axis=1)


def _dsa_sample(Pq, Pk, rel_bias, cache_k, cache_v, cache_kidx, page_table, layer, B, DS):
    n_pages = page_table.shape[1]
    past = n_pages * PAGE_SIZE
    assert n_pages % PG == 0
    ng = n_pages // PG
    n_sel = min(TOPK_MAX, (past + DS) // 4)
    idx_bits = int(math.ceil(math.log2(past + LANE)))
    s_pad = (ng + 1) * GW

    def page_spec(width, i):
        return pl.BlockSpec((None, None, width, PAGE_SIZE),
                            lambda b, g, pt: (layer, pt[b, jnp.minimum(g, ng - 1) * PG + i], 0, 0))

    row = lambda b, g, pt: (b, 0)
    cp = pltpu.CompilerParams(dimension_semantics=("arbitrary", "arbitrary"), vmem_limit_bytes=VMEM_LIMIT)
    scores = pl.pallas_call(
        functools.partial(_dsa_s_scores_kernel, ng=ng, ds=DS),
        grid_spec=pltpu.PrefetchScalarGridSpec(
            num_scalar_prefetch=1, grid=(B, ng + 1),
            in_specs=[pl.BlockSpec((DS, PQ_W), row)] + [page_spec(D_IDX, i) for i in range(PG)],
            out_specs=pl.BlockSpec((None, DS, GW), lambda b, g, pt: (b, 0, g))),
        out_shape=jax.ShapeDtypeStruct((B, DS, s_pad), F32),
        compiler_params=cp, name="dsa_sample_scores",
    )(page_table, Pq, *([cache_kidx] * PG))

    return pl.pallas_call(
        functools.partial(_dsa_s_attn_kernel, ng=ng, ds=DS, n_sel=n_sel, idx_bits=idx_bits),
        grid_spec=pltpu.PrefetchScalarGridSpec(
            num_scalar_prefetch=1, grid=(B, ng + 1),
            in_specs=[pl.BlockSpec(memory_space=pltpu.SMEM),
                      pl.BlockSpec((None, DS, s_pad), lambda b, g, pt: (b, 0, 0)),
                      pl.BlockSpec((DS, PQ_W), row), pl.BlockSpec((DS, PK_W), row)]
                     + [page_spec(DH_B, i) for i in range(PG)] + [page_spec(DH_B, i) for i in range(PG)],
            out_specs=pl.BlockSpec((DS, W_B), lambda b, g, pt: (b, 0)),
            scratch_shapes=[pltpu.VMEM((DS, s_pad), I32), pltpu.VMEM((2, DS, LANE), I32), pltpu.VMEM((3, H_B * DS, LANE), F32),
                            pltpu.VMEM((H_B * DS, 1), F32), pltpu.VMEM((H_B * DS, 1), F32), pltpu.VMEM((H_B * DS, DH_B), F32)]),
        out_shape=jax.ShapeDtypeStruct((B * DS, W_B), F32),
        compiler_params=cp, name="dsa_sample_attn",
    )(page_table, rel_bias, scores, Pq, Pk, *([cache_k] * PG), *([cache_v] * PG))


def _merge_kernel(x_ref, oa_ref, ob_ref, oc_ref, od_ref, gate_ref, wb_ref, wo_ref, o_ref):
    m = None
    r0 = 0
    for br, ref in enumerate((oa_ref, ob_ref, oc_ref, od_ref)):
        w = ref.shape[1]
        t = _sigmoid(gate_ref[:, br * D_MODEL:(br + 1) * D_MODEL]) * jnp.dot(ref[...].astype(BF16), wb_ref[r0:r0 + w, :], preferred_element_type=F32)
        m = t if m is None else m + t
        r0 += w
    o_ref[...] = x_ref[...] + jnp.dot(m.astype(BF16), wo_ref[...], preferred_element_type=F32)


def _merge(x2d, oa, ob, oc, od, P, wb, wo):
    n = x2d.shape[0]
    tm = min(n, 256)
    row = lambda i: (i, 0)
    return pl.pallas_call(
        _merge_kernel,
        grid=(n // tm,),
        in_specs=[pl.BlockSpec((tm, D_MODEL), row), pl.BlockSpec((tm, W_A), row), pl.BlockSpec((tm, W_B), row),
                  pl.BlockSpec((tm, W_C), row), pl.BlockSpec((tm, W_D), row),
                  pl.BlockSpec((tm, N_BRANCH * D_MODEL), lambda i: (i, OFF_GATE // (N_BRANCH * D_MODEL))),
                  _const_spec((MIX_W, D_MODEL)), _const_spec((D_MODEL, D_MODEL))],
        out_specs=pl.BlockSpec((tm, D_MODEL), row),
        out_shape=jax.ShapeDtypeStruct((n, D_MODEL), F32),
        compiler_params=pltpu.CompilerParams(dimension_semantics=("arbitrary",), vmem_limit_bytes=VMEM_LIMIT),
        name="merge",
    )(x2d, oa, ob, oc, od, P, wb, wo)


def _mlp_kernel(x_ref, g_ref, wu_ref, wd_ref, gf_ref, *refs, final):
    if final:
        o_ref, y_ref, h_scr, acc_scr = refs
    else:
        o_ref, h_scr, acc_scr = refs
    f = pl.program_id(1)

    @pl.when(f == 0)
    def _():
        x = x_ref[...]
        r = lax.rsqrt(jnp.mean(x * x, axis=-1, keepdims=True) + EPS)
        h_scr[...] = ((x * r) * g_ref[...]).astype(BF16)
        acc_scr[...] = jnp.zeros(acc_scr.shape, F32)

    a = jnp.maximum(jnp.dot(h_scr[...], wu_ref[...], preferred_element_type=F32), 0.0)
    acc_scr[...] += jnp.dot((a * a).astype(BF16), wd_ref[...], preferred_element_type=F32)

    @pl.when(f == pl.num_programs(1) - 1)
    def _():
        out = x_ref[...] + acc_scr[...]
        o_ref[...] = out
        if final:
            r = lax.rsqrt(jnp.mean(out * out, axis=-1, keepdims=True) + EPS)
            y_ref[...] = (out * r) * gf_ref[...]


def _mlp(x2d, g, wu, wd, gf, final):
    n = x2d.shape[0]
    tm = min(n, 512)
    tf = 1024
    row = lambda i, f: (i, 0)
    out_spec = pl.BlockSpec((tm, D_MODEL), row)
    shp = jax.ShapeDtypeStruct((n, D_MODEL), F32)
    return pl.pallas_call(
        functools.partial(_mlp_kernel, final=final),
        grid=(n // tm, D_FF // tf),
        in_specs=[pl.BlockSpec((tm, D_MODEL), row), _const_spec((1, D_MODEL)),
                  pl.BlockSpec((D_MODEL, tf), lambda i, f: (0, f)), pl.BlockSpec((tf, D_MODEL), lambda i, f: (f, 0)),
                  _const_spec((1, D_MODEL))],
        out_specs=[out_spec, out_spec] if final else [out_spec],
        out_shape=[shp, shp] if final else [shp],
        scratch_shapes=[pltpu.VMEM((tm, D_MODEL), BF16), pltpu.VMEM((tm, D_MODEL), F32)],
        compiler_params=pltpu.CompilerParams(dimension_semantics=("arbitrary", "arbitrary"), vmem_limit_bytes=VMEM_LIMIT),
        name="mlp_final" if final else "mlp",
    )(x2d, g.reshape(1, D_MODEL), wu, wd, gf.reshape(1, D_MODEL))


def _layer(x, conv_state, s_delta, s_ssm, s_hgrn, lw, consts, sample_ctx, final):
    B, T, _ = x.shape
    x2 = x.reshape(B * T, D_MODEL)
    P, Pq, Pk = _in_proj(x2, lw['norm_mix'], lw['w_in'])

    def conv0(lo, hi):
        if conv_state is None:
            return jnp.zeros((B, SUB, hi - lo), F32)
        return jnp.concatenate([jnp.zeros((B, SUB - (CONV_W - 1), hi - lo), F32), conv_state[:, :, lo:hi]], axis=1)

    zeros_state = lambda n: jnp.zeros((B, n, 64, 64), F32)
    c = consts
    half = CONV_CH // 2
    assert CONV_A == half and OFF_PC % half == 0
    oa, s_a = _mix_call(
        _mix_a_kernel, "mix_a", P, B, T,
        [(OFF_PC, half), (OFF_ZA, W_A), (OFF_SM, LANE)],
        [conv0(0, CONV_A), zeros_state(H_A) if s_delta is None else s_delta],
        [lw['conv_w'][:, 0:CONV_A], lw['conv_b'][None, 0:CONV_A], lw['hp'], lw['hpt'], lw['gn_a'], c['tril2a'], c['triu2a'], c['bo64']],
        W_A, (H_A, DK_A, DV_A),
        [pltpu.VMEM((TC_A + SUB, CONV_A), F32), pltpu.VMEM((H_A, DK_A, DV_A), F32), pltpu.VMEM((TC_A, W_A), F32)],
        tc=TC_A)
    oc, s_c = _mix_call(
        _mix_c_kernel, "mix_c", P, B, T,
        [(OFF_PC + half, half), (OFF_ZC, W_C), (OFF_SM, LANE)],
        [conv0(CONV_A, CONV_CH), zeros_state(H_C) if s_ssm is None else s_ssm],
        [lw['conv_w'][:, CONV_A:], lw['conv_b'][None, CONV_A:], lw['hp'], lw['hpt'], lw['gn_c'], c['tril1'], c['triu1'], c['bo256']],
        W_C, (H_C, N_C, P_C),
        [pltpu.VMEM((TC + SUB, CONV_C), F32), pltpu.VMEM((H_C, N_C, P_C), F32), pltpu.VMEM((TC, W_C), F32)])
    sbd0 = jnp.zeros((B, H_D, DK_D, H_D, DV_D), F32)
    if s_hgrn is not None:
        for h in range(H_D):
            sbd0 = sbd0.at[:, h, :, h, :].set(s_hgrn[:, h])
    sbd0 = sbd0.reshape(B, H_D * DK_D, W_D)
    od, sbd = _mix_call(
        _mix_d_kernel, "mix_d", P, B, T,
        [(OFF_D, 4 * W_D)],
        [sbd0],
        [lw['lb'], lw['gn_d'], c['tril2'], c['sel'], c['bo64']],
        W_D, (H_D * DK_D, W_D),
        [pltpu.VMEM((H_D * DK_D, W_D), F32), pltpu.VMEM((TC, W_D), F32)])
    sbd5 = sbd.reshape(B, H_D, DK_D, H_D, DV_D)
    s_d = jnp.stack([sbd5[:, h, :, h, :] for h in range(H_D)], axis=1)
    if sample_ctx is None:
        ob = _dsa_prompt(Pq, Pk, lw['rel_bias'], B, T)
    else:
        ob = _dsa_sample(Pq, Pk, lw['rel_bias'], sample_ctx['cache_k'], sample_ctx['cache_v'], sample_ctx['cache_kidx'],
                         sample_ctx['page_table'], sample_ctx['layer'], B, T)
    x1 = _merge(x2, oa, ob, oc, od, P, lw['w_branch'], lw['w_out'])
    outs = _mlp(x1, lw['norm_mlp'], lw['w_up'], lw['w_down'], lw['norm_final'], final)
    xo = outs[0].reshape(B, T, D_MODEL)
    y = outs[1].reshape(B, T, D_MODEL) if final else None
    P3 = P.reshape(B, T, P_PAD)
    Pk3 = Pk.reshape(B, T, PK_W)
    states = (Pk3[:, :, PK_KV:PK_KV + DH_B], Pk3[:, :, PK_KV + DH_B:PK_KV + 2 * DH_B], Pk3[:, :, PK_SM + SM_KI:PK_SM + SM_KI + D_IDX],
              P3[:, T - (CONV_W - 1):, OFF_PC:OFF_PC + CONV_CH], s_a, s_c, s_d)
    return xo, y, states


def kernel(x_prompt, x_sample, cache_k, cache_v, cache_kidx, state_conv, state_delta, state_ssm, state_hgrn, page_table, norm_mix, w_in, conv_w, conv_b, a_log_a, dt_bias_a, gnorm_a, rel_bias, a_log_c, dt_bias_c, d_skip_c, gnorm_c, hgrn_gamma, gnorm_d, w_branch, w_out, norm_mlp, w_up, w_down, norm_final):
    depth = w_in.shape[0]
    npc = _np_consts()
    consts = {k: jnp.asarray(v, BF16) for k, v in npc.items()}
    pg = jax.nn.softmax(hgrn_gamma.astype(F32), axis=0)
    lower_bounds = jnp.cumsum(pg, axis=0) - pg[0]
    cache_kt, cache_vt, cache_kit = (jnp.swapaxes(c, 2, 3) for c in (cache_k, cache_v, cache_kidx))
    yp, ys = x_prompt, x_sample
    new_p, new_s = [], []
    for l in range(depth):
        hp = jnp.zeros((SUB, LANE), F32)
        hp = hp.at[0, 0:H_A].set(a_log_a[l]).at[1, 0:H_A].set(dt_bias_a[l])
        hp = hp.at[2, 0:H_C].set(a_log_c[l]).at[3, 0:H_C].set(dt_bias_c[l]).at[4, 0:H_C].set(d_skip_c[l])
        lw = dict(norm_mix=norm_mix[l], w_in=_permute_w_in(w_in[l]), conv_w=conv_w[l], conv_b=conv_b[l],
                  hp=hp, hpt=hp.T, gn_a=jnp.tile(gnorm_a[l], H_A)[None, :], gn_c=gnorm_c[l][None, :],
                  gn_d=jnp.tile(gnorm_d[l], H_D)[None, :], lb=lower_bounds[l][None, :], rel_bias=rel_bias,
                  w_branch=w_branch[l].astype(BF16), w_out=w_out[l].astype(BF16), norm_mlp=norm_mlp[l],
                  w_up=w_up[l].astype(BF16), w_down=w_down[l].astype(BF16), norm_final=norm_final)
        final = l == depth - 1
        yp, yp_n, st_p = _layer(yp, None, None, None, None, lw, consts, None, final)
        ctx = dict(cache_k=cache_kt, cache_v=cache_vt, cache_kidx=cache_kit, page_table=page_table, layer=l)
        ys, ys_n, st_s = _layer(ys, state_conv[l], state_delta[l], state_ssm[l], state_hgrn[l], lw, consts, ctx, final)
        new_p.append(st_p)
        new_s.append(st_s)
    stack = lambda per_layer: [jnp.stack(items) for items in zip(*per_layer)]
    return (yp_n, ys_n, *stack(new_p), *stack(new_s))
```

```python
import functools
import math

import numpy as np
import jax
import jax.numpy as jnp
from jax import lax
from jax.experimental import pallas as pl
from jax.experimental.pallas import tpu as pltpu

F32 = jnp.float32
BF16 = jnp.bfloat16
I32 = jnp.int32

D_MODEL = 1024
PAGE_SIZE = 128
EPS = 1e-6
NEG = -1e30
TINY = 1e-30
CONV_W = 4
D_FF = 4 * D_MODEL
N_BRANCH = 4
H_A, DK_A, DV_A = 4, 64, 64
H_B, DH_B = 4, 64
H_IDX, D_IDX = 4, 32
TOPK_MAX = 256
N_BUCKETS = 32
MAX_DISTANCE = 128
H_C, P_C, N_C, G_C = 8, 64, 64, 2
H_D, DK_D, DV_D = 4, 64, 64
QK_A = H_A * DK_A
CONV_A = 2 * QK_A + H_A * DV_A
D_INNER_C = H_C * P_C
CONV_C = D_INNER_C + 2 * G_C * N_C
CONV_CH = CONV_A + CONV_C
W_A, W_B, W_C, W_D = H_A * DV_A, H_B * DH_B, D_INNER_C, H_D * DV_D
MIX_W = W_A + W_B + W_C + W_D
IN_WIDTHS = (CONV_CH, W_A, H_A, H_A, W_B, DH_B, DH_B, H_IDX * D_IDX, D_IDX, H_IDX, D_INNER_C, H_C,
             H_D * DK_D, H_D * DK_D, W_D, W_D, N_BRANCH * D_MODEL)
P_IN = sum(IN_WIDTHS)

LANE = 128
SUB = 8
TC = 128
TC_A = 256
QB = 128
VMEM_LIMIT = 48 * 1024 * 1024

OFF_GATE, OFF_D, OFF_ZC, OFF_ZA, OFF_QB, OFF_PC, OFF_KV, OFF_QI, OFF_SM = 0, 4096, 5120, 5632, 5888, 6144, 7680, 7808, 7936
P_PAD = 8064
SM_KI, SM_WI, SM_AA, SM_BA, SM_DT = 0, 32, 36, 40, 44


def _src_offsets():
    offs, o = [], 0
    for w in IN_WIDTHS:
        offs.append(o)
        o += w
    return offs


def _permute_w_in(w):
    (o_pc, o_za, o_aa, o_ba, o_qb, o_kb, o_vb, o_qi, o_ki, o_wi, o_zc, o_dt, o_qd, o_fd, o_id, o_gd, o_gate) = _src_offsets()
    wt = jnp.swapaxes(w, 0, 1)
    seg = lambda o, n: wt[o:o + n, :]
    small = jnp.concatenate([seg(o_ki, D_IDX), seg(o_wi, H_IDX), seg(o_aa, H_A), seg(o_ba, H_A), seg(o_dt, H_C),
                             jnp.zeros((LANE - (D_IDX + H_IDX + 2 * H_A + H_C), w.shape[0]), w.dtype)], axis=0)
    out = jnp.concatenate([
        seg(o_gate, N_BRANCH * D_MODEL),
        seg(o_qd, 4 * W_D),
        seg(o_zc, D_INNER_C), seg(o_za, W_A), seg(o_qb, W_B), seg(o_pc, CONV_CH),
        seg(o_kb, 2 * DH_B),
        seg(o_qi, H_IDX * D_IDX), small], axis=0)
    assert out.shape[0] == P_PAD
    return out.astype(BF16)


def _bdot(a, b):
    return jnp.dot(a.astype(BF16), b.astype(BF16), preferred_element_type=F32)


def _bdot_nt(a, b):
    return lax.dot_general(a.astype(BF16), b.astype(BF16), (((1,), (1,)), ((), ())), preferred_element_type=F32)


def _split2(a):
    hi = a.astype(BF16)
    lo = (a - hi.astype(F32)).astype(BF16)
    return hi, lo


def _split3(a):
    hi = a.astype(BF16)
    r = a - hi.astype(F32)
    mid = r.astype(BF16)
    lo = (r - mid.astype(F32)).astype(BF16)
    return hi, mid, lo


def _dot01_left(m01, x):
    hi, mid, lo = _split3(x)
    d = lambda p: jnp.dot(m01, p, preferred_element_type=F32)
    return d(hi) + (d(mid) + d(lo))


def _dot01_right(x, m01):
    hi, mid, lo = _split3(x)
    d = lambda p: jnp.dot(p, m01, preferred_element_type=F32)
    return d(hi) + (d(mid) + d(lo))


def _xdot(a, b):
    ah, al = _split2(a)
    bh, bl = _split2(b)
    d = lambda p, q: jnp.dot(p, q, preferred_element_type=F32)
    return d(ah, bh) + (d(ah, bl) + d(al, bh))


def _sigmoid(x):
    return 1.0 / (1.0 + jnp.exp(-x))


def _silu(x):
    return x * _sigmoid(x)


def _softplus(x):
    return jnp.maximum(x, 0.0) + jnp.log(1.0 + jnp.exp(-jnp.abs(x)))


def _iota2(shape, dim):
    return lax.broadcasted_iota(I32, shape, dim)


def _pad_rows(x, rows):
    if x.shape[0] == rows:
        return x
    return jnp.concatenate([x, jnp.zeros((rows - x.shape[0],) + x.shape[1:], x.dtype)], axis=0)


def _tri_inv_many(Ls, top):
    n = Ls[0].shape[0]
    ii, jj = _iota2((n, n), 0), _iota2((n, n), 1)
    xor = ii ^ jj
    eye = jnp.where(ii == jj, 1.0, 0.0)
    Ns = [jnp.where((xor >> 3) == 0, -L, 0.0) for L in Ls]
    Xs = [eye + N for N in Ns]
    N2s = [_bdot(N, N) for N in Ns]
    Xs = [X + _bdot(X, N2) for X, N2 in zip(Xs, N2s)]
    N4s = [_bdot(N2, N2) for N2 in N2s]
    Xs = [X + _bdot(X, N4) for X, N4 in zip(Xs, N4s)]
    sh = 4
    while (1 << sh) <= top:
        XBs = [_bdot(X, jnp.where((xor >> (sh - 1)) == 1, L, 0.0)) for X, L in zip(Xs, Ls)]
        Xs = [X - _bdot(XB, X) for X, XB in zip(Xs, XBs)]
        sh += 1
    return Xs


def _conv_silu(pc, ext_scr, cw_ref, cb_ref):
    tc = pc.shape[0]
    ext_scr[SUB:SUB + tc, :] = pc
    y = cb_ref[...]
    for j in range(CONV_W):
        y = y + ext_scr[SUB - (CONV_W - 1) + j:SUB - (CONV_W - 1) + j + tc, :] * cw_ref[j:j + 1, :]
    ext_scr[0:SUB, :] = ext_scr[tc:tc + SUB, :]
    return _silu(y)


TN_IN = 1152
PQ_W = W_B + 2 * LANE
PK_W = 2 * LANE
PQ_QB, PQ_QI, PQ_SM = 0, W_B, W_B + LANE
PK_KV, PK_SM = 0, LANE
_COMPACT_COPIES = (("pq", PQ_QB, OFF_QB, W_B), ("pq", PQ_QI, OFF_QI, LANE), ("pq", PQ_SM, OFF_SM, LANE),
                   ("pk", PK_KV, OFF_KV, LANE), ("pk", PK_SM, OFF_SM, LANE))


def _in_proj_kernel(x_ref, g_ref, w_ref, o_ref, pq_ref, pk_ref, h_scr):
    j = pl.program_id(1)

    @pl.when(j == 0)
    def _():
        x = x_ref[...]
        r = lax.rsqrt(jnp.mean(x * x, axis=-1, keepdims=True) + EPS)
        h_scr[...] = ((x * r) * g_ref[...]).astype(BF16)

    o_ref[...] = lax.dot_general(h_scr[...], w_ref[...], (((1,), (1,)), ((), ())), preferred_element_type=F32)

    def copy(dst_ref, dst, src, width):
        tile, lo = divmod(src, TN_IN)
        assert lo + width <= TN_IN

        @pl.when(j == tile)
        def _():
            dst_ref[:, dst:dst + width] = o_ref[:, lo:lo + width]

    for name, dst, src, width in _COMPACT_COPIES:
        copy(pq_ref if name == "pq" else pk_ref, dst, src, width)


def _in_proj(x2d, g, w_perm_t):
    n = x2d.shape[0]
    tm = min(n, 1024)
    tn = TN_IN
    return pl.pallas_call(
        _in_proj_kernel,
        grid=(n // tm, P_PAD // tn),
        in_specs=[pl.BlockSpec((tm, D_MODEL), lambda i, j: (i, 0)),
                  pl.BlockSpec((1, D_MODEL), lambda i, j: (0, 0)),
                  pl.BlockSpec((tn, D_MODEL), lambda i, j: (j, 0))],
        out_specs=[pl.BlockSpec((tm, tn), lambda i, j: (i, j)),
                   pl.BlockSpec((tm, PQ_W), lambda i, j: (i, 0)), pl.BlockSpec((tm, PK_W), lambda i, j: (i, 0))],
        out_shape=[jax.ShapeDtypeStruct((n, P_PAD), F32), jax.ShapeDtypeStruct((n, PQ_W), F32),
                   jax.ShapeDtypeStruct((n, PK_W), F32)],
        scratch_shapes=[pltpu.VMEM((tm, D_MODEL), BF16)],
        compiler_params=pltpu.CompilerParams(dimension_semantics=("arbitrary", "arbitrary"), vmem_limit_bytes=VMEM_LIMIT),
        name="in_proj",
    )(x2d, g.reshape(1, D_MODEL), w_perm_t)


def _mix_a_kernel(pc_ref, za_ref, sm_ref, conv0_ref, s0_ref, cw_ref, cb_ref, hp_ref, hpt_ref, gn_ref,
                  tril_ref, triu_ref, bo_ref, o_ref, sout_ref, ext_scr, s_scr, ob_scr, *, t_valid, rows):
    CA = 64
    tc = ob_scr.shape[0]

    @pl.when(pl.program_id(1) == 0)
    def _():
        ext_scr[0:SUB, :] = conv0_ref[...]
        s_scr[...] = s0_ref[...]

    u = _conv_silu(_pad_rows(pc_ref[...], tc), ext_scr, cw_ref, cb_ref)
    bo = bo_ref[...]
    q_raw, k_raw, va = u[:, 0:QK_A], u[:, QK_A:2 * QK_A], u[:, 2 * QK_A:CONV_A]
    qq = q_raw * lax.rsqrt(_dot01_right(q_raw * q_raw, bo) + EPS) * (DK_A ** -0.5)
    ka = k_raw * lax.rsqrt(_dot01_right(k_raw * k_raw, bo) + EPS)
    kat = ka.T

    sm = _pad_rows(sm_ref[...], tc)
    smt = sm.T
    hp, hpt = hp_ref[...], hpt_ref[...]
    g_col = -jnp.exp(hp[0:1, 0:H_A]) * _softplus(sm[:, SM_AA:SM_AA + H_A] + hp[1:2, 0:H_A])
    beta = _sigmoid(sm[:, SM_BA:SM_BA + H_A])
    g_row = -jnp.exp(hpt[0:H_A, 0:1]) * _softplus(smt[SM_AA:SM_AA + H_A, :] + hpt[0:H_A, 1:2])
    if t_valid < tc:
        g_col = jnp.where(_iota2(g_col.shape, 0) < t_valid, g_col, 0.0)
        beta = jnp.where(_iota2(beta.shape, 0) < t_valid, beta, 0.0)
        g_row = jnp.where(_iota2(g_row.shape, 1) < t_valid, g_row, 0.0)
    gcum_col = _dot01_left(tril_ref[...], g_col)
    gcum_row = _dot01_right(g_row, triu_ref[...])

    HS = H_A * CA
    ii, jj = _iota2((HS, HS), 0), _iota2((HS, HS), 1)
    same = (ii >> 6) == (jj >> 6)
    incl, strict = same & (ii >= jj), same & (ii > jj)
    nsc = -(-t_valid // CA)
    pre = []
    for sc in range(nsc):
        r0 = sc * CA
        stack = lambda x, w: jnp.concatenate([x[r0:r0 + CA, h * w:(h + 1) * w] for h in range(H_A)], axis=0)
        ks, qs, vs = stack(ka, DK_A), stack(qq, DK_A), stack(va, DV_A)
        bcol, gc = stack(beta, 1), stack(gcum_col, 1)
        gr = jnp.concatenate([gcum_row[h:h + 1, r0:r0 + CA] for h in range(H_A)], axis=1)
        e = jnp.exp(jnp.where(incl, gc - gr, 0.0))
        kb = ks * bcol
        eg = jnp.exp(gc)
        pre.append(dict(r0=r0, gc=gc, qe=qs * eg, low=jnp.where(strict, _bdot_nt(kb, ks) * e, 0.0),
                        rhs=jnp.concatenate([vs * bcol, kb * eg], axis=1),
                        attn=jnp.where(incl, _bdot_nt(qs, ks) * e, 0.0)))
    tinvs = _tri_inv_many([p['low'] for p in pre], CA)
    sols = [_xdot(t, p['rhs']) for t, p in zip(tinvs, pre)]
    hs = lambda x, h: x[h * CA:(h + 1) * CA]
    for p, sol in zip(pre, sols):
        r0, gc = p['r0'], p['gc']
        states = [s_scr[h] for h in range(H_A)]
        v_new = jnp.concatenate([hs(sol, h)[:, 0:DV_A] - _bdot(hs(sol, h)[:, DV_A:], states[h]) for h in range(H_A)], axis=0)
        o = jnp.concatenate([_bdot(hs(p['qe'], h), states[h]) for h in range(H_A)], axis=0) + _bdot(p['attn'], v_new)
        for h in range(H_A):
            gch = hs(gc, h)
            gl = gch[CA - 1:CA, :]
            s_scr[h] = states[h] * jnp.exp(gl) + _bdot(kat[h * DK_A:(h + 1) * DK_A, r0:r0 + CA], hs(v_new, h) * jnp.exp(gl - gch))
            ob_scr[r0:r0 + CA, h * DV_A:(h + 1) * DV_A] = hs(o, h)
    if nsc * CA < tc:
        ob_scr[nsc * CA:tc, :] = jnp.zeros((tc - nsc * CA, W_A), F32)

    o = ob_scr[...]
    o = o * lax.rsqrt(_dot01_right(o * o, bo) * (1.0 / DV_A) + EPS) * gn_ref[...]
    res = o * _silu(_pad_rows(za_ref[...], tc))
    o_ref[...] = res[0:rows]
    sout_ref[...] = s_scr[...]


def _mix_c_kernel(pc_ref, zc_ref, sm_ref, conv0_ref, s0_ref, cw_ref, cb_ref, hp_ref, hpt_ref, gn_ref,
                  tril_ref, triu_ref, bo_ref, o_ref, sout_ref, ext_scr, s_scr, yb_scr, *, t_valid, rows):
    @pl.when(pl.program_id(1) == 0)
    def _():
        ext_scr[0:SUB, :] = conv0_ref[...]
        s_scr[...] = s0_ref[...]

    u = _conv_silu(_pad_rows(pc_ref[...], TC), ext_scr, cw_ref, cb_ref)
    xc = u[:, 0:D_INNER_C]
    bcm = u[:, D_INNER_C:D_INNER_C + G_C * N_C]
    ccm = u[:, D_INNER_C + G_C * N_C:CONV_C]
    bct = bcm.T
    zc = _pad_rows(zc_ref[...], TC)

    sm = _pad_rows(sm_ref[...], TC)
    smt = sm.T
    hp, hpt = hp_ref[...], hpt_ref[...]
    dt_col = _softplus(sm[:, SM_DT:SM_DT + H_C] + hp[3:4, 0:H_C])
    dt_row = _softplus(smt[SM_DT:SM_DT + H_C, :] + hpt[0:H_C, 3:4])
    if t_valid < TC:
        dt_col = jnp.where(_iota2(dt_col.shape, 0) < t_valid, dt_col, 0.0)
        dt_row = jnp.where(_iota2(dt_row.shape, 1) < t_valid, dt_row, 0.0)
    gcum_col = _dot01_left(tril_ref[...], -jnp.exp(hp[2:3, 0:H_C]) * dt_col)
    gcum_row = _dot01_right(-jnp.exp(hpt[0:H_C, 2:3]) * dt_row, triu_ref[...])

    incl = _iota2((TC, TC), 0) >= _iota2((TC, TC), 1)
    hg = H_C // G_C
    for g in range(G_C):
        cc_g = ccm[:, g * N_C:(g + 1) * N_C]
        gram = _bdot_nt(cc_g, bcm[:, g * N_C:(g + 1) * N_C])
        for hh in range(hg):
            h = g * hg + hh
            c0 = h * P_C
            gc = gcum_col[:, h:h + 1]
            gr = gcum_row[h:h + 1, :]
            dec = jnp.where(incl, jnp.exp(jnp.where(incl, gc - gr, 0.0)), 0.0)
            attn = gram * dt_row[h:h + 1, :] * dec
            xh = xc[:, c0:c0 + P_C]
            S = s_scr[h]
            o = jnp.exp(gc) * _bdot(cc_g, S) + _bdot(attn, xh)
            gl = gc[TC - 1:TC, :]
            s_scr[h] = S * jnp.exp(gl) + _bdot(bct[g * N_C:(g + 1) * N_C, :], xh * (dt_col[:, h:h + 1] * jnp.exp(gl - gc)))
            yb_scr[:, c0:c0 + P_C] = (o + hp[4:5, h:h + 1] * xh) * _silu(zc[:, c0:c0 + P_C])

    y = yb_scr[...]
    y = y * lax.rsqrt(_dot01_right(y * y, bo_ref[...]) * (1.0 / (D_INNER_C // G_C)) + EPS) * gn_ref[...]
    o_ref[...] = y[0:rows]
    sout_ref[...] = s_scr[...]


_D_LEVELS = 6


def _mix_d_kernel(dg_ref, s0_ref, lb_ref, gn_ref, tril_ref, sel_ref, bo_ref, o_ref, sout_ref, s_scr, ob_scr, *, t_valid, rows):
    CD = 64
    W = H_D * DK_D

    @pl.when(pl.program_id(1) == 0)
    def _():
        s_scr[...] = s0_ref[...]

    dg = _pad_rows(dg_ref[...], TC)
    q = _silu(dg[:, 0:W])
    fr = dg[:, W:2 * W]
    v = dg[:, 2 * W:3 * W]
    gd = dg[:, 3 * W:4 * W]
    lb = lb_ref[...]
    logf = jnp.log(jnp.maximum(lb + (1.0 - lb) * _sigmoid(fr), TINY))
    kd = (1.0 - lb) * _sigmoid(-fr)
    if t_valid < TC:
        ok = _iota2((TC, W), 0) < t_valid
        logf = jnp.where(ok, logf, 0.0)
        kd = jnp.where(ok, kd, 0.0)
    gcum = _dot01_left(tril_ref[...], logf)
    gm = _dot01_left(sel_ref[...], gcum)
    qdec = q * jnp.exp(gcum)
    nsc = -(-t_valid // CD)
    gl_rows = jnp.concatenate([jnp.broadcast_to(gcum[(s + 1) * CD - 1:(s + 1) * CD, :], (CD, W)) for s in range(TC // CD)], axis=0)
    kdt = (kd * jnp.exp(gl_rows - gcum)).T
    gct = gcum.T

    HS = H_D * CD
    head_lanes = (_iota2((HS, W), 0) >> 6) == (_iota2((HS, W), 1) >> 6)
    stack = lambda x: jnp.where(head_lanes, jnp.concatenate([x] * H_D, axis=0), 0.0)
    ci, jj = _iota2((HS, CD), 0) & (CD - 1), _iota2((HS, CD), 1)
    later = jnp.where(ci > jj, ci ^ jj, 0)
    lhs, rhs, msk = [q], [kd], [ci == jj]
    for lv in range(1, _D_LEVELS + 1):
        gml = gm[(lv - 1) * TC:lv * TC, :]
        lhs.append(q * jnp.exp(jnp.minimum(gcum - gml, 0.0)))
        rhs.append(kd * jnp.exp(jnp.minimum(gml - gcum, 0.0)))
        msk.append((later >> (lv - 1)) == 1)

    lane_head = _iota2((CD, W), 1) >> 6
    intra = []
    for sc in range(nsc):
        r0 = sc * CD
        attn = jnp.zeros((HS, CD), F32)
        for a, b, m in zip(lhs, rhs, msk):
            attn = attn + jnp.where(m, _bdot_nt(stack(a[r0:r0 + CD]), b[r0:r0 + CD]), 0.0)
        full = _bdot(attn, v[r0:r0 + CD])
        o = jnp.zeros((CD, W), F32)
        for h in range(H_D):
            o = o + jnp.where(lane_head == h, full[h * CD:(h + 1) * CD], 0.0)
        intra.append(o)
    for sc in range(nsc):
        r0 = sc * CD
        sbd = s_scr[...]
        ob_scr[r0:r0 + CD, :] = _bdot(qdec[r0:r0 + CD], sbd) + intra[sc]
        decay = jnp.exp(gct[:, r0 + CD - 1:r0 + CD])
        s_scr[...] = jnp.where(head_lanes, sbd * decay + _bdot(kdt[:, r0:r0 + CD], v[r0:r0 + CD]), 0.0)
    if nsc < TC // CD:
        ob_scr[nsc * CD:TC, :] = jnp.zeros((TC - nsc * CD, W), F32)

    o = ob_scr[...]
    o = o * lax.rsqrt(_dot01_right(o * o, bo_ref[...]) * (1.0 / DV_D) + EPS) * gn_ref[...]
    res = o * _silu(gd)
    o_ref[...] = res[0:rows]
    sout_ref[...] = s_scr[...]


def _np_consts():
    i = np.arange(TC)
    same64 = (i[:, None] // 64) == (i[None, :] // 64)
    tril2 = (same64 & (i[:, None] >= i[None, :])).astype(np.float32)
    tril1 = (i[:, None] >= i[None, :]).astype(np.float32)
    sel = np.zeros((_D_LEVELS * TC, TC), np.float32)
    for lv in range(1, _D_LEVELS + 1):
        m = ((i >> lv) << lv) + (1 << (lv - 1)) - 1
        sel[(lv - 1) * TC + i, m] = 1.0
    bo = lambda n, w: ((np.arange(n)[:, None] // w) == (np.arange(n)[None, :] // w)).astype(np.float32)
    ia = np.arange(TC_A)
    tril2a = (((ia[:, None] // 64) == (ia[None, :] // 64)) & (ia[:, None] >= ia[None, :])).astype(np.float32)
    return dict(tril2=tril2, triu2=tril2.T.copy(), tril1=tril1, triu1=tril1.T.copy(), sel=sel,
                tril2a=tril2a, triu2a=tril2a.T.copy(),
                bo64=bo(256, 64), bo256=bo(D_INNER_C, D_INNER_C // G_C))


def _bucket_starts():
    max_exact = N_BUCKETS // 2
    d = np.arange(0, 4 * MAX_DISTANCE)
    lr = np.log(np.maximum(d, 1).astype(np.float32) / max_exact) / math.log(MAX_DISTANCE / max_exact)
    large = np.minimum(max_exact + (np.maximum(lr, 0.0) * (N_BUCKETS - max_exact)).astype(np.int32), N_BUCKETS - 1)
    bucket = np.where(d < max_exact, d, large)
    return [int(np.argmax(bucket == b)) for b in range(N_BUCKETS)]


_BUCKET_STARTS = _bucket_starts()
_SAT_DIST = _BUCKET_STARTS[N_BUCKETS - 1]


def _bias_from_dist(d, rb_ref, h):
    out = jnp.full(d.shape, rb_ref[N_BUCKETS - 1, h], F32)
    for b in range(N_BUCKETS - 2, -1, -1):
        out = jnp.where(d < _BUCKET_STARTS[b + 1], rb_ref[b, h], out)
    return out


def _sortable(x):
    b = lax.bitcast_convert_type(x + 0.0, I32)
    return jnp.where(b < 0, b ^ 0x7FFFFFFF, b)


_NEG_KEY = int(np.array([NEG], np.float32).view(np.int32)[0]) ^ 0x7FFFFFFF
_INT_MIN = -2 ** 31
_MIN_NORMAL_BITS = 0x00800000


def _selected_i32(k, idx, thr, xthr):
    s = jnp.where(k > thr, 1, jnp.where(k == thr, jnp.where(idx <= xthr, 1, 0), 0))
    return jnp.where(k == _NEG_KEY, 0, s)


def _selected(k, idx, thr, xthr):
    return _selected_i32(k, idx, thr, xthr) > 0


def _const_spec(shape):
    nd = len(shape)
    return pl.BlockSpec(shape, lambda *_: (0,) * nd)


def _mix_call(kernel_fn, name, P, B, T, seq_specs, batch_args, const_args, out_w, st_tail, scratch, tc=TC):
    rows = min(T, tc)
    nc = max(T // tc, 1)
    assert rows * nc == T
    in_specs = [pl.BlockSpec((rows, w), functools.partial(lambda b, c, ci: (b * nc + c, ci), ci=off // w)) for off, w in seq_specs]
    args = [P] * len(seq_specs)
    for a in batch_args:
        nd = a.ndim
        in_specs.append(pl.BlockSpec((None,) + a.shape[1:], functools.partial(lambda b, c, nd: (b,) + (0,) * (nd - 1), nd=nd)))
        args.append(a)
    for a in const_args:
        in_specs.append(_const_spec(a.shape))
        args.append(a)
    st_nd = len(st_tail)
    return pl.pallas_call(
        functools.partial(kernel_fn, t_valid=rows, rows=rows),
        grid=(B, nc),
        in_specs=in_specs,
        out_specs=[pl.BlockSpec((rows, out_w), lambda b, c: (b * nc + c, 0)),
                   pl.BlockSpec((None,) + st_tail, lambda b, c: (b,) + (0,) * st_nd)],
        out_shape=[jax.ShapeDtypeStruct((B * T, out_w), F32), jax.ShapeDtypeStruct((B,) + st_tail, F32)],
        scratch_shapes=scratch,
        compiler_params=pltpu.CompilerParams(dimension_semantics=("arbitrary", "arbitrary"), vmem_limit_bytes=VMEM_LIMIT),
        name=name,
    )(*args)


CU = 4


def _dsa_prompt_kernel(rb_ref, pq_ref, pk_ref, o_ref,
                       kvt_scr, key_scr, lg_scr, bias_scr, acc_scr, sf_scr, *, n_sel, idx_bits):
    j = pl.program_id(1)
    nch = j + 1

    @pl.when((pl.program_id(0) == 0) & (j == 0))
    def _():
        srow, tcol = _iota2((QB, QB), 0), _iota2((QB, QB), 1)
        for h in range(H_B):
            bias_scr[h, 0] = jnp.full((QB, QB), rb_ref[N_BUCKETS - 1, h], F32)
            bias_scr[h, 1] = _bias_from_dist(tcol - srow + QB, rb_ref, h)
            bias_scr[h, 2] = _bias_from_dist(tcol - srow, rb_ref, h)

    @pl.when(j == 0)
    def _():
        def tr(c, _):
            r = pl.multiple_of(c * QB, QB)
            kvt_scr[c] = pk_ref[pl.ds(r, QB), PK_KV:PK_KV + LANE].T[DH_B:2 * DH_B, :].astype(BF16)
            return 0
        lax.fori_loop(0, pk_ref.shape[0] // QB, tr, 0)

    lanes = lambda rows, w: jnp.concatenate([rows[h * w:(h + 1) * w, :] for h in range(H_B)], axis=1)
    w4 = lanes(pq_ref[:, PQ_SM:PQ_SM + LANE].T[SM_WI:SM_WI + H_IDX, :], 1) * (H_IDX ** -0.5 * D_IDX ** -0.5)
    qi_rhs = lanes(pq_ref[:, PQ_QI:PQ_QI + LANE].T, D_IDX).astype(BF16)
    qb_rhs = (lanes(pq_ref[:, PQ_QB:PQ_QB + W_B].T, DH_B) * (DH_B ** -0.5)).astype(BF16)
    srow, tcol = _iota2((QB, QB), 0), _iota2((QB, QB), 1)
    heads = lambda x: [x[:, h * QB:(h + 1) * QB] for h in range(H_B)]

    ngrp = (nch + (CU - 1)) // CU

    def score_body(g, _):
        for u in range(CU):
            r = pl.multiple_of((g * CU + u) * QB, QB)
            ki = pk_ref[pl.ds(r, QB), PK_SM:PK_SM + LANE][:, SM_KI:SM_KI + D_IDX].astype(BF16)
            s = jnp.maximum(jnp.dot(ki, qi_rhs, preferred_element_type=F32), 0.0) * w4
            sh = heads(s)
            sc = (sh[0] + sh[1]) + (sh[2] + sh[3])
            sc = jnp.where(srow + r <= tcol + j * QB, sc, NEG)
            key_scr[pl.ds(r, QB), :] = _sortable(sc)
            sf_scr[pl.ds(r, QB), :] = sc + 0.0
        return 0

    lax.fori_loop(0, ngrp, score_body, 0)

    def count(pred):
        def body(g, acc):
            for u in range(CU):
                r = pl.multiple_of((g * CU + u) * QB, QB)
                acc = acc + pred(key_scr[pl.ds(r, QB), :], r).reshape(QB // SUB, SUB, QB).sum(axis=0)
            return acc
        return lax.fori_loop(0, ngrp, body, jnp.zeros((SUB, QB), I32)).sum(axis=0, keepdims=True)

    def count_ge_static(n_groups):
        def f(tf):
            accs = [jnp.zeros((SUB, QB), F32) for _ in range(2)]
            for c in range(n_groups * CU):
                w = jnp.where(sf_scr[c * QB:(c + 1) * QB, :] >= tf, 1.0, 0.0)
                accs[c % 2] = accs[c % 2] + w.reshape(QB // SUB, SUB, QB).sum(axis=0)
            return (accs[0] + accs[1]).sum(axis=0, keepdims=True)
        return f

    count_variants = [count_ge_static(n) for n in range(1, key_scr.shape[0] // (CU * QB) + 1)]

    def cnt_ge(t):
        tpos = jnp.where(t < _MIN_NORMAL_BITS, jnp.where(t > 0, _MIN_NORMAL_BITS, t), t)
        tf = lax.bitcast_convert_type(jnp.where(t < 0, t ^ 0x7FFFFFFF, tpos), F32)
        return lax.switch(ngrp - 1, count_variants, tf).astype(I32)

    c0 = cnt_ge(jnp.zeros((1, QB), I32))
    thr = jnp.where(c0 >= n_sel, 0, _INT_MIN).astype(I32)
    cthr = jnp.where(c0 >= n_sel, c0, ngrp * (CU * QB))

    def bit_body(i, carry):
        t, ct = carry
        trial = t | jnp.left_shift(jnp.int32(1), 30 - i)
        cnt = cnt_ge(trial)
        ok = cnt >= n_sel
        return jnp.where(ok, trial, t), jnp.where(ok, cnt, ct)

    thr, cthr = lax.fori_loop(0, 31, bit_body, (thr, cthr))

    @pl.when(jnp.max(cthr) > n_sel)
    def _():
        quota = n_sel - count(lambda k, r: jnp.where(k > thr, 1, 0))

        def idx_body(i, x):
            trial = x | jnp.left_shift(jnp.int32(1), idx_bits - 1 - i)
            below = count(lambda k, r: jnp.where(k == thr, jnp.where(srow + r < trial, 1, 0), 0))
            return jnp.where(below < quota, trial, x)

        xthr = lax.fori_loop(0, idx_bits, idx_body, jnp.zeros((1, QB), I32))

        def demote_body(g, _):
            for u in range(CU):
                r = pl.multiple_of((g * CU + u) * QB, QB)
                k = key_scr[pl.ds(r, QB), :]
                key_scr[pl.ds(r, QB), :] = jnp.where(k == thr, jnp.where(srow + r > xthr, k - 1, k), k)
            return 0

        lax.fori_loop(0, ngrp, demote_body, 0)

    thr_sel = jnp.maximum(thr, _NEG_KEY + 1)

    def logit_body(g, ms):
        ms = list(ms)
        for u in range(CU):
            c = g * CU + u
            r = pl.multiple_of(c * QB, QB)
            kc = pk_ref[pl.ds(r, QB), PK_KV:PK_KV + LANE][:, 0:DH_B].astype(BF16)
            l4 = heads(jnp.dot(kc, qb_rhs, preferred_element_type=F32))
            sel = key_scr[pl.ds(r, QB), :] >= thr_sel
            bidx = jnp.clip(c - j + 2, 0, 2)
            for h in range(H_B):
                l = jnp.where(sel, l4[h] + bias_scr[h, bidx], NEG)
                lg_scr[h, pl.ds(r, QB), :] = l
                ms[h] = jnp.maximum(ms[h], l.max(axis=0, keepdims=True))
        return tuple(ms)

    ms = lax.fori_loop(0, ngrp, logit_body, tuple(jnp.full((1, QB), NEG, F32) for _ in range(H_B)))
    acc_scr[...] = jnp.zeros(acc_scr.shape, F32)

    def pv_body(g, ss):
        ss = list(ss)
        acc = acc_scr[...]
        for u in range(CU):
            c = g * CU + u
            r = pl.multiple_of(c * QB, QB)
            es = [jnp.exp(lg_scr[h, pl.ds(r, QB), :] - ms[h]) for h in range(H_B)]
            acc = acc + jnp.dot(kvt_scr[c], jnp.concatenate(es, axis=1).astype(BF16), preferred_element_type=F32)
            for h in range(H_B):
                ss[h] = ss[h] + es[h].sum(axis=0, keepdims=True)
        acc_scr[...] = acc
        return tuple(ss)

    ss = lax.fori_loop(0, ngrp, pv_body, tuple(jnp.zeros((1, QB), F32) for _ in range(H_B)))
    acc = heads(acc_scr[...])
    ot = jnp.concatenate([acc[h] / ss[h] for h in range(H_B)], axis=0)
    o_ref[...] = ot.T


def _dsa_prompt(Pq, Pk, rel_bias, B, T):
    nb = T // QB
    assert nb % CU == 0
    n_sel = min(TOPK_MAX, T // 4)
    idx_bits = max(1, int(math.ceil(math.log2(T))))
    return pl.pallas_call(
        functools.partial(_dsa_prompt_kernel, n_sel=n_sel, idx_bits=idx_bits),
        grid=(B, nb),
        in_specs=[pl.BlockSpec(memory_space=pltpu.SMEM),
                  pl.BlockSpec((QB, PQ_W), lambda b, j: (b * nb + j, 0)),
                  pl.BlockSpec((T, PK_W), lambda b, j: (b, 0))],
        out_specs=pl.BlockSpec((QB, W_B), lambda b, j: (b * nb + j, 0)),
        out_shape=jax.ShapeDtypeStruct((B * T, W_B), F32),
        scratch_shapes=[pltpu.VMEM((nb, DH_B, QB), BF16), pltpu.VMEM((T, QB), I32), pltpu.VMEM((H_B, T, QB), F32),
                        pltpu.VMEM((H_B, 3, QB, QB), F32), pltpu.VMEM((DH_B, H_B * QB), F32), pltpu.VMEM((T, QB), F32)],
        compiler_params=pltpu.CompilerParams(dimension_semantics=("arbitrary", "arbitrary"), vmem_limit_bytes=VMEM_LIMIT),
        name="dsa_prompt",
    )(rel_bias, Pq, Pk)


PG = 16
GW = PG * PAGE_SIZE


def _stack_heads(x, w):
    return jnp.concatenate([x[:, h * w:(h + 1) * w] for h in range(x.shape[1] // w)], axis=0)


def _dsa_s_scores_kernel(pt_ref, pq_ref, *refs, ng, ds):
    page_refs, o_ref = refs[:PG], refs[PG]
    g = pl.program_id(1)
    sm = pq_ref[:, PQ_SM:PQ_SM + LANE]
    qst = _stack_heads(pq_ref[:, PQ_QI:PQ_QI + LANE], D_IDX)
    wcol = jnp.concatenate([sm[:, SM_WI + h:SM_WI + h + 1] for h in range(H_IDX)], axis=0) * (H_IDX ** -0.5 * D_IDX ** -0.5)

    def tile_scores(qk):
        s = jnp.maximum(qk, 0.0) * wcol
        out = s[0:ds]
        for h in range(1, H_IDX):
            out = out + s[h * ds:(h + 1) * ds]
        return out

    @pl.when(g < ng)
    def _():
        for i in range(PG):
            o_ref[:, i * PAGE_SIZE:(i + 1) * PAGE_SIZE] = tile_scores(_bdot(qst, page_refs[i][...]))

    @pl.when(g == ng)
    def _():
        sc = tile_scores(_bdot_nt(qst, _pad_rows(sm[:, SM_KI:SM_KI + D_IDX], PAGE_SIZE)))
        lane, row = _iota2(sc.shape, 1), _iota2(sc.shape, 0)
        o_ref[:, 0:PAGE_SIZE] = jnp.where(lane <= row, sc, NEG)
        o_ref[:, PAGE_SIZE:GW] = jnp.full((ds, GW - PAGE_SIZE), NEG, F32)


def _dsa_s_attn_kernel(pt_ref, rb_ref, sc_ref, pq_ref, pk_ref, *refs, ng, ds, n_sel, idx_bits):
    k_refs, v_refs = refs[:PG], refs[PG:2 * PG]
    o_ref, key_scr, thr_scr, bias_scr, m_scr, l_scr, acc_scr = refs[2 * PG:]
    g = pl.program_id(1)
    ntile = (ng + 1) * PG
    hr = H_B * ds

    @pl.when(g == 0)
    def _():
        for i in range(ntile):
            key_scr[:, i * LANE:(i + 1) * LANE] = _sortable(sc_ref[:, i * LANE:(i + 1) * LANE])
        lane = _iota2((ds, LANE), 1)

        def count(pred):
            accs = [jnp.zeros((ds, LANE), I32) for _ in range(4)]
            for i in range(ng * PG + 1):
                accs[i % 4] = accs[i % 4] + pred(key_scr[:, i * LANE:(i + 1) * LANE], i * LANE)
            return ((accs[0] + accs[1]) + (accs[2] + accs[3])).sum(axis=1, keepdims=True)

        cnt_ge = lambda t: count(lambda k, off: jnp.where(k >= t, 1, 0))
        c0 = cnt_ge(jnp.zeros((ds, 1), I32))
        thr = jnp.where(c0 >= n_sel, 0, _INT_MIN).astype(I32)
        cthr = jnp.where(c0 >= n_sel, c0, ng * GW + LANE)

        def bit_body(i, carry):
            t, ct = carry
            trial = t | jnp.left_shift(jnp.int32(1), 30 - i)
            cnt = cnt_ge(trial)
            ok = cnt >= n_sel
            return jnp.where(ok, trial, t), jnp.where(ok, cnt, ct)

        thr, cthr = lax.fori_loop(0, 31, bit_body, (thr, cthr))

        def tie_search():
            quota = n_sel - count(lambda k, off: jnp.where(k > thr, 1, 0))

            def idx_body(i, x):
                trial = x | jnp.left_shift(jnp.int32(1), idx_bits - 1 - i)
                below = count(lambda k, off: jnp.where(k == thr, jnp.where(lane + off < trial, 1, 0), 0))
                return jnp.where(below < quota, trial, x)

            return lax.fori_loop(0, idx_bits, idx_body, jnp.zeros((ds, 1), I32))

        xthr = lax.cond(jnp.max(cthr) > n_sel, tie_search, lambda: jnp.full((ds, 1), (1 << idx_bits) - 1, I32))
        thr_scr[0] = jnp.broadcast_to(thr, (ds, LANE))
        thr_scr[1] = jnp.broadcast_to(xthr, (ds, LANE))

        qrow = _iota2((ds, LANE), 0)
        for h in range(H_B):
            bias_scr[0, h * ds:(h + 1) * ds, :] = _bias_from_dist(PAGE_SIZE + qrow - lane, rb_ref, h)
            bias_scr[1, h * ds:(h + 1) * ds, :] = _bias_from_dist(qrow - lane, rb_ref, h)
            bias_scr[2, h * ds:(h + 1) * ds, :] = jnp.full((ds, LANE), rb_ref[N_BUCKETS - 1, h], F32)
        m_scr[...] = jnp.full(m_scr.shape, NEG, F32)
        l_scr[...] = jnp.zeros(l_scr.shape, F32)
        acc_scr[...] = jnp.zeros(acc_scr.shape, F32)

    qst = _stack_heads(pq_ref[:, PQ_QB:PQ_QB + W_B], DH_B) * (DH_B ** -0.5)
    thr, xthr = thr_scr[0], thr_scr[1]

    def select(tile):
        k = key_scr[:, pl.ds(pl.multiple_of(tile * LANE, LANE), LANE)]
        idx = _iota2((ds, LANE), 1) + tile * LANE
        sel = _selected_i32(k, idx, thr, xthr)
        return jnp.concatenate([sel] * H_B, axis=0) > 0

    def update(lg, sel, pv):
        m_old = m_scr[...]
        m_new = jnp.maximum(m_old, jnp.where(sel, lg, NEG).max(axis=1, keepdims=True))
        p = jnp.where(sel, jnp.exp(lg - m_new), 0.0)
        corr = jnp.exp(m_old - m_new)
        l_scr[...] = l_scr[...] * corr + p.sum(axis=1, keepdims=True)
        acc_scr[...] = acc_scr[...] * corr + pv(p)
        m_scr[...] = m_new

    @pl.when(g < ng)
    def _():
        lgs, sels = [], []
        for i in range(PG):
            lg = _bdot(qst, k_refs[i][...])
            if i == PG - 1:
                lg = lg + jnp.where(g == ng - 1, bias_scr[0], bias_scr[2])
            else:
                lg = lg + bias_scr[2]
            lgs.append(lg)
            sels.append(select(g * PG + i))
        vt = jnp.concatenate([v_refs[i][...] for i in range(PG)], axis=1)
        update(jnp.concatenate(lgs, axis=1), jnp.concatenate(sels, axis=1), lambda p: _bdot_nt(p, vt))

    @pl.when(g == ng)
    def _():
        kvn = _pad_rows(pk_ref[:, PK_KV:PK_KV + LANE], PAGE_SIZE)
        lg = _bdot_nt(qst, kvn[:, 0:DH_B]) + bias_scr[1]
        update(lg, select(ng * PG), lambda p: _bdot(p, kvn[:, DH_B:2 * DH_B]))
        o = acc_scr[...] / l_scr[...]
        o_ref[...] = jnp.concatenate([o[h * ds:(h + 1) * ds] for h in range(H_B)], axis=1)


def _dsa_sample(Pq, Pk, rel_bias, cache_k, cache_v, cache_kidx, page_table, layer, B, DS):
    n_pages = page_table.shape[1]
    past = n_pages * PAGE_SIZE
    assert n_pages % PG == 0
    ng = n_pages // PG
    n_sel = min(TOPK_MAX, (past + DS) // 4)
    idx_bits = int(math.ceil(math.log2(past + LANE)))
    s_pad = (ng + 1) * GW

    def page_spec(width, i):
        return pl.BlockSpec((None, None, width, PAGE_SIZE),
                            lambda b, g, pt: (layer, pt[b, jnp.minimum(g, ng - 1) * PG + i], 0, 0))

    row = lambda b, g, pt: (b, 0)
    cp = pltpu.CompilerParams(dimension_semantics=("arbitrary", "arbitrary"), vmem_limit_bytes=VMEM_LIMIT)
    scores = pl.pallas_call(
        functools.partial(_dsa_s_scores_kernel, ng=ng, ds=DS),
        grid_spec=pltpu.PrefetchScalarGridSpec(
            num_scalar_prefetch=1, grid=(B, ng + 1),
            in_specs=[pl.BlockSpec((DS, PQ_W), row)] + [page_spec(D_IDX, i) for i in range(PG)],
            out_specs=pl.BlockSpec((None, DS, GW), lambda b, g, pt: (b, 0, g))),
        out_shape=jax.ShapeDtypeStruct((B, DS, s_pad), F32),
        compiler_params=cp, name="dsa_sample_scores",
    )(page_table, Pq, *([cache_kidx] * PG))

    return pl.pallas_call(
        functools.partial(_dsa_s_attn_kernel, ng=ng, ds=DS, n_sel=n_sel, idx_bits=idx_bits),
        grid_spec=pltpu.PrefetchScalarGridSpec(
            num_scalar_prefetch=1, grid=(B, ng + 1),
            in_specs=[pl.BlockSpec(memory_space=pltpu.SMEM),
                      pl.BlockSpec((None, DS, s_pad), lambda b, g, pt: (b, 0, 0)),
                      pl.BlockSpec((DS, PQ_W), row), pl.BlockSpec((DS, PK_W), row)]
                     + [page_spec(DH_B, i) for i in range(PG)] + [page_spec(DH_B, i) for i in range(PG)],
            out_specs=pl.BlockSpec((DS, W_B), lambda b, g, pt: (b, 0)),
            scratch_shapes=[pltpu.VMEM((DS, s_pad), I32), pltpu.VMEM((2, DS, LANE), I32), pltpu.VMEM((3, H_B * DS, LANE), F32),
                            pltpu.VMEM((H_B * DS, 1), F32), pltpu.VMEM((H_B * DS, 1), F32), pltpu.VMEM((H_B * DS, DH_B), F32)]),
        out_shape=jax.ShapeDtypeStruct((B * DS, W_B), F32),
        compiler_params=cp, name="dsa_sample_attn",
    )(page_table, rel_bias, scores, Pq, Pk, *([cache_k] * PG), *([cache_v] * PG))


def _merge_kernel(x_ref, oa_ref, ob_ref, oc_ref, od_ref, gate_ref, wb_ref, wo_ref, o_ref):
    m = None
    r0 = 0
    for br, ref in enumerate((oa_ref, ob_ref, oc_ref, od_ref)):
        w = ref.shape[1]
        t = _sigmoid(gate_ref[:, br * D_MODEL:(br + 1) * D_MODEL]) * jnp.dot(ref[...].astype(BF16), wb_ref[r0:r0 + w, :], preferred_element_type=F32)
        m = t if m is None else m + t
        r0 += w
    o_ref[...] = x_ref[...] + jnp.dot(m.astype(BF16), wo_ref[...], preferred_element_type=F32)


def _merge(x2d, oa, ob, oc, od, P, wb, wo):
    n = x2d.shape[0]
    tm = min(n, 256)
    row = lambda i: (i, 0)
    return pl.pallas_call(
        _merge_kernel,
        grid=(n // tm,),
        in_specs=[pl.BlockSpec((tm, D_MODEL), row), pl.BlockSpec((tm, W_A), row), pl.BlockSpec((tm, W_B), row),
                  pl.BlockSpec((tm, W_C), row), pl.BlockSpec((tm, W_D), row),
                  pl.BlockSpec((tm, N_BRANCH * D_MODEL), lambda i: (i, OFF_GATE // (N_BRANCH * D_MODEL))),
                  _const_spec((MIX_W, D_MODEL)), _const_spec((D_MODEL, D_MODEL))],
        out_specs=pl.BlockSpec((tm, D_MODEL), row),
        out_shape=jax.ShapeDtypeStruct((n, D_MODEL), F32),
        compiler_params=pltpu.CompilerParams(dimension_semantics=("arbitrary",), vmem_limit_bytes=VMEM_LIMIT),
        name="merge",
    )(x2d, oa, ob, oc, od, P, wb, wo)


def _mlp_kernel(x_ref, g_ref, wu_ref, wd_ref, gf_ref, *refs, final):
    if final:
        o_ref, y_ref, h_scr, acc_scr = refs
    else:
        o_ref, h_scr, acc_scr = refs
    f = pl.program_id(1)

    @pl.when(f == 0)
    def _():
        x = x_ref[...]
        r = lax.rsqrt(jnp.mean(x * x, axis=-1, keepdims=True) + EPS)
        h_scr[...] = ((x * r) * g_ref[...]).astype(BF16)
        acc_scr[...] = jnp.zeros(acc_scr.shape, F32)

    a = jnp.maximum(jnp.dot(h_scr[...], wu_ref[...], preferred_element_type=F32), 0.0)
    acc_scr[...] += jnp.dot((a * a).astype(BF16), wd_ref[...], preferred_element_type=F32)

    @pl.when(f == pl.num_programs(1) - 1)
    def _():
        out = x_ref[...] + acc_scr[...]
        o_ref[...] = out
        if final:
            r = lax.rsqrt(jnp.mean(out * out, axis=-1, keepdims=True) + EPS)
            y_ref[...] = (out * r) * gf_ref[...]


def _mlp(x2d, g, wu, wd, gf, final):
    n = x2d.shape[0]
    tm = min(n, 512)
    tf = 1024
    row = lambda i, f: (i, 0)
    out_spec = pl.BlockSpec((tm, D_MODEL), row)
    shp = jax.ShapeDtypeStruct((n, D_MODEL), F32)
    return pl.pallas_call(
        functools.partial(_mlp_kernel, final=final),
        grid=(n // tm, D_FF // tf),
        in_specs=[pl.BlockSpec((tm, D_MODEL), row), _const_spec((1, D_MODEL)),
                  pl.BlockSpec((D_MODEL, tf), lambda i, f: (0, f)), pl.BlockSpec((tf, D_MODEL), lambda i, f: (f, 0)),
                  _const_spec((1, D_MODEL))],
        out_specs=[out_spec, out_spec] if final else [out_spec],
        out_shape=[shp, shp] if final else [shp],
        scratch_shapes=[pltpu.VMEM((tm, D_MODEL), BF16), pltpu.VMEM((tm, D_MODEL), F32)],
        compiler_params=pltpu.CompilerParams(dimension_semantics=("arbitrary", "arbitrary"), vmem_limit_bytes=VMEM_LIMIT),
        name="mlp_final" if final else "mlp",
    )(x2d, g.reshape(1, D_MODEL), wu, wd, gf.reshape(1, D_MODEL))


def _layer(x, conv_state, s_delta, s_ssm, s_hgrn, lw, consts, sample_ctx, final):
    B, T, _ = x.shape
    x2 = x.reshape(B * T, D_MODEL)
    P, Pq, Pk = _in_proj(x2, lw['norm_mix'], lw['w_in'])

    def conv0(lo, hi):
        if conv_state is None:
            return jnp.zeros((B, SUB, hi - lo), F32)
        return jnp.concatenate([jnp.zeros((B, SUB - (CONV_W - 1), hi - lo), F32), conv_state[:, :, lo:hi]], axis=1)

    zeros_state = lambda n: jnp.zeros((B, n, 64, 64), F32)
    c = consts
    half = CONV_CH // 2
    assert CONV_A == half and OFF_PC % half == 0
    oa, s_a = _mix_call(
        _mix_a_kernel, "mix_a", P, B, T,
        [(OFF_PC, half), (OFF_ZA, W_A), (OFF_SM, LANE)],
        [conv0(0, CONV_A), zeros_state(H_A) if s_delta is None else s_delta],
        [lw['conv_w'][:, 0:CONV_A], lw['conv_b'][None, 0:CONV_A], lw['hp'], lw['hpt'], lw['gn_a'], c['tril2a'], c['triu2a'], c['bo64']],
        W_A, (H_A, DK_A, DV_A),
        [pltpu.VMEM((TC_A + SUB, CONV_A), F32), pltpu.VMEM((H_A, DK_A, DV_A), F32), pltpu.VMEM((TC_A, W_A), F32)],
        tc=TC_A)
    oc, s_c = _mix_call(
        _mix_c_kernel, "mix_c", P, B, T,
        [(OFF_PC + half, half), (OFF_ZC, W_C), (OFF_SM, LANE)],
        [conv0(CONV_A, CONV_CH), zeros_state(H_C) if s_ssm is None else s_ssm],
        [lw['conv_w'][:, CONV_A:], lw['conv_b'][None, CONV_A:], lw['hp'], lw['hpt'], lw['gn_c'], c['tril1'], c['triu1'], c['bo256']],
        W_C, (H_C, N_C, P_C),
        [pltpu.VMEM((TC + SUB, CONV_C), F32), pltpu.VMEM((H_C, N_C, P_C), F32), pltpu.VMEM((TC, W_C), F32)])
    sbd0 = jnp.zeros((B, H_D, DK_D, H_D, DV_D), F32)
    if s_hgrn is not None:
        for h in range(H_D):
            sbd0 = sbd0.at[:, h, :, h, :].set(s_hgrn[:, h])
    sbd0 = sbd0.reshape(B, H_D * DK_D, W_D)
    od, sbd = _mix_call(
        _mix_d_kernel, "mix_d", P, B, T,
        [(OFF_D, 4 * W_D)],
        [sbd0],
        [lw['lb'], lw['gn_d'], c['tril2'], c['sel'], c['bo64']],
        W_D, (H_D * DK_D, W_D),
        [pltpu.VMEM((H_D * DK_D, W_D), F32), pltpu.VMEM((TC, W_D), F32)])
    sbd5 = sbd.reshape(B, H_D, DK_D, H_D, DV_D)
    s_d = jnp.stack([sbd5[:, h, :, h, :] for h in range(H_D)], axis=1)
    if sample_ctx is None:
        ob = _dsa_prompt(Pq, Pk, lw['rel_bias'], B, T)
    else:
        ob = _dsa_sample(Pq, Pk, lw['rel_bias'], sample_ctx['cache_k'], sample_ctx['cache_v'], sample_ctx['cache_kidx'],
                         sample_ctx['page_table'], sample_ctx['layer'], B, T)
    x1 = _merge(x2, oa, ob, oc, od, P, lw['w_branch'], lw['w_out'])
    outs = _mlp(x1, lw['norm_mlp'], lw['w_up'], lw['w_down'], lw['norm_final'], final)
    xo = outs[0].reshape(B, T, D_MODEL)
    y = outs[1].reshape(B, T, D_MODEL) if final else None
    P3 = P.reshape(B, T, P_PAD)
    Pk3 = Pk.reshape(B, T, PK_W)
    states = (Pk3[:, :, PK_KV:PK_KV + DH_B], Pk3[:, :, PK_KV + DH_B:PK_KV + 2 * DH_B], Pk3[:, :, PK_SM + SM_KI:PK_SM + SM_KI + D_IDX],
              P3[:, T - (CONV_W - 1):, OFF_PC:OFF_PC + CONV_CH], s_a, s_c, s_d)
    return xo, y, states


def kernel(x_prompt, x_sample, cache_k, cache_v, cache_kidx, state_conv, state_delta, state_ssm, state_hgrn, page_table, norm_mix, w_in, conv_w, conv_b, a_log_a, dt_bias_a, gnorm_a, rel_bias, a_log_c, dt_bias_c, d_skip_c, gnorm_c, hgrn_gamma, gnorm_d, w_branch, w_out, norm_mlp, w_up, w_down, norm_final):
    depth = w_in.shape[0]
    npc = _np_consts()
    consts = {k: jnp.asarray(v, BF16) for k, v in npc.items()}
    pg = jax.nn.softmax(hgrn_gamma.astype(F32), axis=0)
    lower_bounds = jnp.cumsum(pg, axis=0) - pg[0]
    cache_kt, cache_vt, cache_kit = (jnp.swapaxes(c, 2, 3) for c in (cache_k, cache_v, cache_kidx))
    yp, ys = x_prompt, x_sample
    new_p, new_s = [], []
    for l in range(depth):
        hp = jnp.zeros((SUB, LANE), F32)
        hp = hp.at[0, 0:H_A].set(a_log_a[l]).at[1, 0:H_A].set(dt_bias_a[l])
        hp = hp.at[2, 0:H_C].set(a_log_c[l]).at[3, 0:H_C].set(dt_bias_c[l]).at[4, 0:H_C].set(d_skip_c[l])
        lw = dict(norm_mix=norm_mix[l], w_in=_permute_w_in(w_in[l]), conv_w=conv_w[l], conv_b=conv_b[l],
                  hp=hp, hpt=hp.T, gn_a=jnp.tile(gnorm_a[l], H_A)[None, :], gn_c=gnorm_c[l][None, :],
                  gn_d=jnp.tile(gnorm_d[l], H_D)[None, :], lb=lower_bounds[l][None, :], rel_bias=rel_bias,
                  w_branch=w_branch[l].astype(BF16), w_out=w_out[l].astype(BF16), norm_mlp=norm_mlp[l],
                  w_up=w_up[l].astype(BF16), w_down=w_down[l].astype(BF16), norm_final=norm_final)
        final = l == depth - 1
        yp, yp_n, st_p = _layer(yp, None, None, None, None, lw, consts, None, final)
        ctx = dict(cache_k=cache_kt, cache_v=cache_vt, cache_kidx=cache_kit, page_table=page_table, layer=l)
        ys, ys_n, st_s = _layer(ys, state_conv[l], state_delta[l], state_ssm[l], state_hgrn[l], lw, consts, ctx, final)
        new_p.append(st_p)
        new_s.append(st_s)
    stack = lambda per_layer: [jnp.stack(items) for items in zip(*per_layer)]
    return (yp_n, ys_n, *stack(new_p), *stack(new_s))
```

```python
import functools
import math

import numpy as np
import jax
import jax.numpy as jnp
from jax import lax
from jax.experimental import pallas as pl
from jax.experimental.pallas import tpu as pltpu

F32 = jnp.float32
BF16 = jnp.bfloat16
I32 = jnp.int32

D_MODEL = 1024
PAGE_SIZE = 128
EPS = 1e-6
NEG = -1e30
TINY = 1e-30
CONV_W = 4
D_FF = 4 * D_MODEL
N_BRANCH = 4
H_A, DK_A, DV_A = 4, 64, 64
H_B, DH_B = 4, 64
H_IDX, D_IDX = 4, 32
TOPK_MAX = 256
N_BUCKETS = 32
MAX_DISTANCE = 128
H_C, P_C, N_C, G_C = 8, 64, 64, 2
H_D, DK_D, DV_D = 4, 64, 64
QK_A = H_A * DK_A
CONV_A = 2 * QK_A + H_A * DV_A
D_INNER_C = H_C * P_C
CONV_C = D_INNER_C + 2 * G_C * N_C
CONV_CH = CONV_A + CONV_C
W_A, W_B, W_C, W_D = H_A * DV_A, H_B * DH_B, D_INNER_C, H_D * DV_D
MIX_W = W_A + W_B + W_C + W_D
IN_WIDTHS = (CONV_CH, W_A, H_A, H_A, W_B, DH_B, DH_B, H_IDX * D_IDX, D_IDX, H_IDX, D_INNER_C, H_C,
             H_D * DK_D, H_D * DK_D, W_D, W_D, N_BRANCH * D_MODEL)
P_IN = sum(IN_WIDTHS)

LANE = 128
SUB = 8
TC = 128
TC_A = 256
QB = 128
VMEM_LIMIT = 48 * 1024 * 1024

OFF_GATE, OFF_D, OFF_ZC, OFF_ZA, OFF_QB, OFF_PC, OFF_KV, OFF_QI, OFF_SM = 0, 4096, 5120, 5632, 5888, 6144, 7680, 7808, 7936
P_PAD = 8064
SM_KI, SM_WI, SM_AA, SM_BA, SM_DT = 0, 32, 36, 40, 44


def _src_offsets():
    offs, o = [], 0
    for w in IN_WIDTHS:
        offs.append(o)
        o += w
    return offs


def _permute_w_in(w):
    (o_pc, o_za, o_aa, o_ba, o_qb, o_kb, o_vb, o_qi, o_ki, o_wi, o_zc, o_dt, o_qd, o_fd, o_id, o_gd, o_gate) = _src_offsets()
    wt = jnp.swapaxes(w, 0, 1)
    seg = lambda o, n: wt[o:o + n, :]
    small = jnp.concatenate([seg(o_ki, D_IDX), seg(o_wi, H_IDX), seg(o_aa, H_A), seg(o_ba, H_A), seg(o_dt, H_C),
                             jnp.zeros((LANE - (D_IDX + H_IDX + 2 * H_A + H_C), w.shape[0]), w.dtype)], axis=0)
    out = jnp.concatenate([
        seg(o_gate, N_BRANCH * D_MODEL),
        seg(o_qd, 4 * W_D),
        seg(o_zc, D_INNER_C), seg(o_za, W_A), seg(o_qb, W_B), seg(o_pc, CONV_CH),
        seg(o_kb, 2 * DH_B),
        seg(o_qi, H_IDX * D_IDX), small], axis=0)
    assert out.shape[0] == P_PAD
    return out.astype(BF16)


def _bdot(a, b):
    return jnp.dot(a.astype(BF16), b.astype(BF16), preferred_element_type=F32)


def _bdot_nt(a, b):
    return lax.dot_general(a.astype(BF16), b.astype(BF16), (((1,), (1,)), ((), ())), preferred_element_type=F32)


def _split2(a):
    hi = a.astype(BF16)
    lo = (a - hi.astype(F32)).astype(BF16)
    return hi, lo


def _split3(a):
    hi = a.astype(BF16)
    r = a - hi.astype(F32)
    mid = r.astype(BF16)
    lo = (r - mid.astype(F32)).astype(BF16)
    return hi, mid, lo


def _dot01_left(m01, x):
    hi, mid, lo = _split3(x)
    d = lambda p: jnp.dot(m01, p, preferred_element_type=F32)
    return d(hi) + (d(mid) + d(lo))


def _dot01_right(x, m01):
    hi, mid, lo = _split3(x)
    d = lambda p: jnp.dot(p, m01, preferred_element_type=F32)
    return d(hi) + (d(mid) + d(lo))


def _xdot(a, b):
    ah, al = _split2(a)
    bh, bl = _split2(b)
    d = lambda p, q: jnp.dot(p, q, preferred_element_type=F32)
    return d(ah, bh) + (d(ah, bl) + d(al, bh))


def _sigmoid(x):
    return 1.0 / (1.0 + jnp.exp(-x))


def _silu(x):
    return x * _sigmoid(x)


def _softplus(x):
    return jnp.maximum(x, 0.0) + jnp.log(1.0 + jnp.exp(-jnp.abs(x)))


def _iota2(shape, dim):
    return lax.broadcasted_iota(I32, shape, dim)


def _pad_rows(x, rows):
    if x.shape[0] == rows:
        return x
    return jnp.concatenate([x, jnp.zeros((rows - x.shape[0],) + x.shape[1:], x.dtype)], axis=0)


def _tri_inv_many(Ls, top):
    n = Ls[0].shape[0]
    ii, jj = _iota2((n, n), 0), _iota2((n, n), 1)
    xor = ii ^ jj
    eye = jnp.where(ii == jj, 1.0, 0.0)
    Ns = [jnp.where((xor >> 3) == 0, -L, 0.0) for L in Ls]
    Xs = [eye + N for N in Ns]
    N2s = [_bdot(N, N) for N in Ns]
    Xs = [X + _bdot(X, N2) for X, N2 in zip(Xs, N2s)]
    N4s = [_bdot(N2, N2) for N2 in N2s]
    Xs = [X + _bdot(X, N4) for X, N4 in zip(Xs, N4s)]
    sh = 4
    while (1 << sh) <= top:
        XBs = [_bdot(X, jnp.where((xor >> (sh - 1)) == 1, L, 0.0)) for X, L in zip(Xs, Ls)]
        Xs = [X - _bdot(XB, X) for X, XB in zip(Xs, XBs)]
        sh += 1
    return Xs


def _conv_silu(pc, ext_scr, cw_ref, cb_ref):
    tc = pc.shape[0]
    ext_scr[SUB:SUB + tc, :] = pc
    y = cb_ref[...]
    for j in range(CONV_W):
        y = y + ext_scr[SUB - (CONV_W - 1) + j:SUB - (CONV_W - 1) + j + tc, :] * cw_ref[j:j + 1, :]
    ext_scr[0:SUB, :] = ext_scr[tc:tc + SUB, :]
    return _silu(y)


TN_IN = 1152
PQ_W = W_B + 2 * LANE
PK_W = 2 * LANE
PQ_QB, PQ_QI, PQ_SM = 0, W_B, W_B + LANE
PK_KV, PK_SM = 0, LANE
_COMPACT_COPIES = (("pq", PQ_QB, OFF_QB, W_B), ("pq", PQ_QI, OFF_QI, LANE), ("pq", PQ_SM, OFF_SM, LANE),
                   ("pk", PK_KV, OFF_KV, LANE), ("pk", PK_SM, OFF_SM, LANE))


def _in_proj_kernel(x_ref, g_ref, w_ref, o_ref, pq_ref, pk_ref, h_scr):
    j = pl.program_id(1)

    @pl.when(j == 0)
    def _():
        x = x_ref[...]
        r = lax.rsqrt(jnp.mean(x * x, axis=-1, keepdims=True) + EPS)
        h_scr[...] = ((x * r) * g_ref[...]).astype(BF16)

    o_ref[...] = lax.dot_general(h_scr[...], w_ref[...], (((1,), (1,)), ((), ())), preferred_element_type=F32)

    def copy(dst_ref, dst, src, width):
        tile, lo = divmod(src, TN_IN)
        assert lo + width <= TN_IN

        @pl.when(j == tile)
        def _():
            dst_ref[:, dst:dst + width] = o_ref[:, lo:lo + width]

    for name, dst, src, width in _COMPACT_COPIES:
        copy(pq_ref if name == "pq" else pk_ref, dst, src, width)


def _in_proj(x2d, g, w_perm_t):
    n = x2d.shape[0]
    tm = min(n, 1024)
    tn = TN_IN
    return pl.pallas_call(
        _in_proj_kernel,
        grid=(n // tm, P_PAD // tn),
        in_specs=[pl.BlockSpec((tm, D_MODEL), lambda i, j: (i, 0)),
                  pl.BlockSpec((1, D_MODEL), lambda i, j: (0, 0)),
                  pl.BlockSpec((tn, D_MODEL), lambda i, j: (j, 0))],
        out_specs=[pl.BlockSpec((tm, tn), lambda i, j: (i, j)),
                   pl.BlockSpec((tm, PQ_W), lambda i, j: (i, 0)), pl.BlockSpec((tm, PK_W), lambda i, j: (i, 0))],
        out_shape=[jax.ShapeDtypeStruct((n, P_PAD), F32), jax.ShapeDtypeStruct((n, PQ_W), F32),
                   jax.ShapeDtypeStruct((n, PK_W), F32)],
        scratch_shapes=[pltpu.VMEM((tm, D_MODEL), BF16)],
        compiler_params=pltpu.CompilerParams(dimension_semantics=("arbitrary", "arbitrary"), vmem_limit_bytes=VMEM_LIMIT),
        name="in_proj",
    )(x2d, g.reshape(1, D_MODEL), w_perm_t)


def _mix_a_kernel(pc_ref, za_ref, sm_ref, conv0_ref, s0_ref, cw_ref, cb_ref, hp_ref, hpt_ref, gn_ref,
                  tril_ref, triu_ref, bo_ref, o_ref, sout_ref, ext_scr, s_scr, ob_scr, *, t_valid, rows):
    CA = 64
    tc = ob_scr.shape[0]

    @pl.when(pl.program_id(1) == 0)
    def _():
        ext_scr[0:SUB, :] = conv0_ref[...]
        s_scr[...] = s0_ref[...]

    u = _conv_silu(_pad_rows(pc_ref[...], tc), ext_scr, cw_ref, cb_ref)
    bo = bo_ref[...]
    q_raw, k_raw, va = u[:, 0:QK_A], u[:, QK_A:2 * QK_A], u[:, 2 * QK_A:CONV_A]
    qq = q_raw * lax.rsqrt(_dot01_right(q_raw * q_raw, bo) + EPS) * (DK_A ** -0.5)
    ka = k_raw * lax.rsqrt(_dot01_right(k_raw * k_raw, bo) + EPS)
    kat = ka.T

    sm = _pad_rows(sm_ref[...], tc)
    smt = sm.T
    hp, hpt = hp_ref[...], hpt_ref[...]
    g_col = -jnp.exp(hp[0:1, 0:H_A]) * _softplus(sm[:, SM_AA:SM_AA + H_A] + hp[1:2, 0:H_A])
    beta = _sigmoid(sm[:, SM_BA:SM_BA + H_A])
    g_row = -jnp.exp(hpt[0:H_A, 0:1]) * _softplus(smt[SM_AA:SM_AA + H_A, :] + hpt[0:H_A, 1:2])
    if t_valid < tc:
        g_col = jnp.where(_iota2(g_col.shape, 0) < t_valid, g_col, 0.0)
        beta = jnp.where(_iota2(beta.shape, 0) < t_valid, beta, 0.0)
        g_row = jnp.where(_iota2(g_row.shape, 1) < t_valid, g_row, 0.0)
    gcum_col = _dot01_left(tril_ref[...], g_col)
    gcum_row = _dot01_right(g_row, triu_ref[...])

    HS = H_A * CA
    ii, jj = _iota2((HS, HS), 0), _iota2((HS, HS), 1)
    same = (ii >> 6) == (jj >> 6)
    incl, strict = same & (ii >= jj), same & (ii > jj)
    nsc = -(-t_valid // CA)
    pre = []
    for sc in range(nsc):
        r0 = sc * CA
        stack = lambda x, w: jnp.concatenate([x[r0:r0 + CA, h * w:(h + 1) * w] for h in range(H_A)], axis=0)
        ks, qs, vs = stack(ka, DK_A), stack(qq, DK_A), stack(va, DV_A)
        bcol, gc = stack(beta, 1), stack(gcum_col, 1)
        gr = jnp.concatenate([gcum_row[h:h + 1, r0:r0 + CA] for h in range(H_A)], axis=1)
        e = jnp.exp(jnp.where(incl, gc - gr, 0.0))
        kb = ks * bcol
        eg = jnp.exp(gc)
        pre.append(dict(r0=r0, gc=gc, qe=qs * eg, low=jnp.where(strict, _bdot_nt(kb, ks) * e, 0.0),
                        rhs=jnp.concatenate([vs * bcol, kb * eg], axis=1),
                        attn=jnp.where(incl, _bdot_nt(qs, ks) * e, 0.0)))
    tinvs = _tri_inv_many([p['low'] for p in pre], CA)
    sols = [_xdot(t, p['rhs']) for t, p in zip(tinvs, pre)]
    hs = lambda x, h: x[h * CA:(h + 1) * CA]
    for p, sol in zip(pre, sols):
        r0, gc = p['r0'], p['gc']
        states = [s_scr[h] for h in range(H_A)]
        v_new = jnp.concatenate([hs(sol, h)[:, 0:DV_A] - _bdot(hs(sol, h)[:, DV_A:], states[h]) for h in range(H_A)], axis=0)
        o = jnp.concatenate([_bdot(hs(p['qe'], h), states[h]) for h in range(H_A)], axis=0) + _bdot(p['attn'], v_new)
        for h in range(H_A):
            gch = hs(gc, h)
            gl = gch[CA - 1:CA, :]
            s_scr[h] = states[h] * jnp.exp(gl) + _bdot(kat[h * DK_A:(h + 1) * DK_A, r0:r0 + CA], hs(v_new, h) * jnp.exp(gl - gch))
            ob_scr[r0:r0 + CA, h * DV_A:(h + 1) * DV_A] = hs(o, h)
    if nsc * CA < tc:
        ob_scr[nsc * CA:tc, :] = jnp.zeros((tc - nsc * CA, W_A), F32)

    o = ob_scr[...]
    o = o * lax.rsqrt(_dot01_right(o * o, bo) * (1.0 / DV_A) + EPS) * gn_ref[...]
    res = o * _silu(_pad_rows(za_ref[...], tc))
    o_ref[...] = res[0:rows]
    sout_ref[...] = s_scr[...]


def _mix_c_kernel(pc_ref, zc_ref, sm_ref, conv0_ref, s0_ref, cw_ref, cb_ref, hp_ref, hpt_ref, gn_ref,
                  tril_ref, triu_ref, bo_ref, o_ref, sout_ref, ext_scr, s_scr, yb_scr, *, t_valid, rows):
    @pl.when(pl.program_id(1) == 0)
    def _():
        ext_scr[0:SUB, :] = conv0_ref[...]
        s_scr[...] = s0_ref[...]

    u = _conv_silu(_pad_rows(pc_ref[...], TC), ext_scr, cw_ref, cb_ref)
    xc = u[:, 0:D_INNER_C]
    bcm = u[:, D_INNER_C:D_INNER_C + G_C * N_C]
    ccm = u[:, D_INNER_C + G_C * N_C:CONV_C]
    bct = bcm.T
    zc = _pad_rows(zc_ref[...], TC)

    sm = _pad_rows(sm_ref[...], TC)
    smt = sm.T
    hp, hpt = hp_ref[...], hpt_ref[...]
    dt_col = _softplus(sm[:, SM_DT:SM_DT + H_C] + hp[3:4, 0:H_C])
    dt_row = _softplus(smt[SM_DT:SM_DT + H_C, :] + hpt[0:H_C, 3:4])
    if t_valid < TC:
        dt_col = jnp.where(_iota2(dt_col.shape, 0) < t_valid, dt_col, 0.0)
        dt_row = jnp.where(_iota2(dt_row.shape, 1) < t_valid, dt_row, 0.0)
    gcum_col = _dot01_left(tril_ref[...], -jnp.exp(hp[2:3, 0:H_C]) * dt_col)
    gcum_row = _dot01_right(-jnp.exp(hpt[0:H_C, 2:3]) * dt_row, triu_ref[...])

    incl = _iota2((TC, TC), 0) >= _iota2((TC, TC), 1)
    hg = H_C // G_C
    for g in range(G_C):
        cc_g = ccm[:, g * N_C:(g + 1) * N_C]
        gram = _bdot_nt(cc_g, bcm[:, g * N_C:(g + 1) * N_C])
        for hh in range(hg):
            h = g * hg + hh
            c0 = h * P_C
            gc = gcum_col[:, h:h + 1]
            gr = gcum_row[h:h + 1, :]
            dec = jnp.where(incl, jnp.exp(jnp.where(incl, gc - gr, 0.0)), 0.0)
            attn = gram * dt_row[h:h + 1, :] * dec
            xh = xc[:, c0:c0 + P_C]
            S = s_scr[h]
            o = jnp.exp(gc) * _bdot(cc_g, S) + _bdot(attn, xh)
            gl = gc[TC - 1:TC, :]
            s_scr[h] = S * jnp.exp(gl) + _bdot(bct[g * N_C:(g + 1) * N_C, :], xh * (dt_col[:, h:h + 1] * jnp.exp(gl - gc)))
            yb_scr[:, c0:c0 + P_C] = (o + hp[4:5, h:h + 1] * xh) * _silu(zc[:, c0:c0 + P_C])

    y = yb_scr[...]
    y = y * lax.rsqrt(_dot01_right(y * y, bo_ref[...]) * (1.0 / (D_INNER_C // G_C)) + EPS) * gn_ref[...]
    o_ref[...] = y[0:rows]
    sout_ref[...] = s_scr[...]


_D_LEVELS = 6


def _mix_d_kernel(dg_ref, s0_ref, lb_ref, gn_ref, tril_ref, sel_ref, bo_ref, o_ref, sout_ref, s_scr, ob_scr, *, t_valid, rows):
    CD = 64
    W = H_D * DK_D

    @pl.when(pl.program_id(1) == 0)
    def _():
        s_scr[...] = s0_ref[...]

    tc = ob_scr.shape[0]
    dg = _pad_rows(dg_ref[...], tc)
    q = _silu(dg[:, 0:W])
    fr = dg[:, W:2 * W]
    v = dg[:, 2 * W:3 * W]
    gd = dg[:, 3 * W:4 * W]
    lb = lb_ref[...]
    logf = jnp.log(jnp.maximum(lb + (1.0 - lb) * _sigmoid(fr), TINY))
    kd = (1.0 - lb) * _sigmoid(-fr)
    if t_valid < tc:
        ok = _iota2((tc, W), 0) < t_valid
        logf = jnp.where(ok, logf, 0.0)
        kd = jnp.where(ok, kd, 0.0)
    gcum = _dot01_left(tril_ref[...], logf)
    gm = _dot01_left(sel_ref[...], gcum)
    qdec = q * jnp.exp(gcum)
    nsc = -(-t_valid // CD)
    gl_rows = jnp.concatenate([jnp.broadcast_to(gcum[(s + 1) * CD - 1:(s + 1) * CD, :], (CD, W)) for s in range(tc // CD)], axis=0)
    kdt = (kd * jnp.exp(gl_rows - gcum)).T
    gct = gcum.T

    HS = H_D * CD
    head_lanes = (_iota2((HS, W), 0) >> 6) == (_iota2((HS, W), 1) >> 6)
    stack = lambda x: jnp.where(head_lanes, jnp.concatenate([x] * H_D, axis=0), 0.0)
    ci, jj = _iota2((HS, CD), 0) & (CD - 1), _iota2((HS, CD), 1)
    later = jnp.where(ci > jj, ci ^ jj, 0)
    lhs, rhs, msk = [q], [kd], [ci == jj]
    for lv in range(1, _D_LEVELS + 1):
        gml = gm[(lv - 1) * tc:lv * tc, :]
        lhs.append(q * jnp.exp(jnp.minimum(gcum - gml, 0.0)))
        rhs.append(kd * jnp.exp(jnp.minimum(gml - gcum, 0.0)))
        msk.append((later >> (lv - 1)) == 1)

    lane_head = _iota2((CD, W), 1) >> 6
    attns = [jnp.zeros((HS, CD), F32) for _ in range(nsc)]
    for a, b, m in zip(lhs, rhs, msk):
        for sc in range(nsc):
            r0 = sc * CD
            attns[sc] = attns[sc] + jnp.where(m, _bdot_nt(stack(a[r0:r0 + CD]), b[r0:r0 + CD]), 0.0)
    intra = []
    for sc in range(nsc):
        full = _bdot(attns[sc], v[sc * CD:(sc + 1) * CD])
        o = jnp.zeros((CD, W), F32)
        for h in range(H_D):
            o = o + jnp.where(lane_head == h, full[h * CD:(h + 1) * CD], 0.0)
        intra.append(o)
    for sc in range(nsc):
        r0 = sc * CD
        sbd = s_scr[...]
        ob_scr[r0:r0 + CD, :] = _bdot(qdec[r0:r0 + CD], sbd) + intra[sc]
        decay = jnp.exp(gct[:, r0 + CD - 1:r0 + CD])
        s_scr[...] = jnp.where(head_lanes, sbd * decay + _bdot(kdt[:, r0:r0 + CD], v[r0:r0 + CD]), 0.0)
    if nsc * CD < tc:
        ob_scr[nsc * CD:tc, :] = jnp.zeros((tc - nsc * CD, W), F32)

    o = ob_scr[...]
    o = o * lax.rsqrt(_dot01_right(o * o, bo_ref[...]) * (1.0 / DV_D) + EPS) * gn_ref[...]
    res = o * _silu(gd)
    o_ref[...] = res[0:rows]
    sout_ref[...] = s_scr[...]


def _np_consts():
    i = np.arange(TC)
    same64 = (i[:, None] // 64) == (i[None, :] // 64)
    tril2 = (same64 & (i[:, None] >= i[None, :])).astype(np.float32)
    tril1 = (i[:, None] >= i[None, :]).astype(np.float32)
    sel = np.zeros((_D_LEVELS * TC, TC), np.float32)
    for lv in range(1, _D_LEVELS + 1):
        m = ((i >> lv) << lv) + (1 << (lv - 1)) - 1
        sel[(lv - 1) * TC + i, m] = 1.0
    bo = lambda n, w: ((np.arange(n)[:, None] // w) == (np.arange(n)[None, :] // w)).astype(np.float32)
    ia = np.arange(TC_A)
    tril2a = (((ia[:, None] // 64) == (ia[None, :] // 64)) & (ia[:, None] >= ia[None, :])).astype(np.float32)
    sela = np.zeros((_D_LEVELS * TC_A, TC_A), np.float32)
    for lv in range(1, _D_LEVELS + 1):
        sela[(lv - 1) * TC_A + ia, ((ia >> lv) << lv) + (1 << (lv - 1)) - 1] = 1.0
    return dict(tril2=tril2, triu2=tril2.T.copy(), tril1=tril1, triu1=tril1.T.copy(), sel=sel,
                tril2a=tril2a, triu2a=tril2a.T.copy(), sela=sela,
                bo64=bo(256, 64), bo256=bo(D_INNER_C, D_INNER_C // G_C))


def _bucket_starts():
    max_exact = N_BUCKETS // 2
    d = np.arange(0, 4 * MAX_DISTANCE)
    lr = np.log(np.maximum(d, 1).astype(np.float32) / max_exact) / math.log(MAX_DISTANCE / max_exact)
    large = np.minimum(max_exact + (np.maximum(lr, 0.0) * (N_BUCKETS - max_exact)).astype(np.int32), N_BUCKETS - 1)
    bucket = np.where(d < max_exact, d, large)
    return [int(np.argmax(bucket == b)) for b in range(N_BUCKETS)]


_BUCKET_STARTS = _bucket_starts()
_SAT_DIST = _BUCKET_STARTS[N_BUCKETS - 1]


def _bias_from_dist(d, rb_ref, h):
    out = jnp.full(d.shape, rb_ref[N_BUCKETS - 1, h], F32)
    for b in range(N_BUCKETS - 2, -1, -1):
        out = jnp.where(d < _BUCKET_STARTS[b + 1], rb_ref[b, h], out)
    return out


def _sortable(x):
    b = lax.bitcast_convert_type(x + 0.0, I32)
    return jnp.where(b < 0, b ^ 0x7FFFFFFF, b)


_NEG_KEY = int(np.array([NEG], np.float32).view(np.int32)[0]) ^ 0x7FFFFFFF
_INT_MIN = -2 ** 31


def _selected_i32(k, idx, thr, xthr):
    s = jnp.where(k > thr, 1, jnp.where(k == thr, jnp.where(idx <= xthr, 1, 0), 0))
    return jnp.where(k == _NEG_KEY, 0, s)


def _selected(k, idx, thr, xthr):
    return _selected_i32(k, idx, thr, xthr) > 0


def _const_spec(shape):
    nd = len(shape)
    return pl.BlockSpec(shape, lambda *_: (0,) * nd)


def _mix_call(kernel_fn, name, P, B, T, seq_specs, batch_args, const_args, out_w, st_tail, scratch, tc=TC):
    rows = min(T, tc)
    nc = max(T // tc, 1)
    assert rows * nc == T
    in_specs = [pl.BlockSpec((rows, w), functools.partial(lambda b, c, ci: (b * nc + c, ci), ci=off // w)) for off, w in seq_specs]
    args = [P] * len(seq_specs)
    for a in batch_args:
        nd = a.ndim
        in_specs.append(pl.BlockSpec((None,) + a.shape[1:], functools.partial(lambda b, c, nd: (b,) + (0,) * (nd - 1), nd=nd)))
        args.append(a)
    for a in const_args:
        in_specs.append(_const_spec(a.shape))
        args.append(a)
    st_nd = len(st_tail)
    return pl.pallas_call(
        functools.partial(kernel_fn, t_valid=rows, rows=rows),
        grid=(B, nc),
        in_specs=in_specs,
        out_specs=[pl.BlockSpec((rows, out_w), lambda b, c: (b * nc + c, 0)),
                   pl.BlockSpec((None,) + st_tail, lambda b, c: (b,) + (0,) * st_nd)],
        out_shape=[jax.ShapeDtypeStruct((B * T, out_w), F32), jax.ShapeDtypeStruct((B,) + st_tail, F32)],
        scratch_shapes=scratch,
        compiler_params=pltpu.CompilerParams(dimension_semantics=("arbitrary", "arbitrary"), vmem_limit_bytes=VMEM_LIMIT),
        name=name,
    )(*args)


CU = 4


def _dsa_prompt_kernel(rb_ref, pq_ref, pk_ref, o_ref,
                       kvt_scr, key_scr, lg_scr, bias_scr, acc_scr, *, n_sel, idx_bits):
    j = pl.program_id(1)
    nch = j + 1

    @pl.when((pl.program_id(0) == 0) & (j == 0))
    def _():
        srow, tcol = _iota2((QB, QB), 0), _iota2((QB, QB), 1)
        for h in range(H_B):
            bias_scr[h, 0] = jnp.full((QB, QB), rb_ref[N_BUCKETS - 1, h], F32)
            bias_scr[h, 1] = _bias_from_dist(tcol - srow + QB, rb_ref, h)
            bias_scr[h, 2] = _bias_from_dist(tcol - srow, rb_ref, h)

    @pl.when(j == 0)
    def _():
        def tr(c, _):
            r = pl.multiple_of(c * QB, QB)
            kvt_scr[c] = pk_ref[pl.ds(r, QB), PK_KV:PK_KV + LANE].T[DH_B:2 * DH_B, :].astype(BF16)
            return 0
        lax.fori_loop(0, pk_ref.shape[0] // QB, tr, 0)

    lanes = lambda rows, w: jnp.concatenate([rows[h * w:(h + 1) * w, :] for h in range(H_B)], axis=1)
    w4 = lanes(pq_ref[:, PQ_SM:PQ_SM + LANE].T[SM_WI:SM_WI + H_IDX, :], 1) * (H_IDX ** -0.5 * D_IDX ** -0.5)
    qi_rhs = lanes(pq_ref[:, PQ_QI:PQ_QI + LANE].T, D_IDX).astype(BF16)
    qb_rhs = (lanes(pq_ref[:, PQ_QB:PQ_QB + W_B].T, DH_B) * (DH_B ** -0.5)).astype(BF16)
    srow, tcol = _iota2((QB, QB), 0), _iota2((QB, QB), 1)
    heads = lambda x: [x[:, h * QB:(h + 1) * QB] for h in range(H_B)]

    ngrp = (nch + (CU - 1)) // CU

    def score_body(g, _):
        for u in range(CU):
            r = pl.multiple_of((g * CU + u) * QB, QB)
            ki = pk_ref[pl.ds(r, QB), PK_SM:PK_SM + LANE][:, SM_KI:SM_KI + D_IDX].astype(BF16)
            s = jnp.maximum(jnp.dot(ki, qi_rhs, preferred_element_type=F32), 0.0) * w4
            sh = heads(s)
            sc = (sh[0] + sh[1]) + (sh[2] + sh[3])
            sc = jnp.where(srow + r <= tcol + j * QB, sc, NEG)
            key_scr[pl.ds(r, QB), :] = _sortable(sc)
        return 0

    lax.fori_loop(0, ngrp, score_body, 0)

    def count(pred):
        def body(g, acc):
            for u in range(CU):
                r = pl.multiple_of((g * CU + u) * QB, QB)
                acc = acc + pred(key_scr[pl.ds(r, QB), :], r).reshape(QB // SUB, SUB, QB).sum(axis=0)
            return acc
        return lax.fori_loop(0, ngrp, body, jnp.zeros((SUB, QB), I32)).sum(axis=0, keepdims=True)

    def count_ge_static(n_pairs):
        def f(t):
            accs = [jnp.zeros((SUB, QB), I32) for _ in range(2)]
            for c in range(2 * n_pairs):
                w = jnp.where(key_scr[c * QB:(c + 1) * QB, :] >= t, 1, 0)
                accs[c % 2] = accs[c % 2] + w.reshape(QB // SUB, SUB, QB).sum(axis=0)
            return (accs[0] + accs[1]).sum(axis=0, keepdims=True)
        return f

    npair = (nch + 1) // 2
    count_variants = [count_ge_static(n) for n in range(1, key_scr.shape[0] // (2 * QB) + 1)]
    cnt_ge = lambda t: lax.switch(npair - 1, count_variants, t)
    c0 = cnt_ge(jnp.zeros((1, QB), I32))
    thr = jnp.where(c0 >= n_sel, 0, _INT_MIN).astype(I32)
    cthr = jnp.where(c0 >= n_sel, c0, npair * (2 * QB))

    def bit_body(i, carry):
        t, ct = carry
        trial = t | jnp.left_shift(jnp.int32(1), 30 - i)
        cnt = cnt_ge(trial)
        ok = cnt >= n_sel
        return jnp.where(ok, trial, t), jnp.where(ok, cnt, ct)

    thr, cthr = lax.fori_loop(0, 31, bit_body, (thr, cthr))

    @pl.when(jnp.max(cthr) > n_sel)
    def _():
        quota = n_sel - count(lambda k, r: jnp.where(k > thr, 1, 0))

        def idx_body(i, x):
            trial = x | jnp.left_shift(jnp.int32(1), idx_bits - 1 - i)
            below = count(lambda k, r: jnp.where(k == thr, jnp.where(srow + r < trial, 1, 0), 0))
            return jnp.where(below < quota, trial, x)

        xthr = lax.fori_loop(0, idx_bits, idx_body, jnp.zeros((1, QB), I32))

        def demote_body(g, _):
            for u in range(CU):
                r = pl.multiple_of((g * CU + u) * QB, QB)
                k = key_scr[pl.ds(r, QB), :]
                key_scr[pl.ds(r, QB), :] = jnp.where(k == thr, jnp.where(srow + r > xthr, k - 1, k), k)
            return 0

        lax.fori_loop(0, ngrp, demote_body, 0)

    thr_sel = jnp.maximum(thr, _NEG_KEY + 1)

    def logit_body(g, ms):
        ms = list(ms)
        for u in range(CU):
            c = g * CU + u
            r = pl.multiple_of(c * QB, QB)
            kc = pk_ref[pl.ds(r, QB), PK_KV:PK_KV + LANE][:, 0:DH_B].astype(BF16)
            l4 = heads(jnp.dot(kc, qb_rhs, preferred_element_type=F32))
            sel = key_scr[pl.ds(r, QB), :] >= thr_sel
            bidx = jnp.clip(c - j + 2, 0, 2)
            for h in range(H_B):
                l = jnp.where(sel, l4[h] + bias_scr[h, bidx], NEG)
                lg_scr[h, pl.ds(r, QB), :] = l
                ms[h] = jnp.maximum(ms[h], l.max(axis=0, keepdims=True))
        return tuple(ms)

    ms = lax.fori_loop(0, ngrp, logit_body, tuple(jnp.full((1, QB), NEG, F32) for _ in range(H_B)))
    acc_scr[...] = jnp.zeros(acc_scr.shape, F32)

    def pv_body(g, ss):
        ss = list(ss)
        acc = acc_scr[...]
        for u in range(CU):
            c = g * CU + u
            r = pl.multiple_of(c * QB, QB)
            es = [jnp.exp(lg_scr[h, pl.ds(r, QB), :] - ms[h]) for h in range(H_B)]
            acc = acc + jnp.dot(kvt_scr[c], jnp.concatenate(es, axis=1).astype(BF16), preferred_element_type=F32)
            for h in range(H_B):
                ss[h] = ss[h] + es[h].sum(axis=0, keepdims=True)
        acc_scr[...] = acc
        return tuple(ss)

    ss = lax.fori_loop(0, ngrp, pv_body, tuple(jnp.zeros((1, QB), F32) for _ in range(H_B)))
    acc = heads(acc_scr[...])
    ot = jnp.concatenate([acc[h] / ss[h] for h in range(H_B)], axis=0)
    o_ref[...] = ot.T


def _dsa_prompt(Pq, Pk, rel_bias, B, T):
    nb = T // QB
    assert nb % CU == 0
    n_sel = min(TOPK_MAX, T // 4)
    idx_bits = max(1, int(math.ceil(math.log2(T))))
    return pl.pallas_call(
        functools.partial(_dsa_prompt_kernel, n_sel=n_sel, idx_bits=idx_bits),
        grid=(B, nb),
        in_specs=[pl.BlockSpec(memory_space=pltpu.SMEM),
                  pl.BlockSpec((QB, PQ_W), lambda b, j: (b * nb + j, 0)),
                  pl.BlockSpec((T, PK_W), lambda b, j: (b, 0))],
        out_specs=pl.BlockSpec((QB, W_B), lambda b, j: (b * nb + j, 0)),
        out_shape=jax.ShapeDtypeStruct((B * T, W_B), F32),
        scratch_shapes=[pltpu.VMEM((nb, DH_B, QB), BF16), pltpu.VMEM((T, QB), I32), pltpu.VMEM((H_B, T, QB), F32),
                        pltpu.VMEM((H_B, 3, QB, QB), F32), pltpu.VMEM((DH_B, H_B * QB), F32)],
        compiler_params=pltpu.CompilerParams(dimension_semantics=("arbitrary", "arbitrary"), vmem_limit_bytes=VMEM_LIMIT),
        name="dsa_prompt",
    )(rel_bias, Pq, Pk)


PG = 16
GW = PG * PAGE_SIZE


def _stack_heads(x, w):
    return jnp.concatenate([x[:, h * w:(h + 1) * w] for h in range(x.shape[1] // w)], axis=0)


def _dsa_s_scores_kernel(pt_ref, pq_ref, *refs, ng, ds):
    page_refs, o_ref = refs[:PG], refs[PG]
    g = pl.program_id(1)
    sm = pq_ref[:, PQ_SM:PQ_SM + LANE]
    qst = _stack_heads(pq_ref[:, PQ_QI:PQ_QI + LANE], D_IDX)
    wcol = jnp.concatenate([sm[:, SM_WI + h:SM_WI + h + 1] for h in range(H_IDX)], axis=0) * (H_IDX ** -0.5 * D_IDX ** -0.5)

    def tile_scores(qk):
        s = jnp.maximum(qk, 0.0) * wcol
        out = s[0:ds]
        for h in range(1, H_IDX):
            out = out + s[h * ds:(h + 1) * ds]
        return out

    @pl.when(g < ng)
    def _():
        for i in range(PG):
            o_ref[:, i * PAGE_SIZE:(i + 1) * PAGE_SIZE] = tile_scores(_bdot(qst, page_refs[i][...]))

    @pl.when(g == ng)
    def _():
        sc = tile_scores(_bdot_nt(qst, _pad_rows(sm[:, SM_KI:SM_KI + D_IDX], PAGE_SIZE)))
        lane, row = _iota2(sc.shape, 1), _iota2(sc.shape, 0)
        o_ref[:, 0:PAGE_SIZE] = jnp.where(lane <= row, sc, NEG)
        o_ref[:, PAGE_SIZE:GW] = jnp.full((ds, GW - PAGE_SIZE), NEG, F32)


def _dsa_s_attn_kernel(pt_ref, rb_ref, sc_ref, pq_ref, pk_ref, *refs, ng, ds, n_sel, idx_bits):
    k_refs, v_refs = refs[:PG], refs[PG:2 * PG]
    o_ref, key_scr, thr_scr, bias_scr, m_scr, l_scr, acc_scr = refs[2 * PG:]
    g = pl.program_id(1)
    ntile = (ng + 1) * PG
    hr = H_B * ds

    @pl.when(g == 0)
    def _():
        for i in range(ntile):
            key_scr[:, i * LANE:(i + 1) * LANE] = _sortable(sc_ref[:, i * LANE:(i + 1) * LANE])
        lane = _iota2((ds, LANE), 1)

        def count(pred):
            accs = [jnp.zeros((ds, LANE), I32) for _ in range(4)]
            for i in range(ng * PG + 1):
                accs[i % 4] = accs[i % 4] + pred(key_scr[:, i * LANE:(i + 1) * LANE], i * LANE)
            return ((accs[0] + accs[1]) + (accs[2] + accs[3])).sum(axis=1, keepdims=True)

        cnt_ge = lambda t: count(lambda k, off: jnp.where(k >= t, 1, 0))
        c0 = cnt_ge(jnp.zeros((ds, 1), I32))
        thr = jnp.where(c0 >= n_sel, 0, _INT_MIN).astype(I32)
        cthr = jnp.where(c0 >= n_sel, c0, ng * GW + LANE)

        def bit_body(i, carry):
            t, ct = carry
            trial = t | jnp.left_shift(jnp.int32(1), 30 - i)
            cnt = cnt_ge(trial)
            ok = cnt >= n_sel
            return jnp.where(ok, trial, t), jnp.where(ok, cnt, ct)

        thr, cthr = lax.fori_loop(0, 31, bit_body, (thr, cthr))

        def tie_search():
            quota = n_sel - count(lambda k, off: jnp.where(k > thr, 1, 0))

            def idx_body(i, x):
                trial = x | jnp.left_shift(jnp.int32(1), idx_bits - 1 - i)
                below = count(lambda k, off: jnp.where(k == thr, jnp.where(lane + off < trial, 1, 0), 0))
                return jnp.where(below < quota, trial, x)

            return lax.fori_loop(0, idx_bits, idx_body, jnp.zeros((ds, 1), I32))

        xthr = lax.cond(jnp.max(cthr) > n_sel, tie_search, lambda: jnp.full((ds, 1), (1 << idx_bits) - 1, I32))
        thr_scr[0] = jnp.broadcast_to(thr, (ds, LANE))
        thr_scr[1] = jnp.broadcast_to(xthr, (ds, LANE))

        qrow = _iota2((ds, LANE), 0)
        for h in range(H_B):
            bias_scr[0, h * ds:(h + 1) * ds, :] = _bias_from_dist(PAGE_SIZE + qrow - lane, rb_ref, h)
            bias_scr[1, h * ds:(h + 1) * ds, :] = _bias_from_dist(qrow - lane, rb_ref, h)
            bias_scr[2, h * ds:(h + 1) * ds, :] = jnp.full((ds, LANE), rb_ref[N_BUCKETS - 1, h], F32)
        m_scr[...] = jnp.full(m_scr.shape, NEG, F32)
        l_scr[...] = jnp.zeros(l_scr.shape, F32)
        acc_scr[...] = jnp.zeros(acc_scr.shape, F32)

    qst = _stack_heads(pq_ref[:, PQ_QB:PQ_QB + W_B], DH_B) * (DH_B ** -0.5)
    thr, xthr = thr_scr[0], thr_scr[1]

    def select(tile):
        k = key_scr[:, pl.ds(pl.multiple_of(tile * LANE, LANE), LANE)]
        idx = _iota2((ds, LANE), 1) + tile * LANE
        sel = _selected_i32(k, idx, thr, xthr)
        return jnp.concatenate([sel] * H_B, axis=0) > 0

    def update(lg, sel, pv):
        m_old = m_scr[...]
        m_new = jnp.maximum(m_old, jnp.where(sel, lg, NEG).max(axis=1, keepdims=True))
        p = jnp.where(sel, jnp.exp(lg - m_new), 0.0)
        corr = jnp.exp(m_old - m_new)
        l_scr[...] = l_scr[...] * corr + p.sum(axis=1, keepdims=True)
        acc_scr[...] = acc_scr[...] * corr + pv(p)
        m_scr[...] = m_new

    @pl.when(g < ng)
    def _():
        lgs, sels = [], []
        for i in range(PG):
            lg = _bdot(qst, k_refs[i][...])
            if i == PG - 1:
                lg = lg + jnp.where(g == ng - 1, bias_scr[0], bias_scr[2])
            else:
                lg = lg + bias_scr[2]
            lgs.append(lg)
            sels.append(select(g * PG + i))
        vt = jnp.concatenate([v_refs[i][...] for i in range(PG)], axis=1)
        update(jnp.concatenate(lgs, axis=1), jnp.concatenate(sels, axis=1), lambda p: _bdot_nt(p, vt))

    @pl.when(g == ng)
    def _():
        kvn = _pad_rows(pk_ref[:, PK_KV:PK_KV + LANE], PAGE_SIZE)
        lg = _bdot_nt(qst, kvn[:, 0:DH_B]) + bias_scr[1]
        update(lg, select(ng * PG), lambda p: _bdot(p, kvn[:, DH_B:2 * DH_B]))
        o = acc_scr[...] / l_scr[...]
        o_ref[...] = jnp.concatenate([o[h * ds:(h + 1) * ds] for h in range(H_B)], axis=1)


def _dsa_sample(Pq, Pk, rel_bias, cache_k, cache_v, cache_kidx, page_table, layer, B, DS):
    n_pages = page_table.shape[1]
    past = n_pages * PAGE_SIZE
    assert n_pages % PG == 0
    ng = n_pages // PG
    n_sel = min(TOPK_MAX, (past + DS) // 4)
    idx_bits = int(math.ceil(math.log2(past + LANE)))
    s_pad = (ng + 1) * GW

    def page_spec(width, i):
        return pl.BlockSpec((None, None, width, PAGE_SIZE),
                            lambda b, g, pt: (layer, pt[b, jnp.minimum(g, ng - 1) * PG + i], 0, 0))

    row = lambda b, g, pt: (b, 0)
    cp = pltpu.CompilerParams(dimension_semantics=("arbitrary", "arbitrary"), vmem_limit_bytes=VMEM_LIMIT)
    scores = pl.pallas_call(
        functools.partial(_dsa_s_scores_kernel, ng=ng, ds=DS),
        grid_spec=pltpu.PrefetchScalarGridSpec(
            num_scalar_prefetch=1, grid=(B, ng + 1),
            in_specs=[pl.BlockSpec((DS, PQ_W), row)] + [page_spec(D_IDX, i) for i in range(PG)],
            out_specs=pl.BlockSpec((None, DS, GW), lambda b, g, pt: (b, 0, g))),
        out_shape=jax.ShapeDtypeStruct((B, DS, s_pad), F32),
        compiler_params=cp, name="dsa_sample_scores",
    )(page_table, Pq, *([cache_kidx] * PG))

    return pl.pallas_call(
        functools.partial(_dsa_s_attn_kernel, ng=ng, ds=DS, n_sel=n_sel, idx_bits=idx_bits),
        grid_spec=pltpu.PrefetchScalarGridSpec(
            num_scalar_prefetch=1, grid=(B, ng + 1),
            in_specs=[pl.BlockSpec(memory_space=pltpu.SMEM),
                      pl.BlockSpec((None, DS, s_pad), lambda b, g, pt: (b, 0, 0)),
                      pl.BlockSpec((DS, PQ_W), row), pl.BlockSpec((DS, PK_W), row)]
                     + [page_spec(DH_B, i) for i in range(PG)] + [page_spec(DH_B, i) for i in range(PG)],
            out_specs=pl.BlockSpec((DS, W_B), lambda b, g, pt: (b, 0)),
            scratch_shapes=[pltpu.VMEM((DS, s_pad), I32), pltpu.VMEM((2, DS, LANE), I32), pltpu.VMEM((3, H_B * DS, LANE), F32),
                            pltpu.VMEM((H_B * DS, 1), F32), pltpu.VMEM((H_B * DS, 1), F32), pltpu.VMEM((H_B * DS, DH_B), F32)]),
        out_shape=jax.ShapeDtypeStruct((B * DS, W_B), F32),
        compiler_params=cp, name="dsa_sample_attn",
    )(page_table, rel_bias, scores, Pq, Pk, *([cache_k] * PG), *([cache_v] * PG))


def _merge_kernel(x_ref, oa_ref, ob_ref, oc_ref, od_ref, gate_ref, wb_ref, wo_ref, o_ref):
    m = None
    r0 = 0
    for br, ref in enumerate((oa_ref, ob_ref, oc_ref, od_ref)):
        w = ref.shape[1]
        t = _sigmoid(gate_ref[:, br * D_MODEL:(br + 1) * D_MODEL]) * jnp.dot(ref[...].astype(BF16), wb_ref[r0:r0 + w, :], preferred_element_type=F32)
        m = t if m is None else m + t
        r0 += w
    o_ref[...] = x_ref[...] + jnp.dot(m.astype(BF16), wo_ref[...], preferred_element_type=F32)


def _merge(x2d, oa, ob, oc, od, P, wb, wo):
    n = x2d.shape[0]
    tm = min(n, 256)
    row = lambda i: (i, 0)
    return pl.pallas_call(
        _merge_kernel,
        grid=(n // tm,),
        in_specs=[pl.BlockSpec((tm, D_MODEL), row), pl.BlockSpec((tm, W_A), row), pl.BlockSpec((tm, W_B), row),
                  pl.BlockSpec((tm, W_C), row), pl.BlockSpec((tm, W_D), row),
                  pl.BlockSpec((tm, N_BRANCH * D_MODEL), lambda i: (i, OFF_GATE // (N_BRANCH * D_MODEL))),
                  _const_spec((MIX_W, D_MODEL)), _const_spec((D_MODEL, D_MODEL))],
        out_specs=pl.BlockSpec((tm, D_MODEL), row),
        out_shape=jax.ShapeDtypeStruct((n, D_MODEL), F32),
        compiler_params=pltpu.CompilerParams(dimension_semantics=("arbitrary",), vmem_limit_bytes=VMEM_LIMIT),
        name="merge",
    )(x2d, oa, ob, oc, od, P, wb, wo)


def _mlp_kernel(x_ref, g_ref, wu_ref, wd_ref, gf_ref, *refs, final):
    if final:
        o_ref, y_ref, h_scr, acc_scr = refs
    else:
        o_ref, h_scr, acc_scr = refs
    f = pl.program_id(1)

    @pl.when(f == 0)
    def _():
        x = x_ref[...]
        r = lax.rsqrt(jnp.mean(x * x, axis=-1, keepdims=True) + EPS)
        h_scr[...] = ((x * r) * g_ref[...]).astype(BF16)
        acc_scr[...] = jnp.zeros(acc_scr.shape, F32)

    a = jnp.maximum(jnp.dot(h_scr[...], wu_ref[...], preferred_element_type=F32), 0.0)
    acc_scr[...] += jnp.dot((a * a).astype(BF16), wd_ref[...], preferred_element_type=F32)

    @pl.when(f == pl.num_programs(1) - 1)
    def _():
        out = x_ref[...] + acc_scr[...]
        o_ref[...] = out
        if final:
            r = lax.rsqrt(jnp.mean(out * out, axis=-1, keepdims=True) + EPS)
            y_ref[...] = (out * r) * gf_ref[...]


def _mlp(x2d, g, wu, wd, gf, final):
    n = x2d.shape[0]
    tm = min(n, 512)
    tf = 1024
    row = lambda i, f: (i, 0)
    out_spec = pl.BlockSpec((tm, D_MODEL), row)
    shp = jax.ShapeDtypeStruct((n, D_MODEL), F32)
    return pl.pallas_call(
        functools.partial(_mlp_kernel, final=final),
        grid=(n // tm, D_FF // tf),
        in_specs=[pl.BlockSpec((tm, D_MODEL), row), _const_spec((1, D_MODEL)),
                  pl.BlockSpec((D_MODEL, tf), lambda i, f: (0, f)), pl.BlockSpec((tf, D_MODEL), lambda i, f: (f, 0)),
                  _const_spec((1, D_MODEL))],
        out_specs=[out_spec, out_spec] if final else [out_spec],
        out_shape=[shp, shp] if final else [shp],
        scratch_shapes=[pltpu.VMEM((tm, D_MODEL), BF16), pltpu.VMEM((tm, D_MODEL), F32)],
        compiler_params=pltpu.CompilerParams(dimension_semantics=("arbitrary", "arbitrary"), vmem_limit_bytes=VMEM_LIMIT),
        name="mlp_final" if final else "mlp",
    )(x2d, g.reshape(1, D_MODEL), wu, wd, gf.reshape(1, D_MODEL))


def _layer(x, conv_state, s_delta, s_ssm, s_hgrn, lw, consts, sample_ctx, final):
    B, T, _ = x.shape
    x2 = x.reshape(B * T, D_MODEL)
    P, Pq, Pk = _in_proj(x2, lw['norm_mix'], lw['w_in'])

    def conv0(lo, hi):
        if conv_state is None:
            return jnp.zeros((B, SUB, hi - lo), F32)
        return jnp.concatenate([jnp.zeros((B, SUB - (CONV_W - 1), hi - lo), F32), conv_state[:, :, lo:hi]], axis=1)

    zeros_state = lambda n: jnp.zeros((B, n, 64, 64), F32)
    c = consts
    half = CONV_CH // 2
    assert CONV_A == half and OFF_PC % half == 0
    oa, s_a = _mix_call(
        _mix_a_kernel, "mix_a", P, B, T,
        [(OFF_PC, half), (OFF_ZA, W_A), (OFF_SM, LANE)],
        [conv0(0, CONV_A), zeros_state(H_A) if s_delta is None else s_delta],
        [lw['conv_w'][:, 0:CONV_A], lw['conv_b'][None, 0:CONV_A], lw['hp'], lw['hpt'], lw['gn_a'], c['tril2a'], c['triu2a'], c['bo64']],
        W_A, (H_A, DK_A, DV_A),
        [pltpu.VMEM((TC_A + SUB, CONV_A), F32), pltpu.VMEM((H_A, DK_A, DV_A), F32), pltpu.VMEM((TC_A, W_A), F32)],
        tc=TC_A)
    oc, s_c = _mix_call(
        _mix_c_kernel, "mix_c", P, B, T,
        [(OFF_PC + half, half), (OFF_ZC, W_C), (OFF_SM, LANE)],
        [conv0(CONV_A, CONV_CH), zeros_state(H_C) if s_ssm is None else s_ssm],
        [lw['conv_w'][:, CONV_A:], lw['conv_b'][None, CONV_A:], lw['hp'], lw['hpt'], lw['gn_c'], c['tril1'], c['triu1'], c['bo256']],
        W_C, (H_C, N_C, P_C),
        [pltpu.VMEM((TC + SUB, CONV_C), F32), pltpu.VMEM((H_C, N_C, P_C), F32), pltpu.VMEM((TC, W_C), F32)])
    sbd0 = jnp.zeros((B, H_D, DK_D, H_D, DV_D), F32)
    if s_hgrn is not None:
        for h in range(H_D):
            sbd0 = sbd0.at[:, h, :, h, :].set(s_hgrn[:, h])
    sbd0 = sbd0.reshape(B, H_D * DK_D, W_D)
    od, sbd = _mix_call(
        _mix_d_kernel, "mix_d", P, B, T,
        [(OFF_D, 4 * W_D)],
        [sbd0],
        [lw['lb'], lw['gn_d'], c['tril2a'], c['sela'], c['bo64']],
        W_D, (H_D * DK_D, W_D),
        [pltpu.VMEM((H_D * DK_D, W_D), F32), pltpu.VMEM((TC_A, W_D), F32)],
        tc=TC_A)
    sbd5 = sbd.reshape(B, H_D, DK_D, H_D, DV_D)
    s_d = jnp.stack([sbd5[:, h, :, h, :] for h in range(H_D)], axis=1)
    if sample_ctx is None:
        ob = _dsa_prompt(Pq, Pk, lw['rel_bias'], B, T)
    else:
        ob = _dsa_sample(Pq, Pk, lw['rel_bias'], sample_ctx['cache_k'], sample_ctx['cache_v'], sample_ctx['cache_kidx'],
                         sample_ctx['page_table'], sample_ctx['layer'], B, T)
    x1 = _merge(x2, oa, ob, oc, od, P, lw['w_branch'], lw['w_out'])
    outs = _mlp(x1, lw['norm_mlp'], lw['w_up'], lw['w_down'], lw['norm_final'], final)
    xo = outs[0].reshape(B, T, D_MODEL)
    y = outs[1].reshape(B, T, D_MODEL) if final else None
    P3 = P.reshape(B, T, P_PAD)
    Pk3 = Pk.reshape(B, T, PK_W)
    states = (Pk3[:, :, PK_KV:PK_KV + DH_B], Pk3[:, :, PK_KV + DH_B:PK_KV + 2 * DH_B], Pk3[:, :, PK_SM + SM_KI:PK_SM + SM_KI + D_IDX],
              P3[:, T - (CONV_W - 1):, OFF_PC:OFF_PC + CONV_CH], s_a, s_c, s_d)
    return xo, y, states


def kernel(x_prompt, x_sample, cache_k, cache_v, cache_kidx, state_conv, state_delta, state_ssm, state_hgrn, page_table, norm_mix, w_in, conv_w, conv_b, a_log_a, dt_bias_a, gnorm_a, rel_bias, a_log_c, dt_bias_c, d_skip_c, gnorm_c, hgrn_gamma, gnorm_d, w_branch, w_out, norm_mlp, w_up, w_down, norm_final):
    depth = w_in.shape[0]
    npc = _np_consts()
    consts = {k: jnp.asarray(v, BF16) for k, v in npc.items()}
    pg = jax.nn.softmax(hgrn_gamma.astype(F32), axis=0)
    lower_bounds = jnp.cumsum(pg, axis=0) - pg[0]
    cache_kt, cache_vt, cache_kit = (jnp.swapaxes(c, 2, 3) for c in (cache_k, cache_v, cache_kidx))
    yp, ys = x_prompt, x_sample
    new_p, new_s = [], []
    for l in range(depth):
        hp = jnp.zeros((SUB, LANE), F32)
        hp = hp.at[0, 0:H_A].set(a_log_a[l]).at[1, 0:H_A].set(dt_bias_a[l])
        hp = hp.at[2, 0:H_C].set(a_log_c[l]).at[3, 0:H_C].set(dt_bias_c[l]).at[4, 0:H_C].set(d_skip_c[l])
        lw = dict(norm_mix=norm_mix[l], w_in=_permute_w_in(w_in[l]), conv_w=conv_w[l], conv_b=conv_b[l],
                  hp=hp, hpt=hp.T, gn_a=jnp.tile(gnorm_a[l], H_A)[None, :], gn_c=gnorm_c[l][None, :],
                  gn_d=jnp.tile(gnorm_d[l], H_D)[None, :], lb=lower_bounds[l][None, :], rel_bias=rel_bias,
                  w_branch=w_branch[l].astype(BF16), w_out=w_out[l].astype(BF16), norm_mlp=norm_mlp[l],
                  w_up=w_up[l].astype(BF16), w_down=w_down[l].astype(BF16), norm_final=norm_final)
        final = l == depth - 1
        yp, yp_n, st_p = _layer(yp, None, None, None, None, lw, consts, None, final)
        ctx = dict(cache_k=cache_kt, cache_v=cache_vt, cache_kidx=cache_kit, page_table=page_table, layer=l)
        ys, ys_n, st_s = _layer(ys, state_conv[l], state_delta[l], state_ssm[l], state_hgrn[l], lw, consts, ctx, final)
        new_p.append(st_p)
        new_s.append(st_s)
    stack = lambda per_layer: [jnp.stack(items) for items in zip(*per_layer)]
    return (yp_n, ys_n, *stack(new_p), *stack(new_s))
```

```python
import functools
import math

import numpy as np
import jax
import jax.numpy as jnp
from jax import lax
from jax.experimental import pallas as pl
from jax.experimental.pallas import tpu as pltpu

F32 = jnp.float32
BF16 = jnp.bfloat16
I32 = jnp.int32

D_MODEL = 1024
PAGE_SIZE = 128
EPS = 1e-6
NEG = -1e30
TINY = 1e-30
CONV_W = 4
D_FF = 4 * D_MODEL
N_BRANCH = 4
H_A, DK_A, DV_A = 4, 64, 64
H_B, DH_B = 4, 64
H_IDX, D_IDX = 4, 32
TOPK_MAX = 256
N_BUCKETS = 32
MAX_DISTANCE = 128
H_C, P_C, N_C, G_C = 8, 64, 64, 2
H_D, DK_D, DV_D = 4, 64, 64
QK_A = H_A * DK_A
CONV_A = 2 * QK_A + H_A * DV_A
D_INNER_C = H_C * P_C
CONV_C = D_INNER_C + 2 * G_C * N_C
CONV_CH = CONV_A + CONV_C
W_A, W_B, W_C, W_D = H_A * DV_A, H_B * DH_B, D_INNER_C, H_D * DV_D
MIX_W = W_A + W_B + W_C + W_D
IN_WIDTHS = (CONV_CH, W_A, H_A, H_A, W_B, DH_B, DH_B, H_IDX * D_IDX, D_IDX, H_IDX, D_INNER_C, H_C,
             H_D * DK_D, H_D * DK_D, W_D, W_D, N_BRANCH * D_MODEL)
P_IN = sum(IN_WIDTHS)

LANE = 128
SUB = 8
TC = 128
TC_A = 256
QB = 128
VMEM_LIMIT = 48 * 1024 * 1024
TM_IN, TM_MERGE, TM_MLP, TF_MLP = 1024, 256, 512, 1024

OFF_GATE, OFF_D, OFF_ZC, OFF_ZA, OFF_QB, OFF_PC, OFF_KV, OFF_QI, OFF_SM = 0, 4096, 5120, 5632, 5888, 6144, 7680, 7808, 7936
P_PAD = 8064
SM_KI, SM_WI, SM_AA, SM_BA, SM_DT = 0, 32, 36, 40, 44


def _src_offsets():
    offs, o = [], 0
    for w in IN_WIDTHS:
        offs.append(o)
        o += w
    return offs


def _permute_w_in(w):
    (o_pc, o_za, o_aa, o_ba, o_qb, o_kb, o_vb, o_qi, o_ki, o_wi, o_zc, o_dt, o_qd, o_fd, o_id, o_gd, o_gate) = _src_offsets()
    wt = jnp.swapaxes(w, 0, 1)
    seg = lambda o, n: wt[o:o + n, :]
    small = jnp.concatenate([seg(o_ki, D_IDX), seg(o_wi, H_IDX), seg(o_aa, H_A), seg(o_ba, H_A), seg(o_dt, H_C),
                             jnp.zeros((LANE - (D_IDX + H_IDX + 2 * H_A + H_C), w.shape[0]), w.dtype)], axis=0)
    out = jnp.concatenate([
        seg(o_gate, N_BRANCH * D_MODEL),
        seg(o_qd, 4 * W_D),
        seg(o_zc, D_INNER_C), seg(o_za, W_A), seg(o_qb, W_B), seg(o_pc, CONV_CH),
        seg(o_kb, 2 * DH_B),
        seg(o_qi, H_IDX * D_IDX), small], axis=0)
    assert out.shape[0] == P_PAD
    return out.astype(BF16)


def _bdot(a, b):
    return jnp.dot(a.astype(BF16), b.astype(BF16), preferred_element_type=F32)


def _bdot_nt(a, b):
    return lax.dot_general(a.astype(BF16), b.astype(BF16), (((1,), (1,)), ((), ())), preferred_element_type=F32)


def _split2(a):
    hi = a.astype(BF16)
    lo = (a - hi.astype(F32)).astype(BF16)
    return hi, lo


def _split3(a):
    hi = a.astype(BF16)
    r = a - hi.astype(F32)
    mid = r.astype(BF16)
    lo = (r - mid.astype(F32)).astype(BF16)
    return hi, mid, lo


def _dot01_left(m01, x):
    hi, mid, lo = _split3(x)
    d = lambda p: jnp.dot(m01, p, preferred_element_type=F32)
    return d(hi) + (d(mid) + d(lo))


def _dot01_right(x, m01):
    hi, mid, lo = _split3(x)
    d = lambda p: jnp.dot(p, m01, preferred_element_type=F32)
    return d(hi) + (d(mid) + d(lo))


def _xdot(a, b):
    ah, al = _split2(a)
    bh, bl = _split2(b)
    d = lambda p, q: jnp.dot(p, q, preferred_element_type=F32)
    return d(ah, bh) + (d(ah, bl) + d(al, bh))


def _sigmoid(x):
    return 1.0 / (1.0 + jnp.exp(-x))


def _silu(x):
    return x * _sigmoid(x)


def _softplus(x):
    return jnp.maximum(x, 0.0) + jnp.log(1.0 + jnp.exp(-jnp.abs(x)))


def _iota2(shape, dim):
    return lax.broadcasted_iota(I32, shape, dim)


def _pad_rows(x, rows):
    if x.shape[0] == rows:
        return x
    return jnp.concatenate([x, jnp.zeros((rows - x.shape[0],) + x.shape[1:], x.dtype)], axis=0)


def _tri_inv_many(Ls, top):
    n = Ls[0].shape[0]
    ii, jj = _iota2((n, n), 0), _iota2((n, n), 1)
    xor = ii ^ jj
    eye = jnp.where(ii == jj, 1.0, 0.0)
    Ns = [jnp.where((xor >> 3) == 0, -L, 0.0) for L in Ls]
    Xs = [eye + N for N in Ns]
    N2s = [_bdot(N, N) for N in Ns]
    Xs = [X + _bdot(X, N2) for X, N2 in zip(Xs, N2s)]
    N4s = [_bdot(N2, N2) for N2 in N2s]
    Xs = [X + _bdot(X, N4) for X, N4 in zip(Xs, N4s)]
    sh = 4
    while (1 << sh) <= top:
        XBs = [_bdot(X, jnp.where((xor >> (sh - 1)) == 1, L, 0.0)) for X, L in zip(Xs, Ls)]
        Xs = [X - _bdot(XB, X) for X, XB in zip(Xs, XBs)]
        sh += 1
    return Xs


def _conv_silu(pc, ext_scr, cw_ref, cb_ref):
    tc = pc.shape[0]
    ext_scr[SUB:SUB + tc, :] = pc
    y = cb_ref[...]
    for j in range(CONV_W):
        y = y + ext_scr[SUB - (CONV_W - 1) + j:SUB - (CONV_W - 1) + j + tc, :] * cw_ref[j:j + 1, :]
    ext_scr[0:SUB, :] = ext_scr[tc:tc + SUB, :]
    return _silu(y)


TN_IN = 1152
PQ_W = W_B + 2 * LANE
PK_W = 2 * LANE
PQ_QB, PQ_QI, PQ_SM = 0, W_B, W_B + LANE
PK_KV, PK_SM = 0, LANE
_COMPACT_COPIES = (("pq", PQ_QB, OFF_QB, W_B), ("pq", PQ_QI, OFF_QI, LANE), ("pq", PQ_SM, OFF_SM, LANE),
                   ("pk", PK_KV, OFF_KV, LANE), ("pk", PK_SM, OFF_SM, LANE))


def _in_proj_kernel(x_ref, g_ref, w_ref, o_ref, pq_ref, pk_ref, h_scr):
    j = pl.program_id(1)

    @pl.when(j == 0)
    def _():
        x = x_ref[...]
        r = lax.rsqrt(jnp.mean(x * x, axis=-1, keepdims=True) + EPS)
        h_scr[...] = ((x * r) * g_ref[...]).astype(BF16)

    o_ref[...] = lax.dot_general(h_scr[...], w_ref[...], (((1,), (1,)), ((), ())), preferred_element_type=F32)

    def copy(dst_ref, dst, src, width):
        tile, lo = divmod(src, TN_IN)
        assert lo + width <= TN_IN

        @pl.when(j == tile)
        def _():
            dst_ref[:, dst:dst + width] = o_ref[:, lo:lo + width]

    for name, dst, src, width in _COMPACT_COPIES:
        copy(pq_ref if name == "pq" else pk_ref, dst, src, width)


def _in_proj(x2d, g, w_perm_t):
    n = x2d.shape[0]
    tm = min(n, TM_IN)
    tn = TN_IN
    return pl.pallas_call(
        _in_proj_kernel,
        grid=(n // tm, P_PAD // tn),
        in_specs=[pl.BlockSpec((tm, D_MODEL), lambda i, j: (i, 0)),
                  pl.BlockSpec((1, D_MODEL), lambda i, j: (0, 0)),
                  pl.BlockSpec((tn, D_MODEL), lambda i, j: (j, 0))],
        out_specs=[pl.BlockSpec((tm, tn), lambda i, j: (i, j)),
                   pl.BlockSpec((tm, PQ_W), lambda i, j: (i, 0)), pl.BlockSpec((tm, PK_W), lambda i, j: (i, 0))],
        out_shape=[jax.ShapeDtypeStruct((n, P_PAD), F32), jax.ShapeDtypeStruct((n, PQ_W), F32),
                   jax.ShapeDtypeStruct((n, PK_W), F32)],
        scratch_shapes=[pltpu.VMEM((tm, D_MODEL), BF16)],
        compiler_params=pltpu.CompilerParams(dimension_semantics=("arbitrary", "arbitrary"), vmem_limit_bytes=VMEM_LIMIT),
        name="in_proj",
    )(x2d, g.reshape(1, D_MODEL), w_perm_t)


def _mix_a_kernel(pc_ref, za_ref, sm_ref, conv0_ref, s0_ref, cw_ref, cb_ref, hp_ref, hpt_ref, gn_ref,
                  tril_ref, triu_ref, bo_ref, o_ref, sout_ref, ext_scr, s_scr, ob_scr, *, t_valid, rows):
    CA = 64
    tc = ob_scr.shape[0]

    @pl.when(pl.program_id(1) == 0)
    def _():
        ext_scr[0:SUB, :] = conv0_ref[...]
        s_scr[...] = s0_ref[...]

    u = _conv_silu(_pad_rows(pc_ref[...], tc), ext_scr, cw_ref, cb_ref)
    bo = bo_ref[...]
    q_raw, k_raw, va = u[:, 0:QK_A], u[:, QK_A:2 * QK_A], u[:, 2 * QK_A:CONV_A]
    qq = q_raw * lax.rsqrt(_dot01_right(q_raw * q_raw, bo) + EPS) * (DK_A ** -0.5)
    ka = k_raw * lax.rsqrt(_dot01_right(k_raw * k_raw, bo) + EPS)
    kat = ka.T

    sm = _pad_rows(sm_ref[...], tc)
    smt = sm.T
    hp, hpt = hp_ref[...], hpt_ref[...]
    g_col = -jnp.exp(hp[0:1, 0:H_A]) * _softplus(sm[:, SM_AA:SM_AA + H_A] + hp[1:2, 0:H_A])
    beta = _sigmoid(sm[:, SM_BA:SM_BA + H_A])
    g_row = -jnp.exp(hpt[0:H_A, 0:1]) * _softplus(smt[SM_AA:SM_AA + H_A, :] + hpt[0:H_A, 1:2])
    if t_valid < tc:
        g_col = jnp.where(_iota2(g_col.shape, 0) < t_valid, g_col, 0.0)
        beta = jnp.where(_iota2(beta.shape, 0) < t_valid, beta, 0.0)
        g_row = jnp.where(_iota2(g_row.shape, 1) < t_valid, g_row, 0.0)
    gcum_col = _dot01_left(tril_ref[...], g_col)
    gcum_row = _dot01_right(g_row, triu_ref[...])

    HS = H_A * CA
    ii, jj = _iota2((HS, HS), 0), _iota2((HS, HS), 1)
    same = (ii >> 6) == (jj >> 6)
    incl, strict = same & (ii >= jj), same & (ii > jj)
    nsc = -(-t_valid // CA)
    pre = []
    for sc in range(nsc):
        r0 = sc * CA
        stack = lambda x, w: jnp.concatenate([x[r0:r0 + CA, h * w:(h + 1) * w] for h in range(H_A)], axis=0)
        ks, qs, vs = stack(ka, DK_A), stack(qq, DK_A), stack(va, DV_A)
        bcol, gc = stack(beta, 1), stack(gcum_col, 1)
        gr = jnp.concatenate([gcum_row[h:h + 1, r0:r0 + CA] for h in range(H_A)], axis=1)
        e = jnp.exp(jnp.where(incl, gc - gr, 0.0))
        kb = ks * bcol
        eg = jnp.exp(gc)
        pre.append(dict(r0=r0, gc=gc, qe=qs * eg, low=jnp.where(strict, _bdot_nt(kb, ks) * e, 0.0),
                        rhs=jnp.concatenate([vs * bcol, kb * eg], axis=1),
                        attn=jnp.where(incl, _bdot_nt(qs, ks) * e, 0.0)))
    tinvs = _tri_inv_many([p['low'] for p in pre], CA)
    sols = [_xdot(t, p['rhs']) for t, p in zip(tinvs, pre)]
    hs = lambda x, h: x[h * CA:(h + 1) * CA]
    for p, sol in zip(pre, sols):
        r0, gc = p['r0'], p['gc']
        states = [s_scr[h] for h in range(H_A)]
        v_new = jnp.concatenate([hs(sol, h)[:, 0:DV_A] - _bdot(hs(sol, h)[:, DV_A:], states[h]) for h in range(H_A)], axis=0)
        o = jnp.concatenate([_bdot(hs(p['qe'], h), states[h]) for h in range(H_A)], axis=0) + _bdot(p['attn'], v_new)
        for h in range(H_A):
            gch = hs(gc, h)
            gl = gch[CA - 1:CA, :]
            s_scr[h] = states[h] * jnp.exp(gl) + _bdot(kat[h * DK_A:(h + 1) * DK_A, r0:r0 + CA], hs(v_new, h) * jnp.exp(gl - gch))
            ob_scr[r0:r0 + CA, h * DV_A:(h + 1) * DV_A] = hs(o, h)
    if nsc * CA < tc:
        ob_scr[nsc * CA:tc, :] = jnp.zeros((tc - nsc * CA, W_A), F32)

    o = ob_scr[...]
    o = o * lax.rsqrt(_dot01_right(o * o, bo) * (1.0 / DV_A) + EPS) * gn_ref[...]
    res = o * _silu(_pad_rows(za_ref[...], tc))
    o_ref[...] = res[0:rows]
    sout_ref[...] = s_scr[...]


def _mix_c_kernel(pc_ref, zc_ref, sm_ref, conv0_ref, s0_ref, cw_ref, cb_ref, hp_ref, hpt_ref, gn_ref,
                  tril_ref, triu_ref, bo_ref, o_ref, sout_ref, ext_scr, s_scr, yb_scr, *, t_valid, rows):
    @pl.when(pl.program_id(1) == 0)
    def _():
        ext_scr[0:SUB, :] = conv0_ref[...]
        s_scr[...] = s0_ref[...]

    u = _conv_silu(_pad_rows(pc_ref[...], TC), ext_scr, cw_ref, cb_ref)
    xc = u[:, 0:D_INNER_C]
    bcm = u[:, D_INNER_C:D_INNER_C + G_C * N_C]
    ccm = u[:, D_INNER_C + G_C * N_C:CONV_C]
    bct = bcm.T
    zc = _pad_rows(zc_ref[...], TC)

    sm = _pad_rows(sm_ref[...], TC)
    smt = sm.T
    hp, hpt = hp_ref[...], hpt_ref[...]
    dt_col = _softplus(sm[:, SM_DT:SM_DT + H_C] + hp[3:4, 0:H_C])
    dt_row = _softplus(smt[SM_DT:SM_DT + H_C, :] + hpt[0:H_C, 3:4])
    if t_valid < TC:
        dt_col = jnp.where(_iota2(dt_col.shape, 0) < t_valid, dt_col, 0.0)
        dt_row = jnp.where(_iota2(dt_row.shape, 1) < t_valid, dt_row, 0.0)
    gcum_col = _dot01_left(tril_ref[...], -jnp.exp(hp[2:3, 0:H_C]) * dt_col)
    gcum_row = _dot01_right(-jnp.exp(hpt[0:H_C, 2:3]) * dt_row, triu_ref[...])

    incl = _iota2((TC, TC), 0) >= _iota2((TC, TC), 1)
    hg = H_C // G_C
    for g in range(G_C):
        cc_g = ccm[:, g * N_C:(g + 1) * N_C]
        gram = _bdot_nt(cc_g, bcm[:, g * N_C:(g + 1) * N_C])
        for hh in range(hg):
            h = g * hg + hh
            c0 = h * P_C
            gc = gcum_col[:, h:h + 1]
            gr = gcum_row[h:h + 1, :]
            dec = jnp.where(incl, jnp.exp(jnp.where(incl, gc - gr, 0.0)), 0.0)
            attn = gram * dt_row[h:h + 1, :] * dec
            xh = xc[:, c0:c0 + P_C]
            S = s_scr[h]
            o = jnp.exp(gc) * _bdot(cc_g, S) + _bdot(attn, xh)
            gl = gc[TC - 1:TC, :]
            s_scr[h] = S * jnp.exp(gl) + _bdot(bct[g * N_C:(g + 1) * N_C, :], xh * (dt_col[:, h:h + 1] * jnp.exp(gl - gc)))
            yb_scr[:, c0:c0 + P_C] = (o + hp[4:5, h:h + 1] * xh) * _silu(zc[:, c0:c0 + P_C])

    y = yb_scr[...]
    y = y * lax.rsqrt(_dot01_right(y * y, bo_ref[...]) * (1.0 / (D_INNER_C // G_C)) + EPS) * gn_ref[...]
    o_ref[...] = y[0:rows]
    sout_ref[...] = s_scr[...]


_D_LEVELS = 6


def _mix_d_kernel(dg_ref, s0_ref, lb_ref, gn_ref, tril_ref, sel_ref, bo_ref, o_ref, sout_ref, s_scr, ob_scr, *, t_valid, rows):
    CD = 64
    W = H_D * DK_D

    @pl.when(pl.program_id(1) == 0)
    def _():
        s_scr[...] = s0_ref[...]

    tc = ob_scr.shape[0]
    dg = _pad_rows(dg_ref[...], tc)
    q = _silu(dg[:, 0:W])
    fr = dg[:, W:2 * W]
    v = dg[:, 2 * W:3 * W]
    gd = dg[:, 3 * W:4 * W]
    lb = lb_ref[...]
    logf = jnp.log(jnp.maximum(lb + (1.0 - lb) * _sigmoid(fr), TINY))
    kd = (1.0 - lb) * _sigmoid(-fr)
    if t_valid < tc:
        ok = _iota2((tc, W), 0) < t_valid
        logf = jnp.where(ok, logf, 0.0)
        kd = jnp.where(ok, kd, 0.0)
    gcum = _dot01_left(tril_ref[...], logf)
    gm = _dot01_left(sel_ref[...], gcum)
    qdec = q * jnp.exp(gcum)
    nsc = -(-t_valid // CD)
    gl_rows = jnp.concatenate([jnp.broadcast_to(gcum[(s + 1) * CD - 1:(s + 1) * CD, :], (CD, W)) for s in range(tc // CD)], axis=0)
    kdt = (kd * jnp.exp(gl_rows - gcum)).T
    gct = gcum.T

    HS = H_D * CD
    head_lanes = (_iota2((HS, W), 0) >> 6) == (_iota2((HS, W), 1) >> 6)
    stack = lambda x: jnp.where(head_lanes, jnp.concatenate([x] * H_D, axis=0), 0.0)
    ci, jj = _iota2((HS, CD), 0) & (CD - 1), _iota2((HS, CD), 1)
    later = jnp.where(ci > jj, ci ^ jj, 0)
    lhs, rhs, msk = [q], [kd], [ci == jj]
    for lv in range(1, _D_LEVELS + 1):
        gml = gm[(lv - 1) * tc:lv * tc, :]
        lhs.append(q * jnp.exp(jnp.minimum(gcum - gml, 0.0)))
        rhs.append(kd * jnp.exp(jnp.minimum(gml - gcum, 0.0)))
        msk.append((later >> (lv - 1)) == 1)

    lane_head = _iota2((CD, W), 1) >> 6
    attns = [jnp.zeros((HS, CD), F32) for _ in range(nsc)]
    for a, b, m in zip(lhs, rhs, msk):
        for sc in range(nsc):
            r0 = sc * CD
            attns[sc] = attns[sc] + jnp.where(m, _bdot_nt(stack(a[r0:r0 + CD]), b[r0:r0 + CD]), 0.0)
    intra = []
    for sc in range(nsc):
        full = _bdot(attns[sc], v[sc * CD:(sc + 1) * CD])
        o = jnp.zeros((CD, W), F32)
        for h in range(H_D):
            o = o + jnp.where(lane_head == h, full[h * CD:(h + 1) * CD], 0.0)
        intra.append(o)
    for sc in range(nsc):
        r0 = sc * CD
        sbd = s_scr[...]
        ob_scr[r0:r0 + CD, :] = _bdot(qdec[r0:r0 + CD], sbd) + intra[sc]
        decay = jnp.exp(gct[:, r0 + CD - 1:r0 + CD])
        s_scr[...] = jnp.where(head_lanes, sbd * decay + _bdot(kdt[:, r0:r0 + CD], v[r0:r0 + CD]), 0.0)
    if nsc * CD < tc:
        ob_scr[nsc * CD:tc, :] = jnp.zeros((tc - nsc * CD, W), F32)

    o = ob_scr[...]
    o = o * lax.rsqrt(_dot01_right(o * o, bo_ref[...]) * (1.0 / DV_D) + EPS) * gn_ref[...]
    res = o * _silu(gd)
    o_ref[...] = res[0:rows]
    sout_ref[...] = s_scr[...]


def _np_consts():
    i = np.arange(TC)
    same64 = (i[:, None] // 64) == (i[None, :] // 64)
    tril2 = (same64 & (i[:, None] >= i[None, :])).astype(np.float32)
    tril1 = (i[:, None] >= i[None, :]).astype(np.float32)
    sel = np.zeros((_D_LEVELS * TC, TC), np.float32)
    for lv in range(1, _D_LEVELS + 1):
        m = ((i >> lv) << lv) + (1 << (lv - 1)) - 1
        sel[(lv - 1) * TC + i, m] = 1.0
    bo = lambda n, w: ((np.arange(n)[:, None] // w) == (np.arange(n)[None, :] // w)).astype(np.float32)
    ia = np.arange(TC_A)
    tril2a = (((ia[:, None] // 64) == (ia[None, :] // 64)) & (ia[:, None] >= ia[None, :])).astype(np.float32)
    sela = np.zeros((_D_LEVELS * TC_A, TC_A), np.float32)
    for lv in range(1, _D_LEVELS + 1):
        sela[(lv - 1) * TC_A + ia, ((ia >> lv) << lv) + (1 << (lv - 1)) - 1] = 1.0
    return dict(tril2=tril2, triu2=tril2.T.copy(), tril1=tril1, triu1=tril1.T.copy(), sel=sel,
                tril2a=tril2a, triu2a=tril2a.T.copy(), sela=sela,
                bo64=bo(256, 64), bo256=bo(D_INNER_C, D_INNER_C // G_C))


def _bucket_starts():
    max_exact = N_BUCKETS // 2
    d = np.arange(0, 4 * MAX_DISTANCE)
    lr = np.log(np.maximum(d, 1).astype(np.float32) / max_exact) / math.log(MAX_DISTANCE / max_exact)
    large = np.minimum(max_exact + (np.maximum(lr, 0.0) * (N_BUCKETS - max_exact)).astype(np.int32), N_BUCKETS - 1)
    bucket = np.where(d < max_exact, d, large)
    return [int(np.argmax(bucket == b)) for b in range(N_BUCKETS)]


_BUCKET_STARTS = _bucket_starts()


def _bias_from_dist(d, rb_ref, h):
    out = jnp.full(d.shape, rb_ref[N_BUCKETS - 1, h], F32)
    for b in range(N_BUCKETS - 2, -1, -1):
        out = jnp.where(d < _BUCKET_STARTS[b + 1], rb_ref[b, h], out)
    return out


def _sortable(x):
    b = lax.bitcast_convert_type(x + 0.0, I32)
    return jnp.where(b < 0, b ^ 0x7FFFFFFF, b)


_NEG_KEY = int(np.array([NEG], np.float32).view(np.int32)[0]) ^ 0x7FFFFFFF
_INT_MIN = -2 ** 31


def _selected_i32(k, idx, thr, xthr):
    s = jnp.where(k > thr, 1, jnp.where(k == thr, jnp.where(idx <= xthr, 1, 0), 0))
    return jnp.where(k == _NEG_KEY, 0, s)


def _const_spec(shape):
    nd = len(shape)
    return pl.BlockSpec(shape, lambda *_: (0,) * nd)


def _mix_call(kernel_fn, name, P, B, T, seq_specs, batch_args, const_args, out_w, st_tail, scratch, tc=TC):
    rows = min(T, tc)
    nc = max(T // tc, 1)
    assert rows * nc == T
    in_specs = [pl.BlockSpec((rows, w), functools.partial(lambda b, c, ci: (b * nc + c, ci), ci=off // w)) for off, w in seq_specs]
    args = [P] * len(seq_specs)
    for a in batch_args:
        nd = a.ndim
        in_specs.append(pl.BlockSpec((None,) + a.shape[1:], functools.partial(lambda b, c, nd: (b,) + (0,) * (nd - 1), nd=nd)))
        args.append(a)
    for a in const_args:
        in_specs.append(_const_spec(a.shape))
        args.append(a)
    st_nd = len(st_tail)
    return pl.pallas_call(
        functools.partial(kernel_fn, t_valid=rows, rows=rows),
        grid=(B, nc),
        in_specs=in_specs,
        out_specs=[pl.BlockSpec((rows, out_w), lambda b, c: (b * nc + c, 0)),
                   pl.BlockSpec((None,) + st_tail, lambda b, c: (b,) + (0,) * st_nd)],
        out_shape=[jax.ShapeDtypeStruct((B * T, out_w), F32), jax.ShapeDtypeStruct((B,) + st_tail, F32)],
        scratch_shapes=scratch,
        compiler_params=pltpu.CompilerParams(dimension_semantics=("arbitrary", "arbitrary"), vmem_limit_bytes=VMEM_LIMIT),
        name=name,
    )(*args)


CU = 4


def _dsa_prompt_kernel(rb_ref, pq_ref, pk_ref, o_ref,
                       kvt_scr, key_scr, lg_scr, bias_scr, acc_scr, *, n_sel, idx_bits):
    j = pl.program_id(1)
    nch = j + 1

    @pl.when((pl.program_id(0) == 0) & (j == 0))
    def _():
        srow, tcol = _iota2((QB, QB), 0), _iota2((QB, QB), 1)
        for h in range(H_B):
            bias_scr[h, 0] = jnp.full((QB, QB), rb_ref[N_BUCKETS - 1, h], F32)
            bias_scr[h, 1] = _bias_from_dist(tcol - srow + QB, rb_ref, h)
            bias_scr[h, 2] = _bias_from_dist(tcol - srow, rb_ref, h)

    @pl.when(j == 0)
    def _():
        def tr(c, _):
            r = pl.multiple_of(c * QB, QB)
            kvt_scr[c] = pk_ref[pl.ds(r, QB), PK_KV:PK_KV + LANE].T[DH_B:2 * DH_B, :].astype(BF16)
            return 0
        lax.fori_loop(0, pk_ref.shape[0] // QB, tr, 0)

    lanes = lambda rows, w: jnp.concatenate([rows[h * w:(h + 1) * w, :] for h in range(H_B)], axis=1)
    w4 = lanes(pq_ref[:, PQ_SM:PQ_SM + LANE].T[SM_WI:SM_WI + H_IDX, :], 1) * (H_IDX ** -0.5 * D_IDX ** -0.5)
    qi_rhs = lanes(pq_ref[:, PQ_QI:PQ_QI + LANE].T, D_IDX).astype(BF16)
    qb_rhs = (lanes(pq_ref[:, PQ_QB:PQ_QB + W_B].T, DH_B) * (DH_B ** -0.5)).astype(BF16)
    srow, tcol = _iota2((QB, QB), 0), _iota2((QB, QB), 1)
    heads = lambda x: [x[:, h * QB:(h + 1) * QB] for h in range(H_B)]

    ngrp = (nch + (CU - 1)) // CU

    def score_body(g, _):
        for u in range(CU):
            r = pl.multiple_of((g * CU + u) * QB, QB)
            ki = pk_ref[pl.ds(r, QB), PK_SM:PK_SM + LANE][:, SM_KI:SM_KI + D_IDX].astype(BF16)
            s = jnp.maximum(jnp.dot(ki, qi_rhs, preferred_element_type=F32), 0.0) * w4
            sh = heads(s)
            sc = (sh[0] + sh[1]) + (sh[2] + sh[3])
            sc = jnp.where(srow + r <= tcol + j * QB, sc, NEG)
            key_scr[pl.ds(r, QB), :] = _sortable(sc)
        return 0

    lax.fori_loop(0, ngrp, score_body, 0)

    def count(pred):
        def body(g, acc):
            for u in range(CU):
                r = pl.multiple_of((g * CU + u) * QB, QB)
                acc = acc + pred(key_scr[pl.ds(r, QB), :], r).reshape(QB // SUB, SUB, QB).sum(axis=0)
            return acc
        return lax.fori_loop(0, ngrp, body, jnp.zeros((SUB, QB), I32)).sum(axis=0, keepdims=True)

    cnt_ge = lambda t: count(lambda k, r: jnp.where(k >= t, 1, 0))
    c0 = cnt_ge(jnp.zeros((1, QB), I32))
    thr = jnp.where(c0 >= n_sel, 0, _INT_MIN).astype(I32)
    cthr = jnp.where(c0 >= n_sel, c0, ngrp * (CU * QB))

    def bit_body(i, carry):
        t, ct = carry
        trial = t | jnp.left_shift(jnp.int32(1), 30 - i)
        cnt = cnt_ge(trial)
        ok = cnt >= n_sel
        return jnp.where(ok, trial, t), jnp.where(ok, cnt, ct)

    thr, cthr = lax.fori_loop(0, 31, bit_body, (thr, cthr))

    @pl.when(jnp.max(cthr) > n_sel)
    def _():
        quota = n_sel - count(lambda k, r: jnp.where(k > thr, 1, 0))

        def idx_body(i, x):
            trial = x | jnp.left_shift(jnp.int32(1), idx_bits - 1 - i)
            below = count(lambda k, r: jnp.where(k == thr, jnp.where(srow + r < trial, 1, 0), 0))
            return jnp.where(below < quota, trial, x)

        xthr = lax.fori_loop(0, idx_bits, idx_body, jnp.zeros((1, QB), I32))

        def demote_body(g, _):
            for u in range(CU):
                r = pl.multiple_of((g * CU + u) * QB, QB)
                k = key_scr[pl.ds(r, QB), :]
                key_scr[pl.ds(r, QB), :] = jnp.where(k == thr, jnp.where(srow + r > xthr, k - 1, k), k)
            return 0

        lax.fori_loop(0, ngrp, demote_body, 0)

    thr_sel = jnp.maximum(thr, _NEG_KEY + 1)

    def logit_body(g, ms):
        ms = list(ms)
        for u in range(CU):
            c = g * CU + u
            r = pl.multiple_of(c * QB, QB)
            kc = pk_ref[pl.ds(r, QB), PK_KV:PK_KV + LANE][:, 0:DH_B].astype(BF16)
            l4 = heads(jnp.dot(kc, qb_rhs, preferred_element_type=F32))
            sel = key_scr[pl.ds(r, QB), :] >= thr_sel
            bidx = jnp.clip(c - j + 2, 0, 2)
            for h in range(H_B):
                l = jnp.where(sel, l4[h] + bias_scr[h, bidx], NEG)
                lg_scr[h, pl.ds(r, QB), :] = l
                ms[h] = jnp.maximum(ms[h], l.max(axis=0, keepdims=True))
        return tuple(ms)

    ms = lax.fori_loop(0, ngrp, logit_body, tuple(jnp.full((1, QB), NEG, F32) for _ in range(H_B)))
    acc_scr[...] = jnp.zeros(acc_scr.shape, F32)

    def pv_body(g, ss):
        ss = list(ss)
        acc = acc_scr[...]
        for u in range(CU):
            c = g * CU + u
            r = pl.multiple_of(c * QB, QB)
            es = [jnp.exp(lg_scr[h, pl.ds(r, QB), :] - ms[h]) for h in range(H_B)]
            acc = acc + jnp.dot(kvt_scr[c], jnp.concatenate(es, axis=1).astype(BF16), preferred_element_type=F32)
            for h in range(H_B):
                ss[h] = ss[h] + es[h].sum(axis=0, keepdims=True)
        acc_scr[...] = acc
        return tuple(ss)

    ss = lax.fori_loop(0, ngrp, pv_body, tuple(jnp.zeros((1, QB), F32) for _ in range(H_B)))
    acc = heads(acc_scr[...])
    ot = jnp.concatenate([acc[h] / ss[h] for h in range(H_B)], axis=0)
    o_ref[...] = ot.T


def _dsa_prompt(Pq, Pk, rel_bias, B, T):
    nb = T // QB
    assert nb % CU == 0
    assert _BUCKET_STARTS[-1] <= QB + 1
    n_sel = min(TOPK_MAX, T // 4)
    idx_bits = max(1, int(math.ceil(math.log2(T))))
    return pl.pallas_call(
        functools.partial(_dsa_prompt_kernel, n_sel=n_sel, idx_bits=idx_bits),
        grid=(B, nb),
        in_specs=[pl.BlockSpec(memory_space=pltpu.SMEM),
                  pl.BlockSpec((QB, PQ_W), lambda b, j: (b * nb + j, 0)),
                  pl.BlockSpec((T, PK_W), lambda b, j: (b, 0))],
        out_specs=pl.BlockSpec((QB, W_B), lambda b, j: (b * nb + j, 0)),
        out_shape=jax.ShapeDtypeStruct((B * T, W_B), F32),
        scratch_shapes=[pltpu.VMEM((nb, DH_B, QB), BF16), pltpu.VMEM((T, QB), I32), pltpu.VMEM((H_B, T, QB), F32),
                        pltpu.VMEM((H_B, 3, QB, QB), F32), pltpu.VMEM((DH_B, H_B * QB), F32)],
        compiler_params=pltpu.CompilerParams(dimension_semantics=("arbitrary", "arbitrary"), vmem_limit_bytes=VMEM_LIMIT),
        name="dsa_prompt",
    )(rel_bias, Pq, Pk)


PG = 16
GW = PG * PAGE_SIZE


def _stack_heads(x, w):
    return jnp.concatenate([x[:, h * w:(h + 1) * w] for h in range(x.shape[1] // w)], axis=0)


def _dsa_s_scores_kernel(pt_ref, pq_ref, *refs, ng, ds):
    page_refs, o_ref = refs[:PG], refs[PG]
    g = pl.program_id(1)
    sm = pq_ref[:, PQ_SM:PQ_SM + LANE]
    qst = _stack_heads(pq_ref[:, PQ_QI:PQ_QI + LANE], D_IDX)
    wcol = jnp.concatenate([sm[:, SM_WI + h:SM_WI + h + 1] for h in range(H_IDX)], axis=0) * (H_IDX ** -0.5 * D_IDX ** -0.5)

    def tile_scores(qk):
        s = jnp.maximum(qk, 0.0) * wcol
        out = s[0:ds]
        for h in range(1, H_IDX):
            out = out + s[h * ds:(h + 1) * ds]
        return out

    @pl.when(g < ng)
    def _():
        for i in range(PG):
            o_ref[:, i * PAGE_SIZE:(i + 1) * PAGE_SIZE] = tile_scores(_bdot(qst, page_refs[i][...]))

    @pl.when(g == ng)
    def _():
        sc = tile_scores(_bdot_nt(qst, _pad_rows(sm[:, SM_KI:SM_KI + D_IDX], PAGE_SIZE)))
        lane, row = _iota2(sc.shape, 1), _iota2(sc.shape, 0)
        o_ref[:, 0:PAGE_SIZE] = jnp.where(lane <= row, sc, NEG)
        o_ref[:, PAGE_SIZE:GW] = jnp.full((ds, GW - PAGE_SIZE), NEG, F32)


def _dsa_s_attn_kernel(pt_ref, rb_ref, sc_ref, pq_ref, pk_ref, *refs, ng, ds, n_sel, idx_bits):
    k_refs, v_refs = refs[:PG], refs[PG:2 * PG]
    o_ref, key_scr, thr_scr, bias_scr, m_scr, l_scr, acc_scr = refs[2 * PG:]
    g = pl.program_id(1)
    ntile = (ng + 1) * PG
    hr = H_B * ds

    @pl.when(g == 0)
    def _():
        for i in range(ntile):
            key_scr[:, i * LANE:(i + 1) * LANE] = _sortable(sc_ref[:, i * LANE:(i + 1) * LANE])
        lane = _iota2((ds, LANE), 1)

        def count(pred):
            accs = [jnp.zeros((ds, LANE), I32) for _ in range(4)]
            for i in range(ng * PG + 1):
                accs[i % 4] = accs[i % 4] + pred(key_scr[:, i * LANE:(i + 1) * LANE], i * LANE)
            return ((accs[0] + accs[1]) + (accs[2] + accs[3])).sum(axis=1, keepdims=True)

        cnt_ge = lambda t: count(lambda k, off: jnp.where(k >= t, 1, 0))
        c0 = cnt_ge(jnp.zeros((ds, 1), I32))
        thr = jnp.where(c0 >= n_sel, 0, _INT_MIN).astype(I32)
        cthr = jnp.where(c0 >= n_sel, c0, ng * GW + LANE)

        def bit_body(i, carry):
            t, ct = carry
            trial = t | jnp.left_shift(jnp.int32(1), 30 - i)
            cnt = cnt_ge(trial)
            ok = cnt >= n_sel
            return jnp.where(ok, trial, t), jnp.where(ok, cnt, ct)

        thr, cthr = lax.fori_loop(0, 31, bit_body, (thr, cthr))

        def tie_search():
            quota = n_sel - count(lambda k, off: jnp.where(k > thr, 1, 0))

            def idx_body(i, x):
                trial = x | jnp.left_shift(jnp.int32(1), idx_bits - 1 - i)
                below = count(lambda k, off: jnp.where(k == thr, jnp.where(lane + off < trial, 1, 0), 0))
                return jnp.where(below < quota, trial, x)

            return lax.fori_loop(0, idx_bits, idx_body, jnp.zeros((ds, 1), I32))

        xthr = lax.cond(jnp.max(cthr) > n_sel, tie_search, lambda: jnp.full((ds, 1), (1 << idx_bits) - 1, I32))
        thr_scr[0] = jnp.broadcast_to(thr, (ds, LANE))
        thr_scr[1] = jnp.broadcast_to(xthr, (ds, LANE))

        qrow = _iota2((ds, LANE), 0)
        for h in range(H_B):
            bias_scr[0, h * ds:(h + 1) * ds, :] = _bias_from_dist(PAGE_SIZE + qrow - lane, rb_ref, h)
            bias_scr[1, h * ds:(h + 1) * ds, :] = _bias_from_dist(qrow - lane, rb_ref, h)
            bias_scr[2, h * ds:(h + 1) * ds, :] = jnp.full((ds, LANE), rb_ref[N_BUCKETS - 1, h], F32)
        m_scr[...] = jnp.full(m_scr.shape, NEG, F32)
        l_scr[...] = jnp.zeros(l_scr.shape, F32)
        acc_scr[...] = jnp.zeros(acc_scr.shape, F32)

    qst = _stack_heads(pq_ref[:, PQ_QB:PQ_QB + W_B], DH_B) * (DH_B ** -0.5)
    thr, xthr = thr_scr[0], thr_scr[1]

    def select(tile):
        k = key_scr[:, pl.ds(pl.multiple_of(tile * LANE, LANE), LANE)]
        idx = _iota2((ds, LANE), 1) + tile * LANE
        sel = _selected_i32(k, idx, thr, xthr)
        return jnp.concatenate([sel] * H_B, axis=0) > 0

    def update(lg, sel, pv):
        m_old = m_scr[...]
        m_new = jnp.maximum(m_old, jnp.where(sel, lg, NEG).max(axis=1, keepdims=True))
        p = jnp.where(sel, jnp.exp(lg - m_new), 0.0)
        corr = jnp.exp(m_old - m_new)
        l_scr[...] = l_scr[...] * corr + p.sum(axis=1, keepdims=True)
        acc_scr[...] = acc_scr[...] * corr + pv(p)
        m_scr[...] = m_new

    @pl.when(g < ng)
    def _():
        lgs, sels = [], []
        for i in range(PG):
            lg = _bdot(qst, k_refs[i][...])
            if i == PG - 1:
                lg = lg + jnp.where(g == ng - 1, bias_scr[0], bias_scr[2])
            else:
                lg = lg + bias_scr[2]
            lgs.append(lg)
            sels.append(select(g * PG + i))
        vt = jnp.concatenate([v_refs[i][...] for i in range(PG)], axis=1)
        update(jnp.concatenate(lgs, axis=1), jnp.concatenate(sels, axis=1), lambda p: _bdot_nt(p, vt))

    @pl.when(g == ng)
    def _():
        kvn = _pad_rows(pk_ref[:, PK_KV:PK_KV + LANE], PAGE_SIZE)
        lg = _bdot_nt(qst, kvn[:, 0:DH_B]) + bias_scr[1]
        update(lg, select(ng * PG), lambda p: _bdot(p, kvn[:, DH_B:2 * DH_B]))
        o = acc_scr[...] / l_scr[...]
        o_ref[...] = jnp.concatenate([o[h * ds:(h + 1) * ds] for h in range(H_B)], axis=1)


def _dsa_sample(Pq, Pk, rel_bias, cache_k, cache_v, cache_kidx, page_table, layer, B, DS):
    n_pages = page_table.shape[1]
    past = n_pages * PAGE_SIZE
    assert n_pages % PG == 0
    assert _BUCKET_STARTS[-1] <= PAGE_SIZE + 1
    ng = n_pages // PG
    n_sel = min(TOPK_MAX, (past + DS) // 4)
    idx_bits = int(math.ceil(math.log2(past + LANE)))
    s_pad = (ng + 1) * GW

    def page_spec(width, i):
        return pl.BlockSpec((None, None, width, PAGE_SIZE),
                            lambda b, g, pt: (layer, pt[b, jnp.minimum(g, ng - 1) * PG + i], 0, 0))

    row = lambda b, g, pt: (b, 0)
    cp = pltpu.CompilerParams(dimension_semantics=("arbitrary", "arbitrary"), vmem_limit_bytes=VMEM_LIMIT)
    scores = pl.pallas_call(
        functools.partial(_dsa_s_scores_kernel, ng=ng, ds=DS),
        grid_spec=pltpu.PrefetchScalarGridSpec(
            num_scalar_prefetch=1, grid=(B, ng + 1),
            in_specs=[pl.BlockSpec((DS, PQ_W), row)] + [page_spec(D_IDX, i) for i in range(PG)],
            out_specs=pl.BlockSpec((None, DS, GW), lambda b, g, pt: (b, 0, g))),
        out_shape=jax.ShapeDtypeStruct((B, DS, s_pad), F32),
        compiler_params=cp, name="dsa_sample_scores",
    )(page_table, Pq, *([cache_kidx] * PG))

    return pl.pallas_call(
        functools.partial(_dsa_s_attn_kernel, ng=ng, ds=DS, n_sel=n_sel, idx_bits=idx_bits),
        grid_spec=pltpu.PrefetchScalarGridSpec(
            num_scalar_prefetch=1, grid=(B, ng + 1),
            in_specs=[pl.BlockSpec(memory_space=pltpu.SMEM),
                      pl.BlockSpec((None, DS, s_pad), lambda b, g, pt: (b, 0, 0)),
                      pl.BlockSpec((DS, PQ_W), row), pl.BlockSpec((DS, PK_W), row)]
                     + [page_spec(DH_B, i) for i in range(PG)] + [page_spec(DH_B, i) for i in range(PG)],
            out_specs=pl.BlockSpec((DS, W_B), lambda b, g, pt: (b, 0)),
            scratch_shapes=[pltpu.VMEM((DS, s_pad), I32), pltpu.VMEM((2, DS, LANE), I32), pltpu.VMEM((3, H_B * DS, LANE), F32),
                            pltpu.VMEM((H_B * DS, 1), F32), pltpu.VMEM((H_B * DS, 1), F32), pltpu.VMEM((H_B * DS, DH_B), F32)]),
        out_shape=jax.ShapeDtypeStruct((B * DS, W_B), F32),
        compiler_params=cp, name="dsa_sample_attn",
    )(page_table, rel_bias, scores, Pq, Pk, *([cache_k] * PG), *([cache_v] * PG))


def _merge_kernel(x_ref, oa_ref, ob_ref, oc_ref, od_ref, gate_ref, wb_ref, wo_ref, o_ref):
    m = None
    r0 = 0
    for br, ref in enumerate((oa_ref, ob_ref, oc_ref, od_ref)):
        w = ref.shape[1]
        t = _sigmoid(gate_ref[:, br * D_MODEL:(br + 1) * D_MODEL]) * jnp.dot(ref[...].astype(BF16), wb_ref[r0:r0 + w, :], preferred_element_type=F32)
        m = t if m is None else m + t
        r0 += w
    o_ref[...] = x_ref[...] + jnp.dot(m.astype(BF16), wo_ref[...], preferred_element_type=F32)


def _merge(x2d, oa, ob, oc, od, P, wb, wo):
    n = x2d.shape[0]
    tm = min(n, TM_MERGE)
    row = lambda i: (i, 0)
    return pl.pallas_call(
        _merge_kernel,
        grid=(n // tm,),
        in_specs=[pl.BlockSpec((tm, D_MODEL), row), pl.BlockSpec((tm, W_A), row), pl.BlockSpec((tm, W_B), row),
                  pl.BlockSpec((tm, W_C), row), pl.BlockSpec((tm, W_D), row),
                  pl.BlockSpec((tm, N_BRANCH * D_MODEL), lambda i: (i, OFF_GATE // (N_BRANCH * D_MODEL))),
                  _const_spec((MIX_W, D_MODEL)), _const_spec((D_MODEL, D_MODEL))],
        out_specs=pl.BlockSpec((tm, D_MODEL), row),
        out_shape=jax.ShapeDtypeStruct((n, D_MODEL), F32),
        compiler_params=pltpu.CompilerParams(dimension_semantics=("arbitrary",), vmem_limit_bytes=VMEM_LIMIT),
        name="merge",
    )(x2d, oa, ob, oc, od, P, wb, wo)


def _mlp_kernel(x_ref, g_ref, wu_ref, wd_ref, gf_ref, *refs, final):
    if final:
        o_ref, y_ref, h_scr, acc_scr = refs
    else:
        o_ref, h_scr, acc_scr = refs
    f = pl.program_id(1)

    @pl.when(f == 0)
    def _():
        x = x_ref[...]
        r = lax.rsqrt(jnp.mean(x * x, axis=-1, keepdims=True) + EPS)
        h_scr[...] = ((x * r) * g_ref[...]).astype(BF16)
        acc_scr[...] = jnp.zeros(acc_scr.shape, F32)

    a = jnp.maximum(jnp.dot(h_scr[...], wu_ref[...], preferred_element_type=F32), 0.0)
    acc_scr[...] += jnp.dot((a * a).astype(BF16), wd_ref[...], preferred_element_type=F32)

    @pl.when(f == pl.num_programs(1) - 1)
    def _():
        out = x_ref[...] + acc_scr[...]
        o_ref[...] = out
        if final:
            r = lax.rsqrt(jnp.mean(out * out, axis=-1, keepdims=True) + EPS)
            y_ref[...] = (out * r) * gf_ref[...]


def _mlp(x2d, g, wu, wd, gf, final):
    n = x2d.shape[0]
    tm = min(n, TM_MLP)
    tf = TF_MLP
    row = lambda i, f: (i, 0)
    out_spec = pl.BlockSpec((tm, D_MODEL), row)
    shp = jax.ShapeDtypeStruct((n, D_MODEL), F32)
    return pl.pallas_call(
        functools.partial(_mlp_kernel, final=final),
        grid=(n // tm, D_FF // tf),
        in_specs=[pl.BlockSpec((tm, D_MODEL), row), _const_spec((1, D_MODEL)),
                  pl.BlockSpec((D_MODEL, tf), lambda i, f: (0, f)), pl.BlockSpec((tf, D_MODEL), lambda i, f: (f, 0)),
                  _const_spec((1, D_MODEL))],
        out_specs=[out_spec, out_spec] if final else [out_spec],
        out_shape=[shp, shp] if final else [shp],
        scratch_shapes=[pltpu.VMEM((tm, D_MODEL), BF16), pltpu.VMEM((tm, D_MODEL), F32)],
        compiler_params=pltpu.CompilerParams(dimension_semantics=("arbitrary", "arbitrary"), vmem_limit_bytes=VMEM_LIMIT),
        name="mlp_final" if final else "mlp",
    )(x2d, g.reshape(1, D_MODEL), wu, wd, gf.reshape(1, D_MODEL))


def _layer(x, conv_state, s_delta, s_ssm, s_hgrn, lw, consts, sample_ctx, final):
    B, T, _ = x.shape
    x2 = x.reshape(B * T, D_MODEL)
    P, Pq, Pk = _in_proj(x2, lw['norm_mix'], lw['w_in'])

    def conv0(lo, hi):
        if conv_state is None:
            return jnp.zeros((B, SUB, hi - lo), F32)
        return jnp.concatenate([jnp.zeros((B, SUB - (CONV_W - 1), hi - lo), F32), conv_state[:, :, lo:hi]], axis=1)

    zeros_state = lambda n: jnp.zeros((B, n, 64, 64), F32)
    c = consts
    half = CONV_CH // 2
    assert CONV_A == half and OFF_PC % half == 0
    oa, s_a = _mix_call(
        _mix_a_kernel, "mix_a", P, B, T,
        [(OFF_PC, half), (OFF_ZA, W_A), (OFF_SM, LANE)],
        [conv0(0, CONV_A), zeros_state(H_A) if s_delta is None else s_delta],
        [lw['conv_w'][:, 0:CONV_A], lw['conv_b'][None, 0:CONV_A], lw['hp'], lw['hpt'], lw['gn_a'], c['tril2a'], c['triu2a'], c['bo64']],
        W_A, (H_A, DK_A, DV_A),
        [pltpu.VMEM((TC_A + SUB, CONV_A), F32), pltpu.VMEM((H_A, DK_A, DV_A), F32), pltpu.VMEM((TC_A, W_A), F32)],
        tc=TC_A)
    oc, s_c = _mix_call(
        _mix_c_kernel, "mix_c", P, B, T,
        [(OFF_PC + half, half), (OFF_ZC, W_C), (OFF_SM, LANE)],
        [conv0(CONV_A, CONV_CH), zeros_state(H_C) if s_ssm is None else s_ssm],
        [lw['conv_w'][:, CONV_A:], lw['conv_b'][None, CONV_A:], lw['hp'], lw['hpt'], lw['gn_c'], c['tril1'], c['triu1'], c['bo256']],
        W_C, (H_C, N_C, P_C),
        [pltpu.VMEM((TC + SUB, CONV_C), F32), pltpu.VMEM((H_C, N_C, P_C), F32), pltpu.VMEM((TC, W_C), F32)])
    sbd0 = jnp.zeros((B, H_D, DK_D, H_D, DV_D), F32)
    if s_hgrn is not None:
        for h in range(H_D):
            sbd0 = sbd0.at[:, h, :, h, :].set(s_hgrn[:, h])
    sbd0 = sbd0.reshape(B, H_D * DK_D, W_D)
    od, sbd = _mix_call(
        _mix_d_kernel, "mix_d", P, B, T,
        [(OFF_D, 4 * W_D)],
        [sbd0],
        [lw['lb'], lw['gn_d'], c['tril2a'], c['sela'], c['bo64']],
        W_D, (H_D * DK_D, W_D),
        [pltpu.VMEM((H_D * DK_D, W_D), F32), pltpu.VMEM((TC_A, W_D), F32)],
        tc=TC_A)
    sbd5 = sbd.reshape(B, H_D, DK_D, H_D, DV_D)
    s_d = jnp.stack([sbd5[:, h, :, h, :] for h in range(H_D)], axis=1)
    if sample_ctx is None:
        ob = _dsa_prompt(Pq, Pk, lw['rel_bias'], B, T)
    else:
        ob = _dsa_sample(Pq, Pk, lw['rel_bias'], sample_ctx['cache_k'], sample_ctx['cache_v'], sample_ctx['cache_kidx'],
                         sample_ctx['page_table'], sample_ctx['layer'], B, T)
    x1 = _merge(x2, oa, ob, oc, od, P, lw['w_branch'], lw['w_out'])
    outs = _mlp(x1, lw['norm_mlp'], lw['w_up'], lw['w_down'], lw['norm_final'], final)
    xo = outs[0].reshape(B, T, D_MODEL)
    y = outs[1].reshape(B, T, D_MODEL) if final else None
    P3 = P.reshape(B, T, P_PAD)
    Pk3 = Pk.reshape(B, T, PK_W)
    states = (Pk3[:, :, PK_KV:PK_KV + DH_B], Pk3[:, :, PK_KV + DH_B:PK_KV + 2 * DH_B], Pk3[:, :, PK_SM + SM_KI:PK_SM + SM_KI + D_IDX],
              P3[:, T - (CONV_W - 1):, OFF_PC:OFF_PC + CONV_CH], s_a, s_c, s_d)
    return xo, y, states


def kernel(x_prompt, x_sample, cache_k, cache_v, cache_kidx, state_conv, state_delta, state_ssm, state_hgrn, page_table, norm_mix, w_in, conv_w, conv_b, a_log_a, dt_bias_a, gnorm_a, rel_bias, a_log_c, dt_bias_c, d_skip_c, gnorm_c, hgrn_gamma, gnorm_d, w_branch, w_out, norm_mlp, w_up, w_down, norm_final):
    depth = w_in.shape[0]
    npc = _np_consts()
    consts = {k: jnp.asarray(v, BF16) for k, v in npc.items()}
    pg = jax.nn.softmax(hgrn_gamma.astype(F32), axis=0)
    lower_bounds = jnp.cumsum(pg, axis=0) - pg[0]
    cache_kt, cache_vt, cache_kit = (jnp.swapaxes(c, 2, 3) for c in (cache_k, cache_v, cache_kidx))
    yp, ys = x_prompt, x_sample
    new_p, new_s = [], []
    for l in range(depth):
        hp = jnp.zeros((SUB, LANE), F32)
        hp = hp.at[0, 0:H_A].set(a_log_a[l]).at[1, 0:H_A].set(dt_bias_a[l])
        hp = hp.at[2, 0:H_C].set(a_log_c[l]).at[3, 0:H_C].set(dt_bias_c[l]).at[4, 0:H_C].set(d_skip_c[l])
        lw = dict(norm_mix=norm_mix[l], w_in=_permute_w_in(w_in[l]), conv_w=conv_w[l], conv_b=conv_b[l],
                  hp=hp, hpt=hp.T, gn_a=jnp.tile(gnorm_a[l], H_A)[None, :], gn_c=gnorm_c[l][None, :],
                  gn_d=jnp.tile(gnorm_d[l], H_D)[None, :], lb=lower_bounds[l][None, :], rel_bias=rel_bias,
                  w_branch=w_branch[l].astype(BF16), w_out=w_out[l].astype(BF16), norm_mlp=norm_mlp[l],
                  w_up=w_up[l].astype(BF16), w_down=w_down[l].astype(BF16), norm_final=norm_final)
        final = l == depth - 1
        yp, yp_n, st_p = _layer(yp, None, None, None, None, lw, consts, None, final)
        ctx = dict(cache_k=cache_kt, cache_v=cache_vt, cache_kidx=cache_kit, page_table=page_table, layer=l)
        ys, ys_n, st_s = _layer(ys, state_conv[l], state_delta[l], state_ssm[l], state_hgrn[l], lw, consts, ctx, final)
        new_p.append(st_p)
        new_s.append(st_s)
    stack = lambda per_layer: [jnp.stack(items) for items in zip(*per_layer)]
    return (yp_n, ys_n, *stack(new_p), *stack(new_s))
```

```python
import functools
import math

import numpy as np
import jax
import jax.numpy as jnp
from jax import lax
from jax.experimental import pallas as pl
from jax.experimental.pallas import tpu as pltpu

F32 = jnp.float32
BF16 = jnp.bfloat16
I32 = jnp.int32

D_MODEL = 1024
PAGE_SIZE = 128
EPS = 1e-6
NEG = -1e30
TINY = 1e-30
CONV_W = 4
D_FF = 4 * D_MODEL
N_BRANCH = 4
H_A, DK_A, DV_A = 4, 64, 64
H_B, DH_B = 4, 64
H_IDX, D_IDX = 4, 32
TOPK_MAX = 256
N_BUCKETS = 32
MAX_DISTANCE = 128
H_C, P_C, N_C, G_C = 8, 64, 64, 2
H_D, DK_D, DV_D = 4, 64, 64
QK_A = H_A * DK_A
CONV_A = 2 * QK_A + H_A * DV_A
D_INNER_C = H_C * P_C
CONV_C = D_INNER_C + 2 * G_C * N_C
CONV_CH = CONV_A + CONV_C
W_A, W_B, W_C, W_D = H_A * DV_A, H_B * DH_B, D_INNER_C, H_D * DV_D
MIX_W = W_A + W_B + W_C + W_D
IN_WIDTHS = (CONV_CH, W_A, H_A, H_A, W_B, DH_B, DH_B, H_IDX * D_IDX, D_IDX, H_IDX, D_INNER_C, H_C,
             H_D * DK_D, H_D * DK_D, W_D, W_D, N_BRANCH * D_MODEL)
P_IN = sum(IN_WIDTHS)

LANE = 128
SUB = 8
TC = 128
TC_A = 256
QB = 128
VMEM_LIMIT = 48 * 1024 * 1024
TM_IN, TM_MERGE, TM_MLP, TF_MLP = 1024, 256, 512, 1024

OFF_GATE, OFF_D, OFF_ZC, OFF_ZA, OFF_QB, OFF_PC, OFF_KV, OFF_QI, OFF_SM = 0, 4096, 5120, 5632, 5888, 6144, 7680, 7808, 7936
P_PAD = 8064
SM_KI, SM_WI, SM_AA, SM_BA, SM_DT = 0, 32, 36, 40, 44


def _src_offsets():
    offs, o = [], 0
    for w in IN_WIDTHS:
        offs.append(o)
        o += w
    return offs


def _permute_w_in(w):
    (o_pc, o_za, o_aa, o_ba, o_qb, o_kb, o_vb, o_qi, o_ki, o_wi, o_zc, o_dt, o_qd, o_fd, o_id, o_gd, o_gate) = _src_offsets()
    wt = jnp.swapaxes(w, 0, 1)
    seg = lambda o, n: wt[o:o + n, :]
    small = jnp.concatenate([seg(o_ki, D_IDX), seg(o_wi, H_IDX), seg(o_aa, H_A), seg(o_ba, H_A), seg(o_dt, H_C),
                             jnp.zeros((LANE - (D_IDX + H_IDX + 2 * H_A + H_C), w.shape[0]), w.dtype)], axis=0)
    out = jnp.concatenate([
        seg(o_gate, N_BRANCH * D_MODEL),
        seg(o_qd, 4 * W_D),
        seg(o_zc, D_INNER_C), seg(o_za, W_A), seg(o_qb, W_B), seg(o_pc, CONV_CH),
        seg(o_kb, 2 * DH_B),
        seg(o_qi, H_IDX * D_IDX), small], axis=0)
    assert out.shape[0] == P_PAD
    return out.astype(BF16)


def _bdot(a, b):
    return jnp.dot(a.astype(BF16), b.astype(BF16), preferred_element_type=F32)


def _bdot_nt(a, b):
    return lax.dot_general(a.astype(BF16), b.astype(BF16), (((1,), (1,)), ((), ())), preferred_element_type=F32)


def _split2(a):
    hi = a.astype(BF16)
    lo = (a - hi.astype(F32)).astype(BF16)
    return hi, lo


def _split3(a):
    hi = a.astype(BF16)
    r = a - hi.astype(F32)
    mid = r.astype(BF16)
    lo = (r - mid.astype(F32)).astype(BF16)
    return hi, mid, lo


def _dot01_left(m01, x):
    hi, mid, lo = _split3(x)
    d = lambda p: jnp.dot(m01, p, preferred_element_type=F32)
    return d(hi) + (d(mid) + d(lo))


def _dot01_right(x, m01):
    hi, mid, lo = _split3(x)
    d = lambda p: jnp.dot(p, m01, preferred_element_type=F32)
    return d(hi) + (d(mid) + d(lo))


def _xdot(a, b):
    ah, al = _split2(a)
    bh, bl = _split2(b)
    d = lambda p, q: jnp.dot(p, q, preferred_element_type=F32)
    return d(ah, bh) + (d(ah, bl) + d(al, bh))


def _sigmoid(x):
    return 1.0 / (1.0 + jnp.exp(-x))


def _silu(x):
    return x * _sigmoid(x)


def _softplus(x):
    return jnp.maximum(x, 0.0) + jnp.log(1.0 + jnp.exp(-jnp.abs(x)))


def _iota2(shape, dim):
    return lax.broadcasted_iota(I32, shape, dim)


def _pad_rows(x, rows):
    if x.shape[0] == rows:
        return x
    return jnp.concatenate([x, jnp.zeros((rows - x.shape[0],) + x.shape[1:], x.dtype)], axis=0)


def _tri_inv_many(Ls, top):
    n = Ls[0].shape[0]
    ii, jj = _iota2((n, n), 0), _iota2((n, n), 1)
    xor = ii ^ jj
    eye = jnp.where(ii == jj, 1.0, 0.0)
    Ns = [jnp.where((xor >> 3) == 0, -L, 0.0) for L in Ls]
    Xs = [eye + N for N in Ns]
    N2s = [_bdot(N, N) for N in Ns]
    Xs = [X + _bdot(X, N2) for X, N2 in zip(Xs, N2s)]
    N4s = [_bdot(N2, N2) for N2 in N2s]
    Xs = [X + _bdot(X, N4) for X, N4 in zip(Xs, N4s)]
    sh = 4
    while (1 << sh) <= top:
        XBs = [_bdot(X, jnp.where((xor >> (sh - 1)) == 1, L, 0.0)) for X, L in zip(Xs, Ls)]
        Xs = [X - _bdot(XB, X) for X, XB in zip(Xs, XBs)]
        sh += 1
    return Xs


def _conv_silu(pc, ext_scr, cw_ref, cb_ref):
    tc = pc.shape[0]
    ext_scr[SUB:SUB + tc, :] = pc
    y = cb_ref[...]
    for j in range(CONV_W):
        y = y + ext_scr[SUB - (CONV_W - 1) + j:SUB - (CONV_W - 1) + j + tc, :] * cw_ref[j:j + 1, :]
    ext_scr[0:SUB, :] = ext_scr[tc:tc + SUB, :]
    return _silu(y)


TN_IN = 1152
PQ_W = W_B + 2 * LANE
PK_W = 2 * LANE
PQ_QB, PQ_QI, PQ_SM = 0, W_B, W_B + LANE
PK_KV, PK_SM = 0, LANE
_COMPACT_COPIES = (("pq", PQ_QB, OFF_QB, W_B), ("pq", PQ_QI, OFF_QI, LANE), ("pq", PQ_SM, OFF_SM, LANE),
                   ("pk", PK_KV, OFF_KV, LANE), ("pk", PK_SM, OFF_SM, LANE))


def _in_proj_kernel(x_ref, g_ref, w_ref, o_ref, pq_ref, pk_ref, h_scr):
    j = pl.program_id(1)

    @pl.when(j == 0)
    def _():
        x = x_ref[...]
        r = lax.rsqrt(jnp.mean(x * x, axis=-1, keepdims=True) + EPS)
        h_scr[...] = ((x * r) * g_ref[...]).astype(BF16)

    o_ref[...] = lax.dot_general(h_scr[...], w_ref[...], (((1,), (1,)), ((), ())), preferred_element_type=F32)

    def copy(dst_ref, dst, src, width):
        tile, lo = divmod(src, TN_IN)
        assert lo + width <= TN_IN

        @pl.when(j == tile)
        def _():
            dst_ref[:, dst:dst + width] = o_ref[:, lo:lo + width]

    for name, dst, src, width in _COMPACT_COPIES:
        copy(pq_ref if name == "pq" else pk_ref, dst, src, width)


def _in_proj(x2d, g, w_perm_t):
    n = x2d.shape[0]
    tm = min(n, TM_IN)
    tn = TN_IN
    return pl.pallas_call(
        _in_proj_kernel,
        grid=(n // tm, P_PAD // tn),
        in_specs=[pl.BlockSpec((tm, D_MODEL), lambda i, j: (i, 0)),
                  pl.BlockSpec((1, D_MODEL), lambda i, j: (0, 0)),
                  pl.BlockSpec((tn, D_MODEL), lambda i, j: (j, 0))],
        out_specs=[pl.BlockSpec((tm, tn), lambda i, j: (i, j)),
                   pl.BlockSpec((tm, PQ_W), lambda i, j: (i, 0)), pl.BlockSpec((tm, PK_W), lambda i, j: (i, 0))],
        out_shape=[jax.ShapeDtypeStruct((n, P_PAD), F32), jax.ShapeDtypeStruct((n, PQ_W), F32),
                   jax.ShapeDtypeStruct((n, PK_W), F32)],
        scratch_shapes=[pltpu.VMEM((tm, D_MODEL), BF16)],
        compiler_params=pltpu.CompilerParams(dimension_semantics=("arbitrary", "arbitrary"), vmem_limit_bytes=VMEM_LIMIT),
        name="in_proj",
    )(x2d, g.reshape(1, D_MODEL), w_perm_t)


def _mix_a_kernel(pc_ref, za_ref, sm_ref, conv0_ref, s0_ref, cw_ref, cb_ref, hp_ref, hpt_ref, gn_ref,
                  tril_ref, triu_ref, bo_ref, o_ref, sout_ref, ext_scr, s_scr, ob_scr, *, t_valid, rows):
    CA = 64
    tc = ob_scr.shape[0]

    @pl.when(pl.program_id(1) == 0)
    def _():
        ext_scr[0:SUB, :] = conv0_ref[...]
        s_scr[...] = s0_ref[...]

    u = _conv_silu(_pad_rows(pc_ref[...], tc), ext_scr, cw_ref, cb_ref)
    bo = bo_ref[...]
    q_raw, k_raw, va = u[:, 0:QK_A], u[:, QK_A:2 * QK_A], u[:, 2 * QK_A:CONV_A]
    qq = q_raw * lax.rsqrt(_dot01_right(q_raw * q_raw, bo) + EPS) * (DK_A ** -0.5)
    ka = k_raw * lax.rsqrt(_dot01_right(k_raw * k_raw, bo) + EPS)
    kat = ka.T

    sm = _pad_rows(sm_ref[...], tc)
    smt = sm.T
    hp, hpt = hp_ref[...], hpt_ref[...]
    g_col = -jnp.exp(hp[0:1, 0:H_A]) * _softplus(sm[:, SM_AA:SM_AA + H_A] + hp[1:2, 0:H_A])
    beta = _sigmoid(sm[:, SM_BA:SM_BA + H_A])
    g_row = -jnp.exp(hpt[0:H_A, 0:1]) * _softplus(smt[SM_AA:SM_AA + H_A, :] + hpt[0:H_A, 1:2])
    if t_valid < tc:
        g_col = jnp.where(_iota2(g_col.shape, 0) < t_valid, g_col, 0.0)
        beta = jnp.where(_iota2(beta.shape, 0) < t_valid, beta, 0.0)
        g_row = jnp.where(_iota2(g_row.shape, 1) < t_valid, g_row, 0.0)
    gcum_col = _dot01_left(tril_ref[...], g_col)
    gcum_row = _dot01_right(g_row, triu_ref[...])

    HS = H_A * CA
    ii, jj = _iota2((HS, HS), 0), _iota2((HS, HS), 1)
    same = (ii >> 6) == (jj >> 6)
    incl, strict = same & (ii >= jj), same & (ii > jj)
    nsc = -(-t_valid // CA)
    pre = []
    for sc in range(nsc):
        r0 = sc * CA
        stack = lambda x, w: jnp.concatenate([x[r0:r0 + CA, h * w:(h + 1) * w] for h in range(H_A)], axis=0)
        ks, qs, vs = stack(ka, DK_A), stack(qq, DK_A), stack(va, DV_A)
        bcol, gc = stack(beta, 1), stack(gcum_col, 1)
        gr = jnp.concatenate([gcum_row[h:h + 1, r0:r0 + CA] for h in range(H_A)], axis=1)
        e = jnp.exp(jnp.where(incl, gc - gr, 0.0))
        kb = ks * bcol
        eg = jnp.exp(gc)
        pre.append(dict(r0=r0, gc=gc, qe=qs * eg, low=jnp.where(strict, _bdot_nt(kb, ks) * e, 0.0),
                        rhs=jnp.concatenate([vs * bcol, kb * eg], axis=1),
                        attn=jnp.where(incl, _bdot_nt(qs, ks) * e, 0.0)))
    tinvs = _tri_inv_many([p['low'] for p in pre], CA)
    sols = [_xdot(t, p['rhs']) for t, p in zip(tinvs, pre)]
    hs = lambda x, h: x[h * CA:(h + 1) * CA]
    for p, sol in zip(pre, sols):
        r0, gc = p['r0'], p['gc']
        states = [s_scr[h] for h in range(H_A)]
        v_new = jnp.concatenate([hs(sol, h)[:, 0:DV_A] - _bdot(hs(sol, h)[:, DV_A:], states[h]) for h in range(H_A)], axis=0)
        o = jnp.concatenate([_bdot(hs(p['qe'], h), states[h]) for h in range(H_A)], axis=0) + _bdot(p['attn'], v_new)
        for h in range(H_A):
            gch = hs(gc, h)
            gl = gch[CA - 1:CA, :]
            s_scr[h] = states[h] * jnp.exp(gl) + _bdot(kat[h * DK_A:(h + 1) * DK_A, r0:r0 + CA], hs(v_new, h) * jnp.exp(gl - gch))
            ob_scr[r0:r0 + CA, h * DV_A:(h + 1) * DV_A] = hs(o, h)
    if nsc * CA < tc:
        ob_scr[nsc * CA:tc, :] = jnp.zeros((tc - nsc * CA, W_A), F32)

    o = ob_scr[...]
    o = o * lax.rsqrt(_dot01_right(o * o, bo) * (1.0 / DV_A) + EPS) * gn_ref[...]
    res = o * _silu(_pad_rows(za_ref[...], tc))
    o_ref[...] = res[0:rows]
    sout_ref[...] = s_scr[...]


def _mix_c_kernel(pc_ref, zc_ref, sm_ref, conv0_ref, s0_ref, cw_ref, cb_ref, hp_ref, hpt_ref, gn_ref,
                  tril_ref, triu_ref, bo_ref, o_ref, sout_ref, ext_scr, s_scr, yb_scr, *, t_valid, rows):
    @pl.when(pl.program_id(1) == 0)
    def _():
        ext_scr[0:SUB, :] = conv0_ref[...]
        s_scr[...] = s0_ref[...]

    u = _conv_silu(_pad_rows(pc_ref[...], TC), ext_scr, cw_ref, cb_ref)
    xc = u[:, 0:D_INNER_C]
    bcm = u[:, D_INNER_C:D_INNER_C + G_C * N_C]
    ccm = u[:, D_INNER_C + G_C * N_C:CONV_C]
    bct = bcm.T
    zc = _pad_rows(zc_ref[...], TC)

    sm = _pad_rows(sm_ref[...], TC)
    smt = sm.T
    hp, hpt = hp_ref[...], hpt_ref[...]
    dt_col = _softplus(sm[:, SM_DT:SM_DT + H_C] + hp[3:4, 0:H_C])
    dt_row = _softplus(smt[SM_DT:SM_DT + H_C, :] + hpt[0:H_C, 3:4])
    if t_valid < TC:
        dt_col = jnp.where(_iota2(dt_col.shape, 0) < t_valid, dt_col, 0.0)
        dt_row = jnp.where(_iota2(dt_row.shape, 1) < t_valid, dt_row, 0.0)
    gcum_col = _dot01_left(tril_ref[...], -jnp.exp(hp[2:3, 0:H_C]) * dt_col)
    gcum_row = _dot01_right(-jnp.exp(hpt[0:H_C, 2:3]) * dt_row, triu_ref[...])

    incl = _iota2((TC, TC), 0) >= _iota2((TC, TC), 1)
    hg = H_C // G_C
    for g in range(G_C):
        cc_g = ccm[:, g * N_C:(g + 1) * N_C]
        gram = _bdot_nt(cc_g, bcm[:, g * N_C:(g + 1) * N_C])
        for hh in range(hg):
            h = g * hg + hh
            c0 = h * P_C
            gc = gcum_col[:, h:h + 1]
            gr = gcum_row[h:h + 1, :]
            dec = jnp.where(incl, jnp.exp(jnp.where(incl, gc - gr, 0.0)), 0.0)
            attn = gram * dt_row[h:h + 1, :] * dec
            xh = xc[:, c0:c0 + P_C]
            S = s_scr[h]
            o = jnp.exp(gc) * _bdot(cc_g, S) + _bdot(attn, xh)
            gl = gc[TC - 1:TC, :]
            s_scr[h] = S * jnp.exp(gl) + _bdot(bct[g * N_C:(g + 1) * N_C, :], xh * (dt_col[:, h:h + 1] * jnp.exp(gl - gc)))
            yb_scr[:, c0:c0 + P_C] = (o + hp[4:5, h:h + 1] * xh) * _silu(zc[:, c0:c0 + P_C])

    y = yb_scr[...]
    y = y * lax.rsqrt(_dot01_right(y * y, bo_ref[...]) * (1.0 / (D_INNER_C // G_C)) + EPS) * gn_ref[...]
    o_ref[...] = y[0:rows]
    sout_ref[...] = s_scr[...]


_D_LEVELS = 6


def _mix_d_kernel(dg_ref, s0_ref, lb_ref, gn_ref, tril_ref, sel_ref, bo_ref, o_ref, sout_ref, s_scr, ob_scr, *, t_valid, rows):
    CD = 64
    W = H_D * DK_D

    @pl.when(pl.program_id(1) == 0)
    def _():
        s_scr[...] = s0_ref[...]

    tc = ob_scr.shape[0]
    dg = _pad_rows(dg_ref[...], tc)
    q = _silu(dg[:, 0:W])
    fr = dg[:, W:2 * W]
    v = dg[:, 2 * W:3 * W]
    gd = dg[:, 3 * W:4 * W]
    lb = lb_ref[...]
    logf = jnp.log(jnp.maximum(lb + (1.0 - lb) * _sigmoid(fr), TINY))
    kd = (1.0 - lb) * _sigmoid(-fr)
    if t_valid < tc:
        ok = _iota2((tc, W), 0) < t_valid
        logf = jnp.where(ok, logf, 0.0)
        kd = jnp.where(ok, kd, 0.0)
    gcum = _dot01_left(tril_ref[...], logf)
    gm = _dot01_left(sel_ref[...], gcum)
    qdec = q * jnp.exp(gcum)
    nsc = -(-t_valid // CD)
    gl_rows = jnp.concatenate([jnp.broadcast_to(gcum[(s + 1) * CD - 1:(s + 1) * CD, :], (CD, W)) for s in range(tc // CD)], axis=0)
    kdt = (kd * jnp.exp(gl_rows - gcum)).T
    gct = gcum.T

    HS = H_D * CD
    head_lanes = (_iota2((HS, W), 0) >> 6) == (_iota2((HS, W), 1) >> 6)
    stack = lambda x: jnp.where(head_lanes, jnp.concatenate([x] * H_D, axis=0), 0.0)
    ci, jj = _iota2((HS, CD), 0) & (CD - 1), _iota2((HS, CD), 1)
    later = jnp.where(ci > jj, ci ^ jj, 0)
    lhs, rhs, msk = [q], [kd], [ci == jj]
    for lv in range(1, _D_LEVELS + 1):
        gml = gm[(lv - 1) * tc:lv * tc, :]
        lhs.append(q * jnp.exp(jnp.minimum(gcum - gml, 0.0)))
        rhs.append(kd * jnp.exp(jnp.minimum(gml - gcum, 0.0)))
        msk.append((later >> (lv - 1)) == 1)

    lane_head = _iota2((CD, W), 1) >> 6
    attns = [jnp.zeros((HS, CD), F32) for _ in range(nsc)]
    for a, b, m in zip(lhs, rhs, msk):
        for sc in range(nsc):
            r0 = sc * CD
            attns[sc] = attns[sc] + jnp.where(m, _bdot_nt(stack(a[r0:r0 + CD]), b[r0:r0 + CD]), 0.0)
    intra = []
    for sc in range(nsc):
        full = _bdot(attns[sc], v[sc * CD:(sc + 1) * CD])
        o = jnp.zeros((CD, W), F32)
        for h in range(H_D):
            o = o + jnp.where(lane_head == h, full[h * CD:(h + 1) * CD], 0.0)
        intra.append(o)
    for sc in range(nsc):
        r0 = sc * CD
        sbd = s_scr[...]
        ob_scr[r0:r0 + CD, :] = _bdot(qdec[r0:r0 + CD], sbd) + intra[sc]
        decay = jnp.exp(gct[:, r0 + CD - 1:r0 + CD])
        s_scr[...] = jnp.where(head_lanes, sbd * decay + _bdot(kdt[:, r0:r0 + CD], v[r0:r0 + CD]), 0.0)
    if nsc * CD < tc:
        ob_scr[nsc * CD:tc, :] = jnp.zeros((tc - nsc * CD, W), F32)

    o = ob_scr[...]
    o = o * lax.rsqrt(_dot01_right(o * o, bo_ref[...]) * (1.0 / DV_D) + EPS) * gn_ref[...]
    res = o * _silu(gd)
    o_ref[...] = res[0:rows]
    sout_ref[...] = s_scr[...]


def _np_consts():
    i = np.arange(TC)
    same64 = (i[:, None] // 64) == (i[None, :] // 64)
    tril2 = (same64 & (i[:, None] >= i[None, :])).astype(np.float32)
    tril1 = (i[:, None] >= i[None, :]).astype(np.float32)
    sel = np.zeros((_D_LEVELS * TC, TC), np.float32)
    for lv in range(1, _D_LEVELS + 1):
        m = ((i >> lv) << lv) + (1 << (lv - 1)) - 1
        sel[(lv - 1) * TC + i, m] = 1.0
    bo = lambda n, w: ((np.arange(n)[:, None] // w) == (np.arange(n)[None, :] // w)).astype(np.float32)
    ia = np.arange(TC_A)
    tril2a = (((ia[:, None] // 64) == (ia[None, :] // 64)) & (ia[:, None] >= ia[None, :])).astype(np.float32)
    sela = np.zeros((_D_LEVELS * TC_A, TC_A), np.float32)
    for lv in range(1, _D_LEVELS + 1):
        sela[(lv - 1) * TC_A + ia, ((ia >> lv) << lv) + (1 << (lv - 1)) - 1] = 1.0
    return dict(tril2=tril2, triu2=tril2.T.copy(), tril1=tril1, triu1=tril1.T.copy(), sel=sel,
                tril2a=tril2a, triu2a=tril2a.T.copy(), sela=sela,
                bo64=bo(256, 64), bo256=bo(D_INNER_C, D_INNER_C // G_C))


def _bucket_starts():
    max_exact = N_BUCKETS // 2
    d = np.arange(0, 4 * MAX_DISTANCE)
    lr = np.log(np.maximum(d, 1).astype(np.float32) / max_exact) / math.log(MAX_DISTANCE / max_exact)
    large = np.minimum(max_exact + (np.maximum(lr, 0.0) * (N_BUCKETS - max_exact)).astype(np.int32), N_BUCKETS - 1)
    bucket = np.where(d < max_exact, d, large)
    return [int(np.argmax(bucket == b)) for b in range(N_BUCKETS)]


_BUCKET_STARTS = _bucket_starts()


def _bias_from_dist(d, rb_ref, h):
    out = jnp.full(d.shape, rb_ref[N_BUCKETS - 1, h], F32)
    for b in range(N_BUCKETS - 2, -1, -1):
        out = jnp.where(d < _BUCKET_STARTS[b + 1], rb_ref[b, h], out)
    return out


def _sortable(x):
    b = lax.bitcast_convert_type(x + 0.0, I32)
    return jnp.where(b < 0, b ^ 0x7FFFFFFF, b)


_NEG_KEY = int(np.array([NEG], np.float32).view(np.int32)[0]) ^ 0x7FFFFFFF
_INT_MIN = -2 ** 31


def _selected_i32(k, idx, thr, xthr):
    s = jnp.where(k > thr, 1, jnp.where(k == thr, jnp.where(idx <= xthr, 1, 0), 0))
    return jnp.where(k == _NEG_KEY, 0, s)


def _const_spec(shape):
    nd = len(shape)
    return pl.BlockSpec(shape, lambda *_: (0,) * nd)


def _mix_call(kernel_fn, name, P, B, T, seq_specs, batch_args, const_args, out_w, st_tail, scratch, tc=TC):
    rows = min(T, tc)
    nc = max(T // tc, 1)
    assert rows * nc == T
    in_specs = [pl.BlockSpec((rows, w), functools.partial(lambda b, c, ci: (b * nc + c, ci), ci=off // w)) for off, w in seq_specs]
    args = [P] * len(seq_specs)
    for a in batch_args:
        nd = a.ndim
        in_specs.append(pl.BlockSpec((None,) + a.shape[1:], functools.partial(lambda b, c, nd: (b,) + (0,) * (nd - 1), nd=nd)))
        args.append(a)
    for a in const_args:
        in_specs.append(_const_spec(a.shape))
        args.append(a)
    st_nd = len(st_tail)
    return pl.pallas_call(
        functools.partial(kernel_fn, t_valid=rows, rows=rows),
        grid=(B, nc),
        in_specs=in_specs,
        out_specs=[pl.BlockSpec((rows, out_w), lambda b, c: (b * nc + c, 0)),
                   pl.BlockSpec((None,) + st_tail, lambda b, c: (b,) + (0,) * st_nd)],
        out_shape=[jax.ShapeDtypeStruct((B * T, out_w), F32), jax.ShapeDtypeStruct((B,) + st_tail, F32)],
        scratch_shapes=scratch,
        compiler_params=pltpu.CompilerParams(dimension_semantics=("arbitrary", "arbitrary"), vmem_limit_bytes=VMEM_LIMIT),
        name=name,
    )(*args)


CU = 4


def _dsa_prompt_kernel(rb_ref, pq_ref, pk_ref, o_ref,
                       kvt_scr, key_scr, lg_scr, bias_scr, acc_scr, *, n_sel, idx_bits):
    j = pl.program_id(1)
    nch = j + 1

    @pl.when((pl.program_id(0) == 0) & (j == 0))
    def _():
        srow, tcol = _iota2((QB, QB), 0), _iota2((QB, QB), 1)
        for h in range(H_B):
            bias_scr[h, 0] = jnp.full((QB, QB), rb_ref[N_BUCKETS - 1, h], F32)
            bias_scr[h, 1] = _bias_from_dist(tcol - srow + QB, rb_ref, h)
            bias_scr[h, 2] = _bias_from_dist(tcol - srow, rb_ref, h)

    @pl.when(j == 0)
    def _():
        def tr(c, _):
            r = pl.multiple_of(c * QB, QB)
            kvt_scr[c] = pk_ref[pl.ds(r, QB), PK_KV:PK_KV + LANE].T[DH_B:2 * DH_B, :].astype(BF16)
            return 0
        lax.fori_loop(0, pk_ref.shape[0] // QB, tr, 0)

    lanes = lambda rows, w: jnp.concatenate([rows[h * w:(h + 1) * w, :] for h in range(H_B)], axis=1)
    w4 = lanes(pq_ref[:, PQ_SM:PQ_SM + LANE].T[SM_WI:SM_WI + H_IDX, :], 1) * (H_IDX ** -0.5 * D_IDX ** -0.5)
    qi_rhs = lanes(pq_ref[:, PQ_QI:PQ_QI + LANE].T, D_IDX).astype(BF16)
    qb_rhs = (lanes(pq_ref[:, PQ_QB:PQ_QB + W_B].T, DH_B) * (DH_B ** -0.5)).astype(BF16)
    srow, tcol = _iota2((QB, QB), 0), _iota2((QB, QB), 1)
    heads = lambda x: [x[:, h * QB:(h + 1) * QB] for h in range(H_B)]

    ngrp = (nch + (CU - 1)) // CU

    def score_body(g, _):
        for u in range(CU):
            r = pl.multiple_of((g * CU + u) * QB, QB)
            ki = pk_ref[pl.ds(r, QB), PK_SM:PK_SM + LANE][:, SM_KI:SM_KI + D_IDX].astype(BF16)
            s = jnp.maximum(jnp.dot(ki, qi_rhs, preferred_element_type=F32), 0.0) * w4
            sh = heads(s)
            sc = (sh[0] + sh[1]) + (sh[2] + sh[3])
            sc = jnp.where(srow + r <= tcol + j * QB, sc, NEG)
            key_scr[pl.ds(r, QB), :] = _sortable(sc)
        return 0

    lax.fori_loop(0, ngrp, score_body, 0)

    def count(pred):
        def body(g, acc):
            for u in range(CU):
                r = pl.multiple_of((g * CU + u) * QB, QB)
                acc = acc + pred(key_scr[pl.ds(r, QB), :], r).reshape(QB // SUB, SUB, QB).sum(axis=0)
            return acc
        return lax.fori_loop(0, ngrp, body, jnp.zeros((SUB, QB), I32)).sum(axis=0, keepdims=True)

    cnt_ge = lambda t: count(lambda k, r: jnp.where(k >= t, 1, 0))
    c0 = cnt_ge(jnp.zeros((1, QB), I32))
    thr = jnp.where(c0 >= n_sel, 0, _INT_MIN).astype(I32)
    cthr = jnp.where(c0 >= n_sel, c0, ngrp * (CU * QB))

    def count3(t1, t2, t3):
        def body(g, accs):
            accs = list(accs)
            for u in range(CU):
                k = key_scr[pl.ds(pl.multiple_of((g * CU + u) * QB, QB), QB), :]
                for n, t in enumerate((t1, t2, t3)):
                    accs[n] = accs[n] + jnp.where(k >= t, 1, 0).reshape(QB // SUB, SUB, QB).sum(axis=0)
            return tuple(accs)
        accs = lax.fori_loop(0, ngrp, body, tuple(jnp.zeros((SUB, QB), I32) for _ in range(3)))
        return tuple(a.sum(axis=0, keepdims=True) for a in accs)

    def two_bit_body(i, carry):
        t, ct = carry
        lo = jnp.left_shift(jnp.int32(1), 29 - 2 * i)
        hi = jnp.left_shift(jnp.int32(1), 30 - 2 * i)
        t1, t2, t3 = t | lo, t | hi, t | hi | lo
        c1, c2, c3 = count3(t1, t2, t3)
        t = jnp.where(c3 >= n_sel, t3, jnp.where(c2 >= n_sel, t2, jnp.where(c1 >= n_sel, t1, t)))
        ct = jnp.where(c3 >= n_sel, c3, jnp.where(c2 >= n_sel, c2, jnp.where(c1 >= n_sel, c1, ct)))
        return t, ct

    thr, cthr = lax.fori_loop(0, 15, two_bit_body, (thr, cthr))
    last = thr | 1
    c_last = cnt_ge(last)
    thr = jnp.where(c_last >= n_sel, last, thr)
    cthr = jnp.where(c_last >= n_sel, c_last, cthr)

    @pl.when(jnp.max(cthr) > n_sel)
    def _():
        quota = n_sel - count(lambda k, r: jnp.where(k > thr, 1, 0))

        def idx_body(i, x):
            trial = x | jnp.left_shift(jnp.int32(1), idx_bits - 1 - i)
            below = count(lambda k, r: jnp.where(k == thr, jnp.where(srow + r < trial, 1, 0), 0))
            return jnp.where(below < quota, trial, x)

        xthr = lax.fori_loop(0, idx_bits, idx_body, jnp.zeros((1, QB), I32))

        def demote_body(g, _):
            for u in range(CU):
                r = pl.multiple_of((g * CU + u) * QB, QB)
                k = key_scr[pl.ds(r, QB), :]
                key_scr[pl.ds(r, QB), :] = jnp.where(k == thr, jnp.where(srow + r > xthr, k - 1, k), k)
            return 0

        lax.fori_loop(0, ngrp, demote_body, 0)

    thr_sel = jnp.maximum(thr, _NEG_KEY + 1)

    def logit_body(g, ms):
        ms = list(ms)
        for u in range(CU):
            c = g * CU + u
            r = pl.multiple_of(c * QB, QB)
            kc = pk_ref[pl.ds(r, QB), PK_KV:PK_KV + LANE][:, 0:DH_B].astype(BF16)
            l4 = heads(jnp.dot(kc, qb_rhs, preferred_element_type=F32))
            sel = key_scr[pl.ds(r, QB), :] >= thr_sel
            bidx = jnp.clip(c - j + 2, 0, 2)
            for h in range(H_B):
                l = jnp.where(sel, l4[h] + bias_scr[h, bidx], NEG)
                lg_scr[h, pl.ds(r, QB), :] = l
                ms[h] = jnp.maximum(ms[h], l.max(axis=0, keepdims=True))
        return tuple(ms)

    ms = lax.fori_loop(0, ngrp, logit_body, tuple(jnp.full((1, QB), NEG, F32) for _ in range(H_B)))
    acc_scr[...] = jnp.zeros(acc_scr.shape, F32)

    def pv_body(g, ss):
        ss = list(ss)
        acc = acc_scr[...]
        for u in range(CU):
            c = g * CU + u
            r = pl.multiple_of(c * QB, QB)
            es = [jnp.exp(lg_scr[h, pl.ds(r, QB), :] - ms[h]) for h in range(H_B)]
            acc = acc + jnp.dot(kvt_scr[c], jnp.concatenate(es, axis=1).astype(BF16), preferred_element_type=F32)
            for h in range(H_B):
                ss[h] = ss[h] + es[h].sum(axis=0, keepdims=True)
        acc_scr[...] = acc
        return tuple(ss)

    ss = lax.fori_loop(0, ngrp, pv_body, tuple(jnp.zeros((1, QB), F32) for _ in range(H_B)))
    acc = heads(acc_scr[...])
    ot = jnp.concatenate([acc[h] / ss[h] for h in range(H_B)], axis=0)
    o_ref[...] = ot.T


def _dsa_prompt(Pq, Pk, rel_bias, B, T):
    nb = T // QB
    assert nb % CU == 0
    assert _BUCKET_STARTS[-1] <= QB + 1
    n_sel = min(TOPK_MAX, T // 4)
    idx_bits = max(1, int(math.ceil(math.log2(T))))
    return pl.pallas_call(
        functools.partial(_dsa_prompt_kernel, n_sel=n_sel, idx_bits=idx_bits),
        grid=(B, nb),
        in_specs=[pl.BlockSpec(memory_space=pltpu.SMEM),
                  pl.BlockSpec((QB, PQ_W), lambda b, j: (b * nb + j, 0)),
                  pl.BlockSpec((T, PK_W), lambda b, j: (b, 0))],
        out_specs=pl.BlockSpec((QB, W_B), lambda b, j: (b * nb + j, 0)),
        out_shape=jax.ShapeDtypeStruct((B * T, W_B), F32),
        scratch_shapes=[pltpu.VMEM((nb, DH_B, QB), BF16), pltpu.VMEM((T, QB), I32), pltpu.VMEM((H_B, T, QB), F32),
                        pltpu.VMEM((H_B, 3, QB, QB), F32), pltpu.VMEM((DH_B, H_B * QB), F32)],
        compiler_params=pltpu.CompilerParams(dimension_semantics=("arbitrary", "arbitrary"), vmem_limit_bytes=VMEM_LIMIT),
        name="dsa_prompt",
    )(rel_bias, Pq, Pk)


PG = 16
GW = PG * PAGE_SIZE


def _stack_heads(x, w):
    return jnp.concatenate([x[:, h * w:(h + 1) * w] for h in range(x.shape[1] // w)], axis=0)


def _dsa_s_scores_kernel(pt_ref, pq_ref, *refs, ng, ds):
    page_refs, o_ref = refs[:PG], refs[PG]
    g = pl.program_id(1)
    sm = pq_ref[:, PQ_SM:PQ_SM + LANE]
    qst = _stack_heads(pq_ref[:, PQ_QI:PQ_QI + LANE], D_IDX)
    wcol = jnp.concatenate([sm[:, SM_WI + h:SM_WI + h + 1] for h in range(H_IDX)], axis=0) * (H_IDX ** -0.5 * D_IDX ** -0.5)

    def tile_scores(qk):
        s = jnp.maximum(qk, 0.0) * wcol
        out = s[0:ds]
        for h in range(1, H_IDX):
            out = out + s[h * ds:(h + 1) * ds]
        return out

    @pl.when(g < ng)
    def _():
        for i in range(PG):
            o_ref[:, i * PAGE_SIZE:(i + 1) * PAGE_SIZE] = tile_scores(_bdot(qst, page_refs[i][...]))

    @pl.when(g == ng)
    def _():
        sc = tile_scores(_bdot_nt(qst, _pad_rows(sm[:, SM_KI:SM_KI + D_IDX], PAGE_SIZE)))
        lane, row = _iota2(sc.shape, 1), _iota2(sc.shape, 0)
        o_ref[:, 0:PAGE_SIZE] = jnp.where(lane <= row, sc, NEG)
        o_ref[:, PAGE_SIZE:GW] = jnp.full((ds, GW - PAGE_SIZE), NEG, F32)


def _dsa_s_attn_kernel(pt_ref, rb_ref, sc_ref, pq_ref, pk_ref, *refs, ng, ds, n_sel, idx_bits):
    k_refs, v_refs = refs[:PG], refs[PG:2 * PG]
    o_ref, key_scr, thr_scr, bias_scr, m_scr, l_scr, acc_scr = refs[2 * PG:]
    g = pl.program_id(1)
    ntile = (ng + 1) * PG
    hr = H_B * ds

    @pl.when(g == 0)
    def _():
        for i in range(ntile):
            key_scr[:, i * LANE:(i + 1) * LANE] = _sortable(sc_ref[:, i * LANE:(i + 1) * LANE])
        lane = _iota2((ds, LANE), 1)

        def count(pred):
            accs = [jnp.zeros((ds, LANE), I32) for _ in range(4)]
            for i in range(ng * PG + 1):
                accs[i % 4] = accs[i % 4] + pred(key_scr[:, i * LANE:(i + 1) * LANE], i * LANE)
            return ((accs[0] + accs[1]) + (accs[2] + accs[3])).sum(axis=1, keepdims=True)

        cnt_ge = lambda t: count(lambda k, off: jnp.where(k >= t, 1, 0))
        c0 = cnt_ge(jnp.zeros((ds, 1), I32))
        thr = jnp.where(c0 >= n_sel, 0, _INT_MIN).astype(I32)
        cthr = jnp.where(c0 >= n_sel, c0, ng * GW + LANE)

        def bit_body(i, carry):
            t, ct = carry
            trial = t | jnp.left_shift(jnp.int32(1), 30 - i)
            cnt = cnt_ge(trial)
            ok = cnt >= n_sel
            return jnp.where(ok, trial, t), jnp.where(ok, cnt, ct)

        thr, cthr = lax.fori_loop(0, 31, bit_body, (thr, cthr))

        def tie_search():
            quota = n_sel - count(lambda k, off: jnp.where(k > thr, 1, 0))

            def idx_body(i, x):
                trial = x | jnp.left_shift(jnp.int32(1), idx_bits - 1 - i)
                below = count(lambda k, off: jnp.where(k == thr, jnp.where(lane + off < trial, 1, 0), 0))
                return jnp.where(below < quota, trial, x)

            return lax.fori_loop(0, idx_bits, idx_body, jnp.zeros((ds, 1), I32))

        xthr = lax.cond(jnp.max(cthr) > n_sel, tie_search, lambda: jnp.full((ds, 1), (1 << idx_bits) - 1, I32))
        thr_scr[0] = jnp.broadcast_to(thr, (ds, LANE))
        thr_scr[1] = jnp.broadcast_to(xthr, (ds, LANE))

        qrow = _iota2((ds, LANE), 0)
        for h in range(H_B):
            bias_scr[0, h * ds:(h + 1) * ds, :] = _bias_from_dist(PAGE_SIZE + qrow - lane, rb_ref, h)
            bias_scr[1, h * ds:(h + 1) * ds, :] = _bias_from_dist(qrow - lane, rb_ref, h)
            bias_scr[2, h * ds:(h + 1) * ds, :] = jnp.full((ds, LANE), rb_ref[N_BUCKETS - 1, h], F32)
        m_scr[...] = jnp.full(m_scr.shape, NEG, F32)
        l_scr[...] = jnp.zeros(l_scr.shape, F32)
        acc_scr[...] = jnp.zeros(acc_scr.shape, F32)

    qst = _stack_heads(pq_ref[:, PQ_QB:PQ_QB + W_B], DH_B) * (DH_B ** -0.5)
    thr, xthr = thr_scr[0], thr_scr[1]

    def select(tile):
        k = key_scr[:, pl.ds(pl.multiple_of(tile * LANE, LANE), LANE)]
        idx = _iota2((ds, LANE), 1) + tile * LANE
        sel = _selected_i32(k, idx, thr, xthr)
        return jnp.concatenate([sel] * H_B, axis=0) > 0

    def update(lg, sel, pv):
        m_old = m_scr[...]
        m_new = jnp.maximum(m_old, jnp.where(sel, lg, NEG).max(axis=1, keepdims=True))
        p = jnp.where(sel, jnp.exp(lg - m_new), 0.0)
        corr = jnp.exp(m_old - m_new)
        l_scr[...] = l_scr[...] * corr + p.sum(axis=1, keepdims=True)
        acc_scr[...] = acc_scr[...] * corr + pv(p)
        m_scr[...] = m_new

    @pl.when(g < ng)
    def _():
        lgs, sels = [], []
        for i in range(PG):
            lg = _bdot(qst, k_refs[i][...])
            if i == PG - 1:
                lg = lg + jnp.where(g == ng - 1, bias_scr[0], bias_scr[2])
            else:
                lg = lg + bias_scr[2]
            lgs.append(lg)
            sels.append(select(g * PG + i))
        vt = jnp.concatenate([v_refs[i][...] for i in range(PG)], axis=1)
        update(jnp.concatenate(lgs, axis=1), jnp.concatenate(sels, axis=1), lambda p: _bdot_nt(p, vt))

    @pl.when(g == ng)
    def _():
        kvn = _pad_rows(pk_ref[:, PK_KV:PK_KV + LANE], PAGE_SIZE)
        lg = _bdot_nt(qst, kvn[:, 0:DH_B]) + bias_scr[1]
        update(lg, select(ng * PG), lambda p: _bdot(p, kvn[:, DH_B:2 * DH_B]))
        o = acc_scr[...] / l_scr[...]
        o_ref[...] = jnp.concatenate([o[h * ds:(h + 1) * ds] for h in range(H_B)], axis=1)


def _dsa_sample(Pq, Pk, rel_bias, cache_k, cache_v, cache_kidx, page_table, layer, B, DS):
    n_pages = page_table.shape[1]
    past = n_pages * PAGE_SIZE
    assert n_pages % PG == 0
    assert _BUCKET_STARTS[-1] <= PAGE_SIZE + 1
    ng = n_pages // PG
    n_sel = min(TOPK_MAX, (past + DS) // 4)
    idx_bits = int(math.ceil(math.log2(past + LANE)))
    s_pad = (ng + 1) * GW

    def page_spec(width, i):
        return pl.BlockSpec((None, None, width, PAGE_SIZE),
                            lambda b, g, pt: (layer, pt[b, jnp.minimum(g, ng - 1) * PG + i], 0, 0))

    row = lambda b, g, pt: (b, 0)
    cp = pltpu.CompilerParams(dimension_semantics=("arbitrary", "arbitrary"), vmem_limit_bytes=VMEM_LIMIT)
    scores = pl.pallas_call(
        functools.partial(_dsa_s_scores_kernel, ng=ng, ds=DS),
        grid_spec=pltpu.PrefetchScalarGridSpec(
            num_scalar_prefetch=1, grid=(B, ng + 1),
            in_specs=[pl.BlockSpec((DS, PQ_W), row)] + [page_spec(D_IDX, i) for i in range(PG)],
            out_specs=pl.BlockSpec((None, DS, GW), lambda b, g, pt: (b, 0, g))),
        out_shape=jax.ShapeDtypeStruct((B, DS, s_pad), F32),
        compiler_params=cp, name="dsa_sample_scores",
    )(page_table, Pq, *([cache_kidx] * PG))

    return pl.pallas_call(
        functools.partial(_dsa_s_attn_kernel, ng=ng, ds=DS, n_sel=n_sel, idx_bits=idx_bits),
        grid_spec=pltpu.PrefetchScalarGridSpec(
            num_scalar_prefetch=1, grid=(B, ng + 1),
            in_specs=[pl.BlockSpec(memory_space=pltpu.SMEM),
                      pl.BlockSpec((None, DS, s_pad), lambda b, g, pt: (b, 0, 0)),
                      pl.BlockSpec((DS, PQ_W), row), pl.BlockSpec((DS, PK_W), row)]
                     + [page_spec(DH_B, i) for i in range(PG)] + [page_spec(DH_B, i) for i in range(PG)],
            out_specs=pl.BlockSpec((DS, W_B), lambda b, g, pt: (b, 0)),
            scratch_shapes=[pltpu.VMEM((DS, s_pad), I32), pltpu.VMEM((2, DS, LANE), I32), pltpu.VMEM((3, H_B * DS, LANE), F32),
                            pltpu.VMEM((H_B * DS, 1), F32), pltpu.VMEM((H_B * DS, 1), F32), pltpu.VMEM((H_B * DS, DH_B), F32)]),
        out_shape=jax.ShapeDtypeStruct((B * DS, W_B), F32),
        compiler_params=cp, name="dsa_sample_attn",
    )(page_table, rel_bias, scores, Pq, Pk, *([cache_k] * PG), *([cache_v] * PG))


def _merge_kernel(x_ref, oa_ref, ob_ref, oc_ref, od_ref, gate_ref, wb_ref, wo_ref, o_ref):
    m = None
    r0 = 0
    for br, ref in enumerate((oa_ref, ob_ref, oc_ref, od_ref)):
        w = ref.shape[1]
        t = _sigmoid(gate_ref[:, br * D_MODEL:(br + 1) * D_MODEL]) * jnp.dot(ref[...].astype(BF16), wb_ref[r0:r0 + w, :], preferred_element_type=F32)
        m = t if m is None else m + t
        r0 += w
    o_ref[...] = x_ref[...] + jnp.dot(m.astype(BF16), wo_ref[...], preferred_element_type=F32)


def _merge(x2d, oa, ob, oc, od, P, wb, wo):
    n = x2d.shape[0]
    tm = min(n, TM_MERGE)
    row = lambda i: (i, 0)
    return pl.pallas_call(
        _merge_kernel,
        grid=(n // tm,),
        in_specs=[pl.BlockSpec((tm, D_MODEL), row), pl.BlockSpec((tm, W_A), row), pl.BlockSpec((tm, W_B), row),
                  pl.BlockSpec((tm, W_C), row), pl.BlockSpec((tm, W_D), row),
                  pl.BlockSpec((tm, N_BRANCH * D_MODEL), lambda i: (i, OFF_GATE // (N_BRANCH * D_MODEL))),
                  _const_spec((MIX_W, D_MODEL)), _const_spec((D_MODEL, D_MODEL))],
        out_specs=pl.BlockSpec((tm, D_MODEL), row),
        out_shape=jax.ShapeDtypeStruct((n, D_MODEL), F32),
        compiler_params=pltpu.CompilerParams(dimension_semantics=("arbitrary",), vmem_limit_bytes=VMEM_LIMIT),
        name="merge",
    )(x2d, oa, ob, oc, od, P, wb, wo)


def _mlp_kernel(x_ref, g_ref, wu_ref, wd_ref, gf_ref, *refs, final):
    if final:
        o_ref, y_ref, h_scr, acc_scr = refs
    else:
        o_ref, h_scr, acc_scr = refs
    f = pl.program_id(1)

    @pl.when(f == 0)
    def _():
        x = x_ref[...]
        r = lax.rsqrt(jnp.mean(x * x, axis=-1, keepdims=True) + EPS)
        h_scr[...] = ((x * r) * g_ref[...]).astype(BF16)
        acc_scr[...] = jnp.zeros(acc_scr.shape, F32)

    a = jnp.maximum(jnp.dot(h_scr[...], wu_ref[...], preferred_element_type=F32), 0.0)
    acc_scr[...] += jnp.dot((a * a).astype(BF16), wd_ref[...], preferred_element_type=F32)

    @pl.when(f == pl.num_programs(1) - 1)
    def _():
        out = x_ref[...] + acc_scr[...]
        o_ref[...] = out
        if final:
            r = lax.rsqrt(jnp.mean(out * out, axis=-1, keepdims=True) + EPS)
            y_ref[...] = (out * r) * gf_ref[...]


def _mlp(x2d, g, wu, wd, gf, final):
    n = x2d.shape[0]
    tm = min(n, TM_MLP)
    tf = TF_MLP
    row = lambda i, f: (i, 0)
    out_spec = pl.BlockSpec((tm, D_MODEL), row)
    shp = jax.ShapeDtypeStruct((n, D_MODEL), F32)
    return pl.pallas_call(
        functools.partial(_mlp_kernel, final=final),
        grid=(n // tm, D_FF // tf),
        in_specs=[pl.BlockSpec((tm, D_MODEL), row), _const_spec((1, D_MODEL)),
                  pl.BlockSpec((D_MODEL, tf), lambda i, f: (0, f)), pl.BlockSpec((tf, D_MODEL), lambda i, f: (f, 0)),
                  _const_spec((1, D_MODEL))],
        out_specs=[out_spec, out_spec] if final else [out_spec],
        out_shape=[shp, shp] if final else [shp],
        scratch_shapes=[pltpu.VMEM((tm, D_MODEL), BF16), pltpu.VMEM((tm, D_MODEL), F32)],
        compiler_params=pltpu.CompilerParams(dimension_semantics=("arbitrary", "arbitrary"), vmem_limit_bytes=VMEM_LIMIT),
        name="mlp_final" if final else "mlp",
    )(x2d, g.reshape(1, D_MODEL), wu, wd, gf.reshape(1, D_MODEL))


def _layer(x, conv_state, s_delta, s_ssm, s_hgrn, lw, consts, sample_ctx, final):
    B, T, _ = x.shape
    x2 = x.reshape(B * T, D_MODEL)
    P, Pq, Pk = _in_proj(x2, lw['norm_mix'], lw['w_in'])

    def conv0(lo, hi):
        if conv_state is None:
            return jnp.zeros((B, SUB, hi - lo), F32)
        return jnp.concatenate([jnp.zeros((B, SUB - (CONV_W - 1), hi - lo), F32), conv_state[:, :, lo:hi]], axis=1)

    zeros_state = lambda n: jnp.zeros((B, n, 64, 64), F32)
    c = consts
    half = CONV_CH // 2
    assert CONV_A == half and OFF_PC % half == 0
    oa, s_a = _mix_call(
        _mix_a_kernel, "mix_a", P, B, T,
        [(OFF_PC, half), (OFF_ZA, W_A), (OFF_SM, LANE)],
        [conv0(0, CONV_A), zeros_state(H_A) if s_delta is None else s_delta],
        [lw['conv_w'][:, 0:CONV_A], lw['conv_b'][None, 0:CONV_A], lw['hp'], lw['hpt'], lw['gn_a'], c['tril2a'], c['triu2a'], c['bo64']],
        W_A, (H_A, DK_A, DV_A),
        [pltpu.VMEM((TC_A + SUB, CONV_A), F32), pltpu.VMEM((H_A, DK_A, DV_A), F32), pltpu.VMEM((TC_A, W_A), F32)],
        tc=TC_A)
    oc, s_c = _mix_call(
        _mix_c_kernel, "mix_c", P, B, T,
        [(OFF_PC + half, half), (OFF_ZC, W_C), (OFF_SM, LANE)],
        [conv0(CONV_A, CONV_CH), zeros_state(H_C) if s_ssm is None else s_ssm],
        [lw['conv_w'][:, CONV_A:], lw['conv_b'][None, CONV_A:], lw['hp'], lw['hpt'], lw['gn_c'], c['tril1'], c['triu1'], c['bo256']],
        W_C, (H_C, N_C, P_C),
        [pltpu.VMEM((TC + SUB, CONV_C), F32), pltpu.VMEM((H_C, N_C, P_C), F32), pltpu.VMEM((TC, W_C), F32)])
    sbd0 = jnp.zeros((B, H_D, DK_D, H_D, DV_D), F32)
    if s_hgrn is not None:
        for h in range(H_D):
            sbd0 = sbd0.at[:, h, :, h, :].set(s_hgrn[:, h])
    sbd0 = sbd0.reshape(B, H_D * DK_D, W_D)
    od, sbd = _mix_call(
        _mix_d_kernel, "mix_d", P, B, T,
        [(OFF_D, 4 * W_D)],
        [sbd0],
        [lw['lb'], lw['gn_d'], c['tril2a'], c['sela'], c['bo64']],
        W_D, (H_D * DK_D, W_D),
        [pltpu.VMEM((H_D * DK_D, W_D), F32), pltpu.VMEM((TC_A, W_D), F32)],
        tc=TC_A)
    sbd5 = sbd.reshape(B, H_D, DK_D, H_D, DV_D)
    s_d = jnp.stack([sbd5[:, h, :, h, :] for h in range(H_D)], axis=1)
    if sample_ctx is None:
        ob = _dsa_prompt(Pq, Pk, lw['rel_bias'], B, T)
    else:
        ob = _dsa_sample(Pq, Pk, lw['rel_bias'], sample_ctx['cache_k'], sample_ctx['cache_v'], sample_ctx['cache_kidx'],
                         sample_ctx['page_table'], sample_ctx['layer'], B, T)
    x1 = _merge(x2, oa, ob, oc, od, P, lw['w_branch'], lw['w_out'])
    outs = _mlp(x1, lw['norm_mlp'], lw['w_up'], lw['w_down'], lw['norm_final'], final)
    xo = outs[0].reshape(B, T, D_MODEL)
    y = outs[1].reshape(B, T, D_MODEL) if final else None
    P3 = P.reshape(B, T, P_PAD)
    Pk3 = Pk.reshape(B, T, PK_W)
    states = (Pk3[:, :, PK_KV:PK_KV + DH_B], Pk3[:, :, PK_KV + DH_B:PK_KV + 2 * DH_B], Pk3[:, :, PK_SM + SM_KI:PK_SM + SM_KI + D_IDX],
              P3[:, T - (CONV_W - 1):, OFF_PC:OFF_PC + CONV_CH], s_a, s_c, s_d)
    return xo, y, states


def kernel(x_prompt, x_sample, cache_k, cache_v, cache_kidx, state_conv, state_delta, state_ssm, state_hgrn, page_table, norm_mix, w_in, conv_w, conv_b, a_log_a, dt_bias_a, gnorm_a, rel_bias, a_log_c, dt_bias_c, d_skip_c, gnorm_c, hgrn_gamma, gnorm_d, w_branch, w_out, norm_mlp, w_up, w_down, norm_final):
    depth = w_in.shape[0]
    npc = _np_consts()
    consts = {k: jnp.asarray(v, BF16) for k, v in npc.items()}
    pg = jax.nn.softmax(hgrn_gamma.astype(F32), axis=0)
    lower_bounds = jnp.cumsum(pg, axis=0) - pg[0]
    cache_kt, cache_vt, cache_kit = (jnp.swapaxes(c, 2, 3) for c in (cache_k, cache_v, cache_kidx))
    yp, ys = x_prompt, x_sample
    new_p, new_s = [], []
    for l in range(depth):
        hp = jnp.zeros((SUB, LANE), F32)
        hp = hp.at[0, 0:H_A].set(a_log_a[l]).at[1, 0:H_A].set(dt_bias_a[l])
        hp = hp.at[2, 0:H_C].set(a_log_c[l]).at[3, 0:H_C].set(dt_bias_c[l]).at[4, 0:H_C].set(d_skip_c[l])
        lw = dict(norm_mix=norm_mix[l], w_in=_permute_w_in(w_in[l]), conv_w=conv_w[l], conv_b=conv_b[l],
                  hp=hp, hpt=hp.T, gn_a=jnp.tile(gnorm_a[l], H_A)[None, :], gn_c=gnorm_c[l][None, :],
                  gn_d=jnp.tile(gnorm_d[l], H_D)[None, :], lb=lower_bounds[l][None, :], rel_bias=rel_bias,
                  w_branch=w_branch[l].astype(BF16), w_out=w_out[l].astype(BF16), norm_mlp=norm_mlp[l],
                  w_up=w_up[l].astype(BF16), w_down=w_down[l].astype(BF16), norm_final=norm_final)
        final = l == depth - 1
        yp, yp_n, st_p = _layer(yp, None, None, None, None, lw, consts, None, final)
        ctx = dict(cache_k=cache_kt, cache_v=cache_vt, cache_kidx=cache_kit, page_table=page_table, layer=l)
        ys, ys_n, st_s = _layer(ys, state_conv[l], state_delta[l], state_ssm[l], state_hgrn[l], lw, consts, ctx, final)
        new_p.append(st_p)
        new_s.append(st_s)
    stack = lambda per_layer: [jnp.stack(items) for items in zip(*per_layer)]
    return (yp_n, ys_n, *stack(new_p), *stack(new_s))
```

```python
import functools
import math

import numpy as np
import jax
import jax.numpy as jnp
from jax import lax
from jax.experimental import pallas as pl
from jax.experimental.pallas import tpu as pltpu

F32 = jnp.float32
BF16 = jnp.bfloat16
I32 = jnp.int32

D_MODEL = 1024
PAGE_SIZE = 128
EPS = 1e-6
NEG = -1e30
TINY = 1e-30
CONV_W = 4
D_FF = 4 * D_MODEL
N_BRANCH = 4
H_A, DK_A, DV_A = 4, 64, 64
H_B, DH_B = 4, 64
H_IDX, D_IDX = 4, 32
TOPK_MAX = 256
N_BUCKETS = 32
MAX_DISTANCE = 128
H_C, P_C, N_C, G_C = 8, 64, 64, 2
H_D, DK_D, DV_D = 4, 64, 64
QK_A = H_A * DK_A
CONV_A = 2 * QK_A + H_A * DV_A
D_INNER_C = H_C * P_C
CONV_C = D_INNER_C + 2 * G_C * N_C
CONV_CH = CONV_A + CONV_C
W_A, W_B, W_C, W_D = H_A * DV_A, H_B * DH_B, D_INNER_C, H_D * DV_D
MIX_W = W_A + W_B + W_C + W_D
IN_WIDTHS = (CONV_CH, W_A, H_A, H_A, W_B, DH_B, DH_B, H_IDX * D_IDX, D_IDX, H_IDX, D_INNER_C, H_C,
             H_D * DK_D, H_D * DK_D, W_D, W_D, N_BRANCH * D_MODEL)
P_IN = sum(IN_WIDTHS)

LANE = 128
SUB = 8
TC = 128
TC_A = 256
QB = 128
VMEM_LIMIT = 48 * 1024 * 1024
TM_IN, TM_MERGE, TM_MLP, TF_MLP = 1024, 512, 512, 1024

OFF_GATE, OFF_D, OFF_ZC, OFF_ZA, OFF_QB, OFF_PC, OFF_KV, OFF_QI, OFF_SM = 0, 4096, 5120, 5632, 5888, 6144, 7680, 7808, 7936
P_PAD = 8064
SM_KI, SM_WI, SM_AA, SM_BA, SM_DT = 0, 32, 36, 40, 44


def _src_offsets():
    offs, o = [], 0
    for w in IN_WIDTHS:
        offs.append(o)
        o += w
    return offs


def _permute_w_in(w):
    (o_pc, o_za, o_aa, o_ba, o_qb, o_kb, o_vb, o_qi, o_ki, o_wi, o_zc, o_dt, o_qd, o_fd, o_id, o_gd, o_gate) = _src_offsets()
    wt = jnp.swapaxes(w, 0, 1)
    seg = lambda o, n: wt[o:o + n, :]
    small = jnp.concatenate([seg(o_ki, D_IDX), seg(o_wi, H_IDX), seg(o_aa, H_A), seg(o_ba, H_A), seg(o_dt, H_C),
                             jnp.zeros((LANE - (D_IDX + H_IDX + 2 * H_A + H_C), w.shape[0]), w.dtype)], axis=0)
    out = jnp.concatenate([
        seg(o_gate, N_BRANCH * D_MODEL),
        seg(o_qd, 4 * W_D),
        seg(o_zc, D_INNER_C), seg(o_za, W_A), seg(o_qb, W_B), seg(o_pc, CONV_CH),
        seg(o_kb, 2 * DH_B),
        seg(o_qi, H_IDX * D_IDX), small], axis=0)
    assert out.shape[0] == P_PAD
    return out.astype(BF16)


def _bdot(a, b):
    return jnp.dot(a.astype(BF16), b.astype(BF16), preferred_element_type=F32)


def _bdot_nt(a, b):
    return lax.dot_general(a.astype(BF16), b.astype(BF16), (((1,), (1,)), ((), ())), preferred_element_type=F32)


def _split2(a):
    hi = a.astype(BF16)
    lo = (a - hi.astype(F32)).astype(BF16)
    return hi, lo


def _split3(a):
    hi = a.astype(BF16)
    r = a - hi.astype(F32)
    mid = r.astype(BF16)
    lo = (r - mid.astype(F32)).astype(BF16)
    return hi, mid, lo


def _dot01_left(m01, x):
    hi, mid, lo = _split3(x)
    d = lambda p: jnp.dot(m01, p, preferred_element_type=F32)
    return d(hi) + (d(mid) + d(lo))


def _dot01_right(x, m01):
    hi, mid, lo = _split3(x)
    d = lambda p: jnp.dot(p, m01, preferred_element_type=F32)
    return d(hi) + (d(mid) + d(lo))


def _xdot(a, b):
    ah, al = _split2(a)
    bh, bl = _split2(b)
    d = lambda p, q: jnp.dot(p, q, preferred_element_type=F32)
    return d(ah, bh) + (d(ah, bl) + d(al, bh))


def _sigmoid(x):
    return 1.0 / (1.0 + jnp.exp(-x))


def _silu(x):
    return x * _sigmoid(x)


def _softplus(x):
    return jnp.maximum(x, 0.0) + jnp.log(1.0 + jnp.exp(-jnp.abs(x)))


def _iota2(shape, dim):
    return lax.broadcasted_iota(I32, shape, dim)


def _pad_rows(x, rows):
    if x.shape[0] == rows:
        return x
    return jnp.concatenate([x, jnp.zeros((rows - x.shape[0],) + x.shape[1:], x.dtype)], axis=0)


def _tri_inv_many(Ls, top):
    n = Ls[0].shape[0]
    ii, jj = _iota2((n, n), 0), _iota2((n, n), 1)
    xor = ii ^ jj
    eye = jnp.where(ii == jj, 1.0, 0.0)
    Ns = [jnp.where((xor >> 3) == 0, -L, 0.0) for L in Ls]
    Xs = [eye + N for N in Ns]
    N2s = [_bdot(N, N) for N in Ns]
    Xs = [X + _bdot(X, N2) for X, N2 in zip(Xs, N2s)]
    N4s = [_bdot(N2, N2) for N2 in N2s]
    Xs = [X + _bdot(X, N4) for X, N4 in zip(Xs, N4s)]
    sh = 4
    while (1 << sh) <= top:
        XBs = [_bdot(X, jnp.where((xor >> (sh - 1)) == 1, L, 0.0)) for X, L in zip(Xs, Ls)]
        Xs = [X - _bdot(XB, X) for X, XB in zip(Xs, XBs)]
        sh += 1
    return Xs


def _conv_silu(pc, ext_scr, cw_ref, cb_ref):
    tc = pc.shape[0]
    ext_scr[SUB:SUB + tc, :] = pc
    y = cb_ref[...]
    for j in range(CONV_W):
        y = y + ext_scr[SUB - (CONV_W - 1) + j:SUB - (CONV_W - 1) + j + tc, :] * cw_ref[j:j + 1, :]
    ext_scr[0:SUB, :] = ext_scr[tc:tc + SUB, :]
    return _silu(y)


TN_IN = 1152
PQ_W = W_B + 2 * LANE
PK_W = 2 * LANE
PQ_QB, PQ_QI, PQ_SM = 0, W_B, W_B + LANE
PK_KV, PK_SM = 0, LANE
_COMPACT_COPIES = (("pq", PQ_QB, OFF_QB, W_B), ("pq", PQ_QI, OFF_QI, LANE), ("pq", PQ_SM, OFF_SM, LANE),
                   ("pk", PK_KV, OFF_KV, LANE), ("pk", PK_SM, OFF_SM, LANE))


def _in_proj_kernel(x_ref, g_ref, w_ref, o_ref, pq_ref, pk_ref, h_scr):
    j = pl.program_id(1)

    @pl.when(j == 0)
    def _():
        x = x_ref[...]
        r = lax.rsqrt(jnp.mean(x * x, axis=-1, keepdims=True) + EPS)
        h_scr[...] = ((x * r) * g_ref[...]).astype(BF16)

    o_ref[...] = lax.dot_general(h_scr[...], w_ref[...], (((1,), (1,)), ((), ())), preferred_element_type=F32)

    def copy(dst_ref, dst, src, width):
        tile, lo = divmod(src, TN_IN)
        assert lo + width <= TN_IN

        @pl.when(j == tile)
        def _():
            dst_ref[:, dst:dst + width] = o_ref[:, lo:lo + width]

    for name, dst, src, width in _COMPACT_COPIES:
        copy(pq_ref if name == "pq" else pk_ref, dst, src, width)


def _in_proj(x2d, g, w_perm_t):
    n = x2d.shape[0]
    tm = min(n, TM_IN)
    tn = TN_IN
    return pl.pallas_call(
        _in_proj_kernel,
        grid=(n // tm, P_PAD // tn),
        in_specs=[pl.BlockSpec((tm, D_MODEL), lambda i, j: (i, 0)),
                  pl.BlockSpec((1, D_MODEL), lambda i, j: (0, 0)),
                  pl.BlockSpec((tn, D_MODEL), lambda i, j: (j, 0))],
        out_specs=[pl.BlockSpec((tm, tn), lambda i, j: (i, j)),
                   pl.BlockSpec((tm, PQ_W), lambda i, j: (i, 0)), pl.BlockSpec((tm, PK_W), lambda i, j: (i, 0))],
        out_shape=[jax.ShapeDtypeStruct((n, P_PAD), F32), jax.ShapeDtypeStruct((n, PQ_W), F32),
                   jax.ShapeDtypeStruct((n, PK_W), F32)],
        scratch_shapes=[pltpu.VMEM((tm, D_MODEL), BF16)],
        compiler_params=pltpu.CompilerParams(dimension_semantics=("arbitrary", "arbitrary"), vmem_limit_bytes=VMEM_LIMIT),
        name="in_proj",
    )(x2d, g.reshape(1, D_MODEL), w_perm_t)


def _mix_a_kernel(pc_ref, za_ref, sm_ref, conv0_ref, s0_ref, cw_ref, cb_ref, hp_ref, hpt_ref, gn_ref,
                  tril_ref, triu_ref, bo_ref, o_ref, sout_ref, ext_scr, s_scr, ob_scr, *, t_valid, rows):
    CA = 64
    tc = ob_scr.shape[0]

    @pl.when(pl.program_id(1) == 0)
    def _():
        ext_scr[0:SUB, :] = conv0_ref[...]
        s_scr[...] = s0_ref[...]

    u = _conv_silu(_pad_rows(pc_ref[...], tc), ext_scr, cw_ref, cb_ref)
    bo = bo_ref[...]
    q_raw, k_raw, va = u[:, 0:QK_A], u[:, QK_A:2 * QK_A], u[:, 2 * QK_A:CONV_A]
    qq = q_raw * lax.rsqrt(_dot01_right(q_raw * q_raw, bo) + EPS) * (DK_A ** -0.5)
    ka = k_raw * lax.rsqrt(_dot01_right(k_raw * k_raw, bo) + EPS)
    kat = ka.T

    sm = _pad_rows(sm_ref[...], tc)
    smt = sm.T
    hp, hpt = hp_ref[...], hpt_ref[...]
    g_col = -jnp.exp(hp[0:1, 0:H_A]) * _softplus(sm[:, SM_AA:SM_AA + H_A] + hp[1:2, 0:H_A])
    beta = _sigmoid(sm[:, SM_BA:SM_BA + H_A])
    g_row = -jnp.exp(hpt[0:H_A, 0:1]) * _softplus(smt[SM_AA:SM_AA + H_A, :] + hpt[0:H_A, 1:2])
    if t_valid < tc:
        g_col = jnp.where(_iota2(g_col.shape, 0) < t_valid, g_col, 0.0)
        beta = jnp.where(_iota2(beta.shape, 0) < t_valid, beta, 0.0)
        g_row = jnp.where(_iota2(g_row.shape, 1) < t_valid, g_row, 0.0)
    gcum_col = _dot01_left(tril_ref[...], g_col)
    gcum_row = _dot01_right(g_row, triu_ref[...])

    HS = H_A * CA
    ii, jj = _iota2((HS, HS), 0), _iota2((HS, HS), 1)
    same = (ii >> 6) == (jj >> 6)
    incl, strict = same & (ii >= jj), same & (ii > jj)
    nsc = -(-t_valid // CA)
    pre = []
    for sc in range(nsc):
        r0 = sc * CA
        stack = lambda x, w: jnp.concatenate([x[r0:r0 + CA, h * w:(h + 1) * w] for h in range(H_A)], axis=0)
        ks, qs, vs = stack(ka, DK_A), stack(qq, DK_A), stack(va, DV_A)
        bcol, gc = stack(beta, 1), stack(gcum_col, 1)
        gr = jnp.concatenate([gcum_row[h:h + 1, r0:r0 + CA] for h in range(H_A)], axis=1)
        e = jnp.exp(jnp.where(incl, gc - gr, 0.0))
        kb = ks * bcol
        eg = jnp.exp(gc)
        pre.append(dict(r0=r0, gc=gc, qe=qs * eg, low=jnp.where(strict, _bdot_nt(kb, ks) * e, 0.0),
                        rhs=jnp.concatenate([vs * bcol, kb * eg], axis=1),
                        attn=jnp.where(incl, _bdot_nt(qs, ks) * e, 0.0)))
    tinvs = _tri_inv_many([p['low'] for p in pre], CA)
    sols = [_xdot(t, p['rhs']) for t, p in zip(tinvs, pre)]
    hs = lambda x, h: x[h * CA:(h + 1) * CA]
    for p, sol in zip(pre, sols):
        r0, gc = p['r0'], p['gc']
        states = [s_scr[h] for h in range(H_A)]
        v_new = jnp.concatenate([hs(sol, h)[:, 0:DV_A] - _bdot(hs(sol, h)[:, DV_A:], states[h]) for h in range(H_A)], axis=0)
        o = jnp.concatenate([_bdot(hs(p['qe'], h), states[h]) for h in range(H_A)], axis=0) + _bdot(p['attn'], v_new)
        for h in range(H_A):
            gch = hs(gc, h)
            gl = gch[CA - 1:CA, :]
            s_scr[h] = states[h] * jnp.exp(gl) + _bdot(kat[h * DK_A:(h + 1) * DK_A, r0:r0 + CA], hs(v_new, h) * jnp.exp(gl - gch))
            ob_scr[r0:r0 + CA, h * DV_A:(h + 1) * DV_A] = hs(o, h)
    if nsc * CA < tc:
        ob_scr[nsc * CA:tc, :] = jnp.zeros((tc - nsc * CA, W_A), F32)

    o = ob_scr[...]
    o = o * lax.rsqrt(_dot01_right(o * o, bo) * (1.0 / DV_A) + EPS) * gn_ref[...]
    res = o * _silu(_pad_rows(za_ref[...], tc))
    o_ref[...] = res[0:rows]
    sout_ref[...] = s_scr[...]


def _mix_c_kernel(pc_ref, zc_ref, sm_ref, conv0_ref, s0_ref, cw_ref, cb_ref, hp_ref, hpt_ref, gn_ref,
                  tril_ref, triu_ref, bo_ref, o_ref, sout_ref, ext_scr, s_scr, yb_scr, *, t_valid, rows):
    @pl.when(pl.program_id(1) == 0)
    def _():
        ext_scr[0:SUB, :] = conv0_ref[...]
        s_scr[...] = s0_ref[...]

    u = _conv_silu(_pad_rows(pc_ref[...], TC), ext_scr, cw_ref, cb_ref)
    xc = u[:, 0:D_INNER_C]
    bcm = u[:, D_INNER_C:D_INNER_C + G_C * N_C]
    ccm = u[:, D_INNER_C + G_C * N_C:CONV_C]
    bct = bcm.T
    zc = _pad_rows(zc_ref[...], TC)

    sm = _pad_rows(sm_ref[...], TC)
    smt = sm.T
    hp, hpt = hp_ref[...], hpt_ref[...]
    dt_col = _softplus(sm[:, SM_DT:SM_DT + H_C] + hp[3:4, 0:H_C])
    dt_row = _softplus(smt[SM_DT:SM_DT + H_C, :] + hpt[0:H_C, 3:4])
    if t_valid < TC:
        dt_col = jnp.where(_iota2(dt_col.shape, 0) < t_valid, dt_col, 0.0)
        dt_row = jnp.where(_iota2(dt_row.shape, 1) < t_valid, dt_row, 0.0)
    gcum_col = _dot01_left(tril_ref[...], -jnp.exp(hp[2:3, 0:H_C]) * dt_col)
    gcum_row = _dot01_right(-jnp.exp(hpt[0:H_C, 2:3]) * dt_row, triu_ref[...])

    incl = _iota2((TC, TC), 0) >= _iota2((TC, TC), 1)
    hg = H_C // G_C
    for g in range(G_C):
        cc_g = ccm[:, g * N_C:(g + 1) * N_C]
        gram = _bdot_nt(cc_g, bcm[:, g * N_C:(g + 1) * N_C])
        for hh in range(hg):
            h = g * hg + hh
            c0 = h * P_C
            gc = gcum_col[:, h:h + 1]
            gr = gcum_row[h:h + 1, :]
            dec = jnp.where(incl, jnp.exp(jnp.where(incl, gc - gr, 0.0)), 0.0)
            attn = gram * dt_row[h:h + 1, :] * dec
            xh = xc[:, c0:c0 + P_C]
            S = s_scr[h]
            o = jnp.exp(gc) * _bdot(cc_g, S) + _bdot(attn, xh)
            gl = gc[TC - 1:TC, :]
            s_scr[h] = S * jnp.exp(gl) + _bdot(bct[g * N_C:(g + 1) * N_C, :], xh * (dt_col[:, h:h + 1] * jnp.exp(gl - gc)))
            yb_scr[:, c0:c0 + P_C] = (o + hp[4:5, h:h + 1] * xh) * _silu(zc[:, c0:c0 + P_C])

    y = yb_scr[...]
    y = y * lax.rsqrt(_dot01_right(y * y, bo_ref[...]) * (1.0 / (D_INNER_C // G_C)) + EPS) * gn_ref[...]
    o_ref[...] = y[0:rows]
    sout_ref[...] = s_scr[...]


_D_LEVELS = 6


def _mix_d_kernel(dg_ref, s0_ref, lb_ref, gn_ref, tril_ref, sel_ref, bo_ref, o_ref, sout_ref, s_scr, ob_scr, *, t_valid, rows):
    CD = 64
    W = H_D * DK_D

    @pl.when(pl.program_id(1) == 0)
    def _():
        s_scr[...] = s0_ref[...]

    tc = ob_scr.shape[0]
    dg = _pad_rows(dg_ref[...], tc)
    q = _silu(dg[:, 0:W])
    fr = dg[:, W:2 * W]
    v = dg[:, 2 * W:3 * W]
    gd = dg[:, 3 * W:4 * W]
    lb = lb_ref[...]
    logf = jnp.log(jnp.maximum(lb + (1.0 - lb) * _sigmoid(fr), TINY))
    kd = (1.0 - lb) * _sigmoid(-fr)
    if t_valid < tc:
        ok = _iota2((tc, W), 0) < t_valid
        logf = jnp.where(ok, logf, 0.0)
        kd = jnp.where(ok, kd, 0.0)
    gcum = _dot01_left(tril_ref[...], logf)
    gm = _dot01_left(sel_ref[...], gcum)
    qdec = q * jnp.exp(gcum)
    nsc = -(-t_valid // CD)
    gl_rows = jnp.concatenate([jnp.broadcast_to(gcum[(s + 1) * CD - 1:(s + 1) * CD, :], (CD, W)) for s in range(tc // CD)], axis=0)
    kdt = (kd * jnp.exp(gl_rows - gcum)).T
    gct = gcum.T

    HS = H_D * CD
    head_lanes = (_iota2((HS, W), 0) >> 6) == (_iota2((HS, W), 1) >> 6)
    stack = lambda x: jnp.where(head_lanes, jnp.concatenate([x] * H_D, axis=0), 0.0)
    ci, jj = _iota2((HS, CD), 0) & (CD - 1), _iota2((HS, CD), 1)
    later = jnp.where(ci > jj, ci ^ jj, 0)
    lhs, rhs, msk = [q], [kd], [ci == jj]
    for lv in range(1, _D_LEVELS + 1):
        gml = gm[(lv - 1) * tc:lv * tc, :]
        lhs.append(q * jnp.exp(jnp.minimum(gcum - gml, 0.0)))
        rhs.append(kd * jnp.exp(jnp.minimum(gml - gcum, 0.0)))
        msk.append((later >> (lv - 1)) == 1)

    lane_head = _iota2((CD, W), 1) >> 6
    attns = [jnp.zeros((HS, CD), F32) for _ in range(nsc)]
    for a, b, m in zip(lhs, rhs, msk):
        for sc in range(nsc):
            r0 = sc * CD
            attns[sc] = attns[sc] + jnp.where(m, _bdot_nt(stack(a[r0:r0 + CD]), b[r0:r0 + CD]), 0.0)
    intra = []
    for sc in range(nsc):
        full = _bdot(attns[sc], v[sc * CD:(sc + 1) * CD])
        o = jnp.zeros((CD, W), F32)
        for h in range(H_D):
            o = o + jnp.where(lane_head == h, full[h * CD:(h + 1) * CD], 0.0)
        intra.append(o)
    for sc in range(nsc):
        r0 = sc * CD
        sbd = s_scr[...]
        ob_scr[r0:r0 + CD, :] = _bdot(qdec[r0:r0 + CD], sbd) + intra[sc]
        decay = jnp.exp(gct[:, r0 + CD - 1:r0 + CD])
        s_scr[...] = jnp.where(head_lanes, sbd * decay + _bdot(kdt[:, r0:r0 + CD], v[r0:r0 + CD]), 0.0)
    if nsc * CD < tc:
        ob_scr[nsc * CD:tc, :] = jnp.zeros((tc - nsc * CD, W), F32)

    o = ob_scr[...]
    o = o * lax.rsqrt(_dot01_right(o * o, bo_ref[...]) * (1.0 / DV_D) + EPS) * gn_ref[...]
    res = o * _silu(gd)
    o_ref[...] = res[0:rows]
    sout_ref[...] = s_scr[...]


def _np_consts():
    i = np.arange(TC)
    same64 = (i[:, None] // 64) == (i[None, :] // 64)
    tril2 = (same64 & (i[:, None] >= i[None, :])).astype(np.float32)
    tril1 = (i[:, None] >= i[None, :]).astype(np.float32)
    sel = np.zeros((_D_LEVELS * TC, TC), np.float32)
    for lv in range(1, _D_LEVELS + 1):
        m = ((i >> lv) << lv) + (1 << (lv - 1)) - 1
        sel[(lv - 1) * TC + i, m] = 1.0
    bo = lambda n, w: ((np.arange(n)[:, None] // w) == (np.arange(n)[None, :] // w)).astype(np.float32)
    ia = np.arange(TC_A)
    tril2a = (((ia[:, None] // 64) == (ia[None, :] // 64)) & (ia[:, None] >= ia[None, :])).astype(np.float32)
    sela = np.zeros((_D_LEVELS * TC_A, TC_A), np.float32)
    for lv in range(1, _D_LEVELS + 1):
        sela[(lv - 1) * TC_A + ia, ((ia >> lv) << lv) + (1 << (lv - 1)) - 1] = 1.0
    return dict(tril2=tril2, triu2=tril2.T.copy(), tril1=tril1, triu1=tril1.T.copy(), sel=sel,
                tril2a=tril2a, triu2a=tril2a.T.copy(), sela=sela,
                bo64=bo(256, 64), bo256=bo(D_INNER_C, D_INNER_C // G_C))


def _bucket_starts():
    max_exact = N_BUCKETS // 2
    d = np.arange(0, 4 * MAX_DISTANCE)
    lr = np.log(np.maximum(d, 1).astype(np.float32) / max_exact) / math.log(MAX_DISTANCE / max_exact)
    large = np.minimum(max_exact + (np.maximum(lr, 0.0) * (N_BUCKETS - max_exact)).astype(np.int32), N_BUCKETS - 1)
    bucket = np.where(d < max_exact, d, large)
    return [int(np.argmax(bucket == b)) for b in range(N_BUCKETS)]


_BUCKET_STARTS = _bucket_starts()


def _bias_from_dist(d, rb_ref, h):
    out = jnp.full(d.shape, rb_ref[N_BUCKETS - 1, h], F32)
    for b in range(N_BUCKETS - 2, -1, -1):
        out = jnp.where(d < _BUCKET_STARTS[b + 1], rb_ref[b, h], out)
    return out


def _sortable(x):
    b = lax.bitcast_convert_type(x + 0.0, I32)
    return jnp.where(b < 0, b ^ 0x7FFFFFFF, b)


_NEG_KEY = int(np.array([NEG], np.float32).view(np.int32)[0]) ^ 0x7FFFFFFF
_INT_MIN = -2 ** 31


def _selected_i32(k, idx, thr, xthr):
    s = jnp.where(k > thr, 1, jnp.where(k == thr, jnp.where(idx <= xthr, 1, 0), 0))
    return jnp.where(k == _NEG_KEY, 0, s)


def _const_spec(shape):
    nd = len(shape)
    return pl.BlockSpec(shape, lambda *_: (0,) * nd)


def _mix_call(kernel_fn, name, P, B, T, seq_specs, batch_args, const_args, out_w, st_tail, scratch, tc=TC):
    rows = min(T, tc)
    nc = max(T // tc, 1)
    assert rows * nc == T
    in_specs = [pl.BlockSpec((rows, w), functools.partial(lambda b, c, ci: (b * nc + c, ci), ci=off // w)) for off, w in seq_specs]
    args = [P] * len(seq_specs)
    for a in batch_args:
        nd = a.ndim
        in_specs.append(pl.BlockSpec((None,) + a.shape[1:], functools.partial(lambda b, c, nd: (b,) + (0,) * (nd - 1), nd=nd)))
        args.append(a)
    for a in const_args:
        in_specs.append(_const_spec(a.shape))
        args.append(a)
    st_nd = len(st_tail)
    return pl.pallas_call(
        functools.partial(kernel_fn, t_valid=rows, rows=rows),
        grid=(B, nc),
        in_specs=in_specs,
        out_specs=[pl.BlockSpec((rows, out_w), lambda b, c: (b * nc + c, 0)),
                   pl.BlockSpec((None,) + st_tail, lambda b, c: (b,) + (0,) * st_nd)],
        out_shape=[jax.ShapeDtypeStruct((B * T, out_w), F32), jax.ShapeDtypeStruct((B,) + st_tail, F32)],
        scratch_shapes=scratch,
        compiler_params=pltpu.CompilerParams(dimension_semantics=("arbitrary", "arbitrary"), vmem_limit_bytes=VMEM_LIMIT),
        name=name,
    )(*args)


CU = 4


def _dsa_prompt_kernel(rb_ref, pq_ref, pk_ref, o_ref,
                       kvt_scr, key_scr, lg_scr, bias_scr, acc_scr, *, n_sel, idx_bits):
    j = pl.program_id(1)
    nch = j + 1

    @pl.when((pl.program_id(0) == 0) & (j == 0))
    def _():
        srow, tcol = _iota2((QB, QB), 0), _iota2((QB, QB), 1)
        for h in range(H_B):
            bias_scr[h, 0] = jnp.full((QB, QB), rb_ref[N_BUCKETS - 1, h], F32)
            bias_scr[h, 1] = _bias_from_dist(tcol - srow + QB, rb_ref, h)
            bias_scr[h, 2] = _bias_from_dist(tcol - srow, rb_ref, h)

    @pl.when(j == 0)
    def _():
        def tr(c, _):
            r = pl.multiple_of(c * QB, QB)
            kvt_scr[c] = pk_ref[pl.ds(r, QB), PK_KV:PK_KV + LANE].T[DH_B:2 * DH_B, :].astype(BF16)
            return 0
        lax.fori_loop(0, pk_ref.shape[0] // QB, tr, 0)

    lanes = lambda rows, w: jnp.concatenate([rows[h * w:(h + 1) * w, :] for h in range(H_B)], axis=1)
    w4 = lanes(pq_ref[:, PQ_SM:PQ_SM + LANE].T[SM_WI:SM_WI + H_IDX, :], 1) * (H_IDX ** -0.5 * D_IDX ** -0.5)
    qi_rhs = lanes(pq_ref[:, PQ_QI:PQ_QI + LANE].T, D_IDX).astype(BF16)
    qb_rhs = (lanes(pq_ref[:, PQ_QB:PQ_QB + W_B].T, DH_B) * (DH_B ** -0.5)).astype(BF16)
    srow, tcol = _iota2((QB, QB), 0), _iota2((QB, QB), 1)
    heads = lambda x: [x[:, h * QB:(h + 1) * QB] for h in range(H_B)]

    ngrp = (nch + (CU - 1)) // CU

    def score_body(g, _):
        for u in range(CU):
            r = pl.multiple_of((g * CU + u) * QB, QB)
            ki = pk_ref[pl.ds(r, QB), PK_SM:PK_SM + LANE][:, SM_KI:SM_KI + D_IDX].astype(BF16)
            s = jnp.maximum(jnp.dot(ki, qi_rhs, preferred_element_type=F32), 0.0) * w4
            sh = heads(s)
            sc = (sh[0] + sh[1]) + (sh[2] + sh[3])
            sc = jnp.where(srow + r <= tcol + j * QB, sc, NEG)
            key_scr[pl.ds(r, QB), :] = _sortable(sc)
        return 0

    lax.fori_loop(0, ngrp, score_body, 0)

    def count(pred):
        def body(g, acc):
            for u in range(CU):
                r = pl.multiple_of((g * CU + u) * QB, QB)
                acc = acc + pred(key_scr[pl.ds(r, QB), :], r).reshape(QB // SUB, SUB, QB).sum(axis=0)
            return acc
        return lax.fori_loop(0, ngrp, body, jnp.zeros((SUB, QB), I32)).sum(axis=0, keepdims=True)

    cnt_ge = lambda t: count(lambda k, r: jnp.where(k >= t, 1, 0))
    c0 = cnt_ge(jnp.zeros((1, QB), I32))
    thr = jnp.where(c0 >= n_sel, 0, _INT_MIN).astype(I32)
    cthr = jnp.where(c0 >= n_sel, c0, ngrp * (CU * QB))

    def bit_body(i, carry):
        t, ct = carry
        trial = t | jnp.left_shift(jnp.int32(1), 30 - i)
        cnt = cnt_ge(trial)
        ok = cnt >= n_sel
        return jnp.where(ok, trial, t), jnp.where(ok, cnt, ct)

    thr, cthr = lax.fori_loop(0, 31, bit_body, (thr, cthr))

    @pl.when(jnp.max(cthr) > n_sel)
    def _():
        quota = n_sel - count(lambda k, r: jnp.where(k > thr, 1, 0))

        def idx_body(i, x):
            trial = x | jnp.left_shift(jnp.int32(1), idx_bits - 1 - i)
            below = count(lambda k, r: jnp.where(k == thr, jnp.where(srow + r < trial, 1, 0), 0))
            return jnp.where(below < quota, trial, x)

        xthr = lax.fori_loop(0, idx_bits, idx_body, jnp.zeros((1, QB), I32))

        def demote_body(g, _):
            for u in range(CU):
                r = pl.multiple_of((g * CU + u) * QB, QB)
                k = key_scr[pl.ds(r, QB), :]
                key_scr[pl.ds(r, QB), :] = jnp.where(k == thr, jnp.where(srow + r > xthr, k - 1, k), k)
            return 0

        lax.fori_loop(0, ngrp, demote_body, 0)

    thr_sel = jnp.maximum(thr, _NEG_KEY + 1)

    def logit_body(g, ms):
        ms = list(ms)
        for u in range(CU):
            c = g * CU + u
            r = pl.multiple_of(c * QB, QB)
            kc = pk_ref[pl.ds(r, QB), PK_KV:PK_KV + LANE][:, 0:DH_B].astype(BF16)
            l4 = heads(jnp.dot(kc, qb_rhs, preferred_element_type=F32))
            sel = key_scr[pl.ds(r, QB), :] >= thr_sel
            bidx = jnp.clip(c - j + 2, 0, 2)
            for h in range(H_B):
                l = jnp.where(sel, l4[h] + bias_scr[h, bidx], NEG)
                lg_scr[h, pl.ds(r, QB), :] = l
                ms[h] = jnp.maximum(ms[h], l.max(axis=0, keepdims=True))
        return tuple(ms)

    ms = lax.fori_loop(0, ngrp, logit_body, tuple(jnp.full((1, QB), NEG, F32) for _ in range(H_B)))
    acc_scr[...] = jnp.zeros(acc_scr.shape, F32)

    def pv_body(g, ss):
        ss = list(ss)
        acc = acc_scr[...]
        for u in range(CU):
            c = g * CU + u
            r = pl.multiple_of(c * QB, QB)
            es = [jnp.exp(lg_scr[h, pl.ds(r, QB), :] - ms[h]) for h in range(H_B)]
            acc = acc + jnp.dot(kvt_scr[c], jnp.concatenate(es, axis=1).astype(BF16), preferred_element_type=F32)
            for h in range(H_B):
                ss[h] = ss[h] + es[h].sum(axis=0, keepdims=True)
        acc_scr[...] = acc
        return tuple(ss)

    ss = lax.fori_loop(0, ngrp, pv_body, tuple(jnp.zeros((1, QB), F32) for _ in range(H_B)))
    acc = heads(acc_scr[...])
    ot = jnp.concatenate([acc[h] / ss[h] for h in range(H_B)], axis=0)
    o_ref[...] = ot.T


def _dsa_prompt(Pq, Pk, rel_bias, B, T):
    nb = T // QB
    assert nb % CU == 0
    assert _BUCKET_STARTS[-1] <= QB + 1
    n_sel = min(TOPK_MAX, T // 4)
    idx_bits = max(1, int(math.ceil(math.log2(T))))
    return pl.pallas_call(
        functools.partial(_dsa_prompt_kernel, n_sel=n_sel, idx_bits=idx_bits),
        grid=(B, nb),
        in_specs=[pl.BlockSpec(memory_space=pltpu.SMEM),
                  pl.BlockSpec((QB, PQ_W), lambda b, j: (b * nb + j, 0)),
                  pl.BlockSpec((T, PK_W), lambda b, j: (b, 0))],
        out_specs=pl.BlockSpec((QB, W_B), lambda b, j: (b * nb + j, 0)),
        out_shape=jax.ShapeDtypeStruct((B * T, W_B), F32),
        scratch_shapes=[pltpu.VMEM((nb, DH_B, QB), BF16), pltpu.VMEM((T, QB), I32), pltpu.VMEM((H_B, T, QB), F32),
                        pltpu.VMEM((H_B, 3, QB, QB), F32), pltpu.VMEM((DH_B, H_B * QB), F32)],
        compiler_params=pltpu.CompilerParams(dimension_semantics=("arbitrary", "arbitrary"), vmem_limit_bytes=VMEM_LIMIT),
        name="dsa_prompt",
    )(rel_bias, Pq, Pk)


PG = 32
GW = PG * PAGE_SIZE


def _stack_heads(x, w):
    return jnp.concatenate([x[:, h * w:(h + 1) * w] for h in range(x.shape[1] // w)], axis=0)


def _dsa_s_scores_kernel(pt_ref, pq_ref, *refs, ng, ds):
    page_refs, o_ref = refs[:PG], refs[PG]
    g = pl.program_id(1)
    sm = pq_ref[:, PQ_SM:PQ_SM + LANE]
    qst = _stack_heads(pq_ref[:, PQ_QI:PQ_QI + LANE], D_IDX)
    wcol = jnp.concatenate([sm[:, SM_WI + h:SM_WI + h + 1] for h in range(H_IDX)], axis=0) * (H_IDX ** -0.5 * D_IDX ** -0.5)

    def tile_scores(qk):
        s = jnp.maximum(qk, 0.0) * wcol
        out = s[0:ds]
        for h in range(1, H_IDX):
            out = out + s[h * ds:(h + 1) * ds]
        return out

    @pl.when(g < ng)
    def _():
        for i in range(PG):
            o_ref[:, i * PAGE_SIZE:(i + 1) * PAGE_SIZE] = tile_scores(_bdot(qst, page_refs[i][...]))

    @pl.when(g == ng)
    def _():
        sc = tile_scores(_bdot_nt(qst, _pad_rows(sm[:, SM_KI:SM_KI + D_IDX], PAGE_SIZE)))
        lane, row = _iota2(sc.shape, 1), _iota2(sc.shape, 0)
        o_ref[:, 0:PAGE_SIZE] = jnp.where(lane <= row, sc, NEG)
        o_ref[:, PAGE_SIZE:GW] = jnp.full((ds, GW - PAGE_SIZE), NEG, F32)


def _dsa_s_attn_kernel(pt_ref, rb_ref, sc_ref, pq_ref, pk_ref, *refs, ng, ds, n_sel, idx_bits):
    k_refs, v_refs = refs[:PG], refs[PG:2 * PG]
    o_ref, key_scr, thr_scr, bias_scr, m_scr, l_scr, acc_scr = refs[2 * PG:]
    g = pl.program_id(1)
    ntile = (ng + 1) * PG
    hr = H_B * ds

    @pl.when(g == 0)
    def _():
        for i in range(ntile):
            key_scr[:, i * LANE:(i + 1) * LANE] = _sortable(sc_ref[:, i * LANE:(i + 1) * LANE])
        lane = _iota2((ds, LANE), 1)

        def count(pred):
            accs = [jnp.zeros((ds, LANE), I32) for _ in range(4)]
            for i in range(ng * PG + 1):
                accs[i % 4] = accs[i % 4] + pred(key_scr[:, i * LANE:(i + 1) * LANE], i * LANE)
            return ((accs[0] + accs[1]) + (accs[2] + accs[3])).sum(axis=1, keepdims=True)

        cnt_ge = lambda t: count(lambda k, off: jnp.where(k >= t, 1, 0))
        c0 = cnt_ge(jnp.zeros((ds, 1), I32))
        thr = jnp.where(c0 >= n_sel, 0, _INT_MIN).astype(I32)
        cthr = jnp.where(c0 >= n_sel, c0, ng * GW + LANE)

        def bit_body(i, carry):
            t, ct = carry
            trial = t | jnp.left_shift(jnp.int32(1), 30 - i)
            cnt = cnt_ge(trial)
            ok = cnt >= n_sel
            return jnp.where(ok, trial, t), jnp.where(ok, cnt, ct)

        thr, cthr = lax.fori_loop(0, 31, bit_body, (thr, cthr))

        def tie_search():
            quota = n_sel - count(lambda k, off: jnp.where(k > thr, 1, 0))

            def idx_body(i, x):
                trial = x | jnp.left_shift(jnp.int32(1), idx_bits - 1 - i)
                below = count(lambda k, off: jnp.where(k == thr, jnp.where(lane + off < trial, 1, 0), 0))
                return jnp.where(below < quota, trial, x)

            return lax.fori_loop(0, idx_bits, idx_body, jnp.zeros((ds, 1), I32))

        xthr = lax.cond(jnp.max(cthr) > n_sel, tie_search, lambda: jnp.full((ds, 1), (1 << idx_bits) - 1, I32))
        thr_scr[0] = jnp.broadcast_to(thr, (ds, LANE))
        thr_scr[1] = jnp.broadcast_to(xthr, (ds, LANE))

        qrow = _iota2((ds, LANE), 0)
        for h in range(H_B):
            bias_scr[0, h * ds:(h + 1) * ds, :] = _bias_from_dist(PAGE_SIZE + qrow - lane, rb_ref, h)
            bias_scr[1, h * ds:(h + 1) * ds, :] = _bias_from_dist(qrow - lane, rb_ref, h)
            bias_scr[2, h * ds:(h + 1) * ds, :] = jnp.full((ds, LANE), rb_ref[N_BUCKETS - 1, h], F32)
        m_scr[...] = jnp.full(m_scr.shape, NEG, F32)
        l_scr[...] = jnp.zeros(l_scr.shape, F32)
        acc_scr[...] = jnp.zeros(acc_scr.shape, F32)

    qst = _stack_heads(pq_ref[:, PQ_QB:PQ_QB + W_B], DH_B) * (DH_B ** -0.5)
    thr, xthr = thr_scr[0], thr_scr[1]

    def select(tile):
        k = key_scr[:, pl.ds(pl.multiple_of(tile * LANE, LANE), LANE)]
        idx = _iota2((ds, LANE), 1) + tile * LANE
        sel = _selected_i32(k, idx, thr, xthr)
        return jnp.concatenate([sel] * H_B, axis=0) > 0

    def update(lg, sel, pv):
        m_old = m_scr[...]
        m_new = jnp.maximum(m_old, jnp.where(sel, lg, NEG).max(axis=1, keepdims=True))
        p = jnp.where(sel, jnp.exp(lg - m_new), 0.0)
        corr = jnp.exp(m_old - m_new)
        l_scr[...] = l_scr[...] * corr + p.sum(axis=1, keepdims=True)
        acc_scr[...] = acc_scr[...] * corr + pv(p)
        m_scr[...] = m_new

    @pl.when(g < ng)
    def _():
        lgs, sels = [], []
        for i in range(PG):
            lg = _bdot(qst, k_refs[i][...])
            if i == PG - 1:
                lg = lg + jnp.where(g == ng - 1, bias_scr[0], bias_scr[2])
            else:
                lg = lg + bias_scr[2]
            lgs.append(lg)
            sels.append(select(g * PG + i))
        vt = jnp.concatenate([v_refs[i][...] for i in range(PG)], axis=1)
        update(jnp.concatenate(lgs, axis=1), jnp.concatenate(sels, axis=1), lambda p: _bdot_nt(p, vt))

    @pl.when(g == ng)
    def _():
        kvn = _pad_rows(pk_ref[:, PK_KV:PK_KV + LANE], PAGE_SIZE)
        lg = _bdot_nt(qst, kvn[:, 0:DH_B]) + bias_scr[1]
        update(lg, select(ng * PG), lambda p: _bdot(p, kvn[:, DH_B:2 * DH_B]))
        o = acc_scr[...] / l_scr[...]
        o_ref[...] = jnp.concatenate([o[h * ds:(h + 1) * ds] for h in range(H_B)], axis=1)


def _dsa_sample(Pq, Pk, rel_bias, cache_k, cache_v, cache_kidx, page_table, layer, B, DS):
    n_pages = page_table.shape[1]
    past = n_pages * PAGE_SIZE
    assert n_pages % PG == 0
    assert _BUCKET_STARTS[-1] <= PAGE_SIZE + 1
    ng = n_pages // PG
    n_sel = min(TOPK_MAX, (past + DS) // 4)
    idx_bits = int(math.ceil(math.log2(past + LANE)))
    s_pad = (ng + 1) * GW

    def page_spec(width, i):
        return pl.BlockSpec((None, None, width, PAGE_SIZE),
                            lambda b, g, pt: (layer, pt[b, jnp.minimum(g, ng - 1) * PG + i], 0, 0))

    row = lambda b, g, pt: (b, 0)
    cp = pltpu.CompilerParams(dimension_semantics=("arbitrary", "arbitrary"), vmem_limit_bytes=VMEM_LIMIT)
    scores = pl.pallas_call(
        functools.partial(_dsa_s_scores_kernel, ng=ng, ds=DS),
        grid_spec=pltpu.PrefetchScalarGridSpec(
            num_scalar_prefetch=1, grid=(B, ng + 1),
            in_specs=[pl.BlockSpec((DS, PQ_W), row)] + [page_spec(D_IDX, i) for i in range(PG)],
            out_specs=pl.BlockSpec((None, DS, GW), lambda b, g, pt: (b, 0, g))),
        out_shape=jax.ShapeDtypeStruct((B, DS, s_pad), F32),
        compiler_params=cp, name="dsa_sample_scores",
    )(page_table, Pq, *([cache_kidx] * PG))

    return pl.pallas_call(
        functools.partial(_dsa_s_attn_kernel, ng=ng, ds=DS, n_sel=n_sel, idx_bits=idx_bits),
        grid_spec=pltpu.PrefetchScalarGridSpec(
            num_scalar_prefetch=1, grid=(B, ng + 1),
            in_specs=[pl.BlockSpec(memory_space=pltpu.SMEM),
                      pl.BlockSpec((None, DS, s_pad), lambda b, g, pt: (b, 0, 0)),
                      pl.BlockSpec((DS, PQ_W), row), pl.BlockSpec((DS, PK_W), row)]
                     + [page_spec(DH_B, i) for i in range(PG)] + [page_spec(DH_B, i) for i in range(PG)],
            out_specs=pl.BlockSpec((DS, W_B), lambda b, g, pt: (b, 0)),
            scratch_shapes=[pltpu.VMEM((DS, s_pad), I32), pltpu.VMEM((2, DS, LANE), I32), pltpu.VMEM((3, H_B * DS, LANE), F32),
                            pltpu.VMEM((H_B * DS, 1), F32), pltpu.VMEM((H_B * DS, 1), F32), pltpu.VMEM((H_B * DS, DH_B), F32)]),
        out_shape=jax.ShapeDtypeStruct((B * DS, W_B), F32),
        compiler_params=cp, name="dsa_sample_attn",
    )(page_table, rel_bias, scores, Pq, Pk, *([cache_k] * PG), *([cache_v] * PG))


def _merge_kernel(x_ref, oa_ref, ob_ref, oc_ref, od_ref, gate_ref, wb_ref, wo_ref, o_ref):
    m = None
    r0 = 0
    for br, ref in enumerate((oa_ref, ob_ref, oc_ref, od_ref)):
        w = ref.shape[1]
        t = _sigmoid(gate_ref[:, br * D_MODEL:(br + 1) * D_MODEL]) * jnp.dot(ref[...].astype(BF16), wb_ref[r0:r0 + w, :], preferred_element_type=F32)
        m = t if m is None else m + t
        r0 += w
    o_ref[...] = x_ref[...] + jnp.dot(m.astype(BF16), wo_ref[...], preferred_element_type=F32)


def _merge(x2d, oa, ob, oc, od, P, wb, wo):
    n = x2d.shape[0]
    tm = min(n, TM_MERGE)
    row = lambda i: (i, 0)
    return pl.pallas_call(
        _merge_kernel,
        grid=(n // tm,),
        in_specs=[pl.BlockSpec((tm, D_MODEL), row), pl.BlockSpec((tm, W_A), row), pl.BlockSpec((tm, W_B), row),
                  pl.BlockSpec((tm, W_C), row), pl.BlockSpec((tm, W_D), row),
                  pl.BlockSpec((tm, N_BRANCH * D_MODEL), lambda i: (i, OFF_GATE // (N_BRANCH * D_MODEL))),
                  _const_spec((MIX_W, D_MODEL)), _const_spec((D_MODEL, D_MODEL))],
        out_specs=pl.BlockSpec((tm, D_MODEL), row),
        out_shape=jax.ShapeDtypeStruct((n, D_MODEL), F32),
        compiler_params=pltpu.CompilerParams(dimension_semantics=("arbitrary",), vmem_limit_bytes=VMEM_LIMIT),
        name="merge",
    )(x2d, oa, ob, oc, od, P, wb, wo)


def _mlp_kernel(x_ref, g_ref, wu_ref, wd_ref, gf_ref, *refs, final):
    if final:
        o_ref, y_ref, h_scr, acc_scr = refs
    else:
        o_ref, h_scr, acc_scr = refs
    f = pl.program_id(1)

    @pl.when(f == 0)
    def _():
        x = x_ref[...]
        r = lax.rsqrt(jnp.mean(x * x, axis=-1, keepdims=True) + EPS)
        h_scr[...] = ((x * r) * g_ref[...]).astype(BF16)
        acc_scr[...] = jnp.zeros(acc_scr.shape, F32)

    a = jnp.maximum(jnp.dot(h_scr[...], wu_ref[...], preferred_element_type=F32), 0.0)
    acc_scr[...] += jnp.dot((a * a).astype(BF16), wd_ref[...], preferred_element_type=F32)

    @pl.when(f == pl.num_programs(1) - 1)
    def _():
        out = x_ref[...] + acc_scr[...]
        o_ref[...] = out
        if final:
            r = lax.rsqrt(jnp.mean(out * out, axis=-1, keepdims=True) + EPS)
            y_ref[...] = (out * r) * gf_ref[...]


def _mlp(x2d, g, wu, wd, gf, final):
    n = x2d.shape[0]
    tm = min(n, TM_MLP)
    tf = TF_MLP
    row = lambda i, f: (i, 0)
    out_spec = pl.BlockSpec((tm, D_MODEL), row)
    shp = jax.ShapeDtypeStruct((n, D_MODEL), F32)
    return pl.pallas_call(
        functools.partial(_mlp_kernel, final=final),
        grid=(n // tm, D_FF // tf),
        in_specs=[pl.BlockSpec((tm, D_MODEL), row), _const_spec((1, D_MODEL)),
                  pl.BlockSpec((D_MODEL, tf), lambda i, f: (0, f)), pl.BlockSpec((tf, D_MODEL), lambda i, f: (f, 0)),
                  _const_spec((1, D_MODEL))],
        out_specs=[out_spec, out_spec] if final else [out_spec],
        out_shape=[shp, shp] if final else [shp],
        scratch_shapes=[pltpu.VMEM((tm, D_MODEL), BF16), pltpu.VMEM((tm, D_MODEL), F32)],
        compiler_params=pltpu.CompilerParams(dimension_semantics=("arbitrary", "arbitrary"), vmem_limit_bytes=VMEM_LIMIT),
        name="mlp_final" if final else "mlp",
    )(x2d, g.reshape(1, D_MODEL), wu, wd, gf.reshape(1, D_MODEL))


def _layer(x, conv_state, s_delta, s_ssm, s_hgrn, lw, consts, sample_ctx, final):
    B, T, _ = x.shape
    x2 = x.reshape(B * T, D_MODEL)
    P, Pq, Pk = _in_proj(x2, lw['norm_mix'], lw['w_in'])

    def conv0(lo, hi):
        if conv_state is None:
            return jnp.zeros((B, SUB, hi - lo), F32)
        return jnp.concatenate([jnp.zeros((B, SUB - (CONV_W - 1), hi - lo), F32), conv_state[:, :, lo:hi]], axis=1)

    zeros_state = lambda n: jnp.zeros((B, n, 64, 64), F32)
    c = consts
    half = CONV_CH // 2
    assert CONV_A == half and OFF_PC % half == 0
    oa, s_a = _mix_call(
        _mix_a_kernel, "mix_a", P, B, T,
        [(OFF_PC, half), (OFF_ZA, W_A), (OFF_SM, LANE)],
        [conv0(0, CONV_A), zeros_state(H_A) if s_delta is None else s_delta],
        [lw['conv_w'][:, 0:CONV_A], lw['conv_b'][None, 0:CONV_A], lw['hp'], lw['hpt'], lw['gn_a'], c['tril2a'], c['triu2a'], c['bo64']],
        W_A, (H_A, DK_A, DV_A),
        [pltpu.VMEM((TC_A + SUB, CONV_A), F32), pltpu.VMEM((H_A, DK_A, DV_A), F32), pltpu.VMEM((TC_A, W_A), F32)],
        tc=TC_A)
    oc, s_c = _mix_call(
        _mix_c_kernel, "mix_c", P, B, T,
        [(OFF_PC + half, half), (OFF_ZC, W_C), (OFF_SM, LANE)],
        [conv0(CONV_A, CONV_CH), zeros_state(H_C) if s_ssm is None else s_ssm],
        [lw['conv_w'][:, CONV_A:], lw['conv_b'][None, CONV_A:], lw['hp'], lw['hpt'], lw['gn_c'], c['tril1'], c['triu1'], c['bo256']],
        W_C, (H_C, N_C, P_C),
        [pltpu.VMEM((TC + SUB, CONV_C), F32), pltpu.VMEM((H_C, N_C, P_C), F32), pltpu.VMEM((TC, W_C), F32)])
    sbd0 = jnp.zeros((B, H_D, DK_D, H_D, DV_D), F32)
    if s_hgrn is not None:
        for h in range(H_D):
            sbd0 = sbd0.at[:, h, :, h, :].set(s_hgrn[:, h])
    sbd0 = sbd0.reshape(B, H_D * DK_D, W_D)
    od, sbd = _mix_call(
        _mix_d_kernel, "mix_d", P, B, T,
        [(OFF_D, 4 * W_D)],
        [sbd0],
        [lw['lb'], lw['gn_d'], c['tril2a'], c['sela'], c['bo64']],
        W_D, (H_D * DK_D, W_D),
        [pltpu.VMEM((H_D * DK_D, W_D), F32), pltpu.VMEM((TC_A, W_D), F32)],
        tc=TC_A)
    sbd5 = sbd.reshape(B, H_D, DK_D, H_D, DV_D)
    s_d = jnp.stack([sbd5[:, h, :, h, :] for h in range(H_D)], axis=1)
    if sample_ctx is None:
        ob = _dsa_prompt(Pq, Pk, lw['rel_bias'], B, T)
    else:
        ob = _dsa_sample(Pq, Pk, lw['rel_bias'], sample_ctx['cache_k'], sample_ctx['cache_v'], sample_ctx['cache_kidx'],
                         sample_ctx['page_table'], sample_ctx['layer'], B, T)
    x1 = _merge(x2, oa, ob, oc, od, P, lw['w_branch'], lw['w_out'])
    outs = _mlp(x1, lw['norm_mlp'], lw['w_up'], lw['w_down'], lw['norm_final'], final)
    xo = outs[0].reshape(B, T, D_MODEL)
    y = outs[1].reshape(B, T, D_MODEL) if final else None
    P3 = P.reshape(B, T, P_PAD)
    Pk3 = Pk.reshape(B, T, PK_W)
    states = (Pk3[:, :, PK_KV:PK_KV + DH_B], Pk3[:, :, PK_KV + DH_B:PK_KV + 2 * DH_B], Pk3[:, :, PK_SM + SM_KI:PK_SM + SM_KI + D_IDX],
              P3[:, T - (CONV_W - 1):, OFF_PC:OFF_PC + CONV_CH], s_a, s_c, s_d)
    return xo, y, states


def kernel(x_prompt, x_sample, cache_k, cache_v, cache_kidx, state_conv, state_delta, state_ssm, state_hgrn, page_table, norm_mix, w_in, conv_w, conv_b, a_log_a, dt_bias_a, gnorm_a, rel_bias, a_log_c, dt_bias_c, d_skip_c, gnorm_c, hgrn_gamma, gnorm_d, w_branch, w_out, norm_mlp, w_up, w_down, norm_final):
    depth = w_in.shape[0]
    npc = _np_consts()
    consts = {k: jnp.asarray(v, BF16) for k, v in npc.items()}
    pg = jax.nn.softmax(hgrn_gamma.astype(F32), axis=0)
    lower_bounds = jnp.cumsum(pg, axis=0) - pg[0]
    cache_kt, cache_vt, cache_kit = (jnp.swapaxes(c, 2, 3) for c in (cache_k, cache_v, cache_kidx))
    yp, ys = x_prompt, x_sample
    new_p, new_s = [], []
    for l in range(depth):
        hp = jnp.zeros((SUB, LANE), F32)
        hp = hp.at[0, 0:H_A].set(a_log_a[l]).at[1, 0:H_A].set(dt_bias_a[l])
        hp = hp.at[2, 0:H_C].set(a_log_c[l]).at[3, 0:H_C].set(dt_bias_c[l]).at[4, 0:H_C].set(d_skip_c[l])
        lw = dict(norm_mix=norm_mix[l], w_in=_permute_w_in(w_in[l]), conv_w=conv_w[l], conv_b=conv_b[l],
                  hp=hp, hpt=hp.T, gn_a=jnp.tile(gnorm_a[l], H_A)[None, :], gn_c=gnorm_c[l][None, :],
                  gn_d=jnp.tile(gnorm_d[l], H_D)[None, :], lb=lower_bounds[l][None, :], rel_bias=rel_bias,
                  w_branch=w_branch[l].astype(BF16), w_out=w_out[l].astype(BF16), norm_mlp=norm_mlp[l],
                  w_up=w_up[l].astype(BF16), w_down=w_down[l].astype(BF16), norm_final=norm_final)
        final = l == depth - 1
        yp, yp_n, st_p = _layer(yp, None, None, None, None, lw, consts, None, final)
        ctx = dict(cache_k=cache_kt, cache_v=cache_vt, cache_kidx=cache_kit, page_table=page_table, layer=l)
        ys, ys_n, st_s = _layer(ys, state_conv[l], state_delta[l], state_ssm[l], state_hgrn[l], lw, consts, ctx, final)
        new_p.append(st_p)
        new_s.append(st_s)
    stack = lambda per_layer: [jnp.stack(items) for items in zip(*per_layer)]
    return (yp_n, ys_n, *stack(new_p), *stack(new_s))
```
